```python
import math
import jax, jax.numpy as jnp
from jax import lax
import numpy as np

D_MODEL = 1024
BATCH = 8
SEQ = 2048
DEPTH = 1

D_MIX = D_MODEL
ATTN_WIDTH = D_MIX // 2
CONV_WIDTH = D_MIX - ATTN_WIDTH
DV = 128
N_ATTN_HEADS = ATTN_WIDTH // DV
DK = DV // 2
ROT_DIM = DK // 4
ROPE_THETA = 500000.0
Q_COLS = N_ATTN_HEADS * 2 * DK
K_COLS = N_ATTN_HEADS * 2 * DK
V_COLS = N_ATTN_HEADS * DV
IN_COLS = Q_COLS + K_COLS + V_COLS + 3 * CONV_WIDTH
CONV_K = 3
Q_BLOCK = 128
NORM_EPS = 1e-6
N_GROUPS = 4
EXPERTS_PER_GROUP = 8
N_EXPERTS = N_GROUPS * EXPERTS_PER_GROUP
TOP_K = 2
D_FF_EXPERT = D_MODEL // 2
MOE_BLOCK = 256

kernel_name = "hymba_diffattn_shortconv_hiermoe"


def rms_norm(x, gain, eps=NORM_EPS):
    xf = x.astype(jnp.float32)
    y = xf * lax.rsqrt(jnp.mean(xf * xf, axis=-1, keepdims=True) + eps)
    return (y * gain.astype(jnp.float32)).astype(x.dtype)


def partial_rotary(t, positions):
    half = ROT_DIM // 2
    freqs = ROPE_THETA ** (-jnp.arange(0, ROT_DIM, 2, dtype=jnp.float32) / ROT_DIM)
    ang = positions.astype(jnp.float32)[..., None] * freqs
    cos = jnp.cos(ang)[:, :, None, None, :]
    sin = jnp.sin(ang)[:, :, None, None, :]
    tf = t.astype(jnp.float32)
    t1, t2, rest = tf[..., :half], tf[..., half:ROT_DIM], tf[..., ROT_DIM:]
    out = jnp.concatenate([t1 * cos - t2 * sin, t2 * cos + t1 * sin, rest], axis=-1)
    return out.astype(t.dtype)


def diff_attention(q, k, v, lam, lambda_init, subln_gain):
    bsz, seq = q.shape[0], q.shape[1]
    scale = DK ** -0.5
    outs = []
    for i in range(seq // Q_BLOCK):
        q0 = i * Q_BLOCK
        end = q0 + Q_BLOCK
        qb = q[:, q0:end]
        kb = k[:, :end]
        vb = v[:, :end]
        s = jnp.einsum('bqhmd,bkhmd->bhmqk', qb, kb).astype(jnp.float32) * scale
        mask = jnp.arange(end)[None, :] <= (q0 + jnp.arange(Q_BLOCK))[:, None]
        s = jnp.where(mask, s, -jnp.inf)
        p = jax.nn.softmax(s, axis=-1)
        a = p[:, :, 0] - lam * p[:, :, 1]
        outs.append(jnp.einsum('bhqk,bkhd->bqhd', a.astype(v.dtype), vb))
    o = jnp.concatenate(outs, axis=1)
    o = rms_norm(o, subln_gain) * (1.0 - lambda_init)
    return o.reshape(bsz, seq, N_ATTN_HEADS * DV)


def short_conv(cb, cc, cu, conv_w, out_gain):
    y = cc * cu
    yp = jnp.pad(y, ((0, 0), (CONV_K - 1, 0), (0, 0)))
    seq = y.shape[1]
    z = sum(conv_w[j] * yp[:, j:j + seq] for j in range(CONV_K))
    return rms_norm(cb * z, out_gain)


def hier_moe(h, w_group_router, w_expert_router, w_gate, w_up, w_down):
    bsz, seq, d = h.shape
    xf = h.reshape(-1, d)
    n_tok = xf.shape[0]
    g_logits = (xf @ w_group_router).astype(jnp.float32)
    g_prob = jax.nn.softmax(g_logits, axis=-1)
    g_sel = jnp.argmax(g_logits, axis=-1)
    g_gate = jnp.take_along_axis(g_prob, g_sel[:, None], axis=-1)
    e_logits = jnp.einsum('nd,gde->nge', xf, w_expert_router).astype(jnp.float32)
    e_sel = jnp.take_along_axis(e_logits, g_sel[:, None, None], axis=1)[:, 0]
    top_v, top_i = lax.top_k(e_sel, TOP_K)
    weights = jax.nn.softmax(top_v, axis=-1) * g_gate
    expert_id = g_sel[:, None] * EXPERTS_PER_GROUP + top_i

    n_assign = n_tok * TOP_K
    flat_e = expert_id.reshape(-1)
    flat_w = weights.reshape(-1)
    flat_t = jnp.repeat(jnp.arange(n_tok, dtype=jnp.int32), TOP_K)
    order = jnp.argsort(flat_e)
    se, st, sw = flat_e[order], flat_t[order], flat_w[order]
    counts = jnp.bincount(flat_e, length=N_EXPERTS)
    starts = jnp.cumsum(counts) - counts
    padded = ((counts + MOE_BLOCK - 1) // MOE_BLOCK) * MOE_BLOCK
    pad_ends = jnp.cumsum(padded)
    pad_starts = pad_ends - padded
    dest = pad_starts[se] + (jnp.arange(n_assign) - starts[se])
    n_blocks = -(-n_assign // MOE_BLOCK) + N_EXPERTS
    rows = jnp.zeros((n_blocks * MOE_BLOCK, d), xf.dtype).at[dest].set(xf[st])
    block_start = jnp.arange(n_blocks) * MOE_BLOCK
    block_e = jnp.minimum(jnp.sum(pad_ends[None, :] <= block_start[:, None], axis=1), N_EXPERTS - 1)

    def expert_block(args):
        xb, e = args
        return (jax.nn.silu(xb @ w_gate[e]) * (xb @ w_up[e])) @ w_down[e]

    out_rows = lax.map(expert_block, (rows.reshape(n_blocks, MOE_BLOCK, d), block_e))
    out_rows = out_rows.reshape(n_blocks * MOE_BLOCK, d)
    y_assign = out_rows[dest] * sw[:, None].astype(out_rows.dtype)
    y = jax.ops.segment_sum(y_assign, st, num_segments=n_tok)
    return y.reshape(bsz, seq, d)


def setup_inputs(seed: int = 0) -> dict:
    key = jax.random.key(seed)
    ks = jax.random.split(key, 24)
    f32 = jnp.float32

    def nrm(k, shape, scale):
        return jax.random.normal(k, shape, f32) * scale

    def gain(k, shape):
        return 1.0 + 0.02 * jax.random.normal(k, shape, f32)

    x = jax.random.normal(ks[0], (BATCH, SEQ, D_MODEL), f32)
    offsets = jax.random.randint(ks[1], (BATCH, 1), 0, 1024, dtype=jnp.int32)
    positions = (offsets + jnp.arange(SEQ, dtype=jnp.int32)[None, :]).astype(jnp.int32)
    return {
        "x": x,
        "positions": positions,
        "attn_norm_gain": gain(ks[2], (DEPTH, D_MODEL)),
        "w_in": nrm(ks[3], (DEPTH, D_MODEL, IN_COLS), D_MODEL ** -0.5),
        "q_norm_gain": gain(ks[4], (DEPTH, DK)),
        "k_norm_gain": gain(ks[5], (DEPTH, DK)),
        "lambda_q1": nrm(ks[6], (DEPTH, DK), 0.1),
        "lambda_k1": nrm(ks[7], (DEPTH, DK), 0.1),
        "lambda_q2": nrm(ks[8], (DEPTH, DK), 0.1),
        "lambda_k2": nrm(ks[9], (DEPTH, DK), 0.1),
        "subln_gain": gain(ks[10], (DEPTH, DV)),
        "conv_w": nrm(ks[11], (DEPTH, CONV_K, CONV_WIDTH), CONV_K ** -0.5),
        "conv_out_gain": gain(ks[12], (DEPTH, CONV_WIDTH)),
        "w_out": nrm(ks[13], (DEPTH, D_MIX, D_MODEL), D_MIX ** -0.5),
        "ffn_norm_gain": gain(ks[14], (DEPTH, D_MODEL)),
        "w_group_router": nrm(ks[15], (DEPTH, D_MODEL, N_GROUPS), D_MODEL ** -0.5),
        "w_expert_router": nrm(ks[16], (DEPTH, N_GROUPS, D_MODEL, EXPERTS_PER_GROUP), D_MODEL ** -0.5),
        "w_gate": nrm(ks[17], (DEPTH, N_EXPERTS, D_MODEL, D_FF_EXPERT), D_MODEL ** -0.5),
        "w_up": nrm(ks[18], (DEPTH, N_EXPERTS, D_MODEL, D_FF_EXPERT), D_MODEL ** -0.5),
        "w_down": nrm(ks[19], (DEPTH, N_EXPERTS, D_FF_EXPERT, D_MODEL), D_FF_EXPERT ** -0.5),
    }


def reference(x, positions, attn_norm_gain, w_in, q_norm_gain, k_norm_gain,
              lambda_q1, lambda_k1, lambda_q2, lambda_k2, subln_gain, conv_w,
              conv_out_gain, w_out, ffn_norm_gain, w_group_router, w_expert_router,
              w_gate, w_up, w_down):
    bsz, seq, _ = x.shape
    splits = [Q_COLS, Q_COLS + K_COLS, Q_COLS + K_COLS + V_COLS,
              Q_COLS + K_COLS + V_COLS + CONV_WIDTH,
              Q_COLS + K_COLS + V_COLS + 2 * CONV_WIDTH]
    for l in range(DEPTH):
        lambda_init = 0.8 - 0.6 * math.exp(-0.3 * l)
        hn = rms_norm(x, attn_norm_gain[l])
        proj = hn @ w_in[l]
        q, k, v, cb, cc, cu = jnp.split(proj, splits, axis=-1)
        q = q.reshape(bsz, seq, N_ATTN_HEADS, 2, DK)
        k = k.reshape(bsz, seq, N_ATTN_HEADS, 2, DK)
        v = v.reshape(bsz, seq, N_ATTN_HEADS, DV)
        q = partial_rotary(rms_norm(q, q_norm_gain[l]), positions)
        k = partial_rotary(rms_norm(k, k_norm_gain[l]), positions)
        lam = (jnp.exp(jnp.sum(lambda_q1[l].astype(jnp.float32) * lambda_k1[l].astype(jnp.float32)))
               - jnp.exp(jnp.sum(lambda_q2[l].astype(jnp.float32) * lambda_k2[l].astype(jnp.float32)))
               + lambda_init)
        attn_o = diff_attention(q, k, v, lam, lambda_init, subln_gain[l])
        conv_o = short_conv(cb, cc, cu, conv_w[l], conv_out_gain[l])
        x = x + jnp.concatenate([attn_o, conv_o], axis=-1) @ w_out[l]
        hn2 = rms_norm(x, ffn_norm_gain[l])
        x = x + hier_moe(hn2, w_group_router[l], w_expert_router[l],
                         w_gate[l], w_up[l], w_down[l])
    return x
```

```python
import functools
import math

import jax
import jax.numpy as jnp
from jax import lax
from jax.experimental import pallas as pl
from jax.experimental.pallas import tpu as pltpu

D_MODEL = 1024
N_HEADS = 4
DK = 64
DV = 128
ROT_DIM = 16
ROPE_THETA = 500000.0
ATTN_WIDTH = N_HEADS * DV
CONV_WIDTH = 512
NORM_EPS = 1e-6
LAMBDA_INIT = 0.8 - 0.6 * math.exp(-0.3 * 0)
N_GROUPS = 4
EPG = 8
N_EXPERTS = N_GROUPS * EPG
D_FF = 512
MOE_BLOCK = 256

LANES = 128
SUBLANES = 8
ROW_WORDS = D_MODEL // 2
ROW_CHUNKS = ROW_WORDS // LANES
PACKED_DTYPE = jnp.uint32

TM_IN = 512
TQ = 256
T_SORT = 256
VMEM_LIMIT = 48 * 1024 * 1024


def _nt_dot(a, b):
    return lax.dot_general(a, b, (((1,), (1,)), ((), ())), preferred_element_type=jnp.float32)


def _dot(a, b):
    return jnp.dot(a, b, preferred_element_type=jnp.float32)


def _split3(x):
    h = x.astype(jnp.bfloat16)
    r = x - h.astype(jnp.float32)
    m = r.astype(jnp.bfloat16)
    l = (r - m.astype(jnp.float32)).astype(jnp.bfloat16)
    return h, m, l


def _split2(x):
    h = x.astype(jnp.bfloat16)
    l = (x - h.astype(jnp.float32)).astype(jnp.bfloat16)
    return h, l


def _inproj_kernel(x_ref, pos_ref, g1_ref, wqk_ref, wvt_ref, wc_ref, gqk_ref, freq_ref,
                   cw_ref, cg_ref,
                   q_ref, kz_ref, vt_ref, conv_ref,
                   carry_ref, *, tiles_per_seq):
    tm = x_ref.shape[0]
    i = pl.program_id(0)

    x = x_ref[...]
    ms = jnp.mean(x * x, axis=-1, keepdims=True)
    hn = (x * lax.rsqrt(ms + NORM_EPS) * g1_ref[...]).astype(jnp.bfloat16)

    pos = pos_ref[...].astype(jnp.float32)
    ang = freq_ref[...] * pos
    lane_r = lax.broadcasted_iota(jnp.int32, (LANES, SUBLANES), 0)
    f_c = lax.broadcasted_iota(jnp.int32, (LANES, SUBLANES), 1)
    in_rot = (lane_r % DK) < ROT_DIM
    expand = jnp.where(in_rot & ((lane_r % (ROT_DIM // 2)) == f_c), 1.0, 0.0).astype(jnp.bfloat16)

    def to_rows(t):
        h, m, l = _split3(t)
        r = _dot(expand, h) + _dot(expand, m) + _dot(expand, l)
        return r.T

    cos_r = to_rows(jnp.cos(ang))
    sin_r = to_rows(jnp.sin(ang))
    lane = lax.broadcasted_iota(jnp.int32, (tm, LANES), 1)
    d = lane % DK
    cos_r = jnp.where(d < ROT_DIM, cos_r, 1.0)
    sin_lo = jnp.where(d < ROT_DIM // 2, -sin_r, 0.0)
    sin_hi = jnp.where((d >= ROT_DIM // 2) & (d < ROT_DIM), sin_r, 0.0)

    qk = _dot(hn, wqk_ref[...])
    seg_r = lax.broadcasted_iota(jnp.int32, (2 * LANES, 2 * LANES), 0) // DK
    seg_c = lax.broadcasted_iota(jnp.int32, (2 * LANES, 2 * LANES), 1) // DK
    seg_mean = jnp.where(seg_r == seg_c, 1.0 / DK, 0.0).astype(jnp.bfloat16)
    half = ROT_DIM // 2
    for c2 in range(4):
        blk = qk[:, c2 * 256:(c2 + 1) * 256]
        sq_h, sq_l = _split2(blk * blk)
        msq = _dot(sq_h, seg_mean) + _dot(sq_l, seg_mean)
        y2 = blk * lax.rsqrt(msq + NORM_EPS) * gqk_ref[:, c2 * 256:(c2 + 1) * 256]
        for c1 in range(2):
            c = c2 * 2 + c1
            y = y2[:, c1 * LANES:(c1 + 1) * LANES]
            rot = (y * cos_r
                   + pltpu.roll(y, LANES - half, 1) * sin_lo
                   + pltpu.roll(y, half, 1) * sin_hi)
            if c < 4:
                q_ref[:, c * LANES:(c + 1) * LANES] = rot.astype(jnp.bfloat16)
            else:
                h = c - 4
                kz_ref[:, h * 256:h * 256 + LANES] = jnp.where(lane < DK, rot, 0.0).astype(jnp.bfloat16)
                kz_ref[:, h * 256 + LANES:(h + 1) * 256] = jnp.where(lane >= DK, rot, 0.0).astype(jnp.bfloat16)

    vt_ref[...] = _nt_dot(wvt_ref[...], hn).astype(jnp.bfloat16)

    cp = _dot(hn, wc_ref[...])
    cb = cp[:, :CONV_WIDTH]
    y = cp[:, CONV_WIDTH:2 * CONV_WIDTH] * cp[:, 2 * CONV_WIDTH:]

    @pl.when(i % tiles_per_seq == 0)
    def _():
        carry_ref[...] = jnp.zeros_like(carry_ref)

    prev = carry_ref[...]
    row = lax.broadcasted_iota(jnp.int32, (tm, CONV_WIDTH), 0)
    p1 = prev[SUBLANES - 1:SUBLANES, :]
    p2 = prev[SUBLANES - 2:SUBLANES - 1, :]
    y1 = jnp.where(row == 0, p1, pltpu.roll(y, 1, 0))
    y2 = jnp.where(row == 0, p2, jnp.where(row == 1, p1, pltpu.roll(y, 2, 0)))
    carry_ref[...] = y[tm - SUBLANES:, :]
    z = cw_ref[0:1, :] * y2 + cw_ref[1:2, :] * y1 + cw_ref[2:3, :] * y
    co = cb * z
    cms = jnp.mean(co * co, axis=-1, keepdims=True)
    conv_ref[...] = (co * lax.rsqrt(cms + NORM_EPS) * cg_ref[...]).astype(jnp.bfloat16)


def _inproj(x2, pos_row, g1, wqk, wvt, wc, gqk, freqs, cw, cg, seq):
    n = x2.shape[0]
    tm = TM_IN
    grid = (n // tm,)
    const = lambda i: (0, 0)
    return pl.pallas_call(
        functools.partial(_inproj_kernel, tiles_per_seq=seq // tm),
        grid=grid,
        in_specs=[
            pl.BlockSpec((tm, D_MODEL), lambda i: (i, 0)),
            pl.BlockSpec((1, tm), lambda i: (0, i)),
            pl.BlockSpec((1, D_MODEL), const),
            pl.BlockSpec((D_MODEL, 1024), const),
            pl.BlockSpec((512, D_MODEL), const),
            pl.BlockSpec((D_MODEL, 1536), const),
            pl.BlockSpec((1, 1024), const),
            pl.BlockSpec((SUBLANES, 1), const),
            pl.BlockSpec((3, CONV_WIDTH), const),
            pl.BlockSpec((1, CONV_WIDTH), const),
        ],
        out_specs=[
            pl.BlockSpec((tm, ATTN_WIDTH), lambda i: (i, 0)),
            pl.BlockSpec((tm, 2 * ATTN_WIDTH), lambda i: (i, 0)),
            pl.BlockSpec((ATTN_WIDTH, tm), lambda i: (0, i)),
            pl.BlockSpec((tm, CONV_WIDTH), lambda i: (i, 0)),
        ],
        out_shape=[
            jax.ShapeDtypeStruct((n, ATTN_WIDTH), jnp.bfloat16),
            jax.ShapeDtypeStruct((n, 2 * ATTN_WIDTH), jnp.bfloat16),
            jax.ShapeDtypeStruct((ATTN_WIDTH, n), jnp.bfloat16),
            jax.ShapeDtypeStruct((n, CONV_WIDTH), jnp.bfloat16),
        ],
        scratch_shapes=[pltpu.VMEM((SUBLANES, CONV_WIDTH), jnp.float32)],
        compiler_params=pltpu.CompilerParams(
            dimension_semantics=("arbitrary",), vmem_limit_bytes=VMEM_LIMIT),
        name="inproj",
    )(x2, pos_row, g1, wqk, wvt, wc, gqk, freqs, cw, cg)


def _attn_kernel(qa_ref, qb_ref, kz_ref, vt_ref, lq1_ref, lk1_ref, lq2_ref, lk2_ref, sg_ref,
                 o_ref, acc_ref, *, nq):
    c = pl.program_id(2)
    lam = (jnp.exp(jnp.sum(lq1_ref[...] * lk1_ref[...], axis=-1, keepdims=True))
           - jnp.exp(jnp.sum(lq2_ref[...] * lk2_ref[...], axis=-1, keepdims=True))
           + LAMBDA_INIT)
    key = lax.broadcasted_iota(jnp.int32, (TQ, TQ), 0)
    qry = lax.broadcasted_iota(jnp.int32, (TQ, TQ), 1)
    causal = key <= qry

    def branch(tiles):
        qs = [q_ref[...] for q_ref, _ in tiles]
        chains = [(ti, mp) for ti in range(len(tiles)) for mp in range(2)]
        todo = {ch: [tiles[ch[0]][1]] + list(range(tiles[ch[0]][1])) for ch in chains}
        tasks = []
        while any(todo.values()):
            for ch in chains:
                if todo[ch]:
                    tasks.append((ch, todo[ch].pop(0)))
        m_run, l_run = {}, {}

        def scores(task):
            (ti, mp), j = task
            lanes = slice(mp * LANES, (mp + 1) * LANES)
            s = _nt_dot(kz_ref[j * TQ:(j + 1) * TQ, lanes], qs[ti])
            if j == tiles[ti][1]:
                s = jnp.where(causal, s, -jnp.inf)
            return s

        def softmax(task, s):
            ch, _ = task
            mx = jnp.max(s, axis=0, keepdims=True)
            if ch not in m_run:
                alpha = None
                m_new = mx
                p = jnp.exp(s - m_new)
                l_run[ch] = jnp.sum(p, axis=0, keepdims=True)
            else:
                m_new = jnp.maximum(m_run[ch], mx)
                alpha = jnp.exp(m_run[ch] - m_new)
                p = jnp.exp(s - m_new)
                l_run[ch] = alpha * l_run[ch] + jnp.sum(p, axis=0, keepdims=True)
            m_run[ch] = m_new
            return p.astype(jnp.bfloat16), alpha

        def accumulate(task, p, alpha):
            (ti, mp), j = task
            slot = ti * 2 + mp
            pv = _dot(vt_ref[:, j * TQ:(j + 1) * TQ], p)
            acc_ref[slot] = pv if alpha is None else alpha * acc_ref[slot] + pv

        n = len(tasks)
        s_prev, p_prev = None, None
        for step in range(n + 2):
            if step >= 2:
                accumulate(tasks[step - 2], *p_prev)
            if 1 <= step <= n:
                p_prev = softmax(tasks[step - 1], s_prev)
            if step < n:
                s_prev = scores(tasks[step])

        for ti in range(len(tiles)):
            o = (acc_ref[2 * ti] / l_run[(ti, 0)]
                 - lam * (acc_ref[2 * ti + 1] / l_run[(ti, 1)]))
            ms = jnp.mean(o * o, axis=0, keepdims=True)
            on = o * lax.rsqrt(ms + NORM_EPS) * sg_ref[...] * (1.0 - LAMBDA_INIT)
            o_ref[ti * TQ:(ti + 1) * TQ, :] = on.T.astype(jnp.bfloat16)

    for cc in range(nq // 2):
        @pl.when(c == cc)
        def _():
            branch([(qa_ref, cc), (qb_ref, nq - 1 - cc)])


def _attention(q, kz, vt, lq1, lk1, lq2, lk2, sg_col, bsz, seq):
    n = q.shape[0]
    nq = seq // TQ
    assert nq % 2 == 0
    vec = lambda b, h, c: (0, 0)
    return pl.pallas_call(
        functools.partial(_attn_kernel, nq=nq),
        grid=(bsz, N_HEADS, nq // 2),
        in_specs=[
            pl.BlockSpec((TQ, DV), lambda b, h, c: (b * nq + c, h)),
            pl.BlockSpec((TQ, DV), lambda b, h, c: (b * nq + nq - 1 - c, h)),
            pl.BlockSpec((seq, 2 * DV), lambda b, h, c: (b, h)),
            pl.BlockSpec((DV, seq), lambda b, h, c: (h, b)),
            pl.BlockSpec((1, DK), vec), pl.BlockSpec((1, DK), vec),
            pl.BlockSpec((1, DK), vec), pl.BlockSpec((1, DK), vec),
            pl.BlockSpec((DV, 1), vec),
        ],
        out_specs=pl.BlockSpec((2 * TQ, DV), lambda b, h, c: (b * (nq // 2) + c, h)),
        out_shape=jax.ShapeDtypeStruct((n, ATTN_WIDTH), jnp.bfloat16),
        scratch_shapes=[pltpu.VMEM((4, DV, TQ), jnp.float32)],
        compiler_params=pltpu.CompilerParams(
            dimension_semantics=("arbitrary", "arbitrary", "arbitrary"), vmem_limit_bytes=VMEM_LIMIT),
        name="attn",
    )(q, q, kz, vt, lq1, lk1, lq2, lk2, sg_col)


def _pair_order_block(i, nq):
    return jnp.where(i < nq // 2, 2 * i, 2 * (nq - 1 - i) + 1)


ROUTER_ROWS = 128
EXPERT_ROW0 = 32


def _outproj_kernel(x_ref, attn_ref, conv_ref, wo_ref, g2_ref, wrh_ref, wrl_ref,
                    h_ref, xs_ref, meta_ref, cnt_ref):
    t = x_ref.shape[0]
    f32, bf16 = jnp.float32, jnp.bfloat16
    a = jnp.concatenate([attn_ref[...], conv_ref[...]], axis=1)
    h = x_ref[...] + _dot(a, wo_ref[...])
    h_ref[...] = h
    ms = jnp.mean(h * h, axis=-1, keepdims=True)
    hn = h * lax.rsqrt(ms + NORM_EPS) * g2_ref[...]
    hn_hi, hn_lo = _split2(hn)

    lt = _nt_dot(wrh_ref[...], hn_hi) + _nt_dot(wrh_ref[...], hn_lo) + _nt_dot(wrl_ref[...], hn_hi)
    row8 = lax.broadcasted_iota(jnp.int32, (SUBLANES, t), 0).astype(f32)
    neg_inf = -jnp.inf

    def first_argmax(v):
        mx = jnp.max(v, axis=0, keepdims=True)
        idx = jnp.min(jnp.where(v == mx, row8, float(SUBLANES)), axis=0, keepdims=True)
        return mx, idx

    g_log = jnp.where(row8 < N_GROUPS, lt[0:SUBLANES, :], neg_inf)
    g_max, g_sel = first_argmax(g_log)
    g_gate = 1.0 / jnp.sum(jnp.exp(g_log - g_max), axis=0, keepdims=True)
    e_log = jnp.zeros((EPG, t), f32)
    for g in range(N_GROUPS):
        rows = lt[EXPERT_ROW0 + g * EPG:EXPERT_ROW0 + (g + 1) * EPG, :]
        e_log = jnp.where(g_sel == float(g), rows, e_log)
    v1, i1 = first_argmax(e_log)
    v2, i2 = first_argmax(jnp.where(row8 == i1, neg_inf, e_log))
    tt = jnp.exp(v2 - v1)
    w1 = g_gate / (1.0 + tt)
    w2 = g_gate * tt / (1.0 + tt)
    e1 = g_sel * float(EPG) + i1
    e2 = g_sel * float(EPG) + i2

    row32 = lax.broadcasted_iota(jnp.int32, (N_EXPERTS, t), 0).astype(f32)
    oh1 = row32 == e1
    oh2 = row32 == e2
    c = jnp.where(oh1 | oh2, 1.0, 0.0)
    tok_r = lax.broadcasted_iota(jnp.int32, (t, t), 0)
    tok_c = lax.broadcasted_iota(jnp.int32, (t, t), 1)
    earlier = jnp.where(tok_r < tok_c, 1.0, 0.0).astype(bf16)
    rank = _dot(c.astype(bf16), earlier)
    cnt_b = jnp.broadcast_to(jnp.sum(c, axis=1, keepdims=True), (N_EXPERTS, t))
    ex_r = lax.broadcasted_iota(jnp.int32, (N_EXPERTS, N_EXPERTS), 0)
    ex_c = lax.broadcasted_iota(jnp.int32, (N_EXPERTS, N_EXPERTS), 1)
    lower = jnp.where(ex_c < ex_r, 1.0, 0.0).astype(bf16)
    start_b = _dot(lower, cnt_b.astype(bf16))
    pos_e = start_b + rank
    p1 = jnp.sum(jnp.where(oh1, pos_e, 0.0), axis=0, keepdims=True)
    p2 = jnp.sum(jnp.where(oh2, pos_e, 0.0), axis=0, keepdims=True)

    srow = lax.broadcasted_iota(jnp.int32, (2 * t, t), 0).astype(f32)
    perm = jnp.where((srow == p1) | (srow == p2), 1.0, 0.0).astype(bf16)
    xs = _dot(perm, hn_hi)
    xs_ref[...] = _pack_rows(xs)

    meta = jnp.concatenate([p1, p2, w1, w2, jnp.zeros((LANES - 4, t), f32)], axis=0)
    meta_ref[...] = meta.T
    cnt_ref[...] = cnt_b[:, :LANES]


def _outproj(x2, attn_p, conv_o, wo, g2, wrh, wrl, seq):
    n = x2.shape[0]
    t = T_SORT
    nt = n // t
    nq = seq // TQ
    const = lambda i: (0, 0)

    def attn_map(i):
        b = i // nq
        return (b * nq + _pair_order_block(i % nq, nq), 0)

    return pl.pallas_call(
        _outproj_kernel,
        grid=(nt,),
        in_specs=[
            pl.BlockSpec((t, D_MODEL), lambda i: (i, 0)),
            pl.BlockSpec((t, ATTN_WIDTH), attn_map),
            pl.BlockSpec((t, CONV_WIDTH), lambda i: (i, 0)),
            pl.BlockSpec((D_MODEL, D_MODEL), const),
            pl.BlockSpec((1, D_MODEL), const),
            pl.BlockSpec((ROUTER_ROWS, D_MODEL), const),
            pl.BlockSpec((ROUTER_ROWS, D_MODEL), const),
        ],
        out_specs=[
            pl.BlockSpec((t, D_MODEL), lambda i: (i, 0)),
            pl.BlockSpec((2 * t, ROW_CHUNKS, LANES), lambda i: (i, 0, 0)),
            pl.BlockSpec((t, LANES), lambda i: (i, 0)),
            pl.BlockSpec((N_EXPERTS, LANES), lambda i: (i, 0)),
        ],
        out_shape=[
            jax.ShapeDtypeStruct((n, D_MODEL), jnp.float32),
            jax.ShapeDtypeStruct((2 * n, ROW_CHUNKS, LANES), PACKED_DTYPE),
            jax.ShapeDtypeStruct((n, LANES), jnp.float32),
            jax.ShapeDtypeStruct((nt * N_EXPERTS, LANES), jnp.float32),
        ],
        compiler_params=pltpu.CompilerParams(
            dimension_semantics=("arbitrary",), vmem_limit_bytes=VMEM_LIMIT),
        name="outproj",
    )(x2, attn_p, conv_o, wo, g2, wrh, wrl)


def _dispatch_kernel(cnt_ref, src_ref, dst_ref, zstart_ref, zlen_ref, nused_ref, xs_hbm, xg_hbm,
                     zbuf, sem, zsem, *, n_tiles, n_rows, n_blocks):
    i = pl.program_id(0)

    def zero_fill(e):
        return pltpu.make_async_copy(zbuf.at[pl.ds(0, zlen_ref[e])],
                                     xg_hbm.at[pl.ds(zstart_ref[e], zlen_ref[e])], zsem)

    def zero_block(b):
        return pltpu.make_async_copy(zbuf, xg_hbm.at[pl.ds(b * MOE_BLOCK, MOE_BLOCK)], zsem)

    @pl.when(i == 0)
    def _():
        zbuf[...] = _packed_zero_rows(MOE_BLOCK)
        for e in range(N_EXPERTS):
            @pl.when(zlen_ref[e] > 0)
            def _():
                zero_fill(e).start()

        def start_tail(b, carry):
            zero_block(b).start()
            return carry
        lax.fori_loop(nused_ref[0], n_blocks, start_tail, 0)

    for e in range(N_EXPERTS):
        k = i * N_EXPERTS + e
        @pl.when(cnt_ref[k] > 0)
        def _():
            pltpu.make_async_copy(xs_hbm.at[pl.ds(src_ref[k], cnt_ref[k])],
                                  xg_hbm.at[pl.ds(dst_ref[k], cnt_ref[k])], sem).start()

    @pl.when(i == 0)
    def _():
        for e in range(N_EXPERTS):
            @pl.when(zlen_ref[e] > 0)
            def _():
                zero_fill(e).wait()

        def wait_tail(b, carry):
            zero_block(b).wait()
            return carry
        lax.fori_loop(nused_ref[0], n_blocks, wait_tail, 0)

    @pl.when(i == n_tiles - 1)
    def _():
        pltpu.make_async_copy(xs_hbm, xg_hbm.at[pl.ds(0, n_rows)], sem).wait()


def _dispatch(cnt, src, dst, zstart, zlen, nused, xs, n_slots):
    n_rows = xs.shape[0]
    n_tiles = cnt.shape[0] // N_EXPERTS
    return pl.pallas_call(
        functools.partial(_dispatch_kernel, n_tiles=n_tiles, n_rows=n_rows, n_blocks=n_slots // MOE_BLOCK),
        grid_spec=pltpu.PrefetchScalarGridSpec(
            num_scalar_prefetch=6,
            grid=(n_tiles,),
            in_specs=[pl.BlockSpec(memory_space=pl.ANY)],
            out_specs=pl.BlockSpec(memory_space=pl.ANY),
            scratch_shapes=[pltpu.VMEM((MOE_BLOCK, ROW_CHUNKS, LANES), PACKED_DTYPE),
                            pltpu.SemaphoreType.DMA, pltpu.SemaphoreType.DMA],
        ),
        out_shape=jax.ShapeDtypeStruct((n_slots, ROW_CHUNKS, LANES), PACKED_DTYPE),
        compiler_params=pltpu.CompilerParams(dimension_semantics=("arbitrary",)),
        name="dispatch",
    )(cnt, src, dst, zstart, zlen, nused, xs)


def _pack_rows(x):
    r = x.shape[0]
    w = pltpu.pack_elementwise([x[:, :ROW_WORDS], x[:, ROW_WORDS:]], packed_dtype=jnp.bfloat16)
    return pltpu.bitcast(w, PACKED_DTYPE).reshape(r, ROW_CHUNKS, LANES)


def _packed_zero_rows(r):
    z = jnp.zeros((r, ROW_CHUNKS, LANES), jnp.float32)
    w = pltpu.pack_elementwise([z, z], packed_dtype=jnp.bfloat16)
    return pltpu.bitcast(w, PACKED_DTYPE)


def _unpack_rows(u):
    r = u.shape[0]
    w = u.reshape(r, ROW_WORDS)
    lo = pltpu.unpack_elementwise(w, index=0, packed_dtype=jnp.bfloat16, unpacked_dtype=jnp.float32)
    hi = pltpu.unpack_elementwise(w, index=1, packed_dtype=jnp.bfloat16, unpacked_dtype=jnp.float32)
    return jnp.concatenate([lo, hi], axis=1).astype(jnp.bfloat16)


def _expert_kernel(be_ref, nused_ref, x_ref, wg_ref, wu_ref, wd_ref, y_ref, wg_s, wu_s, wd_s):
    b = pl.program_id(0)

    @pl.when(b < nused_ref[0])
    def _():
        prev = be_ref[jnp.maximum(b - 1, 0)]

        @pl.when((b == 0) | (be_ref[b] != prev))
        def _():
            wg_s[...] = wg_ref[0].astype(jnp.bfloat16)
            wu_s[...] = wu_ref[0].astype(jnp.bfloat16)
            wd_s[...] = wd_ref[0].astype(jnp.bfloat16)

        x = _unpack_rows(x_ref[...])
        g = _dot(x, wg_s[...])
        u = _dot(x, wu_s[...])
        act = (g / (1.0 + jnp.exp(-g)) * u).astype(jnp.bfloat16)
        y = _dot(act, wd_s[...])
        y_ref[...] = _pack_rows(y)

    @pl.when(b >= nused_ref[0])
    def _():
        y_ref[...] = _packed_zero_rows(MOE_BLOCK)


def _experts(block_e, nused, xg, w_gate, w_up, w_down):
    n_slots = xg.shape[0]
    n_blocks = n_slots // MOE_BLOCK

    def row_map(b, be, nu):
        return (jnp.minimum(b, nu[0] - 1), 0, 0)

    def w_map(b, be, nu):
        return (be[jnp.minimum(b, nu[0] - 1)], 0, 0)

    return pl.pallas_call(
        _expert_kernel,
        grid_spec=pltpu.PrefetchScalarGridSpec(
            num_scalar_prefetch=2,
            grid=(n_blocks,),
            in_specs=[
                pl.BlockSpec((MOE_BLOCK, ROW_CHUNKS, LANES), row_map),
                pl.BlockSpec((1, D_MODEL, D_FF), w_map),
                pl.BlockSpec((1, D_MODEL, D_FF), w_map),
                pl.BlockSpec((1, D_FF, D_MODEL), w_map),
            ],
            out_specs=pl.BlockSpec((MOE_BLOCK, ROW_CHUNKS, LANES), lambda b, be, nu: (b, 0, 0)),
            scratch_shapes=[pltpu.VMEM((D_MODEL, D_FF), jnp.bfloat16),
                            pltpu.VMEM((D_MODEL, D_FF), jnp.bfloat16),
                            pltpu.VMEM((D_FF, D_MODEL), jnp.bfloat16)],
        ),
        out_shape=jax.ShapeDtypeStruct((n_slots, ROW_CHUNKS, LANES), PACKED_DTYPE),
        compiler_params=pltpu.CompilerParams(
            dimension_semantics=("arbitrary",), vmem_limit_bytes=VMEM_LIMIT),
        name="experts",
    )(block_e, nused, xg, w_gate, w_up, w_down)


def _combine_kernel(cnt_ref, loc_ref, dst_ref, h_ref, meta_ref, yg_hbm, o_ref, ybuf, sem, *, n_tiles):
    i = pl.program_id(0)
    t = h_ref.shape[0]
    f32, bf16 = jnp.float32, jnp.bfloat16

    def issue(tile, slot):
        for e in range(N_EXPERTS):
            k = tile * N_EXPERTS + e
            @pl.when(cnt_ref[k] > 0)
            def _():
                pltpu.make_async_copy(yg_hbm.at[pl.ds(dst_ref[k], cnt_ref[k])],
                                      ybuf.at[slot, pl.ds(loc_ref[k], cnt_ref[k])],
                                      sem.at[slot]).start()

    @pl.when(i == 0)
    def _():
        issue(0, 0)

    @pl.when(i + 1 < n_tiles)
    def _():
        issue(i + 1, (i + 1) % 2)

    slot = i % 2
    pltpu.make_async_copy(yg_hbm.at[pl.ds(0, 2 * t)], ybuf.at[slot], sem.at[slot]).wait()

    y = _unpack_rows(ybuf[slot])
    lane = lax.broadcasted_iota(jnp.int32, (t, 2 * t), 1).astype(f32)
    meta = meta_ref[...]
    pick1 = jnp.where(lane == meta[:, 0:1], 1.0, 0.0).astype(bf16)
    pick2 = jnp.where(lane == meta[:, 1:2], 1.0, 0.0).astype(bf16)
    y1 = _dot(pick1, y)
    y2 = _dot(pick2, y)
    o_ref[...] = h_ref[...] + (meta[:, 2:3] * y1 + meta[:, 3:4] * y2)


def _combine(cnt, loc, dst, h, meta, yg):
    n = h.shape[0]
    t = T_SORT
    n_tiles = n // t
    return pl.pallas_call(
        functools.partial(_combine_kernel, n_tiles=n_tiles),
        grid_spec=pltpu.PrefetchScalarGridSpec(
            num_scalar_prefetch=3,
            grid=(n_tiles,),
            in_specs=[
                pl.BlockSpec((t, D_MODEL), lambda i, *_: (i, 0)),
                pl.BlockSpec((t, LANES), lambda i, *_: (i, 0)),
                pl.BlockSpec(memory_space=pl.ANY),
            ],
            out_specs=pl.BlockSpec((t, D_MODEL), lambda i, *_: (i, 0)),
            scratch_shapes=[pltpu.VMEM((2, 2 * t, ROW_CHUNKS, LANES), PACKED_DTYPE),
                            pltpu.SemaphoreType.DMA((2,))],
        ),
        out_shape=jax.ShapeDtypeStruct((n, D_MODEL), jnp.float32),
        compiler_params=pltpu.CompilerParams(
            dimension_semantics=("arbitrary",), vmem_limit_bytes=VMEM_LIMIT),
        name="combine",
    )(cnt, loc, dst, h, meta, yg)


def _routing_tables(cnt_out, n_tiles, n_tok):
    i32 = jnp.int32
    cnt = cnt_out.reshape(n_tiles, N_EXPERTS, LANES)[:, :, 0].astype(i32)
    count = jnp.sum(cnt, axis=0)
    padded = ((count + MOE_BLOCK - 1) // MOE_BLOCK) * MOE_BLOCK
    pad_end = jnp.cumsum(padded)
    pad_start = pad_end - padded
    rank_base = jnp.cumsum(cnt, axis=0) - cnt
    dst = pad_start[None, :] + rank_base
    loc = jnp.cumsum(cnt, axis=1) - cnt
    src = loc + (jnp.arange(n_tiles, dtype=i32) * (2 * T_SORT))[:, None]
    n_blocks = (2 * n_tok) // MOE_BLOCK + N_EXPERTS
    block_start = jnp.arange(n_blocks, dtype=i32) * MOE_BLOCK
    block_e = jnp.minimum(jnp.sum(pad_end[None, :] <= block_start[:, None], axis=1), N_EXPERTS - 1).astype(i32)
    nused = (pad_end[-1] // MOE_BLOCK).astype(i32).reshape(1)
    flat = lambda a: a.reshape(-1).astype(i32)
    return dict(cnt=flat(cnt), src=flat(src), dst=flat(dst), loc=flat(loc),
                zstart=flat(pad_start + count), zlen=flat(padded - count),
                block_e=block_e, nused=nused, n_slots=n_blocks * MOE_BLOCK)


def _stage1(x, positions, attn_norm_gain, w_in, q_norm_gain, k_norm_gain, conv_w, conv_out_gain):
    bsz, seq, _ = x.shape
    n = bsz * seq
    f32, bf16 = jnp.float32, jnp.bfloat16
    w = w_in[0]
    wqk = w[:, :2 * ATTN_WIDTH].astype(bf16)
    wvt = w[:, 2 * ATTN_WIDTH:3 * ATTN_WIDTH].T.astype(bf16)
    wc = w[:, 3 * ATTN_WIDTH:].astype(bf16)
    scale = DK ** -0.5
    gqk = jnp.concatenate([jnp.tile(q_norm_gain[0].astype(f32), 2 * N_HEADS) * scale,
                           jnp.tile(k_norm_gain[0].astype(f32), 2 * N_HEADS)]).reshape(1, -1)
    freqs = (ROPE_THETA ** (-jnp.arange(0, ROT_DIM, 2, dtype=f32) / ROT_DIM)).reshape(SUBLANES, 1)
    return _inproj(x.reshape(n, D_MODEL), positions.reshape(1, n),
                   attn_norm_gain[0].reshape(1, -1).astype(f32), wqk, wvt, wc, gqk, freqs,
                   conv_w[0].astype(f32), conv_out_gain[0].reshape(1, -1).astype(f32), seq)


def kernel(x, positions, attn_norm_gain, w_in, q_norm_gain, k_norm_gain, lambda_q1, lambda_k1, lambda_q2, lambda_k2, subln_gain, conv_w, conv_out_gain, w_out, ffn_norm_gain, w_group_router, w_expert_router, w_gate, w_up, w_down):
    bsz, seq, _ = x.shape
    n = bsz * seq
    f32, bf16 = jnp.float32, jnp.bfloat16
    assert T_SORT == TQ and seq % TM_IN == 0 and (seq // TQ) % 2 == 0
    q, kz, vt, conv_o = _stage1(x, positions, attn_norm_gain, w_in, q_norm_gain, k_norm_gain,
                                conv_w, conv_out_gain)
    attn_p = _attention(q, kz, vt,
                        lambda_q1[0].reshape(1, -1).astype(f32), lambda_k1[0].reshape(1, -1).astype(f32),
                        lambda_q2[0].reshape(1, -1).astype(f32), lambda_k2[0].reshape(1, -1).astype(f32),
                        subln_gain[0].reshape(-1, 1).astype(f32), bsz, seq)

    wr = jnp.zeros((ROUTER_ROWS, D_MODEL), f32)
    wr = wr.at[0:N_GROUPS].set(w_group_router[0].astype(f32).T)
    wr = wr.at[EXPERT_ROW0:EXPERT_ROW0 + N_EXPERTS].set(
        jnp.transpose(w_expert_router[0].astype(f32), (0, 2, 1)).reshape(N_EXPERTS, D_MODEL))
    wrh = wr.astype(bf16)
    wrl = (wr - wrh.astype(f32)).astype(bf16)
    h, xs, meta, cnt_out = _outproj(x.reshape(n, D_MODEL), attn_p, conv_o, w_out[0].astype(bf16),
                                    ffn_norm_gain[0].reshape(1, -1).astype(f32), wrh, wrl, seq)

    tabs = _routing_tables(cnt_out, n // T_SORT, n)
    xg = _dispatch(tabs["cnt"], tabs["src"], tabs["dst"], tabs["zstart"], tabs["zlen"], tabs["nused"], xs,
                   tabs["n_slots"])
    yg = _experts(tabs["block_e"], tabs["nused"], xg, w_gate[0], w_up[0], w_down[0])
    out = _combine(tabs["cnt"], tabs["loc"], tabs["dst"], h, meta, yg)
    return out.reshape(x.shape)
```

```python
import functools
import math

import jax
import jax.numpy as jnp
from jax import lax
from jax.experimental import pallas as pl
from jax.experimental.pallas import tpu as pltpu

D_MODEL = 1024
N_HEADS = 4
DK = 64
DV = 128
ROT_DIM = 16
ROPE_THETA = 500000.0
ATTN_WIDTH = N_HEADS * DV
CONV_WIDTH = 512
NORM_EPS = 1e-6
LAMBDA_INIT = 0.8 - 0.6 * math.exp(-0.3 * 0)
N_GROUPS = 4
EPG = 8
N_EXPERTS = N_GROUPS * EPG
D_FF = 512
MOE_BLOCK = 256

LANES = 128
SUBLANES = 8
ROW_WORDS = D_MODEL // 2
ROW_CHUNKS = ROW_WORDS // LANES
PACKED_DTYPE = jnp.uint32

TM_IN = 512
TQ = 256
T_SORT = 256
VMEM_LIMIT = 48 * 1024 * 1024


def _nt_dot(a, b):
    return lax.dot_general(a, b, (((1,), (1,)), ((), ())), preferred_element_type=jnp.float32)


def _dot(a, b):
    return jnp.dot(a, b, preferred_element_type=jnp.float32)


def _split3(x):
    h = x.astype(jnp.bfloat16)
    r = x - h.astype(jnp.float32)
    m = r.astype(jnp.bfloat16)
    l = (r - m.astype(jnp.float32)).astype(jnp.bfloat16)
    return h, m, l


def _split2(x):
    h = x.astype(jnp.bfloat16)
    l = (x - h.astype(jnp.float32)).astype(jnp.bfloat16)
    return h, l


def _inproj_kernel(x_ref, pos_ref, g1_ref, wqk_ref, wvt_ref, wc_ref, gqk_ref, freq_ref,
                   cw_ref, cg_ref,
                   q_ref, kz_ref, vt_ref, conv_ref,
                   carry_ref, *, tiles_per_seq):
    tm = x_ref.shape[0]
    i = pl.program_id(0)

    x = x_ref[...]
    ms = jnp.mean(x * x, axis=-1, keepdims=True)
    hn = (x * lax.rsqrt(ms + NORM_EPS) * g1_ref[...]).astype(jnp.bfloat16)

    pos = pos_ref[...].astype(jnp.float32)
    ang = freq_ref[...] * pos
    lane_r = lax.broadcasted_iota(jnp.int32, (LANES, SUBLANES), 0)
    f_c = lax.broadcasted_iota(jnp.int32, (LANES, SUBLANES), 1)
    in_rot = (lane_r % DK) < ROT_DIM
    expand = jnp.where(in_rot & ((lane_r % (ROT_DIM // 2)) == f_c), 1.0, 0.0).astype(jnp.bfloat16)

    def to_rows(t):
        h, m, l = _split3(t)
        r = _dot(expand, h) + _dot(expand, m) + _dot(expand, l)
        return r.T

    cos_r = to_rows(jnp.cos(ang))
    sin_r = to_rows(jnp.sin(ang))
    lane = lax.broadcasted_iota(jnp.int32, (tm, LANES), 1)
    d = lane % DK
    cos_r = jnp.where(d < ROT_DIM, cos_r, 1.0)
    sin_lo = jnp.where(d < ROT_DIM // 2, -sin_r, 0.0)
    sin_hi = jnp.where((d >= ROT_DIM // 2) & (d < ROT_DIM), sin_r, 0.0)

    qk = _dot(hn, wqk_ref[...])
    seg_r = lax.broadcasted_iota(jnp.int32, (2 * LANES, 2 * LANES), 0) // DK
    seg_c = lax.broadcasted_iota(jnp.int32, (2 * LANES, 2 * LANES), 1) // DK
    seg_mean = jnp.where(seg_r == seg_c, 1.0 / DK, 0.0).astype(jnp.bfloat16)
    half = ROT_DIM // 2
    for c2 in range(4):
        blk = qk[:, c2 * 256:(c2 + 1) * 256]
        sq_h, sq_l = _split2(blk * blk)
        msq = _dot(sq_h, seg_mean) + _dot(sq_l, seg_mean)
        y2 = blk * lax.rsqrt(msq + NORM_EPS) * gqk_ref[:, c2 * 256:(c2 + 1) * 256]
        for c1 in range(2):
            c = c2 * 2 + c1
            y = y2[:, c1 * LANES:(c1 + 1) * LANES]
            rot = (y * cos_r
                   + pltpu.roll(y, LANES - half, 1) * sin_lo
                   + pltpu.roll(y, half, 1) * sin_hi)
            if c < 4:
                q_ref[:, c * LANES:(c + 1) * LANES] = rot.astype(jnp.bfloat16)
            else:
                h = c - 4
                kz_ref[:, h * 256:h * 256 + LANES] = jnp.where(lane < DK, rot, 0.0).astype(jnp.bfloat16)
                kz_ref[:, h * 256 + LANES:(h + 1) * 256] = jnp.where(lane >= DK, rot, 0.0).astype(jnp.bfloat16)

    vt_ref[...] = _nt_dot(wvt_ref[...], hn).astype(jnp.bfloat16)

    cp = _dot(hn, wc_ref[...])
    cb = cp[:, :CONV_WIDTH]
    y = cp[:, CONV_WIDTH:2 * CONV_WIDTH] * cp[:, 2 * CONV_WIDTH:]

    @pl.when(i % tiles_per_seq == 0)
    def _():
        carry_ref[...] = jnp.zeros_like(carry_ref)

    prev = carry_ref[...]
    row = lax.broadcasted_iota(jnp.int32, (tm, CONV_WIDTH), 0)
    p1 = prev[SUBLANES - 1:SUBLANES, :]
    p2 = prev[SUBLANES - 2:SUBLANES - 1, :]
    y1 = jnp.where(row == 0, p1, pltpu.roll(y, 1, 0))
    y2 = jnp.where(row == 0, p2, jnp.where(row == 1, p1, pltpu.roll(y, 2, 0)))
    carry_ref[...] = y[tm - SUBLANES:, :]
    z = cw_ref[0:1, :] * y2 + cw_ref[1:2, :] * y1 + cw_ref[2:3, :] * y
    co = cb * z
    cms = jnp.mean(co * co, axis=-1, keepdims=True)
    conv_ref[...] = (co * lax.rsqrt(cms + NORM_EPS) * cg_ref[...]).astype(jnp.bfloat16)


def _inproj(x2, pos_row, g1, wqk, wvt, wc, gqk, freqs, cw, cg, seq):
    n = x2.shape[0]
    tm = TM_IN
    grid = (n // tm,)
    const = lambda i: (0, 0)
    return pl.pallas_call(
        functools.partial(_inproj_kernel, tiles_per_seq=seq // tm),
        grid=grid,
        in_specs=[
            pl.BlockSpec((tm, D_MODEL), lambda i: (i, 0)),
            pl.BlockSpec((1, tm), lambda i: (0, i)),
            pl.BlockSpec((1, D_MODEL), const),
            pl.BlockSpec((D_MODEL, 1024), const),
            pl.BlockSpec((512, D_MODEL), const),
            pl.BlockSpec((D_MODEL, 1536), const),
            pl.BlockSpec((1, 1024), const),
            pl.BlockSpec((SUBLANES, 1), const),
            pl.BlockSpec((3, CONV_WIDTH), const),
            pl.BlockSpec((1, CONV_WIDTH), const),
        ],
        out_specs=[
            pl.BlockSpec((tm, ATTN_WIDTH), lambda i: (i, 0)),
            pl.BlockSpec((tm, 2 * ATTN_WIDTH), lambda i: (i, 0)),
            pl.BlockSpec((ATTN_WIDTH, tm), lambda i: (0, i)),
            pl.BlockSpec((tm, CONV_WIDTH), lambda i: (i, 0)),
        ],
        out_shape=[
            jax.ShapeDtypeStruct((n, ATTN_WIDTH), jnp.bfloat16),
            jax.ShapeDtypeStruct((n, 2 * ATTN_WIDTH), jnp.bfloat16),
            jax.ShapeDtypeStruct((ATTN_WIDTH, n), jnp.bfloat16),
            jax.ShapeDtypeStruct((n, CONV_WIDTH), jnp.bfloat16),
        ],
        scratch_shapes=[pltpu.VMEM((SUBLANES, CONV_WIDTH), jnp.float32)],
        compiler_params=pltpu.CompilerParams(
            dimension_semantics=("arbitrary",), vmem_limit_bytes=VMEM_LIMIT),
        name="inproj",
    )(x2, pos_row, g1, wqk, wvt, wc, gqk, freqs, cw, cg)


def _attn_kernel(qa_ref, qb_ref, kz_ref, vt_ref, lq1_ref, lk1_ref, lq2_ref, lk2_ref, sg_ref,
                 o_ref, acc_ref, *, nq):
    c = pl.program_id(2)
    lam = (jnp.exp(jnp.sum(lq1_ref[...] * lk1_ref[...], axis=-1, keepdims=True))
           - jnp.exp(jnp.sum(lq2_ref[...] * lk2_ref[...], axis=-1, keepdims=True))
           + LAMBDA_INIT)
    key = lax.broadcasted_iota(jnp.int32, (TQ, TQ), 0)
    qry = lax.broadcasted_iota(jnp.int32, (TQ, TQ), 1)
    causal = key <= qry

    def branch(tiles):
        qs = [q_ref[...] for q_ref, _ in tiles]
        chains = [(ti, mp) for ti in range(len(tiles)) for mp in range(2)]
        todo = {ch: [tiles[ch[0]][1]] + list(range(tiles[ch[0]][1])) for ch in chains}
        tasks = []
        while any(todo.values()):
            for ch in chains:
                if todo[ch]:
                    tasks.append((ch, todo[ch].pop(0)))
        m_run, l_run = {}, {}

        def scores(task):
            (ti, mp), j = task
            lanes = slice(mp * LANES, (mp + 1) * LANES)
            s = _nt_dot(kz_ref[j * TQ:(j + 1) * TQ, lanes], qs[ti])
            if j == tiles[ti][1]:
                s = jnp.where(causal, s, -jnp.inf)
            return s

        def softmax(task, s):
            ch, _ = task
            mx = jnp.max(s, axis=0, keepdims=True)
            if ch not in m_run:
                alpha = None
                m_new = mx
                p = jnp.exp(s - m_new)
                l_run[ch] = jnp.sum(p, axis=0, keepdims=True)
            else:
                m_new = jnp.maximum(m_run[ch], mx)
                alpha = jnp.exp(m_run[ch] - m_new)
                p = jnp.exp(s - m_new)
                l_run[ch] = alpha * l_run[ch] + jnp.sum(p, axis=0, keepdims=True)
            m_run[ch] = m_new
            return p.astype(jnp.bfloat16), alpha

        def accumulate(task, p, alpha):
            (ti, mp), j = task
            slot = ti * 2 + mp
            pv = _dot(vt_ref[:, j * TQ:(j + 1) * TQ], p)
            acc_ref[slot] = pv if alpha is None else alpha * acc_ref[slot] + pv

        n = len(tasks)
        s_prev, p_prev = None, None
        for step in range(n + 2):
            if step >= 2:
                accumulate(tasks[step - 2], *p_prev)
            if 1 <= step <= n:
                p_prev = softmax(tasks[step - 1], s_prev)
            if step < n:
                s_prev = scores(tasks[step])

        for ti in range(len(tiles)):
            o = (acc_ref[2 * ti] / l_run[(ti, 0)]
                 - lam * (acc_ref[2 * ti + 1] / l_run[(ti, 1)]))
            ms = jnp.mean(o * o, axis=0, keepdims=True)
            on = o * lax.rsqrt(ms + NORM_EPS) * sg_ref[...] * (1.0 - LAMBDA_INIT)
            o_ref[ti * TQ:(ti + 1) * TQ, :] = on.T.astype(jnp.bfloat16)

    for cc in range(nq // 2):
        @pl.when(c == cc)
        def _():
            branch([(qa_ref, cc), (qb_ref, nq - 1 - cc)])


def _attention(q, kz, vt, lq1, lk1, lq2, lk2, sg_col, bsz, seq):
    n = q.shape[0]
    nq = seq // TQ
    assert nq % 2 == 0
    vec = lambda b, h, c: (0, 0)
    return pl.pallas_call(
        functools.partial(_attn_kernel, nq=nq),
        grid=(bsz, N_HEADS, nq // 2),
        in_specs=[
            pl.BlockSpec((TQ, DV), lambda b, h, c: (b * nq + c, h)),
            pl.BlockSpec((TQ, DV), lambda b, h, c: (b * nq + nq - 1 - c, h)),
            pl.BlockSpec((seq, 2 * DV), lambda b, h, c: (b, h)),
            pl.BlockSpec((DV, seq), lambda b, h, c: (h, b)),
            pl.BlockSpec((1, DK), vec), pl.BlockSpec((1, DK), vec),
            pl.BlockSpec((1, DK), vec), pl.BlockSpec((1, DK), vec),
            pl.BlockSpec((DV, 1), vec),
        ],
        out_specs=pl.BlockSpec((2 * TQ, DV), lambda b, h, c: (b * (nq // 2) + c, h)),
        out_shape=jax.ShapeDtypeStruct((n, ATTN_WIDTH), jnp.bfloat16),
        scratch_shapes=[pltpu.VMEM((4, DV, TQ), jnp.float32)],
        compiler_params=pltpu.CompilerParams(
            dimension_semantics=("arbitrary", "arbitrary", "arbitrary"), vmem_limit_bytes=VMEM_LIMIT),
        name="attn",
    )(q, q, kz, vt, lq1, lk1, lq2, lk2, sg_col)


def _pair_order_block(i, nq):
    return jnp.where(i < nq // 2, 2 * i, 2 * (nq - 1 - i) + 1)


ROUTER_ROWS = 128
EXPERT_ROW0 = 32


def _outproj_kernel(x_ref, attn_ref, conv_ref, wo_ref, g2_ref, wrh_ref, wrl_ref,
                    h_ref, xs_ref, meta_ref, cnt_ref):
    t = x_ref.shape[0]
    f32, bf16 = jnp.float32, jnp.bfloat16
    a = jnp.concatenate([attn_ref[...], conv_ref[...]], axis=1)
    h = x_ref[...] + _dot(a, wo_ref[...])
    h_ref[...] = h
    ms = jnp.mean(h * h, axis=-1, keepdims=True)
    hn = h * lax.rsqrt(ms + NORM_EPS) * g2_ref[...]
    hn_hi, hn_lo = _split2(hn)

    lt = _nt_dot(wrh_ref[...], hn_hi) + _nt_dot(wrh_ref[...], hn_lo) + _nt_dot(wrl_ref[...], hn_hi)
    row8 = lax.broadcasted_iota(jnp.int32, (SUBLANES, t), 0).astype(f32)
    neg_inf = -jnp.inf

    def first_argmax(v):
        mx = jnp.max(v, axis=0, keepdims=True)
        idx = jnp.min(jnp.where(v == mx, row8, float(SUBLANES)), axis=0, keepdims=True)
        return mx, idx

    g_log = jnp.where(row8 < N_GROUPS, lt[0:SUBLANES, :], neg_inf)
    g_max, g_sel = first_argmax(g_log)
    g_gate = 1.0 / jnp.sum(jnp.exp(g_log - g_max), axis=0, keepdims=True)
    e_log = jnp.zeros((EPG, t), f32)
    for g in range(N_GROUPS):
        rows = lt[EXPERT_ROW0 + g * EPG:EXPERT_ROW0 + (g + 1) * EPG, :]
        e_log = jnp.where(g_sel == float(g), rows, e_log)
    v1, i1 = first_argmax(e_log)
    v2, i2 = first_argmax(jnp.where(row8 == i1, neg_inf, e_log))
    tt = jnp.exp(v2 - v1)
    w1 = g_gate / (1.0 + tt)
    w2 = g_gate * tt / (1.0 + tt)
    e1 = g_sel * float(EPG) + i1
    e2 = g_sel * float(EPG) + i2

    row32 = lax.broadcasted_iota(jnp.int32, (N_EXPERTS, t), 0).astype(f32)
    oh1 = row32 == e1
    oh2 = row32 == e2
    c = jnp.where(oh1 | oh2, 1.0, 0.0)
    tok_r = lax.broadcasted_iota(jnp.int32, (t, t), 0)
    tok_c = lax.broadcasted_iota(jnp.int32, (t, t), 1)
    earlier = jnp.where(tok_r < tok_c, 1.0, 0.0).astype(bf16)
    rank = _dot(c.astype(bf16), earlier)
    cnt_b = jnp.broadcast_to(jnp.sum(c, axis=1, keepdims=True), (N_EXPERTS, t))
    ex_r = lax.broadcasted_iota(jnp.int32, (N_EXPERTS, N_EXPERTS), 0)
    ex_c = lax.broadcasted_iota(jnp.int32, (N_EXPERTS, N_EXPERTS), 1)
    lower = jnp.where(ex_c < ex_r, 1.0, 0.0).astype(bf16)
    start_b = _dot(lower, cnt_b.astype(bf16))
    pos_e = start_b + rank
    p1 = jnp.sum(jnp.where(oh1, pos_e, 0.0), axis=0, keepdims=True)
    p2 = jnp.sum(jnp.where(oh2, pos_e, 0.0), axis=0, keepdims=True)

    srow = lax.broadcasted_iota(jnp.int32, (2 * t, t), 0).astype(f32)
    perm = jnp.where((srow == p1) | (srow == p2), 1.0, 0.0).astype(bf16)
    xs = _dot(perm, hn_hi)
    xs_ref[...] = _pack_rows(xs)

    meta = jnp.concatenate([p1, p2, w1, w2, jnp.zeros((LANES - 4, t), f32)], axis=0)
    meta_ref[...] = meta.T
    cnt_ref[...] = cnt_b[:, :LANES]


def _outproj(x2, attn_p, conv_o, wo, g2, wrh, wrl, seq):
    n = x2.shape[0]
    t = T_SORT
    nt = n // t
    nq = seq // TQ
    const = lambda i: (0, 0)

    def attn_map(i):
        b = i // nq
        return (b * nq + _pair_order_block(i % nq, nq), 0)

    return pl.pallas_call(
        _outproj_kernel,
        grid=(nt,),
        in_specs=[
            pl.BlockSpec((t, D_MODEL), lambda i: (i, 0)),
            pl.BlockSpec((t, ATTN_WIDTH), attn_map),
            pl.BlockSpec((t, CONV_WIDTH), lambda i: (i, 0)),
            pl.BlockSpec((D_MODEL, D_MODEL), const),
            pl.BlockSpec((1, D_MODEL), const),
            pl.BlockSpec((ROUTER_ROWS, D_MODEL), const),
            pl.BlockSpec((ROUTER_ROWS, D_MODEL), const),
        ],
        out_specs=[
            pl.BlockSpec((t, D_MODEL), lambda i: (i, 0)),
            pl.BlockSpec((2 * t, ROW_CHUNKS, LANES), lambda i: (i, 0, 0)),
            pl.BlockSpec((t, LANES), lambda i: (i, 0)),
            pl.BlockSpec((N_EXPERTS, LANES), lambda i: (i, 0)),
        ],
        out_shape=[
            jax.ShapeDtypeStruct((n, D_MODEL), jnp.float32),
            jax.ShapeDtypeStruct((2 * n, ROW_CHUNKS, LANES), PACKED_DTYPE),
            jax.ShapeDtypeStruct((n, LANES), jnp.float32),
            jax.ShapeDtypeStruct((nt * N_EXPERTS, LANES), jnp.float32),
        ],
        compiler_params=pltpu.CompilerParams(
            dimension_semantics=("arbitrary",), vmem_limit_bytes=VMEM_LIMIT),
        name="outproj",
    )(x2, attn_p, conv_o, wo, g2, wrh, wrl)


def _pack_rows(x):
    r = x.shape[0]
    w = pltpu.pack_elementwise([x[:, :ROW_WORDS], x[:, ROW_WORDS:]], packed_dtype=jnp.bfloat16)
    return pltpu.bitcast(w, PACKED_DTYPE).reshape(r, ROW_CHUNKS, LANES)


def _packed_zero_rows(r):
    z = jnp.zeros((r, ROW_CHUNKS, LANES), jnp.float32)
    w = pltpu.pack_elementwise([z, z], packed_dtype=jnp.bfloat16)
    return pltpu.bitcast(w, PACKED_DTYPE)


def _unpack_rows(u):
    r = u.shape[0]
    w = u.reshape(r, ROW_WORDS)
    lo = pltpu.unpack_elementwise(w, index=0, packed_dtype=jnp.bfloat16, unpacked_dtype=jnp.float32)
    hi = pltpu.unpack_elementwise(w, index=1, packed_dtype=jnp.bfloat16, unpacked_dtype=jnp.float32)
    return jnp.concatenate([lo, hi], axis=1).astype(jnp.bfloat16)


def _expert_kernel(be_ref, nused_ref, r0_ref, nvalid_ref, tlo_ref, thi_ref, cnt_ref, src_ref, rbase_ref,
                   xs_hbm, wg_ref, wu_ref, wd_ref, y_ref, xbuf, wg_s, wu_s, wd_s, sem):
    b = pl.program_id(0)
    nused = nused_ref[0]

    def gather(blk, slot):
        e = be_ref[blk]
        r0 = r0_ref[blk]

        @pl.when(nvalid_ref[blk] < MOE_BLOCK)
        def _():
            xbuf[slot] = _packed_zero_rows(MOE_BLOCK)

        def run(t, carry):
            k = t * N_EXPERTS + e
            lo = jnp.maximum(rbase_ref[k], r0)
            hi = jnp.minimum(rbase_ref[k] + cnt_ref[k], r0 + MOE_BLOCK)

            @pl.when(hi > lo)
            def _():
                pltpu.make_async_copy(xs_hbm.at[pl.ds(src_ref[k] + lo - rbase_ref[k], hi - lo)],
                                      xbuf.at[slot, pl.ds(lo - r0, hi - lo)], sem.at[slot]).start()
            return carry
        lax.fori_loop(tlo_ref[blk], thi_ref[blk], run, 0)

    @pl.when(b == 0)
    def _():
        gather(0, 0)

    @pl.when(b + 1 < nused)
    def _():
        gather(b + 1, (b + 1) % 2)

    @pl.when(b < nused)
    def _():
        slot = b % 2
        nv = nvalid_ref[b]
        pltpu.make_async_copy(xs_hbm.at[pl.ds(0, nv)], xbuf.at[slot, pl.ds(0, nv)], sem.at[slot]).wait()

        @pl.when((b == 0) | (be_ref[b] != be_ref[jnp.maximum(b - 1, 0)]))
        def _():
            wg_s[...] = wg_ref[0].astype(jnp.bfloat16)
            wu_s[...] = wu_ref[0].astype(jnp.bfloat16)
            wd_s[...] = wd_ref[0].astype(jnp.bfloat16)

        x = _unpack_rows(xbuf[slot])
        g = _dot(x, wg_s[...])
        u = _dot(x, wu_s[...])
        act = (g / (1.0 + jnp.exp(-g)) * u).astype(jnp.bfloat16)
        y = _dot(act, wd_s[...])
        y_ref[...] = _pack_rows(y)

    @pl.when(b >= nused)
    def _():
        y_ref[...] = _packed_zero_rows(MOE_BLOCK)


def _experts(tabs, xs, w_gate, w_up, w_down):
    n_slots = tabs["n_slots"]
    n_blocks = n_slots // MOE_BLOCK

    def w_map(b, be, nu, *_):
        return (be[jnp.minimum(b, nu[0] - 1)], 0, 0)

    return pl.pallas_call(
        _expert_kernel,
        grid_spec=pltpu.PrefetchScalarGridSpec(
            num_scalar_prefetch=9,
            grid=(n_blocks,),
            in_specs=[
                pl.BlockSpec(memory_space=pl.ANY),
                pl.BlockSpec((1, D_MODEL, D_FF), w_map),
                pl.BlockSpec((1, D_MODEL, D_FF), w_map),
                pl.BlockSpec((1, D_FF, D_MODEL), w_map),
            ],
            out_specs=pl.BlockSpec((MOE_BLOCK, ROW_CHUNKS, LANES), lambda b, *_: (b, 0, 0)),
            scratch_shapes=[pltpu.VMEM((2, MOE_BLOCK, ROW_CHUNKS, LANES), PACKED_DTYPE),
                            pltpu.VMEM((D_MODEL, D_FF), jnp.bfloat16),
                            pltpu.VMEM((D_MODEL, D_FF), jnp.bfloat16),
                            pltpu.VMEM((D_FF, D_MODEL), jnp.bfloat16),
                            pltpu.SemaphoreType.DMA((2,))],
        ),
        out_shape=jax.ShapeDtypeStruct((n_slots, ROW_CHUNKS, LANES), PACKED_DTYPE),
        compiler_params=pltpu.CompilerParams(
            dimension_semantics=("arbitrary",), vmem_limit_bytes=VMEM_LIMIT),
        name="experts",
    )(tabs["block_e"], tabs["nused"], tabs["r0"], tabs["nvalid"], tabs["tlo"], tabs["thi"],
      tabs["cnt"], tabs["src"], tabs["rbase"], xs, w_gate, w_up, w_down)


def _combine_kernel(cnt_ref, loc_ref, dst_ref, h_ref, meta_ref, yg_hbm, o_ref, ybuf, sem, *, n_tiles):
    i = pl.program_id(0)
    t = h_ref.shape[0]
    f32, bf16 = jnp.float32, jnp.bfloat16

    def issue(tile, slot):
        for e in range(N_EXPERTS):
            k = tile * N_EXPERTS + e
            @pl.when(cnt_ref[k] > 0)
            def _():
                pltpu.make_async_copy(yg_hbm.at[pl.ds(dst_ref[k], cnt_ref[k])],
                                      ybuf.at[slot, pl.ds(loc_ref[k], cnt_ref[k])],
                                      sem.at[slot]).start()

    @pl.when(i == 0)
    def _():
        issue(0, 0)

    @pl.when(i + 1 < n_tiles)
    def _():
        issue(i + 1, (i + 1) % 2)

    slot = i % 2
    pltpu.make_async_copy(yg_hbm.at[pl.ds(0, 2 * t)], ybuf.at[slot], sem.at[slot]).wait()

    y = _unpack_rows(ybuf[slot])
    lane = lax.broadcasted_iota(jnp.int32, (t, 2 * t), 1).astype(f32)
    meta = meta_ref[...]
    pick1 = jnp.where(lane == meta[:, 0:1], 1.0, 0.0).astype(bf16)
    pick2 = jnp.where(lane == meta[:, 1:2], 1.0, 0.0).astype(bf16)
    y1 = _dot(pick1, y)
    y2 = _dot(pick2, y)
    o_ref[...] = h_ref[...] + (meta[:, 2:3] * y1 + meta[:, 3:4] * y2)


def _combine(cnt, loc, dst, h, meta, yg):
    n = h.shape[0]
    t = T_SORT
    n_tiles = n // t
    return pl.pallas_call(
        functools.partial(_combine_kernel, n_tiles=n_tiles),
        grid_spec=pltpu.PrefetchScalarGridSpec(
            num_scalar_prefetch=3,
            grid=(n_tiles,),
            in_specs=[
                pl.BlockSpec((t, D_MODEL), lambda i, *_: (i, 0)),
                pl.BlockSpec((t, LANES), lambda i, *_: (i, 0)),
                pl.BlockSpec(memory_space=pl.ANY),
            ],
            out_specs=pl.BlockSpec((t, D_MODEL), lambda i, *_: (i, 0)),
            scratch_shapes=[pltpu.VMEM((2, 2 * t, ROW_CHUNKS, LANES), PACKED_DTYPE),
                            pltpu.SemaphoreType.DMA((2,))],
        ),
        out_shape=jax.ShapeDtypeStruct((n, D_MODEL), jnp.float32),
        compiler_params=pltpu.CompilerParams(
            dimension_semantics=("arbitrary",), vmem_limit_bytes=VMEM_LIMIT),
        name="combine",
    )(cnt, loc, dst, h, meta, yg)


def _routing_tables(cnt_out, n_tiles, n_tok):
    i32 = jnp.int32
    cnt = cnt_out.reshape(n_tiles, N_EXPERTS, LANES)[:, :, 0].astype(i32)
    count = jnp.sum(cnt, axis=0)
    padded = ((count + MOE_BLOCK - 1) // MOE_BLOCK) * MOE_BLOCK
    pad_end = jnp.cumsum(padded)
    pad_start = pad_end - padded
    run_end = jnp.cumsum(cnt, axis=0)
    rbase = run_end - cnt
    dst = pad_start[None, :] + rbase
    loc = jnp.cumsum(cnt, axis=1) - cnt
    src = loc + (jnp.arange(n_tiles, dtype=i32) * (2 * T_SORT))[:, None]
    n_blocks = (2 * n_tok) // MOE_BLOCK + N_EXPERTS
    block_start = jnp.arange(n_blocks, dtype=i32) * MOE_BLOCK
    block_e = jnp.minimum(jnp.sum(pad_end[None, :] <= block_start[:, None], axis=1), N_EXPERTS - 1).astype(i32)
    nused = (pad_end[-1] // MOE_BLOCK).astype(i32).reshape(1)
    r0 = block_start - pad_start[block_e]
    nvalid = jnp.clip(count[block_e] - r0, 0, MOE_BLOCK)
    tlo = jnp.sum(run_end[:, block_e] <= r0[None, :], axis=0)
    thi = jnp.sum(rbase[:, block_e] < (r0 + MOE_BLOCK)[None, :], axis=0)
    flat = lambda a: a.reshape(-1).astype(i32)
    return dict(cnt=flat(cnt), src=flat(src), dst=flat(dst), loc=flat(loc), rbase=flat(rbase),
                block_e=block_e, nused=nused, r0=flat(r0), nvalid=flat(nvalid), tlo=flat(tlo), thi=flat(thi),
                n_slots=n_blocks * MOE_BLOCK)


def _stage1(x, positions, attn_norm_gain, w_in, q_norm_gain, k_norm_gain, conv_w, conv_out_gain):
    bsz, seq, _ = x.shape
    n = bsz * seq
    f32, bf16 = jnp.float32, jnp.bfloat16
    w = w_in[0]
    wqk = w[:, :2 * ATTN_WIDTH].astype(bf16)
    wvt = w[:, 2 * ATTN_WIDTH:3 * ATTN_WIDTH].T.astype(bf16)
    wc = w[:, 3 * ATTN_WIDTH:].astype(bf16)
    scale = DK ** -0.5
    gqk = jnp.concatenate([jnp.tile(q_norm_gain[0].astype(f32), 2 * N_HEADS) * scale,
                           jnp.tile(k_norm_gain[0].astype(f32), 2 * N_HEADS)]).reshape(1, -1)
    freqs = (ROPE_THETA ** (-jnp.arange(0, ROT_DIM, 2, dtype=f32) / ROT_DIM)).reshape(SUBLANES, 1)
    return _inproj(x.reshape(n, D_MODEL), positions.reshape(1, n),
                   attn_norm_gain[0].reshape(1, -1).astype(f32), wqk, wvt, wc, gqk, freqs,
                   conv_w[0].astype(f32), conv_out_gain[0].reshape(1, -1).astype(f32), seq)


def kernel(x, positions, attn_norm_gain, w_in, q_norm_gain, k_norm_gain, lambda_q1, lambda_k1, lambda_q2, lambda_k2, subln_gain, conv_w, conv_out_gain, w_out, ffn_norm_gain, w_group_router, w_expert_router, w_gate, w_up, w_down):
    bsz, seq, _ = x.shape
    n = bsz * seq
    f32, bf16 = jnp.float32, jnp.bfloat16
    assert T_SORT == TQ and seq % TM_IN == 0 and (seq // TQ) % 2 == 0
    q, kz, vt, conv_o = _stage1(x, positions, attn_norm_gain, w_in, q_norm_gain, k_norm_gain,
                                conv_w, conv_out_gain)
    attn_p = _attention(q, kz, vt,
                        lambda_q1[0].reshape(1, -1).astype(f32), lambda_k1[0].reshape(1, -1).astype(f32),
                        lambda_q2[0].reshape(1, -1).astype(f32), lambda_k2[0].reshape(1, -1).astype(f32),
                        subln_gain[0].reshape(-1, 1).astype(f32), bsz, seq)

    wr = jnp.zeros((ROUTER_ROWS, D_MODEL), f32)
    wr = wr.at[0:N_GROUPS].set(w_group_router[0].astype(f32).T)
    wr = wr.at[EXPERT_ROW0:EXPERT_ROW0 + N_EXPERTS].set(
        jnp.transpose(w_expert_router[0].astype(f32), (0, 2, 1)).reshape(N_EXPERTS, D_MODEL))
    wrh = wr.astype(bf16)
    wrl = (wr - wrh.astype(f32)).astype(bf16)
    h, xs, meta, cnt_out = _outproj(x.reshape(n, D_MODEL), attn_p, conv_o, w_out[0].astype(bf16),
                                    ffn_norm_gain[0].reshape(1, -1).astype(f32), wrh, wrl, seq)

    tabs = _routing_tables(cnt_out, n // T_SORT, n)
    yg = _experts(tabs, xs, w_gate[0], w_up[0], w_down[0])
    out = _combine(tabs["cnt"], tabs["loc"], tabs["dst"], h, meta, yg)
    return out.reshape(x.shape)
```

```python
import functools
import math

import jax
import jax.numpy as jnp
from jax import lax
from jax.experimental import pallas as pl
from jax.experimental.pallas import tpu as pltpu

D_MODEL = 1024
N_HEADS = 4
DK = 64
DV = 128
ROT_DIM = 16
ROPE_THETA = 500000.0
ATTN_WIDTH = N_HEADS * DV
CONV_WIDTH = 512
NORM_EPS = 1e-6
LOG2E = 1.4426950408889634
LAMBDA_INIT = 0.8 - 0.6 * math.exp(-0.3 * 0)
N_GROUPS = 4
EPG = 8
N_EXPERTS = N_GROUPS * EPG
D_FF = 512
MOE_BLOCK = 256

LANES = 128
SUBLANES = 8
ROW_WORDS = D_MODEL // 2
ROW_CHUNKS = ROW_WORDS // LANES
PACKED_DTYPE = jnp.uint32

TM_IN = 512
TQ = 512
HQ = TQ // 2
T_SORT = 256
VMEM_LIMIT = 48 * 1024 * 1024


def _nt_dot(a, b):
    return lax.dot_general(a, b, (((1,), (1,)), ((), ())), preferred_element_type=jnp.float32)


def _dot(a, b):
    return jnp.dot(a, b, preferred_element_type=jnp.float32)


def _split3(x):
    h = x.astype(jnp.bfloat16)
    r = x - h.astype(jnp.float32)
    m = r.astype(jnp.bfloat16)
    l = (r - m.astype(jnp.float32)).astype(jnp.bfloat16)
    return h, m, l


def _split2(x):
    h = x.astype(jnp.bfloat16)
    l = (x - h.astype(jnp.float32)).astype(jnp.bfloat16)
    return h, l


def _inproj_kernel(x_ref, pos_ref, g1_ref, wqk_ref, wvt_ref, wc_ref, gqk_ref, freq_ref,
                   cw_ref, cg_ref,
                   q_ref, k_ref, vt_ref, conv_ref,
                   carry_ref, *, tiles_per_seq):
    tm = x_ref.shape[0]
    i = pl.program_id(0)

    x = x_ref[...]
    ms = jnp.mean(x * x, axis=-1, keepdims=True)
    hn = (x * lax.rsqrt(ms + NORM_EPS) * g1_ref[...]).astype(jnp.bfloat16)

    pos = pos_ref[...].astype(jnp.float32)
    ang = freq_ref[...] * pos
    lane_r = lax.broadcasted_iota(jnp.int32, (LANES, SUBLANES), 0)
    f_c = lax.broadcasted_iota(jnp.int32, (LANES, SUBLANES), 1)
    in_rot = (lane_r % DK) < ROT_DIM
    expand = jnp.where(in_rot & ((lane_r % (ROT_DIM // 2)) == f_c), 1.0, 0.0).astype(jnp.bfloat16)

    def to_rows(t):
        h, m, l = _split3(t)
        r = _dot(expand, h) + _dot(expand, m) + _dot(expand, l)
        return r.T

    cos_r = to_rows(jnp.cos(ang))
    sin_r = to_rows(jnp.sin(ang))
    lane = lax.broadcasted_iota(jnp.int32, (tm, LANES), 1)
    d = lane % DK
    cos_r = jnp.where(d < ROT_DIM, cos_r, 1.0)
    sin_lo = jnp.where(d < ROT_DIM // 2, -sin_r, 0.0)
    sin_hi = jnp.where((d >= ROT_DIM // 2) & (d < ROT_DIM), sin_r, 0.0)

    qk = _dot(hn, wqk_ref[...])
    seg_r = lax.broadcasted_iota(jnp.int32, (2 * LANES, 2 * LANES), 0) // DK
    seg_c = lax.broadcasted_iota(jnp.int32, (2 * LANES, 2 * LANES), 1) // DK
    seg_mean = jnp.where(seg_r == seg_c, 1.0 / DK, 0.0).astype(jnp.bfloat16)
    half = ROT_DIM // 2
    for c2 in range(4):
        blk = qk[:, c2 * 256:(c2 + 1) * 256]
        sq_h, sq_l = _split2(blk * blk)
        msq = _dot(sq_h, seg_mean) + _dot(sq_l, seg_mean)
        y2 = blk * lax.rsqrt(msq + NORM_EPS) * gqk_ref[:, c2 * 256:(c2 + 1) * 256]
        for c1 in range(2):
            c = c2 * 2 + c1
            y = y2[:, c1 * LANES:(c1 + 1) * LANES]
            rot = (y * cos_r
                   + pltpu.roll(y, LANES - half, 1) * sin_lo
                   + pltpu.roll(y, half, 1) * sin_hi)
            if c < N_HEADS:
                q_ref[:, c * LANES:(c + 1) * LANES] = rot.astype(jnp.bfloat16)
            else:
                h = c - N_HEADS
                k_ref[:, h * LANES:(h + 1) * LANES] = rot.astype(jnp.bfloat16)

    vt_ref[...] = _nt_dot(wvt_ref[...], hn).astype(jnp.bfloat16)

    cp = _dot(hn, wc_ref[...])
    cb = cp[:, :CONV_WIDTH]
    y = cp[:, CONV_WIDTH:2 * CONV_WIDTH] * cp[:, 2 * CONV_WIDTH:]

    @pl.when(i % tiles_per_seq == 0)
    def _():
        carry_ref[...] = jnp.zeros_like(carry_ref)

    prev = carry_ref[...]
    row = lax.broadcasted_iota(jnp.int32, (tm, CONV_WIDTH), 0)
    p1 = prev[SUBLANES - 1:SUBLANES, :]
    p2 = prev[SUBLANES - 2:SUBLANES - 1, :]
    y1 = jnp.where(row == 0, p1, pltpu.roll(y, 1, 0))
    y2 = jnp.where(row == 0, p2, jnp.where(row == 1, p1, pltpu.roll(y, 2, 0)))
    carry_ref[...] = y[tm - SUBLANES:, :]
    z = cw_ref[0:1, :] * y2 + cw_ref[1:2, :] * y1 + cw_ref[2:3, :] * y
    co = cb * z
    cms = jnp.mean(co * co, axis=-1, keepdims=True)
    conv_ref[...] = (co * lax.rsqrt(cms + NORM_EPS) * cg_ref[...]).astype(jnp.bfloat16)


def _inproj(x2, pos_row, g1, wqk, wvt, wc, gqk, freqs, cw, cg, seq):
    n = x2.shape[0]
    tm = TM_IN
    grid = (n // tm,)
    const = lambda i: (0, 0)
    return pl.pallas_call(
        functools.partial(_inproj_kernel, tiles_per_seq=seq // tm),
        grid=grid,
        in_specs=[
            pl.BlockSpec((tm, D_MODEL), lambda i: (i, 0)),
            pl.BlockSpec((1, tm), lambda i: (0, i)),
            pl.BlockSpec((1, D_MODEL), const),
            pl.BlockSpec((D_MODEL, 1024), const),
            pl.BlockSpec((512, D_MODEL), const),
            pl.BlockSpec((D_MODEL, 1536), const),
            pl.BlockSpec((1, 1024), const),
            pl.BlockSpec((SUBLANES, 1), const),
            pl.BlockSpec((3, CONV_WIDTH), const),
            pl.BlockSpec((1, CONV_WIDTH), const),
        ],
        out_specs=[
            pl.BlockSpec((tm, ATTN_WIDTH), lambda i: (i, 0)),
            pl.BlockSpec((tm, ATTN_WIDTH), lambda i: (i, 0)),
            pl.BlockSpec((ATTN_WIDTH, tm), lambda i: (0, i)),
            pl.BlockSpec((tm, CONV_WIDTH), lambda i: (i, 0)),
        ],
        out_shape=[
            jax.ShapeDtypeStruct((n, ATTN_WIDTH), jnp.bfloat16),
            jax.ShapeDtypeStruct((n, ATTN_WIDTH), jnp.bfloat16),
            jax.ShapeDtypeStruct((ATTN_WIDTH, n), jnp.bfloat16),
            jax.ShapeDtypeStruct((n, CONV_WIDTH), jnp.bfloat16),
        ],
        scratch_shapes=[pltpu.VMEM((SUBLANES, CONV_WIDTH), jnp.float32)],
        compiler_params=pltpu.CompilerParams(
            dimension_semantics=("arbitrary",), vmem_limit_bytes=VMEM_LIMIT),
        name="inproj",
    )(x2, pos_row, g1, wqk, wvt, wc, gqk, freqs, cw, cg)


def _attn_kernel(qa_ref, qb_ref, k_ref, vt_ref, lq1_ref, lk1_ref, lq2_ref, lk2_ref, sg_ref,
                 o_ref, acc_ref, *, nq):
    c = pl.program_id(2)
    lam = (jnp.exp(jnp.sum(lq1_ref[...] * lk1_ref[...], axis=-1, keepdims=True))
           - jnp.exp(jnp.sum(lq2_ref[...] * lk2_ref[...], axis=-1, keepdims=True))
           + LAMBDA_INIT)
    key_i = lax.broadcasted_iota(jnp.int32, (HQ, 4 * HQ), 0)
    col_i = lax.broadcasted_iota(jnp.int32, (HQ, 4 * HQ), 1)
    tri = key_i <= col_i % HQ
    mask_mixed = tri | (col_i >= 2 * HQ)
    mask_upper = tri[:, :2 * HQ]
    lane_q = lax.broadcasted_iota(jnp.int32, (HQ, LANES), 1)

    def branch(tiles):
        qz = []
        for q_ref, _ in tiles:
            parts = []
            for half in range(2):
                qh = q_ref[half * HQ:(half + 1) * HQ, :]
                parts.append(jnp.where(lane_q < DK, qh, jnp.zeros_like(qh)))
                parts.append(jnp.where(lane_q >= DK, qh, jnp.zeros_like(qh)))
            qz.append(jnp.concatenate(parts, axis=0))

        todo = {}
        for ti, (_, t) in enumerate(tiles):
            nb = 2 * t
            todo[ti] = [(nb, "mixed"), (nb + 1, "upper")] + [(j, "full") for j in range(nb)]
        tasks = []
        while any(todo.values()):
            for ti in range(len(tiles)):
                if todo[ti]:
                    tasks.append((ti,) + todo[ti].pop(0))
        m_run, l_run = {}, {}

        def scores(task):
            ti, j, kind = task
            kb = k_ref[j * HQ:(j + 1) * HQ, :]
            if kind == "upper":
                return jnp.where(mask_upper, _nt_dot(kb, qz[ti][2 * HQ:, :]), -jnp.inf)
            s = _nt_dot(kb, qz[ti])
            return jnp.where(mask_mixed, s, -jnp.inf) if kind == "mixed" else s

        def softmax(task, s):
            ti, j, kind = task
            mx = jnp.max(s, axis=0, keepdims=True)
            if kind == "mixed":
                p = jnp.exp2(s - mx)
                m_run[ti], l_run[ti] = mx, jnp.sum(p, axis=0, keepdims=True)
                return p.astype(jnp.bfloat16), None
            m_old = m_run[ti][:, 2 * HQ:] if kind == "upper" else m_run[ti]
            l_old = l_run[ti][:, 2 * HQ:] if kind == "upper" else l_run[ti]
            m_new = jnp.maximum(m_old, mx)
            alpha = jnp.exp2(m_old - m_new)
            p = jnp.exp2(s - m_new)
            l_new = alpha * l_old + jnp.sum(p, axis=0, keepdims=True)
            if kind == "upper":
                m_new = jnp.concatenate([m_run[ti][:, :2 * HQ], m_new], axis=1)
                l_new = jnp.concatenate([l_run[ti][:, :2 * HQ], l_new], axis=1)
            m_run[ti], l_run[ti] = m_new, l_new
            return p.astype(jnp.bfloat16), alpha

        def accumulate(task, p, alpha):
            ti, j, kind = task
            pv = _dot(vt_ref[:, j * HQ:(j + 1) * HQ], p)
            if kind == "mixed":
                acc_ref[ti] = pv
            elif kind == "upper":
                acc_ref[ti, :, 2 * HQ:] = alpha * acc_ref[ti, :, 2 * HQ:] + pv
            else:
                acc_ref[ti] = alpha * acc_ref[ti] + pv

        n = len(tasks)
        s_prev, p_prev = None, None
        for step in range(n + 2):
            if step >= 2:
                accumulate(tasks[step - 2], *p_prev)
            if 1 <= step <= n:
                p_prev = softmax(tasks[step - 1], s_prev)
            if step < n:
                s_prev = scores(tasks[step])

        for ti in range(len(tiles)):
            o_all = acc_ref[ti] / l_run[ti]
            for half in range(2):
                o = (o_all[:, (2 * half) * HQ:(2 * half + 1) * HQ]
                     - lam * o_all[:, (2 * half + 1) * HQ:(2 * half + 2) * HQ])
                ms = jnp.mean(o * o, axis=0, keepdims=True)
                on = o * lax.rsqrt(ms + NORM_EPS) * sg_ref[...] * (1.0 - LAMBDA_INIT)
                r0 = ti * TQ + half * HQ
                o_ref[r0:r0 + HQ, :] = on.T.astype(jnp.bfloat16)

    for cc in range(nq // 2):
        @pl.when(c == cc)
        def _():
            branch([(qa_ref, cc), (qb_ref, nq - 1 - cc)])


def _attention(q, k, vt, lq1, lk1, lq2, lk2, sg_col, bsz, seq):
    n = q.shape[0]
    nq = seq // TQ
    assert nq % 2 == 0
    vec = lambda b, h, c: (0, 0)
    return pl.pallas_call(
        functools.partial(_attn_kernel, nq=nq),
        grid=(bsz, N_HEADS, nq // 2),
        in_specs=[
            pl.BlockSpec((TQ, DV), lambda b, h, c: (b * nq + c, h)),
            pl.BlockSpec((TQ, DV), lambda b, h, c: (b * nq + nq - 1 - c, h)),
            pl.BlockSpec((seq, DV), lambda b, h, c: (b, h)),
            pl.BlockSpec((DV, seq), lambda b, h, c: (h, b)),
            pl.BlockSpec((1, DK), vec), pl.BlockSpec((1, DK), vec),
            pl.BlockSpec((1, DK), vec), pl.BlockSpec((1, DK), vec),
            pl.BlockSpec((DV, 1), vec),
        ],
        out_specs=pl.BlockSpec((2 * TQ, DV), lambda b, h, c: (b * (nq // 2) + c, h)),
        out_shape=jax.ShapeDtypeStruct((n, ATTN_WIDTH), jnp.bfloat16),
        scratch_shapes=[pltpu.VMEM((2, DV, 4 * HQ), jnp.float32)],
        compiler_params=pltpu.CompilerParams(
            dimension_semantics=("arbitrary", "arbitrary", "arbitrary"), vmem_limit_bytes=VMEM_LIMIT),
        name="attn",
    )(q, q, k, vt, lq1, lk1, lq2, lk2, sg_col)


def _attn_out_block(i, seq):
    per_seq = seq // T_SORT
    nq = seq // TQ
    b, r = i // per_seq, i % per_seq
    t, sub = r // (TQ // T_SORT), r % (TQ // T_SORT)
    pos = jnp.where(t < nq // 2, 2 * t, 2 * (nq - 1 - t) + 1)
    return b * per_seq + pos * (TQ // T_SORT) + sub


ROUTER_ROWS = 128
EXPERT_ROW0 = 32


def _outproj_kernel(x_ref, attn_ref, conv_ref, wo_ref, g2_ref, wrh_ref, wrl_ref,
                    h_ref, xs_ref, meta_ref, cnt_ref):
    t = x_ref.shape[0]
    f32, bf16 = jnp.float32, jnp.bfloat16
    a = jnp.concatenate([attn_ref[...], conv_ref[...]], axis=1)
    h = x_ref[...] + _dot(a, wo_ref[...])
    h_ref[...] = h
    ms = jnp.mean(h * h, axis=-1, keepdims=True)
    hn = h * lax.rsqrt(ms + NORM_EPS) * g2_ref[...]
    hn_hi, hn_lo = _split2(hn)

    lt = _nt_dot(wrh_ref[...], hn_hi) + _nt_dot(wrh_ref[...], hn_lo) + _nt_dot(wrl_ref[...], hn_hi)
    row8 = lax.broadcasted_iota(jnp.int32, (SUBLANES, t), 0).astype(f32)
    neg_inf = -jnp.inf

    def first_argmax(v):
        mx = jnp.max(v, axis=0, keepdims=True)
        idx = jnp.min(jnp.where(v == mx, row8, float(SUBLANES)), axis=0, keepdims=True)
        return mx, idx

    g_log = jnp.where(row8 < N_GROUPS, lt[0:SUBLANES, :], neg_inf)
    g_max, g_sel = first_argmax(g_log)
    g_gate = 1.0 / jnp.sum(jnp.exp(g_log - g_max), axis=0, keepdims=True)
    e_log = jnp.zeros((EPG, t), f32)
    for g in range(N_GROUPS):
        rows = lt[EXPERT_ROW0 + g * EPG:EXPERT_ROW0 + (g + 1) * EPG, :]
        e_log = jnp.where(g_sel == float(g), rows, e_log)
    v1, i1 = first_argmax(e_log)
    v2, i2 = first_argmax(jnp.where(row8 == i1, neg_inf, e_log))
    tt = jnp.exp(v2 - v1)
    w1 = g_gate / (1.0 + tt)
    w2 = g_gate * tt / (1.0 + tt)
    e1 = g_sel * float(EPG) + i1
    e2 = g_sel * float(EPG) + i2

    row32 = lax.broadcasted_iota(jnp.int32, (N_EXPERTS, t), 0).astype(f32)
    oh1 = row32 == e1
    oh2 = row32 == e2
    c = jnp.where(oh1 | oh2, 1.0, 0.0)
    tok_r = lax.broadcasted_iota(jnp.int32, (t, t), 0)
    tok_c = lax.broadcasted_iota(jnp.int32, (t, t), 1)
    earlier = jnp.where(tok_r < tok_c, 1.0, 0.0).astype(bf16)
    rank = _dot(c.astype(bf16), earlier)
    cnt_b = jnp.broadcast_to(jnp.sum(c, axis=1, keepdims=True), (N_EXPERTS, t))
    ex_r = lax.broadcasted_iota(jnp.int32, (N_EXPERTS, N_EXPERTS), 0)
    ex_c = lax.broadcasted_iota(jnp.int32, (N_EXPERTS, N_EXPERTS), 1)
    lower = jnp.where(ex_c < ex_r, 1.0, 0.0).astype(bf16)
    start_b = _dot(lower, cnt_b.astype(bf16))
    pos_e = start_b + rank
    p1 = jnp.sum(jnp.where(oh1, pos_e, 0.0), axis=0, keepdims=True)
    p2 = jnp.sum(jnp.where(oh2, pos_e, 0.0), axis=0, keepdims=True)

    srow = lax.broadcasted_iota(jnp.int32, (2 * t, t), 0).astype(f32)
    perm = jnp.where((srow == p1) | (srow == p2), 1.0, 0.0).astype(bf16)
    xs = _dot(perm, hn_hi)
    xs_ref[...] = _pack_rows(xs)

    meta = jnp.concatenate([p1, p2, w1, w2, jnp.zeros((LANES - 4, t), f32)], axis=0)
    meta_ref[...] = meta.T
    cnt_ref[...] = cnt_b[:, :LANES]


def _outproj(x2, attn_p, conv_o, wo, g2, wrh, wrl, seq):
    n = x2.shape[0]
    t = T_SORT
    nt = n // t
    const = lambda i: (0, 0)

    def attn_map(i):
        return (_attn_out_block(i, seq), 0)

    return pl.pallas_call(
        _outproj_kernel,
        grid=(nt,),
        in_specs=[
            pl.BlockSpec((t, D_MODEL), lambda i: (i, 0)),
            pl.BlockSpec((t, ATTN_WIDTH), attn_map),
            pl.BlockSpec((t, CONV_WIDTH), lambda i: (i, 0)),
            pl.BlockSpec((D_MODEL, D_MODEL), const),
            pl.BlockSpec((1, D_MODEL), const),
            pl.BlockSpec((ROUTER_ROWS, D_MODEL), const),
            pl.BlockSpec((ROUTER_ROWS, D_MODEL), const),
        ],
        out_specs=[
            pl.BlockSpec((t, D_MODEL), lambda i: (i, 0)),
            pl.BlockSpec((2 * t, ROW_CHUNKS, LANES), lambda i: (i, 0, 0)),
            pl.BlockSpec((t, LANES), lambda i: (i, 0)),
            pl.BlockSpec((N_EXPERTS, LANES), lambda i: (i, 0)),
        ],
        out_shape=[
            jax.ShapeDtypeStruct((n, D_MODEL), jnp.float32),
            jax.ShapeDtypeStruct((2 * n, ROW_CHUNKS, LANES), PACKED_DTYPE),
            jax.ShapeDtypeStruct((n, LANES), jnp.float32),
            jax.ShapeDtypeStruct((nt * N_EXPERTS, LANES), jnp.float32),
        ],
        compiler_params=pltpu.CompilerParams(
            dimension_semantics=("arbitrary",), vmem_limit_bytes=VMEM_LIMIT),
        name="outproj",
    )(x2, attn_p, conv_o, wo, g2, wrh, wrl)


def _pack_rows(x):
    r = x.shape[0]
    w = pltpu.pack_elementwise([x[:, :ROW_WORDS], x[:, ROW_WORDS:]], packed_dtype=jnp.bfloat16)
    return pltpu.bitcast(w, PACKED_DTYPE).reshape(r, ROW_CHUNKS, LANES)


def _packed_zero_rows(r):
    z = jnp.zeros((r, ROW_CHUNKS, LANES), jnp.float32)
    w = pltpu.pack_elementwise([z, z], packed_dtype=jnp.bfloat16)
    return pltpu.bitcast(w, PACKED_DTYPE)


def _unpack_rows(u):
    r = u.shape[0]
    w = u.reshape(r, ROW_WORDS)
    lo = pltpu.unpack_elementwise(w, index=0, packed_dtype=jnp.bfloat16, unpacked_dtype=jnp.float32)
    hi = pltpu.unpack_elementwise(w, index=1, packed_dtype=jnp.bfloat16, unpacked_dtype=jnp.float32)
    return jnp.concatenate([lo, hi], axis=1).astype(jnp.bfloat16)


def _expert_kernel(be_ref, nused_ref, r0_ref, nvalid_ref, tlo_ref, thi_ref, cnt_ref, src_ref, rbase_ref,
                   xs_hbm, wg_ref, wu_ref, wd_ref, y_ref, xbuf, wg_s, wu_s, wd_s, sem):
    b = pl.program_id(0)
    nused = nused_ref[0]

    def gather(blk, slot):
        e = be_ref[blk]
        r0 = r0_ref[blk]

        @pl.when(nvalid_ref[blk] < MOE_BLOCK)
        def _():
            xbuf[slot] = _packed_zero_rows(MOE_BLOCK)

        def run(t, carry):
            k = t * N_EXPERTS + e
            lo = jnp.maximum(rbase_ref[k], r0)
            hi = jnp.minimum(rbase_ref[k] + cnt_ref[k], r0 + MOE_BLOCK)

            @pl.when(hi > lo)
            def _():
                pltpu.make_async_copy(xs_hbm.at[pl.ds(src_ref[k] + lo - rbase_ref[k], hi - lo)],
                                      xbuf.at[slot, pl.ds(lo - r0, hi - lo)], sem.at[slot]).start()
            return carry
        lax.fori_loop(tlo_ref[blk], thi_ref[blk], run, 0)

    @pl.when(b == 0)
    def _():
        gather(0, 0)

    @pl.when(b + 1 < nused)
    def _():
        gather(b + 1, (b + 1) % 2)

    @pl.when(b < nused)
    def _():
        slot = b % 2
        nv = nvalid_ref[b]
        pltpu.make_async_copy(xs_hbm.at[pl.ds(0, nv)], xbuf.at[slot, pl.ds(0, nv)], sem.at[slot]).wait()

        @pl.when((b == 0) | (be_ref[b] != be_ref[jnp.maximum(b - 1, 0)]))
        def _():
            wg_s[...] = wg_ref[0].astype(jnp.bfloat16)
            wu_s[...] = wu_ref[0].astype(jnp.bfloat16)
            wd_s[...] = wd_ref[0].astype(jnp.bfloat16)

        x = _unpack_rows(xbuf[slot])
        g = _dot(x, wg_s[...])
        u = _dot(x, wu_s[...])
        act = (g / (1.0 + jnp.exp(-g)) * u).astype(jnp.bfloat16)
        y = _dot(act, wd_s[...])
        y_ref[...] = _pack_rows(y)

    @pl.when(b >= nused)
    def _():
        y_ref[...] = _packed_zero_rows(MOE_BLOCK)


def _experts(tabs, xs, w_gate, w_up, w_down):
    n_slots = tabs["n_slots"]
    n_blocks = n_slots // MOE_BLOCK

    def w_map(b, be, nu, *_):
        return (be[jnp.minimum(b, nu[0] - 1)], 0, 0)

    return pl.pallas_call(
        _expert_kernel,
        grid_spec=pltpu.PrefetchScalarGridSpec(
            num_scalar_prefetch=9,
            grid=(n_blocks,),
            in_specs=[
                pl.BlockSpec(memory_space=pl.ANY),
                pl.BlockSpec((1, D_MODEL, D_FF), w_map),
                pl.BlockSpec((1, D_MODEL, D_FF), w_map),
                pl.BlockSpec((1, D_FF, D_MODEL), w_map),
            ],
            out_specs=pl.BlockSpec((MOE_BLOCK, ROW_CHUNKS, LANES), lambda b, *_: (b, 0, 0)),
            scratch_shapes=[pltpu.VMEM((2, MOE_BLOCK, ROW_CHUNKS, LANES), PACKED_DTYPE),
                            pltpu.VMEM((D_MODEL, D_FF), jnp.bfloat16),
                            pltpu.VMEM((D_MODEL, D_FF), jnp.bfloat16),
                            pltpu.VMEM((D_FF, D_MODEL), jnp.bfloat16),
                            pltpu.SemaphoreType.DMA((2,))],
        ),
        out_shape=jax.ShapeDtypeStruct((n_slots, ROW_CHUNKS, LANES), PACKED_DTYPE),
        compiler_params=pltpu.CompilerParams(
            dimension_semantics=("arbitrary",), vmem_limit_bytes=VMEM_LIMIT),
        name="experts",
    )(tabs["block_e"], tabs["nused"], tabs["r0"], tabs["nvalid"], tabs["tlo"], tabs["thi"],
      tabs["cnt"], tabs["src"], tabs["rbase"], xs, w_gate, w_up, w_down)


def _combine_kernel(cnt_ref, loc_ref, dst_ref, h_ref, meta_ref, yg_hbm, o_ref, ybuf, sem, *, n_tiles):
    i = pl.program_id(0)
    t = h_ref.shape[0]
    f32, bf16 = jnp.float32, jnp.bfloat16

    def issue(tile, slot):
        for e in range(N_EXPERTS):
            k = tile * N_EXPERTS + e
            @pl.when(cnt_ref[k] > 0)
            def _():
                pltpu.make_async_copy(yg_hbm.at[pl.ds(dst_ref[k], cnt_ref[k])],
                                      ybuf.at[slot, pl.ds(loc_ref[k], cnt_ref[k])],
                                      sem.at[slot]).start()

    @pl.when(i == 0)
    def _():
        issue(0, 0)

    @pl.when(i + 1 < n_tiles)
    def _():
        issue(i + 1, (i + 1) % 2)

    slot = i % 2
    pltpu.make_async_copy(yg_hbm.at[pl.ds(0, 2 * t)], ybuf.at[slot], sem.at[slot]).wait()

    y = _unpack_rows(ybuf[slot])
    lane = lax.broadcasted_iota(jnp.int32, (t, 2 * t), 1).astype(f32)
    meta = meta_ref[...]
    pick1 = jnp.where(lane == meta[:, 0:1], 1.0, 0.0).astype(bf16)
    pick2 = jnp.where(lane == meta[:, 1:2], 1.0, 0.0).astype(bf16)
    y1 = _dot(pick1, y)
    y2 = _dot(pick2, y)
    o_ref[...] = h_ref[...] + (meta[:, 2:3] * y1 + meta[:, 3:4] * y2)


def _combine(cnt, loc, dst, h, meta, yg):
    n = h.shape[0]
    t = T_SORT
    n_tiles = n // t
    return pl.pallas_call(
        functools.partial(_combine_kernel, n_tiles=n_tiles),
        grid_spec=pltpu.PrefetchScalarGridSpec(
            num_scalar_prefetch=3,
            grid=(n_tiles,),
            in_specs=[
                pl.BlockSpec((t, D_MODEL), lambda i, *_: (i, 0)),
                pl.BlockSpec((t, LANES), lambda i, *_: (i, 0)),
                pl.BlockSpec(memory_space=pl.ANY),
            ],
            out_specs=pl.BlockSpec((t, D_MODEL), lambda i, *_: (i, 0)),
            scratch_shapes=[pltpu.VMEM((2, 2 * t, ROW_CHUNKS, LANES), PACKED_DTYPE),
                            pltpu.SemaphoreType.DMA((2,))],
        ),
        out_shape=jax.ShapeDtypeStruct((n, D_MODEL), jnp.float32),
        compiler_params=pltpu.CompilerParams(
            dimension_semantics=("arbitrary",), vmem_limit_bytes=VMEM_LIMIT),
        name="combine",
    )(cnt, loc, dst, h, meta, yg)


def _routing_tables(cnt_out, n_tiles, n_tok):
    i32 = jnp.int32
    cnt = cnt_out.reshape(n_tiles, N_EXPERTS, LANES)[:, :, 0].astype(i32)
    count = jnp.sum(cnt, axis=0)
    padded = ((count + MOE_BLOCK - 1) // MOE_BLOCK) * MOE_BLOCK
    pad_end = jnp.cumsum(padded)
    pad_start = pad_end - padded
    run_end = jnp.cumsum(cnt, axis=0)
    rbase = run_end - cnt
    dst = pad_start[None, :] + rbase
    loc = jnp.cumsum(cnt, axis=1) - cnt
    src = loc + (jnp.arange(n_tiles, dtype=i32) * (2 * T_SORT))[:, None]
    n_blocks = (2 * n_tok) // MOE_BLOCK + N_EXPERTS
    block_start = jnp.arange(n_blocks, dtype=i32) * MOE_BLOCK
    block_e = jnp.minimum(jnp.sum(pad_end[None, :] <= block_start[:, None], axis=1), N_EXPERTS - 1).astype(i32)
    nused = (pad_end[-1] // MOE_BLOCK).astype(i32).reshape(1)
    r0 = block_start - pad_start[block_e]
    nvalid = jnp.clip(count[block_e] - r0, 0, MOE_BLOCK)
    tlo = jnp.sum(run_end[:, block_e] <= r0[None, :], axis=0)
    thi = jnp.sum(rbase[:, block_e] < (r0 + MOE_BLOCK)[None, :], axis=0)
    flat = lambda a: a.reshape(-1).astype(i32)
    return dict(cnt=flat(cnt), src=flat(src), dst=flat(dst), loc=flat(loc), rbase=flat(rbase),
                block_e=block_e, nused=nused, r0=flat(r0), nvalid=flat(nvalid), tlo=flat(tlo), thi=flat(thi),
                n_slots=n_blocks * MOE_BLOCK)


def _stage1(x, positions, attn_norm_gain, w_in, q_norm_gain, k_norm_gain, conv_w, conv_out_gain):
    bsz, seq, _ = x.shape
    n = bsz * seq
    f32, bf16 = jnp.float32, jnp.bfloat16
    w = w_in[0]
    wqk = w[:, :2 * ATTN_WIDTH].astype(bf16)
    wvt = w[:, 2 * ATTN_WIDTH:3 * ATTN_WIDTH].T.astype(bf16)
    wc = w[:, 3 * ATTN_WIDTH:].astype(bf16)
    scale = DK ** -0.5 * LOG2E
    gqk = jnp.concatenate([jnp.tile(q_norm_gain[0].astype(f32), 2 * N_HEADS) * scale,
                           jnp.tile(k_norm_gain[0].astype(f32), 2 * N_HEADS)]).reshape(1, -1)
    freqs = (ROPE_THETA ** (-jnp.arange(0, ROT_DIM, 2, dtype=f32) / ROT_DIM)).reshape(SUBLANES, 1)
    return _inproj(x.reshape(n, D_MODEL), positions.reshape(1, n),
                   attn_norm_gain[0].reshape(1, -1).astype(f32), wqk, wvt, wc, gqk, freqs,
                   conv_w[0].astype(f32), conv_out_gain[0].reshape(1, -1).astype(f32), seq)


def kernel(x, positions, attn_norm_gain, w_in, q_norm_gain, k_norm_gain, lambda_q1, lambda_k1, lambda_q2, lambda_k2, subln_gain, conv_w, conv_out_gain, w_out, ffn_norm_gain, w_group_router, w_expert_router, w_gate, w_up, w_down):
    bsz, seq, _ = x.shape
    n = bsz * seq
    f32, bf16 = jnp.float32, jnp.bfloat16
    assert TQ % T_SORT == 0 and seq % TM_IN == 0 and (seq // TQ) % 2 == 0
    q, k, vt, conv_o = _stage1(x, positions, attn_norm_gain, w_in, q_norm_gain, k_norm_gain,
                                conv_w, conv_out_gain)
    attn_p = _attention(q, k, vt,
                        lambda_q1[0].reshape(1, -1).astype(f32), lambda_k1[0].reshape(1, -1).astype(f32),
                        lambda_q2[0].reshape(1, -1).astype(f32), lambda_k2[0].reshape(1, -1).astype(f32),
                        subln_gain[0].reshape(-1, 1).astype(f32), bsz, seq)

    wr = jnp.zeros((ROUTER_ROWS, D_MODEL), f32)
    wr = wr.at[0:N_GROUPS].set(w_group_router[0].astype(f32).T)
    wr = wr.at[EXPERT_ROW0:EXPERT_ROW0 + N_EXPERTS].set(
        jnp.transpose(w_expert_router[0].astype(f32), (0, 2, 1)).reshape(N_EXPERTS, D_MODEL))
    wrh = wr.astype(bf16)
    wrl = (wr - wrh.astype(f32)).astype(bf16)
    h, xs, meta, cnt_out = _outproj(x.reshape(n, D_MODEL), attn_p, conv_o, w_out[0].astype(bf16),
                                    ffn_norm_gain[0].reshape(1, -1).astype(f32), wrh, wrl, seq)

    tabs = _routing_tables(cnt_out, n // T_SORT, n)
    yg = _experts(tabs, xs, w_gate[0], w_up[0], w_down[0])
    out = _combine(tabs["cnt"], tabs["loc"], tabs["dst"], h, meta, yg)
    return out.reshape(x.shape)
```

```python
import functools
import math

import jax
import jax.numpy as jnp
from jax import lax
from jax.experimental import pallas as pl
from jax.experimental.pallas import tpu as pltpu

D_MODEL = 1024
N_HEADS = 4
DK = 64
DV = 128
ROT_DIM = 16
ROPE_THETA = 500000.0
ATTN_WIDTH = N_HEADS * DV
CONV_WIDTH = 512
NORM_EPS = 1e-6
LOG2E = 1.4426950408889634
LAMBDA_INIT = 0.8 - 0.6 * math.exp(-0.3 * 0)
N_GROUPS = 4
EPG = 8
N_EXPERTS = N_GROUPS * EPG
D_FF = 512
MOE_BLOCK = 256

LANES = 128
SUBLANES = 8
ROW_WORDS = D_MODEL // 2
ROW_CHUNKS = ROW_WORDS // LANES
PACKED_DTYPE = jnp.uint32

TM_IN = 512
TQ = 512
HQ = TQ // 2
T_SORT = 256
VMEM_LIMIT = 48 * 1024 * 1024


def _nt_dot(a, b):
    return lax.dot_general(a, b, (((1,), (1,)), ((), ())), preferred_element_type=jnp.float32)


def _dot(a, b):
    return jnp.dot(a, b, preferred_element_type=jnp.float32)


def _split3(x):
    h = x.astype(jnp.bfloat16)
    r = x - h.astype(jnp.float32)
    m = r.astype(jnp.bfloat16)
    l = (r - m.astype(jnp.float32)).astype(jnp.bfloat16)
    return h, m, l


def _split2(x):
    h = x.astype(jnp.bfloat16)
    l = (x - h.astype(jnp.float32)).astype(jnp.bfloat16)
    return h, l


def _inproj_kernel(x_ref, pos_ref, g1_ref, wqk_ref, wvt_ref, wc_ref, gqk_ref, freq_ref,
                   cw_ref, cg_ref,
                   q_ref, k_ref, vt_ref, conv_ref,
                   carry_ref, *, tiles_per_seq):
    tm = x_ref.shape[0]
    i = pl.program_id(0)

    x = x_ref[...]
    ms = jnp.mean(x * x, axis=-1, keepdims=True)
    hn = (x * lax.rsqrt(ms + NORM_EPS) * g1_ref[...]).astype(jnp.bfloat16)

    pos = pos_ref[...].astype(jnp.float32)
    ang = freq_ref[...] * pos
    lane_r = lax.broadcasted_iota(jnp.int32, (LANES, SUBLANES), 0)
    f_c = lax.broadcasted_iota(jnp.int32, (LANES, SUBLANES), 1)
    in_rot = (lane_r % DK) < ROT_DIM
    expand = jnp.where(in_rot & ((lane_r % (ROT_DIM // 2)) == f_c), 1.0, 0.0).astype(jnp.bfloat16)

    def to_rows(t):
        h, m, l = _split3(t)
        r = _dot(expand, h) + _dot(expand, m) + _dot(expand, l)
        return r.T

    cos_r = to_rows(jnp.cos(ang))
    sin_r = to_rows(jnp.sin(ang))
    lane = lax.broadcasted_iota(jnp.int32, (tm, LANES), 1)
    d = lane % DK
    cos_r = jnp.where(d < ROT_DIM, cos_r, 1.0)
    sin_lo = jnp.where(d < ROT_DIM // 2, -sin_r, 0.0)
    sin_hi = jnp.where((d >= ROT_DIM // 2) & (d < ROT_DIM), sin_r, 0.0)

    qk = _dot(hn, wqk_ref[...])
    seg_r = lax.broadcasted_iota(jnp.int32, (2 * LANES, 2 * LANES), 0) // DK
    seg_c = lax.broadcasted_iota(jnp.int32, (2 * LANES, 2 * LANES), 1) // DK
    seg_mean = jnp.where(seg_r == seg_c, 1.0 / DK, 0.0).astype(jnp.bfloat16)
    half = ROT_DIM // 2
    for c2 in range(4):
        blk = qk[:, c2 * 256:(c2 + 1) * 256]
        sq_h, sq_l = _split2(blk * blk)
        msq = _dot(sq_h, seg_mean) + _dot(sq_l, seg_mean)
        y2 = blk * lax.rsqrt(msq + NORM_EPS) * gqk_ref[:, c2 * 256:(c2 + 1) * 256]
        for c1 in range(2):
            c = c2 * 2 + c1
            y = y2[:, c1 * LANES:(c1 + 1) * LANES]
            rot = (y * cos_r
                   + pltpu.roll(y, LANES - half, 1) * sin_lo
                   + pltpu.roll(y, half, 1) * sin_hi)
            if c < N_HEADS:
                q_ref[:, c * LANES:(c + 1) * LANES] = rot.astype(jnp.bfloat16)
            else:
                h = c - N_HEADS
                k_ref[:, h * LANES:(h + 1) * LANES] = rot.astype(jnp.bfloat16)

    vt_ref[...] = _nt_dot(wvt_ref[...], hn).astype(jnp.bfloat16)

    cp = _dot(hn, wc_ref[...])
    cb = cp[:, :CONV_WIDTH]
    y = cp[:, CONV_WIDTH:2 * CONV_WIDTH] * cp[:, 2 * CONV_WIDTH:]

    @pl.when(i % tiles_per_seq == 0)
    def _():
        carry_ref[...] = jnp.zeros_like(carry_ref)

    prev = carry_ref[...]
    row = lax.broadcasted_iota(jnp.int32, (tm, CONV_WIDTH), 0)
    p1 = prev[SUBLANES - 1:SUBLANES, :]
    p2 = prev[SUBLANES - 2:SUBLANES - 1, :]
    y1 = jnp.where(row == 0, p1, pltpu.roll(y, 1, 0))
    y2 = jnp.where(row == 0, p2, jnp.where(row == 1, p1, pltpu.roll(y, 2, 0)))
    carry_ref[...] = y[tm - SUBLANES:, :]
    z = cw_ref[0:1, :] * y2 + cw_ref[1:2, :] * y1 + cw_ref[2:3, :] * y
    co = cb * z
    cms = jnp.mean(co * co, axis=-1, keepdims=True)
    conv_ref[...] = (co * lax.rsqrt(cms + NORM_EPS) * cg_ref[...]).astype(jnp.bfloat16)


def _inproj(x2, pos_row, g1, wqk, wvt, wc, gqk, freqs, cw, cg, seq):
    n = x2.shape[0]
    tm = TM_IN
    grid = (n // tm,)
    const = lambda i: (0, 0)
    return pl.pallas_call(
        functools.partial(_inproj_kernel, tiles_per_seq=seq // tm),
        grid=grid,
        in_specs=[
            pl.BlockSpec((tm, D_MODEL), lambda i: (i, 0)),
            pl.BlockSpec((1, tm), lambda i: (0, i)),
            pl.BlockSpec((1, D_MODEL), const),
            pl.BlockSpec((D_MODEL, 1024), const),
            pl.BlockSpec((512, D_MODEL), const),
            pl.BlockSpec((D_MODEL, 1536), const),
            pl.BlockSpec((1, 1024), const),
            pl.BlockSpec((SUBLANES, 1), const),
            pl.BlockSpec((3, CONV_WIDTH), const),
            pl.BlockSpec((1, CONV_WIDTH), const),
        ],
        out_specs=[
            pl.BlockSpec((tm, ATTN_WIDTH), lambda i: (i, 0)),
            pl.BlockSpec((tm, ATTN_WIDTH), lambda i: (i, 0)),
            pl.BlockSpec((ATTN_WIDTH, tm), lambda i: (0, i)),
            pl.BlockSpec((tm, CONV_WIDTH), lambda i: (i, 0)),
        ],
        out_shape=[
            jax.ShapeDtypeStruct((n, ATTN_WIDTH), jnp.bfloat16),
            jax.ShapeDtypeStruct((n, ATTN_WIDTH), jnp.bfloat16),
            jax.ShapeDtypeStruct((ATTN_WIDTH, n), jnp.bfloat16),
            jax.ShapeDtypeStruct((n, CONV_WIDTH), jnp.bfloat16),
        ],
        scratch_shapes=[pltpu.VMEM((SUBLANES, CONV_WIDTH), jnp.float32)],
        compiler_params=pltpu.CompilerParams(
            dimension_semantics=("arbitrary",), vmem_limit_bytes=VMEM_LIMIT),
        name="inproj",
    )(x2, pos_row, g1, wqk, wvt, wc, gqk, freqs, cw, cg)


def _attn_kernel(qa_ref, qb_ref, k_ref, vt_ref, lq1_ref, lk1_ref, lq2_ref, lk2_ref, sg_ref,
                 o_ref, acc_ref, *, nq):
    c = pl.program_id(2)
    lam = (jnp.exp(jnp.sum(lq1_ref[...] * lk1_ref[...], axis=-1, keepdims=True))
           - jnp.exp(jnp.sum(lq2_ref[...] * lk2_ref[...], axis=-1, keepdims=True))
           + LAMBDA_INIT)
    key_i = lax.broadcasted_iota(jnp.int32, (HQ, 4 * HQ), 0)
    col_i = lax.broadcasted_iota(jnp.int32, (HQ, 4 * HQ), 1)
    tri = key_i <= col_i % HQ
    mask_mixed = tri | (col_i >= 2 * HQ)
    mask_upper = tri[:, :2 * HQ]
    lane_q = lax.broadcasted_iota(jnp.int32, (HQ, LANES), 1)

    def branch(tiles):
        qz = []
        for q_ref, _ in tiles:
            parts = []
            for half in range(2):
                qh = q_ref[half * HQ:(half + 1) * HQ, :]
                parts.append(jnp.where(lane_q < DK, qh, jnp.zeros_like(qh)))
                parts.append(jnp.where(lane_q >= DK, qh, jnp.zeros_like(qh)))
            qz.append(jnp.concatenate(parts, axis=0))

        todo = {}
        for ti, (_, t) in enumerate(tiles):
            nb = 2 * t
            todo[ti] = [(nb, "mixed"), (nb + 1, "upper")] + [(j, "full") for j in range(nb)]
        tasks = []
        while any(todo.values()):
            for ti in range(len(tiles)):
                if todo[ti]:
                    tasks.append((ti,) + todo[ti].pop(0))
        m_run, l_run = {}, {}

        def scores(task):
            ti, j, kind = task
            kb = k_ref[j * HQ:(j + 1) * HQ, :]
            if kind == "upper":
                return jnp.where(mask_upper, _nt_dot(kb, qz[ti][2 * HQ:, :]), -jnp.inf)
            s = _nt_dot(kb, qz[ti])
            return jnp.where(mask_mixed, s, -jnp.inf) if kind == "mixed" else s

        def softmax(task, s):
            ti, j, kind = task
            mx = jnp.max(s, axis=0, keepdims=True)
            if kind == "mixed":
                p = jnp.exp2(s - mx)
                m_run[ti], l_run[ti] = mx, jnp.sum(p, axis=0, keepdims=True)
                return p.astype(jnp.bfloat16), None
            m_old = m_run[ti][:, 2 * HQ:] if kind == "upper" else m_run[ti]
            l_old = l_run[ti][:, 2 * HQ:] if kind == "upper" else l_run[ti]
            m_new = jnp.maximum(m_old, mx)
            alpha = jnp.exp2(m_old - m_new)
            p = jnp.exp2(s - m_new)
            l_new = alpha * l_old + jnp.sum(p, axis=0, keepdims=True)
            if kind == "upper":
                m_new = jnp.concatenate([m_run[ti][:, :2 * HQ], m_new], axis=1)
                l_new = jnp.concatenate([l_run[ti][:, :2 * HQ], l_new], axis=1)
            m_run[ti], l_run[ti] = m_new, l_new
            return p.astype(jnp.bfloat16), alpha

        def accumulate(task, p, alpha):
            ti, j, kind = task
            pv = _dot(vt_ref[:, j * HQ:(j + 1) * HQ], p)
            if kind == "mixed":
                acc_ref[ti] = pv
            elif kind == "upper":
                acc_ref[ti, :, 2 * HQ:] = alpha * acc_ref[ti, :, 2 * HQ:] + pv
            else:
                acc_ref[ti] = alpha * acc_ref[ti] + pv

        n = len(tasks)
        s_prev, p_prev = None, None
        for step in range(n + 2):
            if step >= 2:
                accumulate(tasks[step - 2], *p_prev)
            if 1 <= step <= n:
                p_prev = softmax(tasks[step - 1], s_prev)
            if step < n:
                s_prev = scores(tasks[step])

        for ti in range(len(tiles)):
            o_all = acc_ref[ti] / l_run[ti]
            for half in range(2):
                o = (o_all[:, (2 * half) * HQ:(2 * half + 1) * HQ]
                     - lam * o_all[:, (2 * half + 1) * HQ:(2 * half + 2) * HQ])
                ms = jnp.mean(o * o, axis=0, keepdims=True)
                on = o * lax.rsqrt(ms + NORM_EPS) * sg_ref[...] * (1.0 - LAMBDA_INIT)
                r0 = ti * TQ + half * HQ
                o_ref[r0:r0 + HQ, :] = on.T.astype(jnp.bfloat16)

    for cc in range(nq // 2):
        @pl.when(c == cc)
        def _():
            branch([(qa_ref, cc), (qb_ref, nq - 1 - cc)])


def _attention(q, k, vt, lq1, lk1, lq2, lk2, sg_col, bsz, seq):
    n = q.shape[0]
    nq = seq // TQ
    assert nq % 2 == 0
    vec = lambda b, h, c: (0, 0)
    return pl.pallas_call(
        functools.partial(_attn_kernel, nq=nq),
        grid=(bsz, N_HEADS, nq // 2),
        in_specs=[
            pl.BlockSpec((TQ, DV), lambda b, h, c: (b * nq + c, h)),
            pl.BlockSpec((TQ, DV), lambda b, h, c: (b * nq + nq - 1 - c, h)),
            pl.BlockSpec((seq, DV), lambda b, h, c: (b, h)),
            pl.BlockSpec((DV, seq), lambda b, h, c: (h, b)),
            pl.BlockSpec((1, DK), vec), pl.BlockSpec((1, DK), vec),
            pl.BlockSpec((1, DK), vec), pl.BlockSpec((1, DK), vec),
            pl.BlockSpec((DV, 1), vec),
        ],
        out_specs=pl.BlockSpec((2 * TQ, DV), lambda b, h, c: (b * (nq // 2) + c, h)),
        out_shape=jax.ShapeDtypeStruct((n, ATTN_WIDTH), jnp.bfloat16),
        scratch_shapes=[pltpu.VMEM((2, DV, 4 * HQ), jnp.float32)],
        compiler_params=pltpu.CompilerParams(
            dimension_semantics=("arbitrary", "arbitrary", "arbitrary"), vmem_limit_bytes=VMEM_LIMIT),
        name="attn",
    )(q, q, k, vt, lq1, lk1, lq2, lk2, sg_col)


def _attn_out_block(i, seq):
    per_seq = seq // T_SORT
    nq = seq // TQ
    b, r = i // per_seq, i % per_seq
    t, sub = r // (TQ // T_SORT), r % (TQ // T_SORT)
    pos = jnp.where(t < nq // 2, 2 * t, 2 * (nq - 1 - t) + 1)
    return b * per_seq + pos * (TQ // T_SORT) + sub


ROUTER_ROWS = 128
EXPERT_ROW0 = 32


def _outproj_kernel(x_ref, attn_ref, conv_ref, wo_ref, g2_ref, wrh_ref, wrl_ref,
                    h_ref, xs_ref, meta_ref, cnt_ref):
    t = x_ref.shape[0]
    f32, bf16 = jnp.float32, jnp.bfloat16
    a = jnp.concatenate([attn_ref[...], conv_ref[...]], axis=1)
    h = x_ref[...] + _dot(a, wo_ref[...])
    h_ref[...] = h
    ms = jnp.mean(h * h, axis=-1, keepdims=True)
    hn = h * lax.rsqrt(ms + NORM_EPS) * g2_ref[...]
    hn_hi, hn_lo = _split2(hn)

    lt = _nt_dot(wrh_ref[...], hn_hi) + _nt_dot(wrh_ref[...], hn_lo) + _nt_dot(wrl_ref[...], hn_hi)
    row8 = lax.broadcasted_iota(jnp.int32, (SUBLANES, t), 0).astype(f32)
    neg_inf = -jnp.inf

    def first_argmax(v):
        mx = jnp.max(v, axis=0, keepdims=True)
        idx = jnp.min(jnp.where(v == mx, row8, float(SUBLANES)), axis=0, keepdims=True)
        return mx, idx

    g_log = jnp.where(row8 < N_GROUPS, lt[0:SUBLANES, :], neg_inf)
    g_max, g_sel = first_argmax(g_log)
    g_gate = 1.0 / jnp.sum(jnp.exp(g_log - g_max), axis=0, keepdims=True)
    e_log = jnp.zeros((EPG, t), f32)
    for g in range(N_GROUPS):
        rows = lt[EXPERT_ROW0 + g * EPG:EXPERT_ROW0 + (g + 1) * EPG, :]
        e_log = jnp.where(g_sel == float(g), rows, e_log)
    v1, i1 = first_argmax(e_log)
    v2, i2 = first_argmax(jnp.where(row8 == i1, neg_inf, e_log))
    tt = jnp.exp(v2 - v1)
    w1 = g_gate / (1.0 + tt)
    w2 = g_gate * tt / (1.0 + tt)
    e1 = g_sel * float(EPG) + i1
    e2 = g_sel * float(EPG) + i2

    row32 = lax.broadcasted_iota(jnp.int32, (N_EXPERTS, t), 0).astype(f32)
    oh1 = row32 == e1
    oh2 = row32 == e2
    c = jnp.where(oh1 | oh2, 1.0, 0.0)
    tok_r = lax.broadcasted_iota(jnp.int32, (t, t), 0)
    tok_c = lax.broadcasted_iota(jnp.int32, (t, t), 1)
    earlier = jnp.where(tok_r < tok_c, 1.0, 0.0).astype(bf16)
    rank = _dot(c.astype(bf16), earlier)
    cnt_b = jnp.broadcast_to(jnp.sum(c, axis=1, keepdims=True), (N_EXPERTS, t))
    ex_r = lax.broadcasted_iota(jnp.int32, (N_EXPERTS, N_EXPERTS), 0)
    ex_c = lax.broadcasted_iota(jnp.int32, (N_EXPERTS, N_EXPERTS), 1)
    lower = jnp.where(ex_c < ex_r, 1.0, 0.0).astype(bf16)
    start_b = _dot(lower, cnt_b.astype(bf16))
    pos_e = start_b + rank
    p1 = jnp.sum(jnp.where(oh1, pos_e, 0.0), axis=0, keepdims=True)
    p2 = jnp.sum(jnp.where(oh2, pos_e, 0.0), axis=0, keepdims=True)

    srow = lax.broadcasted_iota(jnp.int32, (2 * t, t), 0).astype(f32)
    perm = jnp.where((srow == p1) | (srow == p2), 1.0, 0.0).astype(bf16)
    xs = _dot(perm, hn_hi)
    xs_ref[...] = _pack_rows(xs)

    meta = jnp.concatenate([p1, p2, w1, w2, jnp.zeros((LANES - 4, t), f32)], axis=0)
    meta_ref[...] = meta.T
    cnt_ref[...] = cnt_b[:, :LANES]


def _outproj(x2, attn_p, conv_o, wo, g2, wrh, wrl, seq):
    n = x2.shape[0]
    t = T_SORT
    nt = n // t
    const = lambda i: (0, 0)

    def attn_map(i):
        return (_attn_out_block(i, seq), 0)

    return pl.pallas_call(
        _outproj_kernel,
        grid=(nt,),
        in_specs=[
            pl.BlockSpec((t, D_MODEL), lambda i: (i, 0)),
            pl.BlockSpec((t, ATTN_WIDTH), attn_map),
            pl.BlockSpec((t, CONV_WIDTH), lambda i: (i, 0)),
            pl.BlockSpec((D_MODEL, D_MODEL), const),
            pl.BlockSpec((1, D_MODEL), const),
            pl.BlockSpec((ROUTER_ROWS, D_MODEL), const),
            pl.BlockSpec((ROUTER_ROWS, D_MODEL), const),
        ],
        out_specs=[
            pl.BlockSpec((t, D_MODEL), lambda i: (i, 0)),
            pl.BlockSpec((2 * t, ROW_CHUNKS, LANES), lambda i: (i, 0, 0)),
            pl.BlockSpec((t, LANES), lambda i: (i, 0)),
            pl.BlockSpec((N_EXPERTS, LANES), lambda i: (i, 0)),
        ],
        out_shape=[
            jax.ShapeDtypeStruct((n, D_MODEL), jnp.float32),
            jax.ShapeDtypeStruct((2 * n, ROW_CHUNKS, LANES), PACKED_DTYPE),
            jax.ShapeDtypeStruct((n, LANES), jnp.float32),
            jax.ShapeDtypeStruct((nt * N_EXPERTS, LANES), jnp.float32),
        ],
        compiler_params=pltpu.CompilerParams(
            dimension_semantics=("arbitrary",), vmem_limit_bytes=VMEM_LIMIT),
        name="outproj",
    )(x2, attn_p, conv_o, wo, g2, wrh, wrl)


def _pack_rows(x):
    r = x.shape[0]
    w = pltpu.pack_elementwise([x[:, :ROW_WORDS], x[:, ROW_WORDS:]], packed_dtype=jnp.bfloat16)
    return pltpu.bitcast(w, PACKED_DTYPE).reshape(r, ROW_CHUNKS, LANES)


def _packed_zero_rows(r):
    z = jnp.zeros((r, ROW_CHUNKS, LANES), jnp.float32)
    w = pltpu.pack_elementwise([z, z], packed_dtype=jnp.bfloat16)
    return pltpu.bitcast(w, PACKED_DTYPE)


def _unpack_rows(u):
    r = u.shape[0]
    w = u.reshape(r, ROW_WORDS)
    lo = pltpu.unpack_elementwise(w, index=0, packed_dtype=jnp.bfloat16, unpacked_dtype=jnp.float32)
    hi = pltpu.unpack_elementwise(w, index=1, packed_dtype=jnp.bfloat16, unpacked_dtype=jnp.float32)
    return jnp.concatenate([lo, hi], axis=1).astype(jnp.bfloat16)


def _expert_kernel(be_ref, nused_ref, r0_ref, nvalid_ref, tlo_ref, thi_ref, cnt_ref, src_ref, rbase_ref,
                   nxt_ref, xs_hbm, wg_hbm, wu_hbm, wd_hbm, y_ref,
                   xbuf, wg_f, wu_f, wd_f, wg_s, wu_s, wd_s, wslot_ref, sem, wsem):
    b = pl.program_id(0)
    nused = nused_ref[0]

    def gather(blk, slot):
        e = be_ref[blk]
        r0 = r0_ref[blk]

        @pl.when(nvalid_ref[blk] < MOE_BLOCK)
        def _():
            xbuf[slot] = _packed_zero_rows(MOE_BLOCK)

        def run(t, carry):
            k = t * N_EXPERTS + e
            lo = jnp.maximum(rbase_ref[k], r0)
            hi = jnp.minimum(rbase_ref[k] + cnt_ref[k], r0 + MOE_BLOCK)

            @pl.when(hi > lo)
            def _():
                pltpu.make_async_copy(xs_hbm.at[pl.ds(src_ref[k] + lo - rbase_ref[k], hi - lo)],
                                      xbuf.at[slot, pl.ds(lo - r0, hi - lo)], sem.at[slot]).start()
            return carry
        lax.fori_loop(tlo_ref[blk], thi_ref[blk], run, 0)

    def weight_copies(e, slot):
        return [pltpu.make_async_copy(src.at[e], dst.at[slot], wsem.at[slot])
                for src, dst in ((wg_hbm, wg_f), (wu_hbm, wu_f), (wd_hbm, wd_f))]

    @pl.when(b == 0)
    def _():
        gather(0, 0)
        wslot_ref[0] = 0
        for cp in weight_copies(be_ref[0], 0):
            cp.start()

    @pl.when(b + 1 < nused)
    def _():
        gather(b + 1, (b + 1) % 2)

    @pl.when(b < nused)
    def _():
        e = be_ref[b]

        @pl.when((b == 0) | (e != be_ref[jnp.maximum(b - 1, 0)]))
        def _():
            ws = wslot_ref[0]
            for cp in weight_copies(e, ws):
                cp.wait()
            wg_s[...] = wg_f[ws].astype(jnp.bfloat16)
            wu_s[...] = wu_f[ws].astype(jnp.bfloat16)
            wd_s[...] = wd_f[ws].astype(jnp.bfloat16)
            wslot_ref[0] = 1 - ws

            @pl.when(nxt_ref[e] >= 0)
            def _():
                for cp in weight_copies(nxt_ref[e], 1 - ws):
                    cp.start()

        slot = b % 2
        nv = nvalid_ref[b]
        pltpu.make_async_copy(xs_hbm.at[pl.ds(0, nv)], xbuf.at[slot, pl.ds(0, nv)], sem.at[slot]).wait()

        x = _unpack_rows(xbuf[slot])
        g = _dot(x, wg_s[...])
        u = _dot(x, wu_s[...])
        act = (g / (1.0 + jnp.exp(-g)) * u).astype(jnp.bfloat16)
        y = _dot(act, wd_s[...])
        y_ref[...] = _pack_rows(y)

    @pl.when(b >= nused)
    def _():
        y_ref[...] = _packed_zero_rows(MOE_BLOCK)


def _experts(tabs, xs, w_gate, w_up, w_down):
    n_slots = tabs["n_slots"]
    n_blocks = n_slots // MOE_BLOCK
    f32, bf16 = jnp.float32, jnp.bfloat16
    hbm = pl.BlockSpec(memory_space=pl.ANY)
    return pl.pallas_call(
        _expert_kernel,
        grid_spec=pltpu.PrefetchScalarGridSpec(
            num_scalar_prefetch=10,
            grid=(n_blocks,),
            in_specs=[hbm, hbm, hbm, hbm],
            out_specs=pl.BlockSpec((MOE_BLOCK, ROW_CHUNKS, LANES), lambda b, *_: (b, 0, 0)),
            scratch_shapes=[pltpu.VMEM((2, MOE_BLOCK, ROW_CHUNKS, LANES), PACKED_DTYPE),
                            pltpu.VMEM((2, D_MODEL, D_FF), f32),
                            pltpu.VMEM((2, D_MODEL, D_FF), f32),
                            pltpu.VMEM((2, D_FF, D_MODEL), f32),
                            pltpu.VMEM((D_MODEL, D_FF), bf16),
                            pltpu.VMEM((D_MODEL, D_FF), bf16),
                            pltpu.VMEM((D_FF, D_MODEL), bf16),
                            pltpu.SMEM((1,), jnp.int32),
                            pltpu.SemaphoreType.DMA((2,)),
                            pltpu.SemaphoreType.DMA((2,))],
        ),
        out_shape=jax.ShapeDtypeStruct((n_slots, ROW_CHUNKS, LANES), PACKED_DTYPE),
        compiler_params=pltpu.CompilerParams(
            dimension_semantics=("arbitrary",), vmem_limit_bytes=VMEM_LIMIT),
        name="experts",
    )(tabs["block_e"], tabs["nused"], tabs["r0"], tabs["nvalid"], tabs["tlo"], tabs["thi"],
      tabs["cnt"], tabs["src"], tabs["rbase"], tabs["nxt"], xs, w_gate, w_up, w_down)


def _combine_kernel(cnt_ref, loc_ref, dst_ref, h_ref, meta_ref, yg_hbm, o_ref, ybuf, sem, *, n_tiles):
    i = pl.program_id(0)
    t = h_ref.shape[0]
    f32, bf16 = jnp.float32, jnp.bfloat16

    def issue(tile, slot):
        for e in range(N_EXPERTS):
            k = tile * N_EXPERTS + e
            @pl.when(cnt_ref[k] > 0)
            def _():
                pltpu.make_async_copy(yg_hbm.at[pl.ds(dst_ref[k], cnt_ref[k])],
                                      ybuf.at[slot, pl.ds(loc_ref[k], cnt_ref[k])],
                                      sem.at[slot]).start()

    @pl.when(i == 0)
    def _():
        issue(0, 0)

    @pl.when(i + 1 < n_tiles)
    def _():
        issue(i + 1, (i + 1) % 2)

    slot = i % 2
    pltpu.make_async_copy(yg_hbm.at[pl.ds(0, 2 * t)], ybuf.at[slot], sem.at[slot]).wait()

    y = _unpack_rows(ybuf[slot])
    lane = lax.broadcasted_iota(jnp.int32, (t, 2 * t), 1).astype(f32)
    meta = meta_ref[...]
    pick1 = jnp.where(lane == meta[:, 0:1], 1.0, 0.0).astype(bf16)
    pick2 = jnp.where(lane == meta[:, 1:2], 1.0, 0.0).astype(bf16)
    y1 = _dot(pick1, y)
    y2 = _dot(pick2, y)
    o_ref[...] = h_ref[...] + (meta[:, 2:3] * y1 + meta[:, 3:4] * y2)


def _combine(cnt, loc, dst, h, meta, yg):
    n = h.shape[0]
    t = T_SORT
    n_tiles = n // t
    return pl.pallas_call(
        functools.partial(_combine_kernel, n_tiles=n_tiles),
        grid_spec=pltpu.PrefetchScalarGridSpec(
            num_scalar_prefetch=3,
            grid=(n_tiles,),
            in_specs=[
                pl.BlockSpec((t, D_MODEL), lambda i, *_: (i, 0)),
                pl.BlockSpec((t, LANES), lambda i, *_: (i, 0)),
                pl.BlockSpec(memory_space=pl.ANY),
            ],
            out_specs=pl.BlockSpec((t, D_MODEL), lambda i, *_: (i, 0)),
            scratch_shapes=[pltpu.VMEM((2, 2 * t, ROW_CHUNKS, LANES), PACKED_DTYPE),
                            pltpu.SemaphoreType.DMA((2,))],
        ),
        out_shape=jax.ShapeDtypeStruct((n, D_MODEL), jnp.float32),
        compiler_params=pltpu.CompilerParams(
            dimension_semantics=("arbitrary",), vmem_limit_bytes=VMEM_LIMIT),
        name="combine",
    )(cnt, loc, dst, h, meta, yg)


def _routing_tables(cnt_out, n_tiles, n_tok):
    i32 = jnp.int32
    cnt = cnt_out.reshape(n_tiles, N_EXPERTS, LANES)[:, :, 0].astype(i32)
    count = jnp.sum(cnt, axis=0)
    padded = ((count + MOE_BLOCK - 1) // MOE_BLOCK) * MOE_BLOCK
    pad_end = jnp.cumsum(padded)
    pad_start = pad_end - padded
    run_end = jnp.cumsum(cnt, axis=0)
    rbase = run_end - cnt
    dst = pad_start[None, :] + rbase
    loc = jnp.cumsum(cnt, axis=1) - cnt
    src = loc + (jnp.arange(n_tiles, dtype=i32) * (2 * T_SORT))[:, None]
    n_blocks = (2 * n_tok) // MOE_BLOCK + N_EXPERTS
    block_start = jnp.arange(n_blocks, dtype=i32) * MOE_BLOCK
    block_e = jnp.minimum(jnp.sum(pad_end[None, :] <= block_start[:, None], axis=1), N_EXPERTS - 1).astype(i32)
    nused = (pad_end[-1] // MOE_BLOCK).astype(i32).reshape(1)
    ex = jnp.arange(N_EXPERTS, dtype=i32)
    sel = (block_e[None, :] == ex[:, None]).astype(i32)
    pick = lambda per_expert: jnp.sum(per_expert[..., :, None] * sel, axis=-2)
    r0 = block_start - pick(pad_start)
    nvalid = jnp.clip(pick(count) - r0, 0, MOE_BLOCK)
    tlo = jnp.sum(pick(run_end) <= r0[None, :], axis=0)
    thi = jnp.sum(pick(rbase) < (r0 + MOE_BLOCK)[None, :], axis=0)
    later_used = (ex[None, :] > ex[:, None]) & (count[None, :] > 0)
    nxt = jnp.min(jnp.where(later_used, ex[None, :], N_EXPERTS), axis=1)
    nxt = jnp.where(nxt < N_EXPERTS, nxt, -1)
    flat = lambda a: a.reshape(-1).astype(i32)
    return dict(cnt=flat(cnt), src=flat(src), dst=flat(dst), loc=flat(loc), rbase=flat(rbase),
                block_e=block_e, nused=nused, r0=flat(r0), nvalid=flat(nvalid), tlo=flat(tlo), thi=flat(thi),
                nxt=flat(nxt), n_slots=n_blocks * MOE_BLOCK)


def _stage1(x, positions, attn_norm_gain, w_in, q_norm_gain, k_norm_gain, conv_w, conv_out_gain):
    bsz, seq, _ = x.shape
    n = bsz * seq
    f32, bf16 = jnp.float32, jnp.bfloat16
    w = w_in[0]
    wqk = w[:, :2 * ATTN_WIDTH].astype(bf16)
    wvt = w[:, 2 * ATTN_WIDTH:3 * ATTN_WIDTH].T.astype(bf16)
    wc = w[:, 3 * ATTN_WIDTH:].astype(bf16)
    scale = DK ** -0.5 * LOG2E
    gqk = jnp.concatenate([jnp.tile(q_norm_gain[0].astype(f32), 2 * N_HEADS) * scale,
                           jnp.tile(k_norm_gain[0].astype(f32), 2 * N_HEADS)]).reshape(1, -1)
    freqs = (ROPE_THETA ** (-jnp.arange(0, ROT_DIM, 2, dtype=f32) / ROT_DIM)).reshape(SUBLANES, 1)
    return _inproj(x.reshape(n, D_MODEL), positions.reshape(1, n),
                   attn_norm_gain[0].reshape(1, -1).astype(f32), wqk, wvt, wc, gqk, freqs,
                   conv_w[0].astype(f32), conv_out_gain[0].reshape(1, -1).astype(f32), seq)


def kernel(x, positions, attn_norm_gain, w_in, q_norm_gain, k_norm_gain, lambda_q1, lambda_k1, lambda_q2, lambda_k2, subln_gain, conv_w, conv_out_gain, w_out, ffn_norm_gain, w_group_router, w_expert_router, w_gate, w_up, w_down):
    bsz, seq, _ = x.shape
    n = bsz * seq
    f32, bf16 = jnp.float32, jnp.bfloat16
    assert TQ % T_SORT == 0 and seq % TM_IN == 0 and (seq // TQ) % 2 == 0
    q, k, vt, conv_o = _stage1(x, positions, attn_norm_gain, w_in, q_norm_gain, k_norm_gain,
                                conv_w, conv_out_gain)
    attn_p = _attention(q, k, vt,
                        lambda_q1[0].reshape(1, -1).astype(f32), lambda_k1[0].reshape(1, -1).astype(f32),
                        lambda_q2[0].reshape(1, -1).astype(f32), lambda_k2[0].reshape(1, -1).astype(f32),
                        subln_gain[0].reshape(-1, 1).astype(f32), bsz, seq)

    wr = jnp.zeros((ROUTER_ROWS, D_MODEL), f32)
    wr = wr.at[0:N_GROUPS].set(w_group_router[0].astype(f32).T)
    wr = wr.at[EXPERT_ROW0:EXPERT_ROW0 + N_EXPERTS].set(
        jnp.transpose(w_expert_router[0].astype(f32), (0, 2, 1)).reshape(N_EXPERTS, D_MODEL))
    wrh = wr.astype(bf16)
    wrl = (wr - wrh.astype(f32)).astype(bf16)
    h, xs, meta, cnt_out = _outproj(x.reshape(n, D_MODEL), attn_p, conv_o, w_out[0].astype(bf16),
                                    ffn_norm_gain[0].reshape(1, -1).astype(f32), wrh, wrl, seq)

    tabs = _routing_tables(cnt_out, n // T_SORT, n)
    yg = _experts(tabs, xs, w_gate[0], w_up[0], w_down[0])
    out = _combine(tabs["cnt"], tabs["loc"], tabs["dst"], h, meta, yg)
    return out.reshape(x.shape)
```

```python
import functools
import math

import jax
import jax.numpy as jnp
from jax import lax
from jax.experimental import pallas as pl
from jax.experimental.pallas import tpu as pltpu

D_MODEL = 1024
N_HEADS = 4
DK = 64
DV = 128
ROT_DIM = 16
ROPE_THETA = 500000.0
ATTN_WIDTH = N_HEADS * DV
CONV_WIDTH = 512
NORM_EPS = 1e-6
LOG2E = 1.4426950408889634
LAMBDA_INIT = 0.8 - 0.6 * math.exp(-0.3 * 0)
N_GROUPS = 4
EPG = 8
N_EXPERTS = N_GROUPS * EPG
D_FF = 512
MOE_BLOCK = 256

LANES = 128
SUBLANES = 8
ROW_WORDS = D_MODEL // 2
ROW_CHUNKS = ROW_WORDS // LANES
PACKED_DTYPE = jnp.uint32

TM_IN = 512
TQ = 512
HQ = TQ // 2
T_SORT = 256
VMEM_LIMIT = 48 * 1024 * 1024


def _nt_dot(a, b):
    return lax.dot_general(a, b, (((1,), (1,)), ((), ())), preferred_element_type=jnp.float32)


def _dot(a, b):
    return jnp.dot(a, b, preferred_element_type=jnp.float32)


def _split3(x):
    h = x.astype(jnp.bfloat16)
    r = x - h.astype(jnp.float32)
    m = r.astype(jnp.bfloat16)
    l = (r - m.astype(jnp.float32)).astype(jnp.bfloat16)
    return h, m, l


def _split2(x):
    h = x.astype(jnp.bfloat16)
    l = (x - h.astype(jnp.float32)).astype(jnp.bfloat16)
    return h, l


def _inproj_kernel(x_ref, pos_ref, g1_ref, wqk_ref, wvt_ref, wc_ref, gqk_ref, freq_ref,
                   cw_ref, cg_ref,
                   q_ref, k_ref, vt_ref, conv_ref,
                   carry_ref, *, tiles_per_seq):
    tm = x_ref.shape[0]
    i = pl.program_id(0)

    x = x_ref[...]
    ms = jnp.mean(x * x, axis=-1, keepdims=True)
    hn = (x * lax.rsqrt(ms + NORM_EPS) * g1_ref[...]).astype(jnp.bfloat16)

    pos = pos_ref[...].astype(jnp.float32)
    ang = freq_ref[...] * pos
    lane_r = lax.broadcasted_iota(jnp.int32, (LANES, SUBLANES), 0)
    f_c = lax.broadcasted_iota(jnp.int32, (LANES, SUBLANES), 1)
    in_rot = (lane_r % DK) < ROT_DIM
    expand = jnp.where(in_rot & ((lane_r % (ROT_DIM // 2)) == f_c), 1.0, 0.0).astype(jnp.bfloat16)

    def to_rows(t):
        h, m, l = _split3(t)
        r = _dot(expand, h) + _dot(expand, m) + _dot(expand, l)
        return r.T

    cos_r = to_rows(jnp.cos(ang))
    sin_r = to_rows(jnp.sin(ang))
    lane = lax.broadcasted_iota(jnp.int32, (tm, LANES), 1)
    d = lane % DK
    cos_r = jnp.where(d < ROT_DIM, cos_r, 1.0)
    sin_lo = jnp.where(d < ROT_DIM // 2, -sin_r, 0.0)
    sin_hi = jnp.where((d >= ROT_DIM // 2) & (d < ROT_DIM), sin_r, 0.0)

    qk = _dot(hn, wqk_ref[...])
    seg_r = lax.broadcasted_iota(jnp.int32, (2 * LANES, 2 * LANES), 0) // DK
    seg_c = lax.broadcasted_iota(jnp.int32, (2 * LANES, 2 * LANES), 1) // DK
    seg_mean = jnp.where(seg_r == seg_c, 1.0 / DK, 0.0).astype(jnp.bfloat16)
    half = ROT_DIM // 2
    for c2 in range(4):
        blk = qk[:, c2 * 256:(c2 + 1) * 256]
        sq_h, sq_l = _split2(blk * blk)
        msq = _dot(sq_h, seg_mean) + _dot(sq_l, seg_mean)
        y2 = blk * lax.rsqrt(msq + NORM_EPS) * gqk_ref[:, c2 * 256:(c2 + 1) * 256]
        for c1 in range(2):
            c = c2 * 2 + c1
            y = y2[:, c1 * LANES:(c1 + 1) * LANES]
            rot = (y * cos_r
                   + pltpu.roll(y, LANES - half, 1) * sin_lo
                   + pltpu.roll(y, half, 1) * sin_hi)
            if c < N_HEADS:
                q_ref[:, c * LANES:(c + 1) * LANES] = rot.astype(jnp.bfloat16)
            else:
                h = c - N_HEADS
                k_ref[:, h * LANES:(h + 1) * LANES] = rot.astype(jnp.bfloat16)

    vt_ref[...] = _nt_dot(wvt_ref[...], hn).astype(jnp.bfloat16)

    cp = _dot(hn, wc_ref[...])
    cb = cp[:, :CONV_WIDTH]
    y = cp[:, CONV_WIDTH:2 * CONV_WIDTH] * cp[:, 2 * CONV_WIDTH:]

    @pl.when(i % tiles_per_seq == 0)
    def _():
        carry_ref[...] = jnp.zeros_like(carry_ref)

    prev = carry_ref[...]
    row = lax.broadcasted_iota(jnp.int32, (tm, CONV_WIDTH), 0)
    p1 = prev[SUBLANES - 1:SUBLANES, :]
    p2 = prev[SUBLANES - 2:SUBLANES - 1, :]
    y1 = jnp.where(row == 0, p1, pltpu.roll(y, 1, 0))
    y2 = jnp.where(row == 0, p2, jnp.where(row == 1, p1, pltpu.roll(y, 2, 0)))
    carry_ref[...] = y[tm - SUBLANES:, :]
    z = cw_ref[0:1, :] * y2 + cw_ref[1:2, :] * y1 + cw_ref[2:3, :] * y
    co = cb * z
    cms = jnp.mean(co * co, axis=-1, keepdims=True)
    conv_ref[...] = (co * lax.rsqrt(cms + NORM_EPS) * cg_ref[...]).astype(jnp.bfloat16)


def _inproj(x2, pos_row, g1, wqk, wvt, wc, gqk, freqs, cw, cg, seq):
    n = x2.shape[0]
    tm = TM_IN
    grid = (n // tm,)
    const = lambda i: (0, 0)
    return pl.pallas_call(
        functools.partial(_inproj_kernel, tiles_per_seq=seq // tm),
        grid=grid,
        in_specs=[
            pl.BlockSpec((tm, D_MODEL), lambda i: (i, 0)),
            pl.BlockSpec((1, tm), lambda i: (0, i)),
            pl.BlockSpec((1, D_MODEL), const),
            pl.BlockSpec((D_MODEL, 1024), const),
            pl.BlockSpec((512, D_MODEL), const),
            pl.BlockSpec((D_MODEL, 1536), const),
            pl.BlockSpec((1, 1024), const),
            pl.BlockSpec((SUBLANES, 1), const),
            pl.BlockSpec((3, CONV_WIDTH), const),
            pl.BlockSpec((1, CONV_WIDTH), const),
        ],
        out_specs=[
            pl.BlockSpec((tm, ATTN_WIDTH), lambda i: (i, 0)),
            pl.BlockSpec((tm, ATTN_WIDTH), lambda i: (i, 0)),
            pl.BlockSpec((ATTN_WIDTH, tm), lambda i: (0, i)),
            pl.BlockSpec((tm, CONV_WIDTH), lambda i: (i, 0)),
        ],
        out_shape=[
            jax.ShapeDtypeStruct((n, ATTN_WIDTH), jnp.bfloat16),
            jax.ShapeDtypeStruct((n, ATTN_WIDTH), jnp.bfloat16),
            jax.ShapeDtypeStruct((ATTN_WIDTH, n), jnp.bfloat16),
            jax.ShapeDtypeStruct((n, CONV_WIDTH), jnp.bfloat16),
        ],
        scratch_shapes=[pltpu.VMEM((SUBLANES, CONV_WIDTH), jnp.float32)],
        compiler_params=pltpu.CompilerParams(
            dimension_semantics=("arbitrary",), vmem_limit_bytes=VMEM_LIMIT),
        name="inproj",
    )(x2, pos_row, g1, wqk, wvt, wc, gqk, freqs, cw, cg)


def _attn_kernel(qa_ref, qb_ref, k_ref, vt_ref, lq1_ref, lk1_ref, lq2_ref, lk2_ref, sg_ref,
                 o_ref, acc_ref, *, nq):
    c = pl.program_id(2)
    lam = (jnp.exp(jnp.sum(lq1_ref[...] * lk1_ref[...], axis=-1, keepdims=True))
           - jnp.exp(jnp.sum(lq2_ref[...] * lk2_ref[...], axis=-1, keepdims=True))
           + LAMBDA_INIT)
    key_i = lax.broadcasted_iota(jnp.int32, (HQ, 4 * HQ), 0)
    col_i = lax.broadcasted_iota(jnp.int32, (HQ, 4 * HQ), 1)
    tri = key_i <= col_i % HQ
    mask_mixed = tri | (col_i >= 2 * HQ)
    mask_upper = tri[:, :2 * HQ]
    row_d = lax.broadcasted_iota(jnp.int32, (LANES, HQ), 0)

    def branch(tiles):
        qzt = []
        for q_ref, _ in tiles:
            parts = []
            for half in range(2):
                qt = q_ref[half * HQ:(half + 1) * HQ, :].astype(jnp.float32).T
                parts.append(jnp.where(row_d < DK, qt, 0.0))
                parts.append(jnp.where(row_d >= DK, qt, 0.0))
            qzt.append(jnp.concatenate(parts, axis=1).astype(jnp.bfloat16))

        todo = {}
        for ti, (_, t) in enumerate(tiles):
            nb = 2 * t
            todo[ti] = [(nb, "mixed"), (nb + 1, "upper")] + [(j, "full") for j in range(nb)]
        tasks = []
        while any(todo.values()):
            for ti in range(len(tiles)):
                if todo[ti]:
                    tasks.append((ti,) + todo[ti].pop(0))
        m_run, l_run = {}, {}

        def scores(task):
            ti, j, kind = task
            kb = k_ref[j * HQ:(j + 1) * HQ, :]
            if kind == "upper":
                return jnp.where(mask_upper, _dot(kb, qzt[ti][:, 2 * HQ:]), -jnp.inf)
            s = _dot(kb, qzt[ti])
            return jnp.where(mask_mixed, s, -jnp.inf) if kind == "mixed" else s

        def softmax(task, s):
            ti, j, kind = task
            mx = jnp.max(s, axis=0, keepdims=True)
            if kind == "mixed":
                p = jnp.exp2(s - mx)
                m_run[ti], l_run[ti] = mx, jnp.sum(p, axis=0, keepdims=True)
                return p.astype(jnp.bfloat16), None
            m_old = m_run[ti][:, 2 * HQ:] if kind == "upper" else m_run[ti]
            l_old = l_run[ti][:, 2 * HQ:] if kind == "upper" else l_run[ti]
            m_new = jnp.maximum(m_old, mx)
            alpha = jnp.exp2(m_old - m_new)
            p = jnp.exp2(s - m_new)
            l_new = alpha * l_old + jnp.sum(p, axis=0, keepdims=True)
            if kind == "upper":
                m_new = jnp.concatenate([m_run[ti][:, :2 * HQ], m_new], axis=1)
                l_new = jnp.concatenate([l_run[ti][:, :2 * HQ], l_new], axis=1)
            m_run[ti], l_run[ti] = m_new, l_new
            return p.astype(jnp.bfloat16), alpha

        def accumulate(task, p, alpha):
            ti, j, kind = task
            pv = _dot(vt_ref[:, j * HQ:(j + 1) * HQ], p)
            if kind == "mixed":
                acc_ref[ti] = pv
            elif kind == "upper":
                acc_ref[ti, :, 2 * HQ:] = alpha * acc_ref[ti, :, 2 * HQ:] + pv
            else:
                acc_ref[ti] = alpha * acc_ref[ti] + pv

        n = len(tasks)
        s_prev, p_prev = None, None
        for step in range(n + 2):
            if step >= 2:
                accumulate(tasks[step - 2], *p_prev)
            if 1 <= step <= n:
                p_prev = softmax(tasks[step - 1], s_prev)
            if step < n:
                s_prev = scores(tasks[step])

        for ti in range(len(tiles)):
            o_all = acc_ref[ti] / l_run[ti]
            for half in range(2):
                o = (o_all[:, (2 * half) * HQ:(2 * half + 1) * HQ]
                     - lam * o_all[:, (2 * half + 1) * HQ:(2 * half + 2) * HQ])
                ms = jnp.mean(o * o, axis=0, keepdims=True)
                on = o * lax.rsqrt(ms + NORM_EPS) * sg_ref[...] * (1.0 - LAMBDA_INIT)
                r0 = ti * TQ + half * HQ
                o_ref[r0:r0 + HQ, :] = on.T.astype(jnp.bfloat16)

    for cc in range(nq // 2):
        @pl.when(c == cc)
        def _():
            branch([(qa_ref, cc), (qb_ref, nq - 1 - cc)])


def _attention(q, k, vt, lq1, lk1, lq2, lk2, sg_col, bsz, seq):
    n = q.shape[0]
    nq = seq // TQ
    assert nq % 2 == 0
    vec = lambda b, h, c: (0, 0)
    return pl.pallas_call(
        functools.partial(_attn_kernel, nq=nq),
        grid=(bsz, N_HEADS, nq // 2),
        in_specs=[
            pl.BlockSpec((TQ, DV), lambda b, h, c: (b * nq + c, h)),
            pl.BlockSpec((TQ, DV), lambda b, h, c: (b * nq + nq - 1 - c, h)),
            pl.BlockSpec((seq, DV), lambda b, h, c: (b, h)),
            pl.BlockSpec((DV, seq), lambda b, h, c: (h, b)),
            pl.BlockSpec((1, DK), vec), pl.BlockSpec((1, DK), vec),
            pl.BlockSpec((1, DK), vec), pl.BlockSpec((1, DK), vec),
            pl.BlockSpec((DV, 1), vec),
        ],
        out_specs=pl.BlockSpec((2 * TQ, DV), lambda b, h, c: (b * (nq // 2) + c, h)),
        out_shape=jax.ShapeDtypeStruct((n, ATTN_WIDTH), jnp.bfloat16),
        scratch_shapes=[pltpu.VMEM((2, DV, 4 * HQ), jnp.float32)],
        compiler_params=pltpu.CompilerParams(
            dimension_semantics=("arbitrary", "arbitrary", "arbitrary"), vmem_limit_bytes=VMEM_LIMIT),
        name="attn",
    )(q, q, k, vt, lq1, lk1, lq2, lk2, sg_col)


def _attn_out_block(i, seq):
    per_seq = seq // T_SORT
    nq = seq // TQ
    b, r = i // per_seq, i % per_seq
    t, sub = r // (TQ // T_SORT), r % (TQ // T_SORT)
    pos = jnp.where(t < nq // 2, 2 * t, 2 * (nq - 1 - t) + 1)
    return b * per_seq + pos * (TQ // T_SORT) + sub


ROUTER_ROWS = 128
EXPERT_ROW0 = 32


def _outproj_kernel(x_ref, attn_ref, conv_ref, wo_ref, g2_ref, wr2_ref,
                    h_ref, xs_ref, meta_ref, cnt_ref):
    t = x_ref.shape[0]
    f32, bf16 = jnp.float32, jnp.bfloat16
    a = jnp.concatenate([attn_ref[...], conv_ref[...]], axis=1)
    h = x_ref[...] + _dot(a, wo_ref[...])
    h_ref[...] = h
    ms = jnp.mean(h * h, axis=-1, keepdims=True)
    hn = h * lax.rsqrt(ms + NORM_EPS) * g2_ref[...]
    hn_hi, hn_lo = _split2(hn)

    hh = _dot(hn_hi, wr2_ref[...])
    logits = hh[:, :ROUTER_ROWS] + hh[:, ROUTER_ROWS:] + _dot(hn_lo, wr2_ref[:, :ROUTER_ROWS])
    lt = logits.T
    row8 = lax.broadcasted_iota(jnp.int32, (SUBLANES, t), 0).astype(f32)
    neg_inf = -jnp.inf

    def first_argmax(v):
        mx = jnp.max(v, axis=0, keepdims=True)
        idx = jnp.min(jnp.where(v == mx, row8, float(SUBLANES)), axis=0, keepdims=True)
        return mx, idx

    g_log = jnp.where(row8 < N_GROUPS, lt[0:SUBLANES, :], neg_inf)
    g_max, g_sel = first_argmax(g_log)
    g_gate = 1.0 / jnp.sum(jnp.exp(g_log - g_max), axis=0, keepdims=True)
    e_log = jnp.zeros((EPG, t), f32)
    for g in range(N_GROUPS):
        rows = lt[EXPERT_ROW0 + g * EPG:EXPERT_ROW0 + (g + 1) * EPG, :]
        e_log = jnp.where(g_sel == float(g), rows, e_log)
    v1, i1 = first_argmax(e_log)
    v2, i2 = first_argmax(jnp.where(row8 == i1, neg_inf, e_log))
    tt = jnp.exp(v2 - v1)
    w1 = g_gate / (1.0 + tt)
    w2 = g_gate * tt / (1.0 + tt)
    e1 = g_sel * float(EPG) + i1
    e2 = g_sel * float(EPG) + i2

    row32 = lax.broadcasted_iota(jnp.int32, (N_EXPERTS, t), 0).astype(f32)
    oh1 = row32 == e1
    oh2 = row32 == e2
    c = jnp.where(oh1 | oh2, 1.0, 0.0)
    tok_r = lax.broadcasted_iota(jnp.int32, (t, t), 0)
    tok_c = lax.broadcasted_iota(jnp.int32, (t, t), 1)
    earlier = jnp.where(tok_r < tok_c, 1.0, 0.0).astype(bf16)
    rank = _dot(c.astype(bf16), earlier)
    cnt_b = jnp.broadcast_to(jnp.sum(c, axis=1, keepdims=True), (N_EXPERTS, t))
    ex_r = lax.broadcasted_iota(jnp.int32, (N_EXPERTS, N_EXPERTS), 0)
    ex_c = lax.broadcasted_iota(jnp.int32, (N_EXPERTS, N_EXPERTS), 1)
    lower = jnp.where(ex_c < ex_r, 1.0, 0.0).astype(bf16)
    start_b = _dot(lower, cnt_b.astype(bf16))
    pos_e = start_b + rank
    p1 = jnp.sum(jnp.where(oh1, pos_e, 0.0), axis=0, keepdims=True)
    p2 = jnp.sum(jnp.where(oh2, pos_e, 0.0), axis=0, keepdims=True)

    srow = lax.broadcasted_iota(jnp.int32, (2 * t, t), 0).astype(f32)
    perm = jnp.where((srow == p1) | (srow == p2), 1.0, 0.0).astype(bf16)
    xs = _dot(perm, hn_hi)
    xs_ref[...] = _pack_rows(xs)

    meta = jnp.concatenate([p1, p2, w1, w2, jnp.zeros((LANES - 4, t), f32)], axis=0)
    meta_ref[...] = meta.T
    cnt_ref[...] = cnt_b[:, :LANES]


def _outproj(x2, attn_p, conv_o, wo, g2, wr2, seq):
    n = x2.shape[0]
    t = T_SORT
    nt = n // t
    const = lambda i: (0, 0)

    def attn_map(i):
        return (_attn_out_block(i, seq), 0)

    return pl.pallas_call(
        _outproj_kernel,
        grid=(nt,),
        in_specs=[
            pl.BlockSpec((t, D_MODEL), lambda i: (i, 0)),
            pl.BlockSpec((t, ATTN_WIDTH), attn_map),
            pl.BlockSpec((t, CONV_WIDTH), lambda i: (i, 0)),
            pl.BlockSpec((D_MODEL, D_MODEL), const),
            pl.BlockSpec((1, D_MODEL), const),
            pl.BlockSpec((D_MODEL, 2 * ROUTER_ROWS), const),
        ],
        out_specs=[
            pl.BlockSpec((t, D_MODEL), lambda i: (i, 0)),
            pl.BlockSpec((2 * t, ROW_CHUNKS, LANES), lambda i: (i, 0, 0)),
            pl.BlockSpec((t, LANES), lambda i: (i, 0)),
            pl.BlockSpec((N_EXPERTS, LANES), lambda i: (i, 0)),
        ],
        out_shape=[
            jax.ShapeDtypeStruct((n, D_MODEL), jnp.float32),
            jax.ShapeDtypeStruct((2 * n, ROW_CHUNKS, LANES), PACKED_DTYPE),
            jax.ShapeDtypeStruct((n, LANES), jnp.float32),
            jax.ShapeDtypeStruct((nt * N_EXPERTS, LANES), jnp.float32),
        ],
        compiler_params=pltpu.CompilerParams(
            dimension_semantics=("arbitrary",), vmem_limit_bytes=VMEM_LIMIT),
        name="outproj",
    )(x2, attn_p, conv_o, wo, g2, wr2)


def _pack_rows(x):
    r = x.shape[0]
    w = pltpu.pack_elementwise([x[:, :ROW_WORDS], x[:, ROW_WORDS:]], packed_dtype=jnp.bfloat16)
    return pltpu.bitcast(w, PACKED_DTYPE).reshape(r, ROW_CHUNKS, LANES)


def _packed_zero_rows(r):
    z = jnp.zeros((r, ROW_CHUNKS, LANES), jnp.float32)
    w = pltpu.pack_elementwise([z, z], packed_dtype=jnp.bfloat16)
    return pltpu.bitcast(w, PACKED_DTYPE)


def _unpack_rows(u):
    r = u.shape[0]
    w = u.reshape(r, ROW_WORDS)
    lo = pltpu.unpack_elementwise(w, index=0, packed_dtype=jnp.bfloat16, unpacked_dtype=jnp.float32)
    hi = pltpu.unpack_elementwise(w, index=1, packed_dtype=jnp.bfloat16, unpacked_dtype=jnp.float32)
    return jnp.concatenate([lo, hi], axis=1).astype(jnp.bfloat16)


def _expert_kernel(be_ref, nused_ref, r0_ref, nvalid_ref, tlo_ref, thi_ref, cnt_ref, src_ref, rbase_ref,
                   nxt_ref, xs_hbm, wg_hbm, wu_hbm, wd_hbm, y_ref,
                   xbuf, wg_f, wu_f, wd_f, wg_s, wu_s, wd_s, wslot_ref, sem, wsem):
    b = pl.program_id(0)
    nused = nused_ref[0]

    def gather(blk, slot):
        e = be_ref[blk]
        r0 = r0_ref[blk]

        @pl.when(nvalid_ref[blk] < MOE_BLOCK)
        def _():
            xbuf[slot] = _packed_zero_rows(MOE_BLOCK)

        def run(t, carry):
            k = t * N_EXPERTS + e
            lo = jnp.maximum(rbase_ref[k], r0)
            hi = jnp.minimum(rbase_ref[k] + cnt_ref[k], r0 + MOE_BLOCK)

            @pl.when(hi > lo)
            def _():
                pltpu.make_async_copy(xs_hbm.at[pl.ds(src_ref[k] + lo - rbase_ref[k], hi - lo)],
                                      xbuf.at[slot, pl.ds(lo - r0, hi - lo)], sem.at[slot]).start()
            return carry
        lax.fori_loop(tlo_ref[blk], thi_ref[blk], run, 0)

    def weight_copies(e, slot):
        return [pltpu.make_async_copy(src.at[e], dst.at[slot], wsem.at[slot])
                for src, dst in ((wg_hbm, wg_f), (wu_hbm, wu_f), (wd_hbm, wd_f))]

    @pl.when(b == 0)
    def _():
        gather(0, 0)
        wslot_ref[0] = 0
        for cp in weight_copies(be_ref[0], 0):
            cp.start()

    @pl.when(b + 1 < nused)
    def _():
        gather(b + 1, (b + 1) % 2)

    @pl.when(b < nused)
    def _():
        e = be_ref[b]

        @pl.when((b == 0) | (e != be_ref[jnp.maximum(b - 1, 0)]))
        def _():
            ws = wslot_ref[0]
            for cp in weight_copies(e, ws):
                cp.wait()
            wg_s[...] = wg_f[ws].astype(jnp.bfloat16)
            wu_s[...] = wu_f[ws].astype(jnp.bfloat16)
            wd_s[...] = wd_f[ws].astype(jnp.bfloat16)
            wslot_ref[0] = 1 - ws

            @pl.when(nxt_ref[e] >= 0)
            def _():
                for cp in weight_copies(nxt_ref[e], 1 - ws):
                    cp.start()

        slot = b % 2
        nv = nvalid_ref[b]
        pltpu.make_async_copy(xs_hbm.at[pl.ds(0, nv)], xbuf.at[slot, pl.ds(0, nv)], sem.at[slot]).wait()

        x = _unpack_rows(xbuf[slot])
        g = _dot(x, wg_s[...])
        u = _dot(x, wu_s[...])
        act = (g / (1.0 + jnp.exp(-g)) * u).astype(jnp.bfloat16)
        y = _dot(act, wd_s[...])
        y_ref[...] = _pack_rows(y)

    @pl.when(b >= nused)
    def _():
        y_ref[...] = _packed_zero_rows(MOE_BLOCK)


def _experts(tabs, xs, w_gate, w_up, w_down):
    n_slots = tabs["n_slots"]
    n_blocks = n_slots // MOE_BLOCK
    f32, bf16 = jnp.float32, jnp.bfloat16
    hbm = pl.BlockSpec(memory_space=pl.ANY)
    return pl.pallas_call(
        _expert_kernel,
        grid_spec=pltpu.PrefetchScalarGridSpec(
            num_scalar_prefetch=10,
            grid=(n_blocks,),
            in_specs=[hbm, hbm, hbm, hbm],
            out_specs=pl.BlockSpec((MOE_BLOCK, ROW_CHUNKS, LANES), lambda b, *_: (b, 0, 0)),
            scratch_shapes=[pltpu.VMEM((2, MOE_BLOCK, ROW_CHUNKS, LANES), PACKED_DTYPE),
                            pltpu.VMEM((2, D_MODEL, D_FF), f32),
                            pltpu.VMEM((2, D_MODEL, D_FF), f32),
                            pltpu.VMEM((2, D_FF, D_MODEL), f32),
                            pltpu.VMEM((D_MODEL, D_FF), bf16),
                            pltpu.VMEM((D_MODEL, D_FF), bf16),
                            pltpu.VMEM((D_FF, D_MODEL), bf16),
                            pltpu.SMEM((1,), jnp.int32),
                            pltpu.SemaphoreType.DMA((2,)),
                            pltpu.SemaphoreType.DMA((2,))],
        ),
        out_shape=jax.ShapeDtypeStruct((n_slots, ROW_CHUNKS, LANES), PACKED_DTYPE),
        compiler_params=pltpu.CompilerParams(
            dimension_semantics=("arbitrary",), vmem_limit_bytes=VMEM_LIMIT),
        name="experts",
    )(tabs["block_e"], tabs["nused"], tabs["r0"], tabs["nvalid"], tabs["tlo"], tabs["thi"],
      tabs["cnt"], tabs["src"], tabs["rbase"], tabs["nxt"], xs, w_gate, w_up, w_down)


def _combine_kernel(cnt_ref, loc_ref, dst_ref, h_ref, meta_ref, yg_hbm, o_ref, ybuf, sem, *, n_tiles):
    i = pl.program_id(0)
    t = h_ref.shape[0]
    f32, bf16 = jnp.float32, jnp.bfloat16

    def issue(tile, slot):
        for e in range(N_EXPERTS):
            k = tile * N_EXPERTS + e
            @pl.when(cnt_ref[k] > 0)
            def _():
                pltpu.make_async_copy(yg_hbm.at[pl.ds(dst_ref[k], cnt_ref[k])],
                                      ybuf.at[slot, pl.ds(loc_ref[k], cnt_ref[k])],
                                      sem.at[slot]).start()

    @pl.when(i == 0)
    def _():
        issue(0, 0)

    @pl.when(i + 1 < n_tiles)
    def _():
        issue(i + 1, (i + 1) % 2)

    slot = i % 2
    pltpu.make_async_copy(yg_hbm.at[pl.ds(0, 2 * t)], ybuf.at[slot], sem.at[slot]).wait()

    y = _unpack_rows(ybuf[slot])
    lane = lax.broadcasted_iota(jnp.int32, (t, 2 * t), 1).astype(f32)
    meta = meta_ref[...]
    pick1 = jnp.where(lane == meta[:, 0:1], 1.0, 0.0).astype(bf16)
    pick2 = jnp.where(lane == meta[:, 1:2], 1.0, 0.0).astype(bf16)
    y1 = _dot(pick1, y)
    y2 = _dot(pick2, y)
    o_ref[...] = h_ref[...] + (meta[:, 2:3] * y1 + meta[:, 3:4] * y2)


def _combine(cnt, loc, dst, h, meta, yg):
    n = h.shape[0]
    t = T_SORT
    n_tiles = n // t
    return pl.pallas_call(
        functools.partial(_combine_kernel, n_tiles=n_tiles),
        grid_spec=pltpu.PrefetchScalarGridSpec(
            num_scalar_prefetch=3,
            grid=(n_tiles,),
            in_specs=[
                pl.BlockSpec((t, D_MODEL), lambda i, *_: (i, 0)),
                pl.BlockSpec((t, LANES), lambda i, *_: (i, 0)),
                pl.BlockSpec(memory_space=pl.ANY),
            ],
            out_specs=pl.BlockSpec((t, D_MODEL), lambda i, *_: (i, 0)),
            scratch_shapes=[pltpu.VMEM((2, 2 * t, ROW_CHUNKS, LANES), PACKED_DTYPE),
                            pltpu.SemaphoreType.DMA((2,))],
        ),
        out_shape=jax.ShapeDtypeStruct((n, D_MODEL), jnp.float32),
        compiler_params=pltpu.CompilerParams(
            dimension_semantics=("arbitrary",), vmem_limit_bytes=VMEM_LIMIT),
        name="combine",
    )(cnt, loc, dst, h, meta, yg)


def _routing_tables(cnt_out, n_tiles, n_tok):
    i32 = jnp.int32
    cnt = cnt_out.reshape(n_tiles, N_EXPERTS, LANES)[:, :, 0].astype(i32)
    count = jnp.sum(cnt, axis=0)
    padded = ((count + MOE_BLOCK - 1) // MOE_BLOCK) * MOE_BLOCK
    pad_end = jnp.cumsum(padded)
    pad_start = pad_end - padded
    run_end = jnp.cumsum(cnt, axis=0)
    rbase = run_end - cnt
    dst = pad_start[None, :] + rbase
    loc = jnp.cumsum(cnt, axis=1) - cnt
    src = loc + (jnp.arange(n_tiles, dtype=i32) * (2 * T_SORT))[:, None]
    n_blocks = (2 * n_tok) // MOE_BLOCK + N_EXPERTS
    block_start = jnp.arange(n_blocks, dtype=i32) * MOE_BLOCK
    block_e = jnp.minimum(jnp.sum(pad_end[None, :] <= block_start[:, None], axis=1), N_EXPERTS - 1).astype(i32)
    nused = (pad_end[-1] // MOE_BLOCK).astype(i32).reshape(1)
    ex = jnp.arange(N_EXPERTS, dtype=i32)
    sel = (block_e[None, :] == ex[:, None]).astype(i32)
    pick = lambda per_expert: jnp.sum(per_expert[..., :, None] * sel, axis=-2)
    r0 = block_start - pick(pad_start)
    nvalid = jnp.clip(pick(count) - r0, 0, MOE_BLOCK)
    tlo = jnp.sum(pick(run_end) <= r0[None, :], axis=0)
    thi = jnp.sum(pick(rbase) < (r0 + MOE_BLOCK)[None, :], axis=0)
    later_used = (ex[None, :] > ex[:, None]) & (count[None, :] > 0)
    nxt = jnp.min(jnp.where(later_used, ex[None, :], N_EXPERTS), axis=1)
    nxt = jnp.where(nxt < N_EXPERTS, nxt, -1)
    flat = lambda a: a.reshape(-1).astype(i32)
    return dict(cnt=flat(cnt), src=flat(src), dst=flat(dst), loc=flat(loc), rbase=flat(rbase),
                block_e=block_e, nused=nused, r0=flat(r0), nvalid=flat(nvalid), tlo=flat(tlo), thi=flat(thi),
                nxt=flat(nxt), n_slots=n_blocks * MOE_BLOCK)


def _stage1(x, positions, attn_norm_gain, w_in, q_norm_gain, k_norm_gain, conv_w, conv_out_gain):
    bsz, seq, _ = x.shape
    n = bsz * seq
    f32, bf16 = jnp.float32, jnp.bfloat16
    w = w_in[0]
    wqk = w[:, :2 * ATTN_WIDTH].astype(bf16)
    wvt = w[:, 2 * ATTN_WIDTH:3 * ATTN_WIDTH].T.astype(bf16)
    wc = w[:, 3 * ATTN_WIDTH:].astype(bf16)
    scale = DK ** -0.5 * LOG2E
    gqk = jnp.concatenate([jnp.tile(q_norm_gain[0].astype(f32), 2 * N_HEADS) * scale,
                           jnp.tile(k_norm_gain[0].astype(f32), 2 * N_HEADS)]).reshape(1, -1)
    freqs = (ROPE_THETA ** (-jnp.arange(0, ROT_DIM, 2, dtype=f32) / ROT_DIM)).reshape(SUBLANES, 1)
    return _inproj(x.reshape(n, D_MODEL), positions.reshape(1, n),
                   attn_norm_gain[0].reshape(1, -1).astype(f32), wqk, wvt, wc, gqk, freqs,
                   conv_w[0].astype(f32), conv_out_gain[0].reshape(1, -1).astype(f32), seq)


def kernel(x, positions, attn_norm_gain, w_in, q_norm_gain, k_norm_gain, lambda_q1, lambda_k1, lambda_q2, lambda_k2, subln_gain, conv_w, conv_out_gain, w_out, ffn_norm_gain, w_group_router, w_expert_router, w_gate, w_up, w_down):
    bsz, seq, _ = x.shape
    n = bsz * seq
    f32, bf16 = jnp.float32, jnp.bfloat16
    assert TQ % T_SORT == 0 and seq % TM_IN == 0 and (seq // TQ) % 2 == 0
    q, k, vt, conv_o = _stage1(x, positions, attn_norm_gain, w_in, q_norm_gain, k_norm_gain,
                                conv_w, conv_out_gain)
    attn_p = _attention(q, k, vt,
                        lambda_q1[0].reshape(1, -1).astype(f32), lambda_k1[0].reshape(1, -1).astype(f32),
                        lambda_q2[0].reshape(1, -1).astype(f32), lambda_k2[0].reshape(1, -1).astype(f32),
                        subln_gain[0].reshape(-1, 1).astype(f32), bsz, seq)

    wr = jnp.zeros((D_MODEL, ROUTER_ROWS), f32)
    wr = wr.at[:, 0:N_GROUPS].set(w_group_router[0].astype(f32))
    wr = wr.at[:, EXPERT_ROW0:EXPERT_ROW0 + N_EXPERTS].set(
        jnp.transpose(w_expert_router[0].astype(f32), (1, 0, 2)).reshape(D_MODEL, N_EXPERTS))
    wrh = wr.astype(bf16)
    wr2 = jnp.concatenate([wrh, (wr - wrh.astype(f32)).astype(bf16)], axis=1)
    h, xs, meta, cnt_out = _outproj(x.reshape(n, D_MODEL), attn_p, conv_o, w_out[0].astype(bf16),
                                    ffn_norm_gain[0].reshape(1, -1).astype(f32), wr2, seq)

    tabs = _routing_tables(cnt_out, n // T_SORT, n)
    yg = _experts(tabs, xs, w_gate[0], w_up[0], w_down[0])
    out = _combine(tabs["cnt"], tabs["loc"], tabs["dst"], h, meta, yg)
    return out.reshape(x.shape)
```

```python
import functools
import math

import jax
import jax.numpy as jnp
from jax import lax
from jax.experimental import pallas as pl
from jax.experimental.pallas import tpu as pltpu

D_MODEL = 1024
N_HEADS = 4
DK = 64
DV = 128
ROT_DIM = 16
ROPE_THETA = 500000.0
ATTN_WIDTH = N_HEADS * DV
CONV_WIDTH = 512
NORM_EPS = 1e-6
LOG2E = 1.4426950408889634
LAMBDA_INIT = 0.8 - 0.6 * math.exp(-0.3 * 0)
N_GROUPS = 4
EPG = 8
N_EXPERTS = N_GROUPS * EPG
D_FF = 512
MOE_BLOCK = 256

LANES = 128
SUBLANES = 8
ROW_WORDS = D_MODEL // 2
ROW_CHUNKS = ROW_WORDS // LANES
PACKED_DTYPE = jnp.uint32

TM_IN = 512
TQ = 512
HQ = TQ // 2
T_SORT = 256
TM_OUT = 512
VMEM_LIMIT = 48 * 1024 * 1024


def _nt_dot(a, b):
    return lax.dot_general(a, b, (((1,), (1,)), ((), ())), preferred_element_type=jnp.float32)


def _dot(a, b):
    return jnp.dot(a, b, preferred_element_type=jnp.float32)


def _split3(x):
    h = x.astype(jnp.bfloat16)
    r = x - h.astype(jnp.float32)
    m = r.astype(jnp.bfloat16)
    l = (r - m.astype(jnp.float32)).astype(jnp.bfloat16)
    return h, m, l


def _split2(x):
    h = x.astype(jnp.bfloat16)
    l = (x - h.astype(jnp.float32)).astype(jnp.bfloat16)
    return h, l


def _inproj_kernel(x_ref, pos_ref, g1_ref, wqk_ref, wvt_ref, wc_ref, gqk_ref, freq_ref,
                   cw_ref, cg_ref,
                   q_ref, k_ref, vt_ref, conv_ref,
                   carry_ref, *, tiles_per_seq):
    tm = x_ref.shape[0]
    i = pl.program_id(0)

    x = x_ref[...]
    ms = jnp.mean(x * x, axis=-1, keepdims=True)
    hn = (x * lax.rsqrt(ms + NORM_EPS) * g1_ref[...]).astype(jnp.bfloat16)

    pos = pos_ref[...].astype(jnp.float32)
    ang = freq_ref[...] * pos
    lane_r = lax.broadcasted_iota(jnp.int32, (LANES, SUBLANES), 0)
    f_c = lax.broadcasted_iota(jnp.int32, (LANES, SUBLANES), 1)
    in_rot = (lane_r % DK) < ROT_DIM
    expand = jnp.where(in_rot & ((lane_r % (ROT_DIM // 2)) == f_c), 1.0, 0.0).astype(jnp.bfloat16)

    def to_rows(t):
        h, m, l = _split3(t)
        r = _dot(expand, h) + _dot(expand, m) + _dot(expand, l)
        return r.T

    cos_r = to_rows(jnp.cos(ang))
    sin_r = to_rows(jnp.sin(ang))
    lane = lax.broadcasted_iota(jnp.int32, (tm, LANES), 1)
    d = lane % DK
    cos_r = jnp.where(d < ROT_DIM, cos_r, 1.0)
    sin_lo = jnp.where(d < ROT_DIM // 2, -sin_r, 0.0)
    sin_hi = jnp.where((d >= ROT_DIM // 2) & (d < ROT_DIM), sin_r, 0.0)

    qk = _dot(hn, wqk_ref[...])
    seg_r = lax.broadcasted_iota(jnp.int32, (2 * LANES, 2 * LANES), 0) // DK
    seg_c = lax.broadcasted_iota(jnp.int32, (2 * LANES, 2 * LANES), 1) // DK
    seg_mean = jnp.where(seg_r == seg_c, 1.0 / DK, 0.0).astype(jnp.bfloat16)
    half = ROT_DIM // 2
    for c2 in range(4):
        blk = qk[:, c2 * 256:(c2 + 1) * 256]
        msq = _dot((blk * blk).astype(jnp.bfloat16), seg_mean)
        y2 = blk * lax.rsqrt(msq + NORM_EPS) * gqk_ref[:, c2 * 256:(c2 + 1) * 256]
        for c1 in range(2):
            c = c2 * 2 + c1
            y = y2[:, c1 * LANES:(c1 + 1) * LANES]
            rot = (y * cos_r
                   + pltpu.roll(y, LANES - half, 1) * sin_lo
                   + pltpu.roll(y, half, 1) * sin_hi)
            if c < N_HEADS:
                q_ref[:, c * LANES:(c + 1) * LANES] = rot.astype(jnp.bfloat16)
            else:
                h = c - N_HEADS
                k_ref[:, h * LANES:(h + 1) * LANES] = rot.astype(jnp.bfloat16)

    vt_ref[...] = _nt_dot(wvt_ref[...], hn).astype(jnp.bfloat16)

    cp = _dot(hn, wc_ref[...])
    cb = cp[:, :CONV_WIDTH]
    y = cp[:, CONV_WIDTH:2 * CONV_WIDTH] * cp[:, 2 * CONV_WIDTH:]

    @pl.when(i % tiles_per_seq == 0)
    def _():
        carry_ref[...] = jnp.zeros_like(carry_ref)

    prev = carry_ref[...]
    row = lax.broadcasted_iota(jnp.int32, (tm, CONV_WIDTH), 0)
    p1 = prev[SUBLANES - 1:SUBLANES, :]
    p2 = prev[SUBLANES - 2:SUBLANES - 1, :]
    y1 = jnp.where(row == 0, p1, pltpu.roll(y, 1, 0))
    y2 = jnp.where(row == 0, p2, jnp.where(row == 1, p1, pltpu.roll(y, 2, 0)))
    carry_ref[...] = y[tm - SUBLANES:, :]
    z = cw_ref[0:1, :] * y2 + cw_ref[1:2, :] * y1 + cw_ref[2:3, :] * y
    co = cb * z
    cms = jnp.mean(co * co, axis=-1, keepdims=True)
    conv_ref[...] = (co * lax.rsqrt(cms + NORM_EPS) * cg_ref[...]).astype(jnp.bfloat16)


def _inproj(x2, pos_row, g1, wqk, wvt, wc, gqk, freqs, cw, cg, seq):
    n = x2.shape[0]
    tm = TM_IN
    grid = (n // tm,)
    const = lambda i: (0, 0)
    return pl.pallas_call(
        functools.partial(_inproj_kernel, tiles_per_seq=seq // tm),
        grid=grid,
        in_specs=[
            pl.BlockSpec((tm, D_MODEL), lambda i: (i, 0)),
            pl.BlockSpec((1, tm), lambda i: (0, i)),
            pl.BlockSpec((1, D_MODEL), const),
            pl.BlockSpec((D_MODEL, 1024), const),
            pl.BlockSpec((512, D_MODEL), const),
            pl.BlockSpec((D_MODEL, 1536), const),
            pl.BlockSpec((1, 1024), const),
            pl.BlockSpec((SUBLANES, 1), const),
            pl.BlockSpec((3, CONV_WIDTH), const),
            pl.BlockSpec((1, CONV_WIDTH), const),
        ],
        out_specs=[
            pl.BlockSpec((tm, ATTN_WIDTH), lambda i: (i, 0)),
            pl.BlockSpec((tm, ATTN_WIDTH), lambda i: (i, 0)),
            pl.BlockSpec((ATTN_WIDTH, tm), lambda i: (0, i)),
            pl.BlockSpec((tm, CONV_WIDTH), lambda i: (i, 0)),
        ],
        out_shape=[
            jax.ShapeDtypeStruct((n, ATTN_WIDTH), jnp.bfloat16),
            jax.ShapeDtypeStruct((n, ATTN_WIDTH), jnp.bfloat16),
            jax.ShapeDtypeStruct((ATTN_WIDTH, n), jnp.bfloat16),
            jax.ShapeDtypeStruct((n, CONV_WIDTH), jnp.bfloat16),
        ],
        scratch_shapes=[pltpu.VMEM((SUBLANES, CONV_WIDTH), jnp.float32)],
        compiler_params=pltpu.CompilerParams(
            dimension_semantics=("arbitrary",), vmem_limit_bytes=VMEM_LIMIT),
        name="inproj",
    )(x2, pos_row, g1, wqk, wvt, wc, gqk, freqs, cw, cg)


def _attn_kernel(qa_ref, qb_ref, k_ref, vt_ref, lq1_ref, lk1_ref, lq2_ref, lk2_ref, sg_ref,
                 o_ref, acc_ref, *, nq):
    c = pl.program_id(2)
    lam = (jnp.exp(jnp.sum(lq1_ref[...] * lk1_ref[...], axis=-1, keepdims=True))
           - jnp.exp(jnp.sum(lq2_ref[...] * lk2_ref[...], axis=-1, keepdims=True))
           + LAMBDA_INIT)
    key_i = lax.broadcasted_iota(jnp.int32, (HQ, 4 * HQ), 0)
    col_i = lax.broadcasted_iota(jnp.int32, (HQ, 4 * HQ), 1)
    tri = key_i <= col_i % HQ
    mask_mixed = tri | (col_i >= 2 * HQ)
    mask_upper = tri[:, :2 * HQ]
    row_d = lax.broadcasted_iota(jnp.int32, (LANES, HQ), 0)

    def branch(tiles):
        qzt = []
        for q_ref, _ in tiles:
            parts = []
            for half in range(2):
                qt = q_ref[half * HQ:(half + 1) * HQ, :].astype(jnp.float32).T
                parts.append(jnp.where(row_d < DK, qt, 0.0))
                parts.append(jnp.where(row_d >= DK, qt, 0.0))
            qzt.append(jnp.concatenate(parts, axis=1).astype(jnp.bfloat16))

        todo = {}
        for ti, (_, t) in enumerate(tiles):
            nb = 2 * t
            todo[ti] = [(nb, "mixed"), (nb + 1, "upper")] + [(j, "full") for j in range(nb)]
        tasks = []
        while any(todo.values()):
            for ti in range(len(tiles)):
                if todo[ti]:
                    tasks.append((ti,) + todo[ti].pop(0))
        m_run, l_run = {}, {}

        def scores(task):
            ti, j, kind = task
            kb = k_ref[j * HQ:(j + 1) * HQ, :]
            if kind == "upper":
                return jnp.where(mask_upper, _dot(kb, qzt[ti][:, 2 * HQ:]), -jnp.inf)
            s = _dot(kb, qzt[ti])
            return jnp.where(mask_mixed, s, -jnp.inf) if kind == "mixed" else s

        def softmax(task, s):
            ti, j, kind = task
            mx = jnp.max(s, axis=0, keepdims=True)
            if kind == "mixed":
                p = jnp.exp2(s - mx)
                m_run[ti], l_run[ti] = mx, jnp.sum(p, axis=0, keepdims=True)
                return p.astype(jnp.bfloat16), None
            m_old = m_run[ti][:, 2 * HQ:] if kind == "upper" else m_run[ti]
            l_old = l_run[ti][:, 2 * HQ:] if kind == "upper" else l_run[ti]
            m_new = jnp.maximum(m_old, mx)
            alpha = jnp.exp2(m_old - m_new)
            p = jnp.exp2(s - m_new)
            l_new = alpha * l_old + jnp.sum(p, axis=0, keepdims=True)
            if kind == "upper":
                m_new = jnp.concatenate([m_run[ti][:, :2 * HQ], m_new], axis=1)
                l_new = jnp.concatenate([l_run[ti][:, :2 * HQ], l_new], axis=1)
            m_run[ti], l_run[ti] = m_new, l_new
            return p.astype(jnp.bfloat16), alpha

        def accumulate(task, p, alpha):
            ti, j, kind = task
            pv = _dot(vt_ref[:, j * HQ:(j + 1) * HQ], p)
            if kind == "mixed":
                acc_ref[ti] = pv
            elif kind == "upper":
                acc_ref[ti, :, 2 * HQ:] = alpha * acc_ref[ti, :, 2 * HQ:] + pv
            else:
                acc_ref[ti] = alpha * acc_ref[ti] + pv

        n = len(tasks)
        s_prev, p_prev = None, None
        for step in range(n + 2):
            if step >= 2:
                accumulate(tasks[step - 2], *p_prev)
            if 1 <= step <= n:
                p_prev = softmax(tasks[step - 1], s_prev)
            if step < n:
                s_prev = scores(tasks[step])

        for ti in range(len(tiles)):
            o_all = acc_ref[ti] / l_run[ti]
            for half in range(2):
                o = (o_all[:, (2 * half) * HQ:(2 * half + 1) * HQ]
                     - lam * o_all[:, (2 * half + 1) * HQ:(2 * half + 2) * HQ])
                ms = jnp.mean(o * o, axis=0, keepdims=True)
                on = o * lax.rsqrt(ms + NORM_EPS) * sg_ref[...] * (1.0 - LAMBDA_INIT)
                r0 = ti * TQ + half * HQ
                o_ref[r0:r0 + HQ, :] = on.T.astype(jnp.bfloat16)

    for cc in range(nq // 2):
        @pl.when(c == cc)
        def _():
            branch([(qa_ref, cc), (qb_ref, nq - 1 - cc)])


def _attention(q, k, vt, lq1, lk1, lq2, lk2, sg_col, bsz, seq):
    n = q.shape[0]
    nq = seq // TQ
    assert nq % 2 == 0
    vec = lambda b, h, c: (0, 0)
    return pl.pallas_call(
        functools.partial(_attn_kernel, nq=nq),
        grid=(bsz, N_HEADS, nq // 2),
        in_specs=[
            pl.BlockSpec((TQ, DV), lambda b, h, c: (b * nq + c, h)),
            pl.BlockSpec((TQ, DV), lambda b, h, c: (b * nq + nq - 1 - c, h)),
            pl.BlockSpec((seq, DV), lambda b, h, c: (b, h)),
            pl.BlockSpec((DV, seq), lambda b, h, c: (h, b)),
            pl.BlockSpec((1, DK), vec), pl.BlockSpec((1, DK), vec),
            pl.BlockSpec((1, DK), vec), pl.BlockSpec((1, DK), vec),
            pl.BlockSpec((DV, 1), vec),
        ],
        out_specs=pl.BlockSpec((2 * TQ, DV), lambda b, h, c: (b * (nq // 2) + c, h)),
        out_shape=jax.ShapeDtypeStruct((n, ATTN_WIDTH), jnp.bfloat16),
        scratch_shapes=[pltpu.VMEM((2, DV, 4 * HQ), jnp.float32)],
        compiler_params=pltpu.CompilerParams(
            dimension_semantics=("arbitrary", "arbitrary", "arbitrary"), vmem_limit_bytes=VMEM_LIMIT),
        name="attn",
    )(q, q, k, vt, lq1, lk1, lq2, lk2, sg_col)


def _attn_out_block(i, seq, rows):
    per_seq = seq // rows
    nq = seq // TQ
    b, r = i // per_seq, i % per_seq
    t, sub = r // (TQ // rows), r % (TQ // rows)
    pos = jnp.where(t < nq // 2, 2 * t, 2 * (nq - 1 - t) + 1)
    return b * per_seq + pos * (TQ // rows) + sub


ROUTER_ROWS = 128
EXPERT_ROW0 = 32


def _outproj_kernel(x_ref, attn_ref, conv_ref, wo_ref, g2_ref, wr2_ref,
                    h_ref, xs_ref, meta_ref, cnt_ref):
    tm = x_ref.shape[0]
    t = T_SORT
    f32, bf16 = jnp.float32, jnp.bfloat16
    a = jnp.concatenate([attn_ref[...], conv_ref[...]], axis=1)
    h = x_ref[...] + _dot(a, wo_ref[...])
    h_ref[...] = h
    ms = jnp.mean(h * h, axis=-1, keepdims=True)
    hn = h * lax.rsqrt(ms + NORM_EPS) * g2_ref[...]
    hn_hi, hn_lo = _split2(hn)

    hh = _dot(hn_hi, wr2_ref[...])
    logits = hh[:, :ROUTER_ROWS] + hh[:, ROUTER_ROWS:] + _dot(hn_lo, wr2_ref[:, :ROUTER_ROWS])
    lt = logits.T
    row8 = lax.broadcasted_iota(jnp.int32, (SUBLANES, tm), 0).astype(f32)
    neg_inf = -jnp.inf

    def first_argmax(v):
        mx = jnp.max(v, axis=0, keepdims=True)
        idx = jnp.min(jnp.where(v == mx, row8, float(SUBLANES)), axis=0, keepdims=True)
        return mx, idx

    g_log = jnp.where(row8 < N_GROUPS, lt[0:SUBLANES, :], neg_inf)
    g_max, g_sel = first_argmax(g_log)
    g_gate = 1.0 / jnp.sum(jnp.exp(g_log - g_max), axis=0, keepdims=True)
    e_log = jnp.zeros((EPG, tm), f32)
    for g in range(N_GROUPS):
        rows = lt[EXPERT_ROW0 + g * EPG:EXPERT_ROW0 + (g + 1) * EPG, :]
        e_log = jnp.where(g_sel == float(g), rows, e_log)
    v1, i1 = first_argmax(e_log)
    v2, i2 = first_argmax(jnp.where(row8 == i1, neg_inf, e_log))
    tt = jnp.exp(v2 - v1)
    w1 = g_gate / (1.0 + tt)
    w2 = g_gate * tt / (1.0 + tt)
    e1 = g_sel * float(EPG) + i1
    e2 = g_sel * float(EPG) + i2

    row32 = lax.broadcasted_iota(jnp.int32, (N_EXPERTS, tm), 0).astype(f32)
    oh1 = row32 == e1
    oh2 = row32 == e2
    c = jnp.where(oh1 | oh2, 1.0, 0.0).astype(bf16)
    tok_r = lax.broadcasted_iota(jnp.int32, (tm, tm), 0)
    tok_c = lax.broadcasted_iota(jnp.int32, (tm, tm), 1)
    same_tile = (tok_r // t) == (tok_c // t)
    rank = _dot(c, jnp.where(same_tile & (tok_r < tok_c), 1.0, 0.0).astype(bf16))
    cnt_b = _dot(c, jnp.where(same_tile, 1.0, 0.0).astype(bf16))
    ex_r = lax.broadcasted_iota(jnp.int32, (N_EXPERTS, N_EXPERTS), 0)
    ex_c = lax.broadcasted_iota(jnp.int32, (N_EXPERTS, N_EXPERTS), 1)
    lower = jnp.where(ex_c < ex_r, 1.0, 0.0).astype(bf16)
    start_b = _dot(lower, cnt_b.astype(bf16))
    pos_e = start_b + rank
    p1 = jnp.sum(jnp.where(oh1, pos_e, 0.0), axis=0, keepdims=True)
    p2 = jnp.sum(jnp.where(oh2, pos_e, 0.0), axis=0, keepdims=True)

    srow = lax.broadcasted_iota(jnp.int32, (2 * t, t), 0).astype(f32)
    for s in range(tm // t):
        cols = slice(s * t, (s + 1) * t)
        perm = jnp.where((srow == p1[:, cols]) | (srow == p2[:, cols]), 1.0, 0.0).astype(bf16)
        xs = _dot(perm, hn_hi[cols, :])
        xs_ref[s * 2 * t:(s + 1) * 2 * t] = _pack_rows(xs)
        cnt_ref[s * N_EXPERTS:(s + 1) * N_EXPERTS, :] = cnt_b[:, s * t:s * t + LANES]

    meta = jnp.concatenate([p1, p2, w1, w2, jnp.zeros((LANES - 4, tm), f32)], axis=0)
    meta_ref[...] = meta.T


def _outproj(x2, attn_p, conv_o, wo, g2, wr2, seq):
    n = x2.shape[0]
    tm = TM_OUT
    nsub = tm // T_SORT
    const = lambda i: (0, 0)

    def attn_map(i):
        return (_attn_out_block(i, seq, tm), 0)

    return pl.pallas_call(
        _outproj_kernel,
        grid=(n // tm,),
        in_specs=[
            pl.BlockSpec((tm, D_MODEL), lambda i: (i, 0)),
            pl.BlockSpec((tm, ATTN_WIDTH), attn_map),
            pl.BlockSpec((tm, CONV_WIDTH), lambda i: (i, 0)),
            pl.BlockSpec((D_MODEL, D_MODEL), const),
            pl.BlockSpec((1, D_MODEL), const),
            pl.BlockSpec((D_MODEL, 2 * ROUTER_ROWS), const),
        ],
        out_specs=[
            pl.BlockSpec((tm, D_MODEL), lambda i: (i, 0)),
            pl.BlockSpec((2 * tm, ROW_CHUNKS, LANES), lambda i: (i, 0, 0)),
            pl.BlockSpec((tm, LANES), lambda i: (i, 0)),
            pl.BlockSpec((nsub * N_EXPERTS, LANES), lambda i: (i, 0)),
        ],
        out_shape=[
            jax.ShapeDtypeStruct((n, D_MODEL), jnp.float32),
            jax.ShapeDtypeStruct((2 * n, ROW_CHUNKS, LANES), PACKED_DTYPE),
            jax.ShapeDtypeStruct((n, LANES), jnp.float32),
            jax.ShapeDtypeStruct((n // T_SORT * N_EXPERTS, LANES), jnp.float32),
        ],
        compiler_params=pltpu.CompilerParams(
            dimension_semantics=("arbitrary",), vmem_limit_bytes=VMEM_LIMIT),
        name="outproj",
    )(x2, attn_p, conv_o, wo, g2, wr2)


def _pack_rows(x):
    r = x.shape[0]
    w = pltpu.pack_elementwise([x[:, :ROW_WORDS], x[:, ROW_WORDS:]], packed_dtype=jnp.bfloat16)
    return pltpu.bitcast(w, PACKED_DTYPE).reshape(r, ROW_CHUNKS, LANES)


def _packed_zero_rows(r):
    z = jnp.zeros((r, ROW_CHUNKS, LANES), jnp.float32)
    w = pltpu.pack_elementwise([z, z], packed_dtype=jnp.bfloat16)
    return pltpu.bitcast(w, PACKED_DTYPE)


def _unpack_rows(u):
    r = u.shape[0]
    w = u.reshape(r, ROW_WORDS)
    lo = pltpu.unpack_elementwise(w, index=0, packed_dtype=jnp.bfloat16, unpacked_dtype=jnp.float32)
    hi = pltpu.unpack_elementwise(w, index=1, packed_dtype=jnp.bfloat16, unpacked_dtype=jnp.float32)
    return jnp.concatenate([lo, hi], axis=1).astype(jnp.bfloat16)


def _expert_kernel(be_ref, nused_ref, r0_ref, nvalid_ref, tlo_ref, thi_ref, cnt_ref, src_ref, rbase_ref,
                   nxt_ref, xs_hbm, wg_hbm, wu_hbm, wd_hbm, y_ref,
                   xbuf, wg_f, wu_f, wd_f, wg_s, wu_s, wd_s, wslot_ref, sem, wsem):
    b = pl.program_id(0)
    nused = nused_ref[0]

    def gather(blk, slot):
        e = be_ref[blk]
        r0 = r0_ref[blk]

        @pl.when(nvalid_ref[blk] < MOE_BLOCK)
        def _():
            xbuf[slot] = _packed_zero_rows(MOE_BLOCK)

        def run(t, carry):
            k = t * N_EXPERTS + e
            lo = jnp.maximum(rbase_ref[k], r0)
            hi = jnp.minimum(rbase_ref[k] + cnt_ref[k], r0 + MOE_BLOCK)

            @pl.when(hi > lo)
            def _():
                pltpu.make_async_copy(xs_hbm.at[pl.ds(src_ref[k] + lo - rbase_ref[k], hi - lo)],
                                      xbuf.at[slot, pl.ds(lo - r0, hi - lo)], sem.at[slot]).start()
            return carry
        lax.fori_loop(tlo_ref[blk], thi_ref[blk], run, 0)

    def weight_copies(e, slot):
        return [pltpu.make_async_copy(src.at[e], dst.at[slot], wsem.at[slot])
                for src, dst in ((wg_hbm, wg_f), (wu_hbm, wu_f), (wd_hbm, wd_f))]

    @pl.when(b == 0)
    def _():
        gather(0, 0)
        wslot_ref[0] = 0
        for cp in weight_copies(be_ref[0], 0):
            cp.start()

    @pl.when(b + 1 < nused)
    def _():
        gather(b + 1, (b + 1) % 2)

    @pl.when(b < nused)
    def _():
        e = be_ref[b]

        @pl.when((b == 0) | (e != be_ref[jnp.maximum(b - 1, 0)]))
        def _():
            ws = wslot_ref[0]
            for cp in weight_copies(e, ws):
                cp.wait()
            wg_s[...] = wg_f[ws].astype(jnp.bfloat16)
            wu_s[...] = wu_f[ws].astype(jnp.bfloat16)
            wd_s[...] = wd_f[ws].astype(jnp.bfloat16)
            wslot_ref[0] = 1 - ws

            @pl.when(nxt_ref[e] >= 0)
            def _():
                for cp in weight_copies(nxt_ref[e], 1 - ws):
                    cp.start()

        slot = b % 2
        nv = nvalid_ref[b]
        pltpu.make_async_copy(xs_hbm.at[pl.ds(0, nv)], xbuf.at[slot, pl.ds(0, nv)], sem.at[slot]).wait()

        x = _unpack_rows(xbuf[slot])
        g = _dot(x, wg_s[...])
        u = _dot(x, wu_s[...])
        act = (g / (1.0 + jnp.exp(-g)) * u).astype(jnp.bfloat16)
        y = _dot(act, wd_s[...])
        y_ref[...] = _pack_rows(y)

    @pl.when(b >= nused)
    def _():
        y_ref[...] = _packed_zero_rows(MOE_BLOCK)


def _experts(tabs, xs, w_gate, w_up, w_down):
    n_slots = tabs["n_slots"]
    n_blocks = n_slots // MOE_BLOCK
    f32, bf16 = jnp.float32, jnp.bfloat16
    hbm = pl.BlockSpec(memory_space=pl.ANY)
    return pl.pallas_call(
        _expert_kernel,
        grid_spec=pltpu.PrefetchScalarGridSpec(
            num_scalar_prefetch=10,
            grid=(n_blocks,),
            in_specs=[hbm, hbm, hbm, hbm],
            out_specs=pl.BlockSpec((MOE_BLOCK, ROW_CHUNKS, LANES), lambda b, *_: (b, 0, 0)),
            scratch_shapes=[pltpu.VMEM((2, MOE_BLOCK, ROW_CHUNKS, LANES), PACKED_DTYPE),
                            pltpu.VMEM((2, D_MODEL, D_FF), f32),
                            pltpu.VMEM((2, D_MODEL, D_FF), f32),
                            pltpu.VMEM((2, D_FF, D_MODEL), f32),
                            pltpu.VMEM((D_MODEL, D_FF), bf16),
                            pltpu.VMEM((D_MODEL, D_FF), bf16),
                            pltpu.VMEM((D_FF, D_MODEL), bf16),
                            pltpu.SMEM((1,), jnp.int32),
                            pltpu.SemaphoreType.DMA((2,)),
                            pltpu.SemaphoreType.DMA((2,))],
        ),
        out_shape=jax.ShapeDtypeStruct((n_slots, ROW_CHUNKS, LANES), PACKED_DTYPE),
        compiler_params=pltpu.CompilerParams(
            dimension_semantics=("arbitrary",), vmem_limit_bytes=VMEM_LIMIT),
        name="experts",
    )(tabs["block_e"], tabs["nused"], tabs["r0"], tabs["nvalid"], tabs["tlo"], tabs["thi"],
      tabs["cnt"], tabs["src"], tabs["rbase"], tabs["nxt"], xs, w_gate, w_up, w_down)


def _combine_kernel(cnt_ref, loc_ref, dst_ref, h_ref, meta_ref, yg_hbm, o_ref, ybuf, sem, *, n_steps, nsub):
    i = pl.program_id(0)
    t = T_SORT
    f32, bf16 = jnp.float32, jnp.bfloat16

    slot_in = i % 2
    tile0 = jnp.minimum(i, n_steps - 1) * nsub
    for s in range(nsub):
        for e in range(N_EXPERTS):
            k = (tile0 + s) * N_EXPERTS + e
            rows = jnp.where(i < n_steps, cnt_ref[k], 0)

            @pl.when(rows > 0)
            def _():
                pltpu.make_async_copy(yg_hbm.at[pl.ds(dst_ref[k], rows)],
                                      ybuf.at[slot_in, pl.ds(s * 2 * t + loc_ref[k], rows)],
                                      sem.at[slot_in]).start()

    @pl.when(i > 0)
    def _():
        slot = (i - 1) % 2
        pltpu.make_async_copy(yg_hbm.at[pl.ds(0, nsub * 2 * t)], ybuf.at[slot], sem.at[slot]).wait()
        lane = lax.broadcasted_iota(jnp.int32, (t, 2 * t), 1).astype(f32)
        for s in range(nsub):
            y = _unpack_rows(ybuf[slot, s * 2 * t:(s + 1) * 2 * t])
            meta = meta_ref[s * t:(s + 1) * t, :]
            pick1 = jnp.where(lane == meta[:, 0:1], 1.0, 0.0).astype(bf16)
            pick2 = jnp.where(lane == meta[:, 1:2], 1.0, 0.0).astype(bf16)
            y1 = _dot(pick1, y)
            y2 = _dot(pick2, y)
            o_ref[s * t:(s + 1) * t, :] = h_ref[s * t:(s + 1) * t, :] + (meta[:, 2:3] * y1 + meta[:, 3:4] * y2)


def _combine(cnt, loc, dst, h, meta, yg):
    n = h.shape[0]
    tm = TM_OUT
    nsub = tm // T_SORT
    n_steps = n // tm
    prev = lambda i, *_: (jnp.maximum(i - 1, 0), 0)
    return pl.pallas_call(
        functools.partial(_combine_kernel, n_steps=n_steps, nsub=nsub),
        grid_spec=pltpu.PrefetchScalarGridSpec(
            num_scalar_prefetch=3,
            grid=(n_steps + 1,),
            in_specs=[
                pl.BlockSpec((tm, D_MODEL), prev),
                pl.BlockSpec((tm, LANES), prev),
                pl.BlockSpec(memory_space=pl.ANY),
            ],
            out_specs=pl.BlockSpec((tm, D_MODEL), prev),
            scratch_shapes=[pltpu.VMEM((2, nsub * 2 * T_SORT, ROW_CHUNKS, LANES), PACKED_DTYPE),
                            pltpu.SemaphoreType.DMA((2,))],
        ),
        out_shape=jax.ShapeDtypeStruct((n, D_MODEL), jnp.float32),
        compiler_params=pltpu.CompilerParams(
            dimension_semantics=("arbitrary",), vmem_limit_bytes=VMEM_LIMIT),
        name="combine",
    )(cnt, loc, dst, h, meta, yg)


def _routing_tables(cnt_out, n_tiles, n_tok):
    i32 = jnp.int32
    cnt = cnt_out.reshape(n_tiles, N_EXPERTS, LANES)[:, :, 0].astype(i32)
    count = jnp.sum(cnt, axis=0)
    padded = ((count + MOE_BLOCK - 1) // MOE_BLOCK) * MOE_BLOCK
    pad_end = jnp.cumsum(padded)
    pad_start = pad_end - padded
    run_end = jnp.cumsum(cnt, axis=0)
    rbase = run_end - cnt
    dst = pad_start[None, :] + rbase
    loc = jnp.cumsum(cnt, axis=1) - cnt
    src = loc + (jnp.arange(n_tiles, dtype=i32) * (2 * T_SORT))[:, None]
    n_blocks = (2 * n_tok) // MOE_BLOCK + N_EXPERTS
    block_start = jnp.arange(n_blocks, dtype=i32) * MOE_BLOCK
    block_e = jnp.minimum(jnp.sum(pad_end[None, :] <= block_start[:, None], axis=1), N_EXPERTS - 1).astype(i32)
    nused = (pad_end[-1] // MOE_BLOCK).astype(i32).reshape(1)
    ex = jnp.arange(N_EXPERTS, dtype=i32)
    sel = (block_e[None, :] == ex[:, None]).astype(i32)
    pick = lambda per_expert: jnp.sum(per_expert[..., :, None] * sel, axis=-2)
    r0 = block_start - pick(pad_start)
    nvalid = jnp.clip(pick(count) - r0, 0, MOE_BLOCK)
    tlo = jnp.sum(pick(run_end) <= r0[None, :], axis=0)
    thi = jnp.sum(pick(rbase) < (r0 + MOE_BLOCK)[None, :], axis=0)
    later_used = (ex[None, :] > ex[:, None]) & (count[None, :] > 0)
    nxt = jnp.min(jnp.where(later_used, ex[None, :], N_EXPERTS), axis=1)
    nxt = jnp.where(nxt < N_EXPERTS, nxt, -1)
    flat = lambda a: a.reshape(-1).astype(i32)
    return dict(cnt=flat(cnt), src=flat(src), dst=flat(dst), loc=flat(loc), rbase=flat(rbase),
                block_e=block_e, nused=nused, r0=flat(r0), nvalid=flat(nvalid), tlo=flat(tlo), thi=flat(thi),
                nxt=flat(nxt), n_slots=n_blocks * MOE_BLOCK)


def _stage1(x, positions, attn_norm_gain, w_in, q_norm_gain, k_norm_gain, conv_w, conv_out_gain):
    bsz, seq, _ = x.shape
    n = bsz * seq
    f32, bf16 = jnp.float32, jnp.bfloat16
    w = w_in[0]
    wqk = w[:, :2 * ATTN_WIDTH].astype(bf16)
    wvt = w[:, 2 * ATTN_WIDTH:3 * ATTN_WIDTH].T.astype(bf16)
    wc = w[:, 3 * ATTN_WIDTH:].astype(bf16)
    scale = DK ** -0.5 * LOG2E
    gqk = jnp.concatenate([jnp.tile(q_norm_gain[0].astype(f32), 2 * N_HEADS) * scale,
                           jnp.tile(k_norm_gain[0].astype(f32), 2 * N_HEADS)]).reshape(1, -1)
    freqs = (ROPE_THETA ** (-jnp.arange(0, ROT_DIM, 2, dtype=f32) / ROT_DIM)).reshape(SUBLANES, 1)
    return _inproj(x.reshape(n, D_MODEL), positions.reshape(1, n),
                   attn_norm_gain[0].reshape(1, -1).astype(f32), wqk, wvt, wc, gqk, freqs,
                   conv_w[0].astype(f32), conv_out_gain[0].reshape(1, -1).astype(f32), seq)


def kernel(x, positions, attn_norm_gain, w_in, q_norm_gain, k_norm_gain, lambda_q1, lambda_k1, lambda_q2, lambda_k2, subln_gain, conv_w, conv_out_gain, w_out, ffn_norm_gain, w_group_router, w_expert_router, w_gate, w_up, w_down):
    bsz, seq, _ = x.shape
    n = bsz * seq
    f32, bf16 = jnp.float32, jnp.bfloat16
    assert TQ % TM_OUT == 0 and TM_OUT % T_SORT == 0 and seq % TM_IN == 0 and (seq // TQ) % 2 == 0
    q, k, vt, conv_o = _stage1(x, positions, attn_norm_gain, w_in, q_norm_gain, k_norm_gain,
                                conv_w, conv_out_gain)
    attn_p = _attention(q, k, vt,
                        lambda_q1[0].reshape(1, -1).astype(f32), lambda_k1[0].reshape(1, -1).astype(f32),
                        lambda_q2[0].reshape(1, -1).astype(f32), lambda_k2[0].reshape(1, -1).astype(f32),
                        subln_gain[0].reshape(-1, 1).astype(f32), bsz, seq)

    wr = jnp.zeros((D_MODEL, ROUTER_ROWS), f32)
    wr = wr.at[:, 0:N_GROUPS].set(w_group_router[0].astype(f32))
    wr = wr.at[:, EXPERT_ROW0:EXPERT_ROW0 + N_EXPERTS].set(
        jnp.transpose(w_expert_router[0].astype(f32), (1, 0, 2)).reshape(D_MODEL, N_EXPERTS))
    wrh = wr.astype(bf16)
    wr2 = jnp.concatenate([wrh, (wr - wrh.astype(f32)).astype(bf16)], axis=1)
    h, xs, meta, cnt_out = _outproj(x.reshape(n, D_MODEL), attn_p, conv_o, w_out[0].astype(bf16),
                                    ffn_norm_gain[0].reshape(1, -1).astype(f32), wr2, seq)

    tabs = _routing_tables(cnt_out, n // T_SORT, n)
    yg = _experts(tabs, xs, w_gate[0], w_up[0], w_down[0])
    out = _combine(tabs["cnt"], tabs["loc"], tabs["dst"], h, meta, yg)
    return out.reshape(x.shape)
```

```python
import functools
import math

import jax
import jax.numpy as jnp
from jax import lax
from jax.experimental import pallas as pl
from jax.experimental.pallas import tpu as pltpu

D_MODEL = 1024
N_HEADS = 4
DK = 64
DV = 128
ROT_DIM = 16
ROPE_THETA = 500000.0
ATTN_WIDTH = N_HEADS * DV
CONV_WIDTH = 512
NORM_EPS = 1e-6
LOG2E = 1.4426950408889634
LAMBDA_INIT = 0.8 - 0.6 * math.exp(-0.3 * 0)
N_GROUPS = 4
EPG = 8
N_EXPERTS = N_GROUPS * EPG
D_FF = 512
MOE_BLOCK = 256

LANES = 128
SUBLANES = 8
ROW_WORDS = D_MODEL // 2
ROW_CHUNKS = ROW_WORDS // LANES
PACKED_DTYPE = jnp.uint32

TM_IN = 512
TQ = 512
HQ = TQ // 2
T_SORT = 256
TM_OUT = 512
N_XBUF = 3
GATHER_UNROLL = 24
VMEM_LIMIT = 48 * 1024 * 1024


def _nt_dot(a, b):
    return lax.dot_general(a, b, (((1,), (1,)), ((), ())), preferred_element_type=jnp.float32)


def _dot(a, b):
    return jnp.dot(a, b, preferred_element_type=jnp.float32)


def _split3(x):
    h = x.astype(jnp.bfloat16)
    r = x - h.astype(jnp.float32)
    m = r.astype(jnp.bfloat16)
    l = (r - m.astype(jnp.float32)).astype(jnp.bfloat16)
    return h, m, l


def _split2(x):
    h = x.astype(jnp.bfloat16)
    l = (x - h.astype(jnp.float32)).astype(jnp.bfloat16)
    return h, l


def _inproj_kernel(x_ref, pos_ref, g1_ref, wqk_ref, wvt_ref, wc_ref, gqk_ref, freq_ref,
                   cw_ref, cg_ref,
                   q_ref, k_ref, vt_ref, conv_ref,
                   carry_ref, *, tiles_per_seq):
    tm = x_ref.shape[0]
    i = pl.program_id(0)

    x = x_ref[...]
    ms = jnp.mean(x * x, axis=-1, keepdims=True)
    hn = (x * lax.rsqrt(ms + NORM_EPS) * g1_ref[...]).astype(jnp.bfloat16)

    pos = pos_ref[...].astype(jnp.float32)
    ang = freq_ref[...] * pos
    lane_r = lax.broadcasted_iota(jnp.int32, (LANES, SUBLANES), 0)
    f_c = lax.broadcasted_iota(jnp.int32, (LANES, SUBLANES), 1)
    in_rot = (lane_r % DK) < ROT_DIM
    expand = jnp.where(in_rot & ((lane_r % (ROT_DIM // 2)) == f_c), 1.0, 0.0).astype(jnp.bfloat16)

    def to_rows(t):
        h, m, l = _split3(t)
        r = _dot(expand, h) + _dot(expand, m) + _dot(expand, l)
        return r.T

    cos_r = to_rows(jnp.cos(ang))
    sin_r = to_rows(jnp.sin(ang))
    lane = lax.broadcasted_iota(jnp.int32, (tm, LANES), 1)
    d = lane % DK
    cos_r = jnp.where(d < ROT_DIM, cos_r, 1.0)
    sin_lo = jnp.where(d < ROT_DIM // 2, -sin_r, 0.0)
    sin_hi = jnp.where((d >= ROT_DIM // 2) & (d < ROT_DIM), sin_r, 0.0)

    qk = _dot(hn, wqk_ref[...])
    seg_r = lax.broadcasted_iota(jnp.int32, (2 * LANES, 2 * LANES), 0) // DK
    seg_c = lax.broadcasted_iota(jnp.int32, (2 * LANES, 2 * LANES), 1) // DK
    seg_mean = jnp.where(seg_r == seg_c, 1.0 / DK, 0.0).astype(jnp.bfloat16)
    half = ROT_DIM // 2
    for c2 in range(4):
        blk = qk[:, c2 * 256:(c2 + 1) * 256]
        msq = _dot((blk * blk).astype(jnp.bfloat16), seg_mean)
        y2 = blk * lax.rsqrt(msq + NORM_EPS) * gqk_ref[:, c2 * 256:(c2 + 1) * 256]
        for c1 in range(2):
            c = c2 * 2 + c1
            y = y2[:, c1 * LANES:(c1 + 1) * LANES]
            rot = (y * cos_r
                   + pltpu.roll(y, LANES - half, 1) * sin_lo
                   + pltpu.roll(y, half, 1) * sin_hi)
            if c < N_HEADS:
                q_ref[:, c * LANES:(c + 1) * LANES] = rot.astype(jnp.bfloat16)
            else:
                h = c - N_HEADS
                k_ref[:, h * LANES:(h + 1) * LANES] = rot.astype(jnp.bfloat16)

    vt_ref[...] = _nt_dot(wvt_ref[...], hn).astype(jnp.bfloat16)

    cp = _dot(hn, wc_ref[...])
    cb = cp[:, :CONV_WIDTH]
    y = cp[:, CONV_WIDTH:2 * CONV_WIDTH] * cp[:, 2 * CONV_WIDTH:]

    @pl.when(i % tiles_per_seq == 0)
    def _():
        carry_ref[...] = jnp.zeros_like(carry_ref)

    prev = carry_ref[...]
    row = lax.broadcasted_iota(jnp.int32, (tm, CONV_WIDTH), 0)
    p1 = prev[SUBLANES - 1:SUBLANES, :]
    p2 = prev[SUBLANES - 2:SUBLANES - 1, :]
    y1 = jnp.where(row == 0, p1, pltpu.roll(y, 1, 0))
    y2 = jnp.where(row == 0, p2, jnp.where(row == 1, p1, pltpu.roll(y, 2, 0)))
    carry_ref[...] = y[tm - SUBLANES:, :]
    z = cw_ref[0:1, :] * y2 + cw_ref[1:2, :] * y1 + cw_ref[2:3, :] * y
    co = cb * z
    cms = jnp.mean(co * co, axis=-1, keepdims=True)
    conv_ref[...] = (co * lax.rsqrt(cms + NORM_EPS) * cg_ref[...]).astype(jnp.bfloat16)


def _inproj(x2, pos_row, g1, wqk, wvt, wc, gqk, freqs, cw, cg, seq):
    n = x2.shape[0]
    tm = TM_IN
    grid = (n // tm,)
    const = lambda i: (0, 0)
    return pl.pallas_call(
        functools.partial(_inproj_kernel, tiles_per_seq=seq // tm),
        grid=grid,
        in_specs=[
            pl.BlockSpec((tm, D_MODEL), lambda i: (i, 0)),
            pl.BlockSpec((1, tm), lambda i: (0, i)),
            pl.BlockSpec((1, D_MODEL), const),
            pl.BlockSpec((D_MODEL, 1024), const),
            pl.BlockSpec((512, D_MODEL), const),
            pl.BlockSpec((D_MODEL, 1536), const),
            pl.BlockSpec((1, 1024), const),
            pl.BlockSpec((SUBLANES, 1), const),
            pl.BlockSpec((3, CONV_WIDTH), const),
            pl.BlockSpec((1, CONV_WIDTH), const),
        ],
        out_specs=[
            pl.BlockSpec((tm, ATTN_WIDTH), lambda i: (i, 0)),
            pl.BlockSpec((tm, ATTN_WIDTH), lambda i: (i, 0)),
            pl.BlockSpec((ATTN_WIDTH, tm), lambda i: (0, i)),
            pl.BlockSpec((tm, CONV_WIDTH), lambda i: (i, 0)),
        ],
        out_shape=[
            jax.ShapeDtypeStruct((n, ATTN_WIDTH), jnp.bfloat16),
            jax.ShapeDtypeStruct((n, ATTN_WIDTH), jnp.bfloat16),
            jax.ShapeDtypeStruct((ATTN_WIDTH, n), jnp.bfloat16),
            jax.ShapeDtypeStruct((n, CONV_WIDTH), jnp.bfloat16),
        ],
        scratch_shapes=[pltpu.VMEM((SUBLANES, CONV_WIDTH), jnp.float32)],
        compiler_params=pltpu.CompilerParams(
            dimension_semantics=("arbitrary",), vmem_limit_bytes=VMEM_LIMIT),
        name="inproj",
    )(x2, pos_row, g1, wqk, wvt, wc, gqk, freqs, cw, cg)


def _attn_kernel(qa_ref, qb_ref, k_ref, vt_ref, lq1_ref, lk1_ref, lq2_ref, lk2_ref, sg_ref,
                 o_ref, acc_ref, *, nq):
    c = pl.program_id(2)
    lam = (jnp.exp(jnp.sum(lq1_ref[...] * lk1_ref[...], axis=-1, keepdims=True))
           - jnp.exp(jnp.sum(lq2_ref[...] * lk2_ref[...], axis=-1, keepdims=True))
           + LAMBDA_INIT)
    key_i = lax.broadcasted_iota(jnp.int32, (HQ, 4 * HQ), 0)
    col_i = lax.broadcasted_iota(jnp.int32, (HQ, 4 * HQ), 1)
    tri = key_i <= col_i % HQ
    mask_mixed = tri | (col_i >= 2 * HQ)
    mask_upper = tri[:, :2 * HQ]
    row_d = lax.broadcasted_iota(jnp.int32, (LANES, HQ), 0)

    def branch(tiles):
        qzt = []
        for q_ref, _ in tiles:
            parts = []
            for half in range(2):
                qt = q_ref[half * HQ:(half + 1) * HQ, :].astype(jnp.float32).T
                parts.append(jnp.where(row_d < DK, qt, 0.0))
                parts.append(jnp.where(row_d >= DK, qt, 0.0))
            qzt.append(jnp.concatenate(parts, axis=1).astype(jnp.bfloat16))

        todo = {}
        for ti, (_, t) in enumerate(tiles):
            nb = 2 * t
            todo[ti] = [(nb, "mixed"), (nb + 1, "upper")] + [(j, "full") for j in range(nb)]
        tasks = []
        while any(todo.values()):
            for ti in range(len(tiles)):
                if todo[ti]:
                    tasks.append((ti,) + todo[ti].pop(0))
        m_run, l_run = {}, {}

        def scores(task):
            ti, j, kind = task
            kb = k_ref[j * HQ:(j + 1) * HQ, :]
            if kind == "upper":
                return jnp.where(mask_upper, _dot(kb, qzt[ti][:, 2 * HQ:]), -jnp.inf)
            s = _dot(kb, qzt[ti])
            return jnp.where(mask_mixed, s, -jnp.inf) if kind == "mixed" else s

        def softmax(task, s):
            ti, j, kind = task
            mx = jnp.max(s, axis=0, keepdims=True)
            if kind == "mixed":
                p = jnp.exp2(s - mx)
                m_run[ti], l_run[ti] = mx, jnp.sum(p, axis=0, keepdims=True)
                return p.astype(jnp.bfloat16), None
            m_old = m_run[ti][:, 2 * HQ:] if kind == "upper" else m_run[ti]
            l_old = l_run[ti][:, 2 * HQ:] if kind == "upper" else l_run[ti]
            m_new = jnp.maximum(m_old, mx)
            alpha = jnp.exp2(m_old - m_new)
            p = jnp.exp2(s - m_new)
            l_new = alpha * l_old + jnp.sum(p, axis=0, keepdims=True)
            if kind == "upper":
                m_new = jnp.concatenate([m_run[ti][:, :2 * HQ], m_new], axis=1)
                l_new = jnp.concatenate([l_run[ti][:, :2 * HQ], l_new], axis=1)
            m_run[ti], l_run[ti] = m_new, l_new
            return p.astype(jnp.bfloat16), alpha

        def accumulate(task, p, alpha):
            ti, j, kind = task
            pv = _dot(vt_ref[:, j * HQ:(j + 1) * HQ], p)
            if kind == "mixed":
                acc_ref[ti] = pv
            elif kind == "upper":
                acc_ref[ti, :, 2 * HQ:] = alpha * acc_ref[ti, :, 2 * HQ:] + pv
            else:
                acc_ref[ti] = alpha * acc_ref[ti] + pv

        n = len(tasks)
        s_prev, p_prev = None, None
        for step in range(n + 2):
            if step >= 2:
                accumulate(tasks[step - 2], *p_prev)
            if 1 <= step <= n:
                p_prev = softmax(tasks[step - 1], s_prev)
            if step < n:
                s_prev = scores(tasks[step])

        for ti in range(len(tiles)):
            o_all = acc_ref[ti] / l_run[ti]
            for half in range(2):
                o = (o_all[:, (2 * half) * HQ:(2 * half + 1) * HQ]
                     - lam * o_all[:, (2 * half + 1) * HQ:(2 * half + 2) * HQ])
                ms = jnp.mean(o * o, axis=0, keepdims=True)
                on = o * lax.rsqrt(ms + NORM_EPS) * sg_ref[...] * (1.0 - LAMBDA_INIT)
                r0 = ti * TQ + half * HQ
                o_ref[r0:r0 + HQ, :] = on.T.astype(jnp.bfloat16)

    for cc in range(nq // 2):
        @pl.when(c == cc)
        def _():
            branch([(qa_ref, cc), (qb_ref, nq - 1 - cc)])


def _attention(q, k, vt, lq1, lk1, lq2, lk2, sg_col, bsz, seq):
    n = q.shape[0]
    nq = seq // TQ
    assert nq % 2 == 0
    vec = lambda b, h, c: (0, 0)
    return pl.pallas_call(
        functools.partial(_attn_kernel, nq=nq),
        grid=(bsz, N_HEADS, nq // 2),
        in_specs=[
            pl.BlockSpec((TQ, DV), lambda b, h, c: (b * nq + c, h)),
            pl.BlockSpec((TQ, DV), lambda b, h, c: (b * nq + nq - 1 - c, h)),
            pl.BlockSpec((seq, DV), lambda b, h, c: (b, h)),
            pl.BlockSpec((DV, seq), lambda b, h, c: (h, b)),
            pl.BlockSpec((1, DK), vec), pl.BlockSpec((1, DK), vec),
            pl.BlockSpec((1, DK), vec), pl.BlockSpec((1, DK), vec),
            pl.BlockSpec((DV, 1), vec),
        ],
        out_specs=pl.BlockSpec((2 * TQ, DV), lambda b, h, c: (b * (nq // 2) + c, h)),
        out_shape=jax.ShapeDtypeStruct((n, ATTN_WIDTH), jnp.bfloat16),
        scratch_shapes=[pltpu.VMEM((2, DV, 4 * HQ), jnp.float32)],
        compiler_params=pltpu.CompilerParams(
            dimension_semantics=("arbitrary", "arbitrary", "arbitrary"), vmem_limit_bytes=VMEM_LIMIT),
        name="attn",
    )(q, q, k, vt, lq1, lk1, lq2, lk2, sg_col)


def _attn_out_block(i, seq, rows):
    per_seq = seq // rows
    nq = seq // TQ
    b, r = i // per_seq, i % per_seq
    t, sub = r // (TQ // rows), r % (TQ // rows)
    pos = jnp.where(t < nq // 2, 2 * t, 2 * (nq - 1 - t) + 1)
    return b * per_seq + pos * (TQ // rows) + sub


ROUTER_ROWS = 128
EXPERT_ROW0 = 32


def _outproj_kernel(x_ref, attn_ref, conv_ref, wo_ref, g2_ref, wr2_ref,
                    h_ref, xs_ref, meta_ref, cnt_ref):
    tm = x_ref.shape[0]
    t = T_SORT
    f32, bf16 = jnp.float32, jnp.bfloat16
    a = jnp.concatenate([attn_ref[...], conv_ref[...]], axis=1)
    h = x_ref[...] + _dot(a, wo_ref[...])
    h_ref[...] = h
    ms = jnp.mean(h * h, axis=-1, keepdims=True)
    hn = h * lax.rsqrt(ms + NORM_EPS) * g2_ref[...]
    hn_hi, hn_lo = _split2(hn)

    hh = _dot(hn_hi, wr2_ref[...])
    logits = hh[:, :ROUTER_ROWS] + hh[:, ROUTER_ROWS:] + _dot(hn_lo, wr2_ref[:, :ROUTER_ROWS])
    lt = logits.T
    row8 = lax.broadcasted_iota(jnp.int32, (SUBLANES, tm), 0).astype(f32)
    neg_inf = -jnp.inf

    def first_argmax(v):
        mx = jnp.max(v, axis=0, keepdims=True)
        idx = jnp.min(jnp.where(v == mx, row8, float(SUBLANES)), axis=0, keepdims=True)
        return mx, idx

    g_log = jnp.where(row8 < N_GROUPS, lt[0:SUBLANES, :], neg_inf)
    g_max, g_sel = first_argmax(g_log)
    g_gate = 1.0 / jnp.sum(jnp.exp(g_log - g_max), axis=0, keepdims=True)
    e_log = jnp.zeros((EPG, tm), f32)
    for g in range(N_GROUPS):
        rows = lt[EXPERT_ROW0 + g * EPG:EXPERT_ROW0 + (g + 1) * EPG, :]
        e_log = jnp.where(g_sel == float(g), rows, e_log)
    v1, i1 = first_argmax(e_log)
    v2, i2 = first_argmax(jnp.where(row8 == i1, neg_inf, e_log))
    tt = jnp.exp(v2 - v1)
    w1 = g_gate / (1.0 + tt)
    w2 = g_gate * tt / (1.0 + tt)
    e1 = g_sel * float(EPG) + i1
    e2 = g_sel * float(EPG) + i2

    row32 = lax.broadcasted_iota(jnp.int32, (N_EXPERTS, tm), 0).astype(f32)
    oh1 = row32 == e1
    oh2 = row32 == e2
    c = jnp.where(oh1 | oh2, 1.0, 0.0).astype(bf16)
    tok_r = lax.broadcasted_iota(jnp.int32, (tm, tm), 0)
    tok_c = lax.broadcasted_iota(jnp.int32, (tm, tm), 1)
    same_tile = (tok_r // t) == (tok_c // t)
    rank = _dot(c, jnp.where(same_tile & (tok_r < tok_c), 1.0, 0.0).astype(bf16))
    cnt_b = _dot(c, jnp.where(same_tile, 1.0, 0.0).astype(bf16))
    ex_r = lax.broadcasted_iota(jnp.int32, (N_EXPERTS, N_EXPERTS), 0)
    ex_c = lax.broadcasted_iota(jnp.int32, (N_EXPERTS, N_EXPERTS), 1)
    lower = jnp.where(ex_c < ex_r, 1.0, 0.0).astype(bf16)
    start_b = _dot(lower, cnt_b.astype(bf16))
    pos_e = start_b + rank
    p1 = jnp.sum(jnp.where(oh1, pos_e, 0.0), axis=0, keepdims=True)
    p2 = jnp.sum(jnp.where(oh2, pos_e, 0.0), axis=0, keepdims=True)

    srow = lax.broadcasted_iota(jnp.int32, (2 * t, t), 0).astype(f32)
    for s in range(tm // t):
        cols = slice(s * t, (s + 1) * t)
        perm = jnp.where((srow == p1[:, cols]) | (srow == p2[:, cols]), 1.0, 0.0).astype(bf16)
        xs = _dot(perm, hn_hi[cols, :])
        xs_ref[s * 2 * t:(s + 1) * 2 * t] = _pack_rows(xs)
        cnt_ref[s * N_EXPERTS:(s + 1) * N_EXPERTS, :] = cnt_b[:, s * t:s * t + LANES]

    meta = jnp.concatenate([p1, p2, w1, w2, jnp.zeros((LANES - 4, tm), f32)], axis=0)
    meta_ref[...] = meta.T


def _outproj(x2, attn_p, conv_o, wo, g2, wr2, seq):
    n = x2.shape[0]
    tm = TM_OUT
    nsub = tm // T_SORT
    const = lambda i: (0, 0)

    def attn_map(i):
        return (_attn_out_block(i, seq, tm), 0)

    return pl.pallas_call(
        _outproj_kernel,
        grid=(n // tm,),
        in_specs=[
            pl.BlockSpec((tm, D_MODEL), lambda i: (i, 0)),
            pl.BlockSpec((tm, ATTN_WIDTH), attn_map),
            pl.BlockSpec((tm, CONV_WIDTH), lambda i: (i, 0)),
            pl.BlockSpec((D_MODEL, D_MODEL), const),
            pl.BlockSpec((1, D_MODEL), const),
            pl.BlockSpec((D_MODEL, 2 * ROUTER_ROWS), const),
        ],
        out_specs=[
            pl.BlockSpec((tm, D_MODEL), lambda i: (i, 0)),
            pl.BlockSpec((2 * tm, ROW_CHUNKS, LANES), lambda i: (i, 0, 0)),
            pl.BlockSpec((tm, LANES), lambda i: (i, 0)),
            pl.BlockSpec((nsub * N_EXPERTS, LANES), lambda i: (i, 0)),
        ],
        out_shape=[
            jax.ShapeDtypeStruct((n, D_MODEL), jnp.float32),
            jax.ShapeDtypeStruct((2 * n, ROW_CHUNKS, LANES), PACKED_DTYPE),
            jax.ShapeDtypeStruct((n, LANES), jnp.float32),
            jax.ShapeDtypeStruct((n // T_SORT * N_EXPERTS, LANES), jnp.float32),
        ],
        compiler_params=pltpu.CompilerParams(
            dimension_semantics=("arbitrary",), vmem_limit_bytes=VMEM_LIMIT),
        name="outproj",
    )(x2, attn_p, conv_o, wo, g2, wr2)


def _pack_rows(x):
    r = x.shape[0]
    w = pltpu.pack_elementwise([x[:, :ROW_WORDS], x[:, ROW_WORDS:]], packed_dtype=jnp.bfloat16)
    return pltpu.bitcast(w, PACKED_DTYPE).reshape(r, ROW_CHUNKS, LANES)


def _packed_zero_rows(r):
    z = jnp.zeros((r, ROW_CHUNKS, LANES), jnp.float32)
    w = pltpu.pack_elementwise([z, z], packed_dtype=jnp.bfloat16)
    return pltpu.bitcast(w, PACKED_DTYPE)


def _unpack_rows(u):
    r = u.shape[0]
    w = u.reshape(r, ROW_WORDS)
    lo = pltpu.unpack_elementwise(w, index=0, packed_dtype=jnp.bfloat16, unpacked_dtype=jnp.float32)
    hi = pltpu.unpack_elementwise(w, index=1, packed_dtype=jnp.bfloat16, unpacked_dtype=jnp.float32)
    return jnp.concatenate([lo, hi], axis=1).astype(jnp.bfloat16)


def _expert_kernel(be_ref, nused_ref, r0_ref, nvalid_ref, tlo_ref, thi_ref, cnt_ref, src_ref, rbase_ref,
                   nxt_ref, xs_hbm, wg_hbm, wu_hbm, wd_hbm, y_ref,
                   xbuf, wg_f, wu_f, wd_f, wg_s, wu_s, wd_s, wslot_ref, sem, wsem):
    b = pl.program_id(0)
    nused = nused_ref[0]
    n_blocks = be_ref.shape[0]
    n_tiles = cnt_ref.shape[0] // N_EXPERTS

    def start_run(blk, slot, t, live):
        r0 = r0_ref[blk]
        k = t * N_EXPERTS + be_ref[blk]
        lo = jnp.maximum(rbase_ref[k], r0)
        hi = jnp.minimum(rbase_ref[k] + cnt_ref[k], r0 + MOE_BLOCK)
        rows = jnp.where(live, hi - lo, 0)

        @pl.when(rows > 0)
        def _():
            pltpu.make_async_copy(xs_hbm.at[pl.ds(src_ref[k] + lo - rbase_ref[k], rows)],
                                  xbuf.at[slot, pl.ds(lo - r0, rows)], sem.at[slot]).start()

    def gather_rolled(blk, slot, t_from):
        def run(t, carry):
            start_run(blk, slot, t, True)
            return carry
        lax.fori_loop(t_from, thi_ref[blk], run, 0)

    def gather_unrolled(blk, slot, live):
        blk = jnp.minimum(blk, n_blocks - 1)
        for r in range(GATHER_UNROLL):
            t = tlo_ref[blk] + r
            start_run(blk, slot, jnp.minimum(t, n_tiles - 1), live & (t < thi_ref[blk]))

        @pl.when(live & (tlo_ref[blk] + GATHER_UNROLL < thi_ref[blk]))
        def _():
            gather_rolled(blk, slot, tlo_ref[blk] + GATHER_UNROLL)

    def weight_copies(e, slot):
        return [pltpu.make_async_copy(src.at[e], dst.at[slot], wsem.at[slot])
                for src, dst in ((wg_hbm, wg_f), (wu_hbm, wu_f), (wd_hbm, wd_f))]

    @pl.when(b == 0)
    def _():
        for slot in range(N_XBUF):
            xbuf[slot] = _packed_zero_rows(MOE_BLOCK)
        gather_rolled(0, 0, tlo_ref[0])

        @pl.when(nused > 1)
        def _():
            gather_rolled(1, 1, tlo_ref[1])
        wslot_ref[0] = 0
        for cp in weight_copies(be_ref[0], 0):
            cp.start()

    @pl.when(b < nused)
    def _():
        e = be_ref[b]

        @pl.when((b == 0) | (e != be_ref[jnp.maximum(b - 1, 0)]))
        def _():
            ws = wslot_ref[0]
            for cp in weight_copies(e, ws):
                cp.wait()
            wg_s[...] = wg_f[ws].astype(jnp.bfloat16)
            wu_s[...] = wu_f[ws].astype(jnp.bfloat16)
            wd_s[...] = wd_f[ws].astype(jnp.bfloat16)
            wslot_ref[0] = 1 - ws

            @pl.when(nxt_ref[e] >= 0)
            def _():
                for cp in weight_copies(nxt_ref[e], 1 - ws):
                    cp.start()

        slot = b % N_XBUF
        nv = nvalid_ref[b]
        pltpu.make_async_copy(xs_hbm.at[pl.ds(0, nv)], xbuf.at[slot, pl.ds(0, nv)], sem.at[slot]).wait()

        x = _unpack_rows(xbuf[slot])
        g = _dot(x, wg_s[...])
        u = _dot(x, wu_s[...])
        act = (g / (1.0 + jnp.exp(-g)) * u).astype(jnp.bfloat16)
        y = _dot(act, wd_s[...])
        y_ref[...] = _pack_rows(y)

        gather_unrolled(b + 2, (b + 2) % N_XBUF, b + 2 < nused)

    @pl.when(b >= nused)
    def _():
        y_ref[...] = _packed_zero_rows(MOE_BLOCK)


def _experts(tabs, xs, w_gate, w_up, w_down):
    n_slots = tabs["n_slots"]
    n_blocks = n_slots // MOE_BLOCK
    f32, bf16 = jnp.float32, jnp.bfloat16
    hbm = pl.BlockSpec(memory_space=pl.ANY)
    return pl.pallas_call(
        _expert_kernel,
        grid_spec=pltpu.PrefetchScalarGridSpec(
            num_scalar_prefetch=10,
            grid=(n_blocks,),
            in_specs=[hbm, hbm, hbm, hbm],
            out_specs=pl.BlockSpec((MOE_BLOCK, ROW_CHUNKS, LANES), lambda b, *_: (b, 0, 0)),
            scratch_shapes=[pltpu.VMEM((N_XBUF, MOE_BLOCK, ROW_CHUNKS, LANES), PACKED_DTYPE),
                            pltpu.VMEM((2, D_MODEL, D_FF), f32),
                            pltpu.VMEM((2, D_MODEL, D_FF), f32),
                            pltpu.VMEM((2, D_FF, D_MODEL), f32),
                            pltpu.VMEM((D_MODEL, D_FF), bf16),
                            pltpu.VMEM((D_MODEL, D_FF), bf16),
                            pltpu.VMEM((D_FF, D_MODEL), bf16),
                            pltpu.SMEM((1,), jnp.int32),
                            pltpu.SemaphoreType.DMA((N_XBUF,)),
                            pltpu.SemaphoreType.DMA((2,))],
        ),
        out_shape=jax.ShapeDtypeStruct((n_slots, ROW_CHUNKS, LANES), PACKED_DTYPE),
        compiler_params=pltpu.CompilerParams(
            dimension_semantics=("arbitrary",), vmem_limit_bytes=VMEM_LIMIT),
        name="experts",
    )(tabs["block_e"], tabs["nused"], tabs["r0"], tabs["nvalid"], tabs["tlo"], tabs["thi"],
      tabs["cnt"], tabs["src"], tabs["rbase"], tabs["nxt"], xs, w_gate, w_up, w_down)


def _combine_kernel(cnt_ref, loc_ref, dst_ref, h_ref, meta_ref, yg_hbm, o_ref, ybuf, sem, *, n_steps, nsub):
    i = pl.program_id(0)
    t = T_SORT
    f32, bf16 = jnp.float32, jnp.bfloat16

    slot_in = i % 2
    tile0 = jnp.minimum(i, n_steps - 1) * nsub
    for s in range(nsub):
        for e in range(N_EXPERTS):
            k = (tile0 + s) * N_EXPERTS + e
            rows = jnp.where(i < n_steps, cnt_ref[k], 0)

            @pl.when(rows > 0)
            def _():
                pltpu.make_async_copy(yg_hbm.at[pl.ds(dst_ref[k], rows)],
                                      ybuf.at[slot_in, pl.ds(s * 2 * t + loc_ref[k], rows)],
                                      sem.at[slot_in]).start()

    @pl.when(i > 0)
    def _():
        slot = (i - 1) % 2
        pltpu.make_async_copy(yg_hbm.at[pl.ds(0, nsub * 2 * t)], ybuf.at[slot], sem.at[slot]).wait()
        lane = lax.broadcasted_iota(jnp.int32, (t, 2 * t), 1).astype(f32)
        for s in range(nsub):
            y = _unpack_rows(ybuf[slot, s * 2 * t:(s + 1) * 2 * t])
            meta = meta_ref[s * t:(s + 1) * t, :]
            pick1 = jnp.where(lane == meta[:, 0:1], 1.0, 0.0).astype(bf16)
            pick2 = jnp.where(lane == meta[:, 1:2], 1.0, 0.0).astype(bf16)
            y1 = _dot(pick1, y)
            y2 = _dot(pick2, y)
            o_ref[s * t:(s + 1) * t, :] = h_ref[s * t:(s + 1) * t, :] + (meta[:, 2:3] * y1 + meta[:, 3:4] * y2)


def _combine(cnt, loc, dst, h, meta, yg):
    n = h.shape[0]
    tm = TM_OUT
    nsub = tm // T_SORT
    n_steps = n // tm
    prev = lambda i, *_: (jnp.maximum(i - 1, 0), 0)
    return pl.pallas_call(
        functools.partial(_combine_kernel, n_steps=n_steps, nsub=nsub),
        grid_spec=pltpu.PrefetchScalarGridSpec(
            num_scalar_prefetch=3,
            grid=(n_steps + 1,),
            in_specs=[
                pl.BlockSpec((tm, D_MODEL), prev),
                pl.BlockSpec((tm, LANES), prev),
                pl.BlockSpec(memory_space=pl.ANY),
            ],
            out_specs=pl.BlockSpec((tm, D_MODEL), prev),
            scratch_shapes=[pltpu.VMEM((2, nsub * 2 * T_SORT, ROW_CHUNKS, LANES), PACKED_DTYPE),
                            pltpu.SemaphoreType.DMA((2,))],
        ),
        out_shape=jax.ShapeDtypeStruct((n, D_MODEL), jnp.float32),
        compiler_params=pltpu.CompilerParams(
            dimension_semantics=("arbitrary",), vmem_limit_bytes=VMEM_LIMIT),
        name="combine",
    )(cnt, loc, dst, h, meta, yg)


def _routing_tables(cnt_out, n_tiles, n_tok):
    i32 = jnp.int32
    cnt = cnt_out.reshape(n_tiles, N_EXPERTS, LANES)[:, :, 0].astype(i32)
    count = jnp.sum(cnt, axis=0)
    padded = ((count + MOE_BLOCK - 1) // MOE_BLOCK) * MOE_BLOCK
    pad_end = jnp.cumsum(padded)
    pad_start = pad_end - padded
    run_end = jnp.cumsum(cnt, axis=0)
    rbase = run_end - cnt
    dst = pad_start[None, :] + rbase
    loc = jnp.cumsum(cnt, axis=1) - cnt
    src = loc + (jnp.arange(n_tiles, dtype=i32) * (2 * T_SORT))[:, None]
    n_blocks = (2 * n_tok) // MOE_BLOCK + N_EXPERTS
    block_start = jnp.arange(n_blocks, dtype=i32) * MOE_BLOCK
    block_e = jnp.minimum(jnp.sum(pad_end[None, :] <= block_start[:, None], axis=1), N_EXPERTS - 1).astype(i32)
    nused = (pad_end[-1] // MOE_BLOCK).astype(i32).reshape(1)
    ex = jnp.arange(N_EXPERTS, dtype=i32)
    sel = (block_e[None, :] == ex[:, None]).astype(i32)
    pick = lambda per_expert: jnp.sum(per_expert[..., :, None] * sel, axis=-2)
    r0 = block_start - pick(pad_start)
    nvalid = jnp.clip(pick(count) - r0, 0, MOE_BLOCK)
    tlo = jnp.sum(pick(run_end) <= r0[None, :], axis=0)
    thi = jnp.sum(pick(rbase) < (r0 + MOE_BLOCK)[None, :], axis=0)
    later_used = (ex[None, :] > ex[:, None]) & (count[None, :] > 0)
    nxt = jnp.min(jnp.where(later_used, ex[None, :], N_EXPERTS), axis=1)
    nxt = jnp.where(nxt < N_EXPERTS, nxt, -1)
    flat = lambda a: a.reshape(-1).astype(i32)
    return dict(cnt=flat(cnt), src=flat(src), dst=flat(dst), loc=flat(loc), rbase=flat(rbase),
                block_e=block_e, nused=nused, r0=flat(r0), nvalid=flat(nvalid), tlo=flat(tlo), thi=flat(thi),
                nxt=flat(nxt), n_slots=n_blocks * MOE_BLOCK)


def _stage1(x, positions, attn_norm_gain, w_in, q_norm_gain, k_norm_gain, conv_w, conv_out_gain):
    bsz, seq, _ = x.shape
    n = bsz * seq
    f32, bf16 = jnp.float32, jnp.bfloat16
    w = w_in[0]
    wqk = w[:, :2 * ATTN_WIDTH].astype(bf16)
    wvt = w[:, 2 * ATTN_WIDTH:3 * ATTN_WIDTH].T.astype(bf16)
    wc = w[:, 3 * ATTN_WIDTH:].astype(bf16)
    scale = DK ** -0.5 * LOG2E
    gqk = jnp.concatenate([jnp.tile(q_norm_gain[0].astype(f32), 2 * N_HEADS) * scale,
                           jnp.tile(k_norm_gain[0].astype(f32), 2 * N_HEADS)]).reshape(1, -1)
    freqs = (ROPE_THETA ** (-jnp.arange(0, ROT_DIM, 2, dtype=f32) / ROT_DIM)).reshape(SUBLANES, 1)
    return _inproj(x.reshape(n, D_MODEL), positions.reshape(1, n),
                   attn_norm_gain[0].reshape(1, -1).astype(f32), wqk, wvt, wc, gqk, freqs,
                   conv_w[0].astype(f32), conv_out_gain[0].reshape(1, -1).astype(f32), seq)


def kernel(x, positions, attn_norm_gain, w_in, q_norm_gain, k_norm_gain, lambda_q1, lambda_k1, lambda_q2, lambda_k2, subln_gain, conv_w, conv_out_gain, w_out, ffn_norm_gain, w_group_router, w_expert_router, w_gate, w_up, w_down):
    bsz, seq, _ = x.shape
    n = bsz * seq
    f32, bf16 = jnp.float32, jnp.bfloat16
    assert TQ % TM_OUT == 0 and TM_OUT % T_SORT == 0 and seq % TM_IN == 0 and (seq // TQ) % 2 == 0
    q, k, vt, conv_o = _stage1(x, positions, attn_norm_gain, w_in, q_norm_gain, k_norm_gain,
                                conv_w, conv_out_gain)
    attn_p = _attention(q, k, vt,
                        lambda_q1[0].reshape(1, -1).astype(f32), lambda_k1[0].reshape(1, -1).astype(f32),
                        lambda_q2[0].reshape(1, -1).astype(f32), lambda_k2[0].reshape(1, -1).astype(f32),
                        subln_gain[0].reshape(-1, 1).astype(f32), bsz, seq)

    wr = jnp.zeros((D_MODEL, ROUTER_ROWS), f32)
    wr = wr.at[:, 0:N_GROUPS].set(w_group_router[0].astype(f32))
    wr = wr.at[:, EXPERT_ROW0:EXPERT_ROW0 + N_EXPERTS].set(
        jnp.transpose(w_expert_router[0].astype(f32), (1, 0, 2)).reshape(D_MODEL, N_EXPERTS))
    wrh = wr.astype(bf16)
    wr2 = jnp.concatenate([wrh, (wr - wrh.astype(f32)).astype(bf16)], axis=1)
    h, xs, meta, cnt_out = _outproj(x.reshape(n, D_MODEL), attn_p, conv_o, w_out[0].astype(bf16),
                                    ffn_norm_gain[0].reshape(1, -1).astype(f32), wr2, seq)

    tabs = _routing_tables(cnt_out, n // T_SORT, n)
    yg = _experts(tabs, xs, w_gate[0], w_up[0], w_down[0])
    out = _combine(tabs["cnt"], tabs["loc"], tabs["dst"], h, meta, yg)
    return out.reshape(x.shape)
```

```python
import functools
import math

import jax
import jax.numpy as jnp
from jax import lax
from jax.experimental import pallas as pl
from jax.experimental.pallas import tpu as pltpu

D_MODEL = 1024
N_HEADS = 4
DK = 64
DV = 128
ROT_DIM = 16
ROPE_THETA = 500000.0
ATTN_WIDTH = N_HEADS * DV
CONV_WIDTH = 512
NORM_EPS = 1e-6
LOG2E = 1.4426950408889634
LAMBDA_INIT = 0.8 - 0.6 * math.exp(-0.3 * 0)
N_GROUPS = 4
EPG = 8
N_EXPERTS = N_GROUPS * EPG
D_FF = 512
MOE_BLOCK = 256

LANES = 128
SUBLANES = 8
ROW_WORDS = D_MODEL // 2
ROW_CHUNKS = ROW_WORDS // LANES
PACKED_DTYPE = jnp.uint32

TM_IN = 512
TQ = 512
HQ = TQ // 2
KV_FULL = 512
SUM_ROWS = 16
T_SORT = 256
TM_OUT = 512
N_XBUF = 3
GATHER_UNROLL = 24
VMEM_LIMIT = 48 * 1024 * 1024


def _nt_dot(a, b):
    return lax.dot_general(a, b, (((1,), (1,)), ((), ())), preferred_element_type=jnp.float32)


def _dot(a, b):
    return jnp.dot(a, b, preferred_element_type=jnp.float32)


def _split3(x):
    h = x.astype(jnp.bfloat16)
    r = x - h.astype(jnp.float32)
    m = r.astype(jnp.bfloat16)
    l = (r - m.astype(jnp.float32)).astype(jnp.bfloat16)
    return h, m, l


def _split2(x):
    h = x.astype(jnp.bfloat16)
    l = (x - h.astype(jnp.float32)).astype(jnp.bfloat16)
    return h, l


def _inproj_kernel(x_ref, pos_ref, g1_ref, wqk_ref, wvt_ref, wc_ref, gqk_ref, freq_ref,
                   cw_ref, cg_ref,
                   q_ref, k_ref, vt_ref, conv_ref,
                   carry_ref, *, tiles_per_seq):
    tm = x_ref.shape[0]
    i = pl.program_id(0)

    x = x_ref[...]
    ms = jnp.mean(x * x, axis=-1, keepdims=True)
    hn = (x * lax.rsqrt(ms + NORM_EPS) * g1_ref[...]).astype(jnp.bfloat16)

    pos = pos_ref[...].astype(jnp.float32)
    ang = freq_ref[...] * pos
    lane_r = lax.broadcasted_iota(jnp.int32, (LANES, SUBLANES), 0)
    f_c = lax.broadcasted_iota(jnp.int32, (LANES, SUBLANES), 1)
    in_rot = (lane_r % DK) < ROT_DIM
    expand = jnp.where(in_rot & ((lane_r % (ROT_DIM // 2)) == f_c), 1.0, 0.0).astype(jnp.bfloat16)

    def to_rows(t):
        h, m, l = _split3(t)
        r = _dot(expand, h) + _dot(expand, m) + _dot(expand, l)
        return r.T

    cos_r = to_rows(jnp.cos(ang))
    sin_r = to_rows(jnp.sin(ang))
    lane = lax.broadcasted_iota(jnp.int32, (tm, LANES), 1)
    d = lane % DK
    cos_r = jnp.where(d < ROT_DIM, cos_r, 1.0)
    sin_lo = jnp.where(d < ROT_DIM // 2, -sin_r, 0.0)
    sin_hi = jnp.where((d >= ROT_DIM // 2) & (d < ROT_DIM), sin_r, 0.0)

    qk = _dot(hn, wqk_ref[...])
    seg_r = lax.broadcasted_iota(jnp.int32, (2 * LANES, 2 * LANES), 0) // DK
    seg_c = lax.broadcasted_iota(jnp.int32, (2 * LANES, 2 * LANES), 1) // DK
    seg_mean = jnp.where(seg_r == seg_c, 1.0 / DK, 0.0).astype(jnp.bfloat16)
    half = ROT_DIM // 2
    for c2 in range(4):
        blk = qk[:, c2 * 256:(c2 + 1) * 256]
        msq = _dot((blk * blk).astype(jnp.bfloat16), seg_mean)
        y2 = blk * lax.rsqrt(msq + NORM_EPS) * gqk_ref[:, c2 * 256:(c2 + 1) * 256]
        for c1 in range(2):
            c = c2 * 2 + c1
            y = y2[:, c1 * LANES:(c1 + 1) * LANES]
            rot = (y * cos_r
                   + pltpu.roll(y, LANES - half, 1) * sin_lo
                   + pltpu.roll(y, half, 1) * sin_hi)
            if c < N_HEADS:
                q_ref[:, c * LANES:(c + 1) * LANES] = rot.astype(jnp.bfloat16)
            else:
                h = c - N_HEADS
                k_ref[:, h * LANES:(h + 1) * LANES] = rot.astype(jnp.bfloat16)

    vt_ref[...] = _nt_dot(wvt_ref[...], hn).astype(jnp.bfloat16)

    cp = _dot(hn, wc_ref[...])
    cb = cp[:, :CONV_WIDTH]
    y = cp[:, CONV_WIDTH:2 * CONV_WIDTH] * cp[:, 2 * CONV_WIDTH:]

    @pl.when(i % tiles_per_seq == 0)
    def _():
        carry_ref[...] = jnp.zeros_like(carry_ref)

    prev = carry_ref[...]
    row = lax.broadcasted_iota(jnp.int32, (tm, CONV_WIDTH), 0)
    p1 = prev[SUBLANES - 1:SUBLANES, :]
    p2 = prev[SUBLANES - 2:SUBLANES - 1, :]
    y1 = jnp.where(row == 0, p1, pltpu.roll(y, 1, 0))
    y2 = jnp.where(row == 0, p2, jnp.where(row == 1, p1, pltpu.roll(y, 2, 0)))
    carry_ref[...] = y[tm - SUBLANES:, :]
    z = cw_ref[0:1, :] * y2 + cw_ref[1:2, :] * y1 + cw_ref[2:3, :] * y
    co = cb * z
    cms = jnp.mean(co * co, axis=-1, keepdims=True)
    conv_ref[...] = (co * lax.rsqrt(cms + NORM_EPS) * cg_ref[...]).astype(jnp.bfloat16)


def _inproj(x2, pos_row, g1, wqk, wvt, wc, gqk, freqs, cw, cg, seq):
    n = x2.shape[0]
    tm = TM_IN
    grid = (n // tm,)
    const = lambda i: (0, 0)
    return pl.pallas_call(
        functools.partial(_inproj_kernel, tiles_per_seq=seq // tm),
        grid=grid,
        in_specs=[
            pl.BlockSpec((tm, D_MODEL), lambda i: (i, 0)),
            pl.BlockSpec((1, tm), lambda i: (0, i)),
            pl.BlockSpec((1, D_MODEL), const),
            pl.BlockSpec((D_MODEL, 1024), const),
            pl.BlockSpec((512, D_MODEL), const),
            pl.BlockSpec((D_MODEL, 1536), const),
            pl.BlockSpec((1, 1024), const),
            pl.BlockSpec((SUBLANES, 1), const),
            pl.BlockSpec((3, CONV_WIDTH), const),
            pl.BlockSpec((1, CONV_WIDTH), const),
        ],
        out_specs=[
            pl.BlockSpec((tm, ATTN_WIDTH), lambda i: (i, 0)),
            pl.BlockSpec((tm, ATTN_WIDTH), lambda i: (i, 0)),
            pl.BlockSpec((ATTN_WIDTH, tm), lambda i: (0, i)),
            pl.BlockSpec((tm, CONV_WIDTH), lambda i: (i, 0)),
        ],
        out_shape=[
            jax.ShapeDtypeStruct((n, ATTN_WIDTH), jnp.bfloat16),
            jax.ShapeDtypeStruct((n, ATTN_WIDTH), jnp.bfloat16),
            jax.ShapeDtypeStruct((ATTN_WIDTH, n), jnp.bfloat16),
            jax.ShapeDtypeStruct((n, CONV_WIDTH), jnp.bfloat16),
        ],
        scratch_shapes=[pltpu.VMEM((SUBLANES, CONV_WIDTH), jnp.float32)],
        compiler_params=pltpu.CompilerParams(
            dimension_semantics=("arbitrary",), vmem_limit_bytes=VMEM_LIMIT),
        name="inproj",
    )(x2, pos_row, g1, wqk, wvt, wc, gqk, freqs, cw, cg)


def _attn_kernel(q_ref, k_ref, vt_ref, lq1_ref, lk1_ref, lq2_ref, lk2_ref, sg_ref, o_ref, acc_ref, *, nq):
    lam = (jnp.exp(jnp.sum(lq1_ref[...] * lk1_ref[...], axis=-1, keepdims=True))
           - jnp.exp(jnp.sum(lq2_ref[...] * lk2_ref[...], axis=-1, keepdims=True))
           + LAMBDA_INIT)
    tri = (lax.broadcasted_iota(jnp.int32, (HQ, HQ), 0)
           <= lax.broadcasted_iota(jnp.int32, (HQ, HQ), 1))
    row_d = lax.broadcasted_iota(jnp.int32, (LANES, HQ), 0)
    ones_rows = jnp.ones((SUM_ROWS, KV_FULL), jnp.bfloat16)

    def masked(s, n_tri):
        blocks = [jnp.where(tri, s[:, c * HQ:(c + 1) * HQ], -jnp.inf) for c in range(n_tri)]
        return jnp.concatenate(blocks + [s[:, n_tri * HQ:]], axis=1) if n_tri * HQ < s.shape[1] else \
            jnp.concatenate(blocks, axis=1)

    todo = {}
    for t in range(nq):
        lo = t * TQ
        todo[t] = ([(lo, lo + HQ, "mixed"), (lo + HQ, lo + TQ, "upper")]
                   + [(k0, k0 + KV_FULL, "full") for k0 in range(0, lo, KV_FULL)])
    tasks = []
    while any(todo.values()):
        for t in reversed(range(nq)):
            if todo[t]:
                tasks.append((t,) + todo[t].pop(0))
    last_task = {t: max(i for i, tk in enumerate(tasks) if tk[0] == t) for t in range(nq)}
    qzt, m_run = {}, {}

    def q_transposed(t):
        parts = []
        for half in range(2):
            r0 = t * TQ + half * HQ
            qt = q_ref[r0:r0 + HQ, :].astype(jnp.float32).T
            parts.append(jnp.where(row_d < DK, qt, 0.0))
            parts.append(jnp.where(row_d >= DK, qt, 0.0))
        return jnp.concatenate(parts, axis=1).astype(jnp.bfloat16)

    def scores(task):
        t, k0, k1, kind = task
        if t not in qzt:
            qzt[t] = q_transposed(t)
        kb = k_ref[k0:k1, :]
        if kind == "upper":
            return masked(_dot(kb, qzt[t][:, 2 * HQ:]), 2)
        s = _dot(kb, qzt[t])
        return masked(s, 2) if kind == "mixed" else s

    def softmax(task, s):
        t, _, _, kind = task
        mx = jnp.max(s, axis=0, keepdims=True)
        if kind == "mixed":
            m_run[t] = mx
            return jnp.exp2(s - mx).astype(jnp.bfloat16), None
        m_old = m_run[t][:, 2 * HQ:] if kind == "upper" else m_run[t]
        m_new = jnp.maximum(m_old, mx)
        alpha = jnp.exp2(m_old - m_new)
        p = jnp.exp2(s - m_new)
        m_run[t] = jnp.concatenate([m_run[t][:, :2 * HQ], m_new], axis=1) if kind == "upper" else m_new
        return p.astype(jnp.bfloat16), alpha

    def accumulate(task, p, alpha):
        t, k0, k1, kind = task
        vta = jnp.concatenate([vt_ref[:, k0:k1], ones_rows[:, :k1 - k0]], axis=0)
        pv = _dot(vta, p)
        if kind == "mixed":
            acc_ref[t] = pv
        elif kind == "upper":
            acc_ref[t, :, 2 * HQ:] = alpha * acc_ref[t, :, 2 * HQ:] + pv
        else:
            acc_ref[t] = alpha * acc_ref[t] + pv

    def finish(t):
        o_all = acc_ref[t, 0:DV, :] * (1.0 / acc_ref[t, DV:DV + 1, :])
        for half in range(2):
            o = (o_all[:, (2 * half) * HQ:(2 * half + 1) * HQ]
                 - lam * o_all[:, (2 * half + 1) * HQ:(2 * half + 2) * HQ])
            ms = jnp.mean(o * o, axis=0, keepdims=True)
            on = o * lax.rsqrt(ms + NORM_EPS) * sg_ref[...] * (1.0 - LAMBDA_INIT)
            r0 = t * TQ + half * HQ
            o_ref[r0:r0 + HQ, :] = on.T.astype(jnp.bfloat16)

    n = len(tasks)
    s_prev, p_prev = None, None
    for step in range(n + 2):
        if step >= 2:
            accumulate(tasks[step - 2], *p_prev)
            if last_task[tasks[step - 2][0]] == step - 2:
                finish(tasks[step - 2][0])
        if 1 <= step <= n:
            p_prev = softmax(tasks[step - 1], s_prev)
        if step < n:
            s_prev = scores(tasks[step])


def _attention(q, k, vt, lq1, lk1, lq2, lk2, sg_col, bsz, seq):
    n = q.shape[0]
    nq = seq // TQ
    vec = lambda b, h: (0, 0)
    return pl.pallas_call(
        functools.partial(_attn_kernel, nq=nq),
        grid=(bsz, N_HEADS),
        in_specs=[
            pl.BlockSpec((seq, DV), lambda b, h: (b, h)),
            pl.BlockSpec((seq, DV), lambda b, h: (b, h)),
            pl.BlockSpec((DV, seq), lambda b, h: (h, b)),
            pl.BlockSpec((1, DK), vec), pl.BlockSpec((1, DK), vec),
            pl.BlockSpec((1, DK), vec), pl.BlockSpec((1, DK), vec),
            pl.BlockSpec((DV, 1), vec),
        ],
        out_specs=pl.BlockSpec((seq, DV), lambda b, h: (b, h)),
        out_shape=jax.ShapeDtypeStruct((n, ATTN_WIDTH), jnp.bfloat16),
        scratch_shapes=[pltpu.VMEM((nq, DV + SUM_ROWS, 4 * HQ), jnp.float32)],
        compiler_params=pltpu.CompilerParams(
            dimension_semantics=("arbitrary", "arbitrary"), vmem_limit_bytes=VMEM_LIMIT),
        name="attn",
    )(q, k, vt, lq1, lk1, lq2, lk2, sg_col)


ROUTER_ROWS = 128
EXPERT_ROW0 = 32


def _outproj_kernel(x_ref, attn_ref, conv_ref, wo_ref, g2_ref, wr2_ref,
                    h_ref, xs_ref, meta_ref, cnt_ref):
    tm = x_ref.shape[0]
    t = T_SORT
    f32, bf16 = jnp.float32, jnp.bfloat16
    a = jnp.concatenate([attn_ref[...], conv_ref[...]], axis=1)
    h = x_ref[...] + _dot(a, wo_ref[...])
    h_ref[...] = h
    ms = jnp.mean(h * h, axis=-1, keepdims=True)
    hn = h * lax.rsqrt(ms + NORM_EPS) * g2_ref[...]
    hn_hi, hn_lo = _split2(hn)

    hh = _dot(hn_hi, wr2_ref[...])
    logits = hh[:, :ROUTER_ROWS] + hh[:, ROUTER_ROWS:] + _dot(hn_lo, wr2_ref[:, :ROUTER_ROWS])
    lt = logits.T
    row8 = lax.broadcasted_iota(jnp.int32, (SUBLANES, tm), 0).astype(f32)
    neg_inf = -jnp.inf

    def first_argmax(v):
        mx = jnp.max(v, axis=0, keepdims=True)
        idx = jnp.min(jnp.where(v == mx, row8, float(SUBLANES)), axis=0, keepdims=True)
        return mx, idx

    g_log = jnp.where(row8 < N_GROUPS, lt[0:SUBLANES, :], neg_inf)
    g_max, g_sel = first_argmax(g_log)
    g_gate = 1.0 / jnp.sum(jnp.exp(g_log - g_max), axis=0, keepdims=True)
    e_log = jnp.zeros((EPG, tm), f32)
    for g in range(N_GROUPS):
        rows = lt[EXPERT_ROW0 + g * EPG:EXPERT_ROW0 + (g + 1) * EPG, :]
        e_log = jnp.where(g_sel == float(g), rows, e_log)
    v1, i1 = first_argmax(e_log)
    v2, i2 = first_argmax(jnp.where(row8 == i1, neg_inf, e_log))
    tt = jnp.exp(v2 - v1)
    w1 = g_gate / (1.0 + tt)
    w2 = g_gate * tt / (1.0 + tt)
    e1 = g_sel * float(EPG) + i1
    e2 = g_sel * float(EPG) + i2

    row32 = lax.broadcasted_iota(jnp.int32, (N_EXPERTS, tm), 0).astype(f32)
    oh1 = row32 == e1
    oh2 = row32 == e2
    c = jnp.where(oh1 | oh2, 1.0, 0.0).astype(bf16)
    tok_r = lax.broadcasted_iota(jnp.int32, (tm, tm), 0)
    tok_c = lax.broadcasted_iota(jnp.int32, (tm, tm), 1)
    same_tile = (tok_r // t) == (tok_c // t)
    rank = _dot(c, jnp.where(same_tile & (tok_r < tok_c), 1.0, 0.0).astype(bf16))
    cnt_b = _dot(c, jnp.where(same_tile, 1.0, 0.0).astype(bf16))
    ex_r = lax.broadcasted_iota(jnp.int32, (N_EXPERTS, N_EXPERTS), 0)
    ex_c = lax.broadcasted_iota(jnp.int32, (N_EXPERTS, N_EXPERTS), 1)
    lower = jnp.where(ex_c < ex_r, 1.0, 0.0).astype(bf16)
    start_b = _dot(lower, cnt_b.astype(bf16))
    pos_e = start_b + rank
    p1 = jnp.sum(jnp.where(oh1, pos_e, 0.0), axis=0, keepdims=True)
    p2 = jnp.sum(jnp.where(oh2, pos_e, 0.0), axis=0, keepdims=True)

    srow = lax.broadcasted_iota(jnp.int32, (2 * t, t), 0).astype(f32)
    for s in range(tm // t):
        cols = slice(s * t, (s + 1) * t)
        perm = jnp.where((srow == p1[:, cols]) | (srow == p2[:, cols]), 1.0, 0.0).astype(bf16)
        xs = _dot(perm, hn_hi[cols, :])
        xs_ref[s * 2 * t:(s + 1) * 2 * t] = _pack_rows(xs)
        cnt_ref[s * N_EXPERTS:(s + 1) * N_EXPERTS, :] = cnt_b[:, s * t:s * t + LANES]

    meta = jnp.concatenate([p1, p2, w1, w2, jnp.zeros((LANES - 4, tm), f32)], axis=0)
    meta_ref[...] = meta.T


def _outproj(x2, attn_o, conv_o, wo, g2, wr2):
    n = x2.shape[0]
    tm = TM_OUT
    nsub = tm // T_SORT
    const = lambda i: (0, 0)

    return pl.pallas_call(
        _outproj_kernel,
        grid=(n // tm,),
        in_specs=[
            pl.BlockSpec((tm, D_MODEL), lambda i: (i, 0)),
            pl.BlockSpec((tm, ATTN_WIDTH), lambda i: (i, 0)),
            pl.BlockSpec((tm, CONV_WIDTH), lambda i: (i, 0)),
            pl.BlockSpec((D_MODEL, D_MODEL), const),
            pl.BlockSpec((1, D_MODEL), const),
            pl.BlockSpec((D_MODEL, 2 * ROUTER_ROWS), const),
        ],
        out_specs=[
            pl.BlockSpec((tm, D_MODEL), lambda i: (i, 0)),
            pl.BlockSpec((2 * tm, ROW_CHUNKS, LANES), lambda i: (i, 0, 0)),
            pl.BlockSpec((tm, LANES), lambda i: (i, 0)),
            pl.BlockSpec((nsub * N_EXPERTS, LANES), lambda i: (i, 0)),
        ],
        out_shape=[
            jax.ShapeDtypeStruct((n, D_MODEL), jnp.float32),
            jax.ShapeDtypeStruct((2 * n, ROW_CHUNKS, LANES), PACKED_DTYPE),
            jax.ShapeDtypeStruct((n, LANES), jnp.float32),
            jax.ShapeDtypeStruct((n // T_SORT * N_EXPERTS, LANES), jnp.float32),
        ],
        compiler_params=pltpu.CompilerParams(
            dimension_semantics=("arbitrary",), vmem_limit_bytes=VMEM_LIMIT),
        name="outproj",
    )(x2, attn_o, conv_o, wo, g2, wr2)


def _pack_rows(x):
    r = x.shape[0]
    w = pltpu.pack_elementwise([x[:, :ROW_WORDS], x[:, ROW_WORDS:]], packed_dtype=jnp.bfloat16)
    return pltpu.bitcast(w, PACKED_DTYPE).reshape(r, ROW_CHUNKS, LANES)


def _packed_zero_rows(r):
    z = jnp.zeros((r, ROW_CHUNKS, LANES), jnp.float32)
    w = pltpu.pack_elementwise([z, z], packed_dtype=jnp.bfloat16)
    return pltpu.bitcast(w, PACKED_DTYPE)


def _unpack_rows(u):
    r = u.shape[0]
    w = u.reshape(r, ROW_WORDS)
    lo = pltpu.unpack_elementwise(w, index=0, packed_dtype=jnp.bfloat16, unpacked_dtype=jnp.float32)
    hi = pltpu.unpack_elementwise(w, index=1, packed_dtype=jnp.bfloat16, unpacked_dtype=jnp.float32)
    return jnp.concatenate([lo, hi], axis=1).astype(jnp.bfloat16)


def _expert_kernel(be_ref, nused_ref, r0_ref, nvalid_ref, tlo_ref, thi_ref, cnt_ref, src_ref, rbase_ref,
                   nxt_ref, xs_hbm, wg_hbm, wu_hbm, wd_hbm, y_ref,
                   xbuf, wg_f, wu_f, wd_f, wg_s, wu_s, wd_s, wslot_ref, sem, wsem):
    b = pl.program_id(0)
    nused = nused_ref[0]
    n_blocks = be_ref.shape[0]
    n_tiles = cnt_ref.shape[0] // N_EXPERTS

    def start_run(blk, slot, t, live):
        r0 = r0_ref[blk]
        k = t * N_EXPERTS + be_ref[blk]
        lo = jnp.maximum(rbase_ref[k], r0)
        hi = jnp.minimum(rbase_ref[k] + cnt_ref[k], r0 + MOE_BLOCK)
        rows = jnp.where(live, hi - lo, 0)

        @pl.when(rows > 0)
        def _():
            pltpu.make_async_copy(xs_hbm.at[pl.ds(src_ref[k] + lo - rbase_ref[k], rows)],
                                  xbuf.at[slot, pl.ds(lo - r0, rows)], sem.at[slot]).start()

    def gather_rolled(blk, slot, t_from):
        def run(t, carry):
            start_run(blk, slot, t, True)
            return carry
        lax.fori_loop(t_from, thi_ref[blk], run, 0)

    def gather_unrolled(blk, slot, live):
        blk = jnp.minimum(blk, n_blocks - 1)
        for r in range(GATHER_UNROLL):
            t = tlo_ref[blk] + r
            start_run(blk, slot, jnp.minimum(t, n_tiles - 1), live & (t < thi_ref[blk]))

        @pl.when(live & (tlo_ref[blk] + GATHER_UNROLL < thi_ref[blk]))
        def _():
            gather_rolled(blk, slot, tlo_ref[blk] + GATHER_UNROLL)

    def weight_copies(e, slot):
        return [pltpu.make_async_copy(src.at[e], dst.at[slot], wsem.at[slot])
                for src, dst in ((wg_hbm, wg_f), (wu_hbm, wu_f), (wd_hbm, wd_f))]

    @pl.when(b == 0)
    def _():
        for slot in range(N_XBUF):
            xbuf[slot] = _packed_zero_rows(MOE_BLOCK)
        gather_rolled(0, 0, tlo_ref[0])

        @pl.when(nused > 1)
        def _():
            gather_rolled(1, 1, tlo_ref[1])
        wslot_ref[0] = 0
        for cp in weight_copies(be_ref[0], 0):
            cp.start()

    @pl.when(b < nused)
    def _():
        e = be_ref[b]

        @pl.when((b == 0) | (e != be_ref[jnp.maximum(b - 1, 0)]))
        def _():
            ws = wslot_ref[0]
            for cp in weight_copies(e, ws):
                cp.wait()
            wg_s[...] = wg_f[ws].astype(jnp.bfloat16)
            wu_s[...] = wu_f[ws].astype(jnp.bfloat16)
            wd_s[...] = wd_f[ws].astype(jnp.bfloat16)
            wslot_ref[0] = 1 - ws

            @pl.when(nxt_ref[e] >= 0)
            def _():
                for cp in weight_copies(nxt_ref[e], 1 - ws):
                    cp.start()

        slot = b % N_XBUF
        nv = nvalid_ref[b]
        pltpu.make_async_copy(xs_hbm.at[pl.ds(0, nv)], xbuf.at[slot, pl.ds(0, nv)], sem.at[slot]).wait()

        x = _unpack_rows(xbuf[slot])
        g = _dot(x, wg_s[...])
        u = _dot(x, wu_s[...])
        act = (g / (1.0 + jnp.exp(-g)) * u).astype(jnp.bfloat16)
        y = _dot(act, wd_s[...])
        y_ref[...] = _pack_rows(y)

        gather_unrolled(b + 2, (b + 2) % N_XBUF, b + 2 < nused)

    @pl.when(b >= nused)
    def _():
        y_ref[...] = _packed_zero_rows(MOE_BLOCK)


def _experts(tabs, xs, w_gate, w_up, w_down):
    n_slots = tabs["n_slots"]
    n_blocks = n_slots // MOE_BLOCK
    f32, bf16 = jnp.float32, jnp.bfloat16
    hbm = pl.BlockSpec(memory_space=pl.ANY)
    return pl.pallas_call(
        _expert_kernel,
        grid_spec=pltpu.PrefetchScalarGridSpec(
            num_scalar_prefetch=10,
            grid=(n_blocks,),
            in_specs=[hbm, hbm, hbm, hbm],
            out_specs=pl.BlockSpec((MOE_BLOCK, ROW_CHUNKS, LANES), lambda b, *_: (b, 0, 0)),
            scratch_shapes=[pltpu.VMEM((N_XBUF, MOE_BLOCK, ROW_CHUNKS, LANES), PACKED_DTYPE),
                            pltpu.VMEM((2, D_MODEL, D_FF), f32),
                            pltpu.VMEM((2, D_MODEL, D_FF), f32),
                            pltpu.VMEM((2, D_FF, D_MODEL), f32),
                            pltpu.VMEM((D_MODEL, D_FF), bf16),
                            pltpu.VMEM((D_MODEL, D_FF), bf16),
                            pltpu.VMEM((D_FF, D_MODEL), bf16),
                            pltpu.SMEM((1,), jnp.int32),
                            pltpu.SemaphoreType.DMA((N_XBUF,)),
                            pltpu.SemaphoreType.DMA((2,))],
        ),
        out_shape=jax.ShapeDtypeStruct((n_slots, ROW_CHUNKS, LANES), PACKED_DTYPE),
        compiler_params=pltpu.CompilerParams(
            dimension_semantics=("arbitrary",), vmem_limit_bytes=VMEM_LIMIT),
        name="experts",
    )(tabs["block_e"], tabs["nused"], tabs["r0"], tabs["nvalid"], tabs["tlo"], tabs["thi"],
      tabs["cnt"], tabs["src"], tabs["rbase"], tabs["nxt"], xs, w_gate, w_up, w_down)


def _combine_kernel(cnt_ref, loc_ref, dst_ref, h_ref, meta_ref, yg_hbm, o_ref, ybuf, sem, *, n_steps, nsub):
    i = pl.program_id(0)
    t = T_SORT
    f32, bf16 = jnp.float32, jnp.bfloat16

    slot_in = i % 2
    tile0 = jnp.minimum(i, n_steps - 1) * nsub
    for s in range(nsub):
        for e in range(N_EXPERTS):
            k = (tile0 + s) * N_EXPERTS + e
            rows = jnp.where(i < n_steps, cnt_ref[k], 0)

            @pl.when(rows > 0)
            def _():
                pltpu.make_async_copy(yg_hbm.at[pl.ds(dst_ref[k], rows)],
                                      ybuf.at[slot_in, pl.ds(s * 2 * t + loc_ref[k], rows)],
                                      sem.at[slot_in]).start()

    @pl.when(i > 0)
    def _():
        slot = (i - 1) % 2
        pltpu.make_async_copy(yg_hbm.at[pl.ds(0, nsub * 2 * t)], ybuf.at[slot], sem.at[slot]).wait()
        lane = lax.broadcasted_iota(jnp.int32, (t, 2 * t), 1).astype(f32)
        for s in range(nsub):
            y = _unpack_rows(ybuf[slot, s * 2 * t:(s + 1) * 2 * t])
            meta = meta_ref[s * t:(s + 1) * t, :]
            pick1 = jnp.where(lane == meta[:, 0:1], 1.0, 0.0).astype(bf16)
            pick2 = jnp.where(lane == meta[:, 1:2], 1.0, 0.0).astype(bf16)
            y1 = _dot(pick1, y)
            y2 = _dot(pick2, y)
            o_ref[s * t:(s + 1) * t, :] = h_ref[s * t:(s + 1) * t, :] + (meta[:, 2:3] * y1 + meta[:, 3:4] * y2)


def _combine(cnt, loc, dst, h, meta, yg):
    n = h.shape[0]
    tm = TM_OUT
    nsub = tm // T_SORT
    n_steps = n // tm
    prev = lambda i, *_: (jnp.maximum(i - 1, 0), 0)
    return pl.pallas_call(
        functools.partial(_combine_kernel, n_steps=n_steps, nsub=nsub),
        grid_spec=pltpu.PrefetchScalarGridSpec(
            num_scalar_prefetch=3,
            grid=(n_steps + 1,),
            in_specs=[
                pl.BlockSpec((tm, D_MODEL), prev),
                pl.BlockSpec((tm, LANES), prev),
                pl.BlockSpec(memory_space=pl.ANY),
            ],
            out_specs=pl.BlockSpec((tm, D_MODEL), prev),
            scratch_shapes=[pltpu.VMEM((2, nsub * 2 * T_SORT, ROW_CHUNKS, LANES), PACKED_DTYPE),
                            pltpu.SemaphoreType.DMA((2,))],
        ),
        out_shape=jax.ShapeDtypeStruct((n, D_MODEL), jnp.float32),
        compiler_params=pltpu.CompilerParams(
            dimension_semantics=("arbitrary",), vmem_limit_bytes=VMEM_LIMIT),
        name="combine",
    )(cnt, loc, dst, h, meta, yg)


def _routing_tables(cnt_out, n_tiles, n_tok):
    i32 = jnp.int32
    cnt = cnt_out.reshape(n_tiles, N_EXPERTS, LANES)[:, :, 0].astype(i32)
    count = jnp.sum(cnt, axis=0)
    padded = ((count + MOE_BLOCK - 1) // MOE_BLOCK) * MOE_BLOCK
    pad_end = jnp.cumsum(padded)
    pad_start = pad_end - padded
    run_end = jnp.cumsum(cnt, axis=0)
    rbase = run_end - cnt
    dst = pad_start[None, :] + rbase
    loc = jnp.cumsum(cnt, axis=1) - cnt
    src = loc + (jnp.arange(n_tiles, dtype=i32) * (2 * T_SORT))[:, None]
    n_blocks = (2 * n_tok) // MOE_BLOCK + N_EXPERTS
    block_start = jnp.arange(n_blocks, dtype=i32) * MOE_BLOCK
    block_e = jnp.minimum(jnp.sum(pad_end[None, :] <= block_start[:, None], axis=1), N_EXPERTS - 1).astype(i32)
    nused = (pad_end[-1] // MOE_BLOCK).astype(i32).reshape(1)
    ex = jnp.arange(N_EXPERTS, dtype=i32)
    sel = (block_e[None, :] == ex[:, None]).astype(i32)
    pick = lambda per_expert: jnp.sum(per_expert[..., :, None] * sel, axis=-2)
    r0 = block_start - pick(pad_start)
    nvalid = jnp.clip(pick(count) - r0, 0, MOE_BLOCK)
    tlo = jnp.sum(pick(run_end) <= r0[None, :], axis=0)
    thi = jnp.sum(pick(rbase) < (r0 + MOE_BLOCK)[None, :], axis=0)
    later_used = (ex[None, :] > ex[:, None]) & (count[None, :] > 0)
    nxt = jnp.min(jnp.where(later_used, ex[None, :], N_EXPERTS), axis=1)
    nxt = jnp.where(nxt < N_EXPERTS, nxt, -1)
    flat = lambda a: a.reshape(-1).astype(i32)
    return dict(cnt=flat(cnt), src=flat(src), dst=flat(dst), loc=flat(loc), rbase=flat(rbase),
                block_e=block_e, nused=nused, r0=flat(r0), nvalid=flat(nvalid), tlo=flat(tlo), thi=flat(thi),
                nxt=flat(nxt), n_slots=n_blocks * MOE_BLOCK)


def _stage1(x, positions, attn_norm_gain, w_in, q_norm_gain, k_norm_gain, conv_w, conv_out_gain):
    bsz, seq, _ = x.shape
    n = bsz * seq
    f32, bf16 = jnp.float32, jnp.bfloat16
    w = w_in[0]
    wqk = w[:, :2 * ATTN_WIDTH].astype(bf16)
    wvt = w[:, 2 * ATTN_WIDTH:3 * ATTN_WIDTH].T.astype(bf16)
    wc = w[:, 3 * ATTN_WIDTH:].astype(bf16)
    scale = DK ** -0.5 * LOG2E
    gqk = jnp.concatenate([jnp.tile(q_norm_gain[0].astype(f32), 2 * N_HEADS) * scale,
                           jnp.tile(k_norm_gain[0].astype(f32), 2 * N_HEADS)]).reshape(1, -1)
    freqs = (ROPE_THETA ** (-jnp.arange(0, ROT_DIM, 2, dtype=f32) / ROT_DIM)).reshape(SUBLANES, 1)
    return _inproj(x.reshape(n, D_MODEL), positions.reshape(1, n),
                   attn_norm_gain[0].reshape(1, -1).astype(f32), wqk, wvt, wc, gqk, freqs,
                   conv_w[0].astype(f32), conv_out_gain[0].reshape(1, -1).astype(f32), seq)


def kernel(x, positions, attn_norm_gain, w_in, q_norm_gain, k_norm_gain, lambda_q1, lambda_k1, lambda_q2, lambda_k2, subln_gain, conv_w, conv_out_gain, w_out, ffn_norm_gain, w_group_router, w_expert_router, w_gate, w_up, w_down):
    bsz, seq, _ = x.shape
    n = bsz * seq
    f32, bf16 = jnp.float32, jnp.bfloat16
    assert TM_OUT % T_SORT == 0 and seq % TM_IN == 0 and seq % TQ == 0 and TQ % KV_FULL == 0
    q, k, vt, conv_o = _stage1(x, positions, attn_norm_gain, w_in, q_norm_gain, k_norm_gain,
                                conv_w, conv_out_gain)
    attn_o = _attention(q, k, vt,
                        lambda_q1[0].reshape(1, -1).astype(f32), lambda_k1[0].reshape(1, -1).astype(f32),
                        lambda_q2[0].reshape(1, -1).astype(f32), lambda_k2[0].reshape(1, -1).astype(f32),
                        subln_gain[0].reshape(-1, 1).astype(f32), bsz, seq)

    wr = jnp.zeros((D_MODEL, ROUTER_ROWS), f32)
    wr = wr.at[:, 0:N_GROUPS].set(w_group_router[0].astype(f32))
    wr = wr.at[:, EXPERT_ROW0:EXPERT_ROW0 + N_EXPERTS].set(
        jnp.transpose(w_expert_router[0].astype(f32), (1, 0, 2)).reshape(D_MODEL, N_EXPERTS))
    wrh = wr.astype(bf16)
    wr2 = jnp.concatenate([wrh, (wr - wrh.astype(f32)).astype(bf16)], axis=1)
    h, xs, meta, cnt_out = _outproj(x.reshape(n, D_MODEL), attn_o, conv_o, w_out[0].astype(bf16),
                                    ffn_norm_gain[0].reshape(1, -1).astype(f32), wr2)

    tabs = _routing_tables(cnt_out, n // T_SORT, n)
    yg = _experts(tabs, xs, w_gate[0], w_up[0], w_down[0])
    out = _combine(tabs["cnt"], tabs["loc"], tabs["dst"], h, meta, yg)
    return out.reshape(x.shape)
```

```python
import functools
import math

import jax
import jax.numpy as jnp
from jax import lax
from jax.experimental import pallas as pl
from jax.experimental.pallas import tpu as pltpu

D_MODEL = 1024
N_HEADS = 4
DK = 64
DV = 128
ROT_DIM = 16
ROPE_THETA = 500000.0
ATTN_WIDTH = N_HEADS * DV
CONV_WIDTH = 512
NORM_EPS = 1e-6
LOG2E = 1.4426950408889634
LAMBDA_INIT = 0.8 - 0.6 * math.exp(-0.3 * 0)
N_GROUPS = 4
EPG = 8
N_EXPERTS = N_GROUPS * EPG
D_FF = 512
MOE_BLOCK = 256

LANES = 128
SUBLANES = 8
ROW_WORDS = D_MODEL // 2
ROW_CHUNKS = ROW_WORDS // LANES
PACKED_DTYPE = jnp.uint32

TM_IN = 512
TQ = 512
HQ = TQ // 2
KV_FULL = 512
SUM_ROWS = 16
T_SORT = 256
TM_OUT = 512
N_XBUF = 3
GATHER_UNROLL = 24
VMEM_LIMIT = 48 * 1024 * 1024


def _nt_dot(a, b):
    return lax.dot_general(a, b, (((1,), (1,)), ((), ())), preferred_element_type=jnp.float32)


def _dot(a, b):
    return jnp.dot(a, b, preferred_element_type=jnp.float32)


def _split3(x):
    h = x.astype(jnp.bfloat16)
    r = x - h.astype(jnp.float32)
    m = r.astype(jnp.bfloat16)
    l = (r - m.astype(jnp.float32)).astype(jnp.bfloat16)
    return h, m, l


def _split2(x):
    h = x.astype(jnp.bfloat16)
    l = (x - h.astype(jnp.float32)).astype(jnp.bfloat16)
    return h, l


def _inproj_kernel(x_ref, pos_ref, g1_ref, wqk_ref, wvt_ref, wc_ref, gqk_ref, freq_ref,
                   cw_ref, cg_ref,
                   q_ref, k_ref, vt_ref, conv_ref,
                   carry_ref, *, tiles_per_seq):
    tm = x_ref.shape[0]
    i = pl.program_id(0)

    x = x_ref[...]
    ms = jnp.mean(x * x, axis=-1, keepdims=True)
    hn = (x * lax.rsqrt(ms + NORM_EPS) * g1_ref[...]).astype(jnp.bfloat16)

    pos = pos_ref[...].astype(jnp.float32)
    ang = freq_ref[...] * pos
    lane_r = lax.broadcasted_iota(jnp.int32, (LANES, SUBLANES), 0)
    f_c = lax.broadcasted_iota(jnp.int32, (LANES, SUBLANES), 1)
    in_rot = (lane_r % DK) < ROT_DIM
    expand = jnp.where(in_rot & ((lane_r % (ROT_DIM // 2)) == f_c), 1.0, 0.0).astype(jnp.bfloat16)

    def to_rows(t):
        h, m, l = _split3(t)
        r = _dot(expand, h) + _dot(expand, m) + _dot(expand, l)
        return r.T

    cos_r = to_rows(jnp.cos(ang))
    sin_r = to_rows(jnp.sin(ang))
    lane = lax.broadcasted_iota(jnp.int32, (tm, LANES), 1)
    d = lane % DK
    cos_r = jnp.where(d < ROT_DIM, cos_r, 1.0)
    sin_lo = jnp.where(d < ROT_DIM // 2, -sin_r, 0.0)
    sin_hi = jnp.where((d >= ROT_DIM // 2) & (d < ROT_DIM), sin_r, 0.0)

    qk = _dot(hn, wqk_ref[...])
    seg_r = lax.broadcasted_iota(jnp.int32, (2 * LANES, 2 * LANES), 0) // DK
    seg_c = lax.broadcasted_iota(jnp.int32, (2 * LANES, 2 * LANES), 1) // DK
    seg_mean = jnp.where(seg_r == seg_c, 1.0 / DK, 0.0).astype(jnp.bfloat16)
    half = ROT_DIM // 2
    for c2 in range(4):
        blk = qk[:, c2 * 256:(c2 + 1) * 256]
        msq = _dot((blk * blk).astype(jnp.bfloat16), seg_mean)
        y2 = blk * lax.rsqrt(msq + NORM_EPS) * gqk_ref[:, c2 * 256:(c2 + 1) * 256]
        for c1 in range(2):
            c = c2 * 2 + c1
            y = y2[:, c1 * LANES:(c1 + 1) * LANES]
            rot = (y * cos_r
                   + pltpu.roll(y, LANES - half, 1) * sin_lo
                   + pltpu.roll(y, half, 1) * sin_hi)
            if c < N_HEADS:
                q_ref[:, c * LANES:(c + 1) * LANES] = rot.astype(jnp.bfloat16)
            else:
                h = c - N_HEADS
                k_ref[:, h * LANES:(h + 1) * LANES] = rot.astype(jnp.bfloat16)

    vt_ref[...] = _nt_dot(wvt_ref[...], hn).astype(jnp.bfloat16)

    cp = _dot(hn, wc_ref[...])
    cb = cp[:, :CONV_WIDTH]
    y = cp[:, CONV_WIDTH:2 * CONV_WIDTH] * cp[:, 2 * CONV_WIDTH:]

    @pl.when(i % tiles_per_seq == 0)
    def _():
        carry_ref[...] = jnp.zeros_like(carry_ref)

    prev = carry_ref[...]
    row = lax.broadcasted_iota(jnp.int32, (tm, CONV_WIDTH), 0)
    p1 = prev[SUBLANES - 1:SUBLANES, :]
    p2 = prev[SUBLANES - 2:SUBLANES - 1, :]
    y1 = jnp.where(row == 0, p1, pltpu.roll(y, 1, 0))
    y2 = jnp.where(row == 0, p2, jnp.where(row == 1, p1, pltpu.roll(y, 2, 0)))
    carry_ref[...] = y[tm - SUBLANES:, :]
    z = cw_ref[0:1, :] * y2 + cw_ref[1:2, :] * y1 + cw_ref[2:3, :] * y
    co = cb * z
    cms = jnp.mean(co * co, axis=-1, keepdims=True)
    conv_ref[...] = (co * lax.rsqrt(cms + NORM_EPS) * cg_ref[...]).astype(jnp.bfloat16)


def _inproj(x2, pos_row, g1, wqk, wvt, wc, gqk, freqs, cw, cg, seq):
    n = x2.shape[0]
    tm = TM_IN
    grid = (n // tm,)
    const = lambda i: (0, 0)
    return pl.pallas_call(
        functools.partial(_inproj_kernel, tiles_per_seq=seq // tm),
        grid=grid,
        in_specs=[
            pl.BlockSpec((tm, D_MODEL), lambda i: (i, 0)),
            pl.BlockSpec((1, tm), lambda i: (0, i)),
            pl.BlockSpec((1, D_MODEL), const),
            pl.BlockSpec((D_MODEL, 1024), const),
            pl.BlockSpec((512, D_MODEL), const),
            pl.BlockSpec((D_MODEL, 1536), const),
            pl.BlockSpec((1, 1024), const),
            pl.BlockSpec((SUBLANES, 1), const),
            pl.BlockSpec((3, CONV_WIDTH), const),
            pl.BlockSpec((1, CONV_WIDTH), const),
        ],
        out_specs=[
            pl.BlockSpec((tm, ATTN_WIDTH), lambda i: (i, 0)),
            pl.BlockSpec((tm, ATTN_WIDTH), lambda i: (i, 0)),
            pl.BlockSpec((ATTN_WIDTH, tm), lambda i: (0, i)),
            pl.BlockSpec((tm, CONV_WIDTH), lambda i: (i, 0)),
        ],
        out_shape=[
            jax.ShapeDtypeStruct((n, ATTN_WIDTH), jnp.bfloat16),
            jax.ShapeDtypeStruct((n, ATTN_WIDTH), jnp.bfloat16),
            jax.ShapeDtypeStruct((ATTN_WIDTH, n), jnp.bfloat16),
            jax.ShapeDtypeStruct((n, CONV_WIDTH), jnp.bfloat16),
        ],
        scratch_shapes=[pltpu.VMEM((SUBLANES, CONV_WIDTH), jnp.float32)],
        compiler_params=pltpu.CompilerParams(
            dimension_semantics=("arbitrary",), vmem_limit_bytes=VMEM_LIMIT),
        name="inproj",
    )(x2, pos_row, g1, wqk, wvt, wc, gqk, freqs, cw, cg)


def _attn_kernel(q_ref, k_ref, vt_ref, lq1_ref, lk1_ref, lq2_ref, lk2_ref, sg_ref, o_ref, acc_ref, *, nq):
    lam = (jnp.exp(jnp.sum(lq1_ref[...] * lk1_ref[...], axis=-1, keepdims=True))
           - jnp.exp(jnp.sum(lq2_ref[...] * lk2_ref[...], axis=-1, keepdims=True))
           + LAMBDA_INIT)
    tri = (lax.broadcasted_iota(jnp.int32, (HQ, HQ), 0)
           <= lax.broadcasted_iota(jnp.int32, (HQ, HQ), 1))
    row_d = lax.broadcasted_iota(jnp.int32, (LANES, HQ), 0)
    ones_rows = jnp.ones((SUM_ROWS, KV_FULL), jnp.bfloat16)

    def masked(s, n_tri):
        blocks = [jnp.where(tri, s[:, c * HQ:(c + 1) * HQ], -jnp.inf) for c in range(n_tri)]
        return jnp.concatenate(blocks + [s[:, n_tri * HQ:]], axis=1) if n_tri * HQ < s.shape[1] else \
            jnp.concatenate(blocks, axis=1)

    todo = {}
    for t in range(nq):
        lo = t * TQ
        todo[t] = ([(lo, lo + HQ, "mixed"), (lo + HQ, lo + TQ, "upper")]
                   + [(k0, k0 + KV_FULL, "full") for k0 in range(0, lo, KV_FULL)])
    tasks = []
    while any(todo.values()):
        for t in reversed(range(nq)):
            if todo[t]:
                tasks.append((t,) + todo[t].pop(0))
    last_task = {t: max(i for i, tk in enumerate(tasks) if tk[0] == t) for t in range(nq)}
    qzt, m_run = {}, {}

    def q_transposed(t):
        parts = []
        for half in range(2):
            r0 = t * TQ + half * HQ
            qt = q_ref[r0:r0 + HQ, :].astype(jnp.float32).T
            parts.append(jnp.where(row_d < DK, qt, 0.0))
            parts.append(jnp.where(row_d >= DK, qt, 0.0))
        return jnp.concatenate(parts, axis=1).astype(jnp.bfloat16)

    def scores(task):
        t, k0, k1, kind = task
        if t not in qzt:
            qzt[t] = q_transposed(t)
        kb = k_ref[k0:k1, :]
        if kind == "upper":
            return masked(_dot(kb, qzt[t][:, 2 * HQ:]), 2)
        s = _dot(kb, qzt[t])
        return masked(s, 2) if kind == "mixed" else s

    def softmax(task, s):
        t, _, _, kind = task
        mx = jnp.max(s, axis=0, keepdims=True)
        if kind == "mixed":
            m_run[t] = mx
            return jnp.exp2(s - mx).astype(jnp.bfloat16), None
        m_old = m_run[t][:, 2 * HQ:] if kind == "upper" else m_run[t]
        m_new = jnp.maximum(m_old, mx)
        alpha = jnp.exp2(m_old - m_new)
        p = jnp.exp2(s - m_new)
        m_run[t] = jnp.concatenate([m_run[t][:, :2 * HQ], m_new], axis=1) if kind == "upper" else m_new
        return p.astype(jnp.bfloat16), alpha

    def accumulate(task, p, alpha):
        t, k0, k1, kind = task
        vta = jnp.concatenate([vt_ref[:, k0:k1], ones_rows[:, :k1 - k0]], axis=0)
        pv = _dot(vta, p)
        if kind == "mixed":
            acc_ref[t] = pv
        elif kind == "upper":
            acc_ref[t, :, 2 * HQ:] = alpha * acc_ref[t, :, 2 * HQ:] + pv
        else:
            acc_ref[t] = alpha * acc_ref[t] + pv

    def finish(t):
        o_all = acc_ref[t, 0:DV, :] * (1.0 / acc_ref[t, DV:DV + 1, :])
        for half in range(2):
            o = (o_all[:, (2 * half) * HQ:(2 * half + 1) * HQ]
                 - lam * o_all[:, (2 * half + 1) * HQ:(2 * half + 2) * HQ])
            ms = jnp.mean(o * o, axis=0, keepdims=True)
            on = o * lax.rsqrt(ms + NORM_EPS) * sg_ref[...] * (1.0 - LAMBDA_INIT)
            r0 = t * TQ + half * HQ
            o_ref[r0:r0 + HQ, :] = on.T.astype(jnp.bfloat16)

    n = len(tasks)
    s_prev, p_prev = None, None
    for step in range(n + 2):
        if step >= 2:
            accumulate(tasks[step - 2], *p_prev)
            if last_task[tasks[step - 2][0]] == step - 2:
                finish(tasks[step - 2][0])
        if 1 <= step <= n:
            p_prev = softmax(tasks[step - 1], s_prev)
        if step < n:
            s_prev = scores(tasks[step])


def _attention(q, k, vt, lq1, lk1, lq2, lk2, sg_col, bsz, seq):
    n = q.shape[0]
    nq = seq // TQ
    vec = lambda b, h: (0, 0)
    return pl.pallas_call(
        functools.partial(_attn_kernel, nq=nq),
        grid=(bsz, N_HEADS),
        in_specs=[
            pl.BlockSpec((seq, DV), lambda b, h: (b, h)),
            pl.BlockSpec((seq, DV), lambda b, h: (b, h)),
            pl.BlockSpec((DV, seq), lambda b, h: (h, b)),
            pl.BlockSpec((1, DK), vec), pl.BlockSpec((1, DK), vec),
            pl.BlockSpec((1, DK), vec), pl.BlockSpec((1, DK), vec),
            pl.BlockSpec((DV, 1), vec),
        ],
        out_specs=pl.BlockSpec((seq, DV), lambda b, h: (b, h)),
        out_shape=jax.ShapeDtypeStruct((n, ATTN_WIDTH), jnp.bfloat16),
        scratch_shapes=[pltpu.VMEM((nq, DV + SUM_ROWS, 4 * HQ), jnp.float32)],
        compiler_params=pltpu.CompilerParams(
            dimension_semantics=("arbitrary", "arbitrary"), vmem_limit_bytes=VMEM_LIMIT),
        name="attn",
    )(q, k, vt, lq1, lk1, lq2, lk2, sg_col)


ROUTER_ROWS = 128
EXPERT_ROW0 = 32


def _outproj_kernel(x_ref, attn_ref, conv_ref, wo_ref, g2_ref, wr2_ref,
                    h_ref, xs_ref, meta_ref, cnt_ref):
    tm = x_ref.shape[0]
    t = T_SORT
    f32, bf16 = jnp.float32, jnp.bfloat16
    a = jnp.concatenate([attn_ref[...], conv_ref[...]], axis=1)
    h = x_ref[...] + _dot(a, wo_ref[...])
    h_ref[...] = h
    ms = jnp.mean(h * h, axis=-1, keepdims=True)
    hn = h * lax.rsqrt(ms + NORM_EPS) * g2_ref[...]
    hn_hi, hn_lo = _split2(hn)

    hh = _dot(hn_hi, wr2_ref[...])
    logits = hh[:, :ROUTER_ROWS] + hh[:, ROUTER_ROWS:] + _dot(hn_lo, wr2_ref[:, :ROUTER_ROWS])
    lt = logits.T
    row8 = lax.broadcasted_iota(jnp.int32, (SUBLANES, tm), 0).astype(f32)
    neg_inf = -jnp.inf

    def first_argmax(v):
        mx = jnp.max(v, axis=0, keepdims=True)
        idx = jnp.min(jnp.where(v == mx, row8, float(SUBLANES)), axis=0, keepdims=True)
        return mx, idx

    g_log = jnp.where(row8 < N_GROUPS, lt[0:SUBLANES, :], neg_inf)
    g_max, g_sel = first_argmax(g_log)
    g_gate = 1.0 / jnp.sum(jnp.exp(g_log - g_max), axis=0, keepdims=True)
    e_log = jnp.zeros((EPG, tm), f32)
    for g in range(N_GROUPS):
        rows = lt[EXPERT_ROW0 + g * EPG:EXPERT_ROW0 + (g + 1) * EPG, :]
        e_log = jnp.where(g_sel == float(g), rows, e_log)
    v1, i1 = first_argmax(e_log)
    v2, i2 = first_argmax(jnp.where(row8 == i1, neg_inf, e_log))
    tt = jnp.exp(v2 - v1)
    w1 = g_gate / (1.0 + tt)
    w2 = g_gate * tt / (1.0 + tt)
    e1 = g_sel * float(EPG) + i1
    e2 = g_sel * float(EPG) + i2

    row32 = lax.broadcasted_iota(jnp.int32, (N_EXPERTS, tm), 0).astype(f32)
    oh1 = row32 == e1
    oh2 = row32 == e2
    c = jnp.where(oh1 | oh2, 1.0, 0.0).astype(bf16)
    tok_r = lax.broadcasted_iota(jnp.int32, (tm, tm), 0)
    tok_c = lax.broadcasted_iota(jnp.int32, (tm, tm), 1)
    same_tile = (tok_r // t) == (tok_c // t)
    rank = _dot(c, jnp.where(same_tile & (tok_r < tok_c), 1.0, 0.0).astype(bf16))
    cnt_b = _dot(c, jnp.where(same_tile, 1.0, 0.0).astype(bf16))
    ex_r = lax.broadcasted_iota(jnp.int32, (N_EXPERTS, N_EXPERTS), 0)
    ex_c = lax.broadcasted_iota(jnp.int32, (N_EXPERTS, N_EXPERTS), 1)
    lower = jnp.where(ex_c < ex_r, 1.0, 0.0).astype(bf16)
    start_b = _dot(lower, cnt_b.astype(bf16))
    pos_e = start_b + rank
    p1 = jnp.sum(jnp.where(oh1, pos_e, 0.0), axis=0, keepdims=True)
    p2 = jnp.sum(jnp.where(oh2, pos_e, 0.0), axis=0, keepdims=True)

    srow = lax.broadcasted_iota(jnp.int32, (2 * t, t), 0).astype(f32)
    for s in range(tm // t):
        cols = slice(s * t, (s + 1) * t)
        perm = jnp.where((srow == p1[:, cols]) | (srow == p2[:, cols]), 1.0, 0.0).astype(bf16)
        xs = _dot(perm, hn_hi[cols, :])
        xs_ref[s * 2 * t:(s + 1) * 2 * t] = _pack_rows(xs)
        cnt_ref[s * N_EXPERTS:(s + 1) * N_EXPERTS, :] = cnt_b[:, s * t:s * t + LANES]

    meta = jnp.concatenate([p1, p2, w1, w2, jnp.zeros((LANES - 4, tm), f32)], axis=0)
    meta_ref[...] = meta.T


def _outproj(x2, attn_o, conv_o, wo, g2, wr2):
    n = x2.shape[0]
    tm = TM_OUT
    nsub = tm // T_SORT
    const = lambda i: (0, 0)

    return pl.pallas_call(
        _outproj_kernel,
        grid=(n // tm,),
        in_specs=[
            pl.BlockSpec((tm, D_MODEL), lambda i: (i, 0)),
            pl.BlockSpec((tm, ATTN_WIDTH), lambda i: (i, 0)),
            pl.BlockSpec((tm, CONV_WIDTH), lambda i: (i, 0)),
            pl.BlockSpec((D_MODEL, D_MODEL), const),
            pl.BlockSpec((1, D_MODEL), const),
            pl.BlockSpec((D_MODEL, 2 * ROUTER_ROWS), const),
        ],
        out_specs=[
            pl.BlockSpec((tm, D_MODEL), lambda i: (i, 0)),
            pl.BlockSpec((2 * tm, ROW_CHUNKS, LANES), lambda i: (i, 0, 0)),
            pl.BlockSpec((tm, LANES), lambda i: (i, 0)),
            pl.BlockSpec((nsub * N_EXPERTS, LANES), lambda i: (i, 0)),
        ],
        out_shape=[
            jax.ShapeDtypeStruct((n, D_MODEL), jnp.float32),
            jax.ShapeDtypeStruct((2 * n, ROW_CHUNKS, LANES), PACKED_DTYPE),
            jax.ShapeDtypeStruct((n, LANES), jnp.float32),
            jax.ShapeDtypeStruct((n // T_SORT * N_EXPERTS, LANES), jnp.float32),
        ],
        compiler_params=pltpu.CompilerParams(
            dimension_semantics=("arbitrary",), vmem_limit_bytes=VMEM_LIMIT),
        name="outproj",
    )(x2, attn_o, conv_o, wo, g2, wr2)


def _pack_rows(x):
    r = x.shape[0]
    w = pltpu.pack_elementwise([x[:, :ROW_WORDS], x[:, ROW_WORDS:]], packed_dtype=jnp.bfloat16)
    return pltpu.bitcast(w, PACKED_DTYPE).reshape(r, ROW_CHUNKS, LANES)


def _packed_zero_rows(r):
    z = jnp.zeros((r, ROW_CHUNKS, LANES), jnp.float32)
    w = pltpu.pack_elementwise([z, z], packed_dtype=jnp.bfloat16)
    return pltpu.bitcast(w, PACKED_DTYPE)


def _unpack_rows(u):
    r = u.shape[0]
    w = u.reshape(r, ROW_WORDS)
    lo = pltpu.unpack_elementwise(w, index=0, packed_dtype=jnp.bfloat16, unpacked_dtype=jnp.float32)
    hi = pltpu.unpack_elementwise(w, index=1, packed_dtype=jnp.bfloat16, unpacked_dtype=jnp.float32)
    return jnp.concatenate([lo, hi], axis=1).astype(jnp.bfloat16)


def _expert_kernel(nblk_ref, blk0_ref, be_ref, nused_ref, r0_ref, nvalid_ref, tlo_ref, thi_ref,
                   cnt_ref, src_ref, rbase_ref,
                   xs_hbm, wg_ref, wu_ref, wd_ref, y_hbm,
                   xbuf, ybuf, wg_s, wu_s, wd_s, sem, ysem):
    e = pl.program_id(0)
    nused = nused_ref[0]
    n_blocks = be_ref.shape[0]
    n_tiles = cnt_ref.shape[0] // N_EXPERTS

    def start_run(blk, slot, t, live):
        r0 = r0_ref[blk]
        k = t * N_EXPERTS + be_ref[blk]
        lo = jnp.maximum(rbase_ref[k], r0)
        hi = jnp.minimum(rbase_ref[k] + cnt_ref[k], r0 + MOE_BLOCK)
        rows = jnp.where(live, hi - lo, 0)

        @pl.when(rows > 0)
        def _():
            pltpu.make_async_copy(xs_hbm.at[pl.ds(src_ref[k] + lo - rbase_ref[k], rows)],
                                  xbuf.at[slot, pl.ds(lo - r0, rows)], sem.at[slot]).start()

    def gather_rolled(blk, slot, t_from):
        def run(t, carry):
            start_run(blk, slot, t, True)
            return carry
        lax.fori_loop(t_from, thi_ref[blk], run, 0)

    def gather_unrolled(blk, slot, live):
        blk = jnp.minimum(blk, n_blocks - 1)
        for r in range(GATHER_UNROLL):
            t = tlo_ref[blk] + r
            start_run(blk, slot, jnp.minimum(t, n_tiles - 1), live & (t < thi_ref[blk]))

        @pl.when(live & (tlo_ref[blk] + GATHER_UNROLL < thi_ref[blk]))
        def _():
            gather_rolled(blk, slot, tlo_ref[blk] + GATHER_UNROLL)

    def y_copy(blk, slot):
        return pltpu.make_async_copy(ybuf.at[slot], y_hbm.at[pl.ds(blk * MOE_BLOCK, MOE_BLOCK)], ysem.at[slot])

    @pl.when(e == 0)
    def _():
        for slot in range(N_XBUF):
            xbuf[slot] = _packed_zero_rows(MOE_BLOCK)
        gather_rolled(0, 0, tlo_ref[0])

        @pl.when(nused > 1)
        def _():
            gather_rolled(1, 1, tlo_ref[1])

    @pl.when(nblk_ref[e] > 0)
    def _():
        wg_s[...] = wg_ref[0].astype(jnp.bfloat16)
        wu_s[...] = wu_ref[0].astype(jnp.bfloat16)
        wd_s[...] = wd_ref[0].astype(jnp.bfloat16)

        def block(c, carry):
            b = blk0_ref[e] + c
            slot = b % N_XBUF
            nv = nvalid_ref[b]
            pltpu.make_async_copy(xs_hbm.at[pl.ds(0, nv)], xbuf.at[slot, pl.ds(0, nv)], sem.at[slot]).wait()

            x = _unpack_rows(xbuf[slot])
            g = _dot(x, wg_s[...])
            u = _dot(x, wu_s[...])
            act = (g / (1.0 + jnp.exp(-g)) * u).astype(jnp.bfloat16)
            y = _dot(act, wd_s[...])

            yslot = b % 2

            @pl.when(b >= 2)
            def _():
                y_copy(b - 2, yslot).wait()
            ybuf[yslot] = _pack_rows(y)
            y_copy(b, yslot).start()

            gather_unrolled(b + 2, (b + 2) % N_XBUF, b + 2 < nused)
            return carry
        lax.fori_loop(0, nblk_ref[e], block, 0)

    @pl.when(e == N_EXPERTS - 1)
    def _():
        for back in (2, 1):
            @pl.when(nused >= back)
            def _():
                y_copy(nused - back, (nused - back) % 2).wait()
        ybuf[0] = _packed_zero_rows(MOE_BLOCK)

        def zero_block(b, carry):
            y_copy(b, 0).start()
            y_copy(b, 0).wait()
            return carry
        lax.fori_loop(nused, n_blocks, zero_block, 0)


def _experts(tabs, xs, w_gate, w_up, w_down):
    n_slots = tabs["n_slots"]
    bf16 = jnp.bfloat16
    w_map = lambda e, *_: (e, 0, 0)
    return pl.pallas_call(
        _expert_kernel,
        grid_spec=pltpu.PrefetchScalarGridSpec(
            num_scalar_prefetch=11,
            grid=(N_EXPERTS,),
            in_specs=[
                pl.BlockSpec(memory_space=pl.ANY),
                pl.BlockSpec((1, D_MODEL, D_FF), w_map),
                pl.BlockSpec((1, D_MODEL, D_FF), w_map),
                pl.BlockSpec((1, D_FF, D_MODEL), w_map),
            ],
            out_specs=pl.BlockSpec(memory_space=pl.ANY),
            scratch_shapes=[pltpu.VMEM((N_XBUF, MOE_BLOCK, ROW_CHUNKS, LANES), PACKED_DTYPE),
                            pltpu.VMEM((2, MOE_BLOCK, ROW_CHUNKS, LANES), PACKED_DTYPE),
                            pltpu.VMEM((D_MODEL, D_FF), bf16),
                            pltpu.VMEM((D_MODEL, D_FF), bf16),
                            pltpu.VMEM((D_FF, D_MODEL), bf16),
                            pltpu.SemaphoreType.DMA((N_XBUF,)),
                            pltpu.SemaphoreType.DMA((2,))],
        ),
        out_shape=jax.ShapeDtypeStruct((n_slots, ROW_CHUNKS, LANES), PACKED_DTYPE),
        compiler_params=pltpu.CompilerParams(
            dimension_semantics=("arbitrary",), vmem_limit_bytes=VMEM_LIMIT),
        name="experts",
    )(tabs["nblk"], tabs["blk0"], tabs["block_e"], tabs["nused"], tabs["r0"], tabs["nvalid"], tabs["tlo"],
      tabs["thi"], tabs["cnt"], tabs["src"], tabs["rbase"], xs, w_gate, w_up, w_down)


def _combine_kernel(cnt_ref, loc_ref, dst_ref, h_ref, meta_ref, yg_hbm, o_ref, ybuf, sem, *, n_steps, nsub):
    i = pl.program_id(0)
    t = T_SORT
    f32, bf16 = jnp.float32, jnp.bfloat16

    slot_in = i % 2
    tile0 = jnp.minimum(i, n_steps - 1) * nsub
    for s in range(nsub):
        for e in range(N_EXPERTS):
            k = (tile0 + s) * N_EXPERTS + e
            rows = jnp.where(i < n_steps, cnt_ref[k], 0)

            @pl.when(rows > 0)
            def _():
                pltpu.make_async_copy(yg_hbm.at[pl.ds(dst_ref[k], rows)],
                                      ybuf.at[slot_in, pl.ds(s * 2 * t + loc_ref[k], rows)],
                                      sem.at[slot_in]).start()

    @pl.when(i > 0)
    def _():
        slot = (i - 1) % 2
        pltpu.make_async_copy(yg_hbm.at[pl.ds(0, nsub * 2 * t)], ybuf.at[slot], sem.at[slot]).wait()
        lane = lax.broadcasted_iota(jnp.int32, (t, 2 * t), 1).astype(f32)
        for s in range(nsub):
            y = _unpack_rows(ybuf[slot, s * 2 * t:(s + 1) * 2 * t])
            meta = meta_ref[s * t:(s + 1) * t, :]
            pick1 = jnp.where(lane == meta[:, 0:1], 1.0, 0.0).astype(bf16)
            pick2 = jnp.where(lane == meta[:, 1:2], 1.0, 0.0).astype(bf16)
            y1 = _dot(pick1, y)
            y2 = _dot(pick2, y)
            o_ref[s * t:(s + 1) * t, :] = h_ref[s * t:(s + 1) * t, :] + (meta[:, 2:3] * y1 + meta[:, 3:4] * y2)


def _combine(cnt, loc, dst, h, meta, yg):
    n = h.shape[0]
    tm = TM_OUT
    nsub = tm // T_SORT
    n_steps = n // tm
    prev = lambda i, *_: (jnp.maximum(i - 1, 0), 0)
    return pl.pallas_call(
        functools.partial(_combine_kernel, n_steps=n_steps, nsub=nsub),
        grid_spec=pltpu.PrefetchScalarGridSpec(
            num_scalar_prefetch=3,
            grid=(n_steps + 1,),
            in_specs=[
                pl.BlockSpec((tm, D_MODEL), prev),
                pl.BlockSpec((tm, LANES), prev),
                pl.BlockSpec(memory_space=pl.ANY),
            ],
            out_specs=pl.BlockSpec((tm, D_MODEL), prev),
            scratch_shapes=[pltpu.VMEM((2, nsub * 2 * T_SORT, ROW_CHUNKS, LANES), PACKED_DTYPE),
                            pltpu.SemaphoreType.DMA((2,))],
        ),
        out_shape=jax.ShapeDtypeStruct((n, D_MODEL), jnp.float32),
        compiler_params=pltpu.CompilerParams(
            dimension_semantics=("arbitrary",), vmem_limit_bytes=VMEM_LIMIT),
        name="combine",
    )(cnt, loc, dst, h, meta, yg)


def _routing_tables(cnt_out, n_tiles, n_tok):
    i32 = jnp.int32
    cnt = cnt_out.reshape(n_tiles, N_EXPERTS, LANES)[:, :, 0].astype(i32)
    count = jnp.sum(cnt, axis=0)
    padded = ((count + MOE_BLOCK - 1) // MOE_BLOCK) * MOE_BLOCK
    pad_end = jnp.cumsum(padded)
    pad_start = pad_end - padded
    run_end = jnp.cumsum(cnt, axis=0)
    rbase = run_end - cnt
    dst = pad_start[None, :] + rbase
    loc = jnp.cumsum(cnt, axis=1) - cnt
    src = loc + (jnp.arange(n_tiles, dtype=i32) * (2 * T_SORT))[:, None]
    n_blocks = (2 * n_tok) // MOE_BLOCK + N_EXPERTS
    block_start = jnp.arange(n_blocks, dtype=i32) * MOE_BLOCK
    block_e = jnp.minimum(jnp.sum(pad_end[None, :] <= block_start[:, None], axis=1), N_EXPERTS - 1).astype(i32)
    nused = (pad_end[-1] // MOE_BLOCK).astype(i32).reshape(1)
    ex = jnp.arange(N_EXPERTS, dtype=i32)
    sel = (block_e[None, :] == ex[:, None]).astype(i32)
    pick = lambda per_expert: jnp.sum(per_expert[..., :, None] * sel, axis=-2)
    r0 = block_start - pick(pad_start)
    nvalid = jnp.clip(pick(count) - r0, 0, MOE_BLOCK)
    tlo = jnp.sum(pick(run_end) <= r0[None, :], axis=0)
    thi = jnp.sum(pick(rbase) < (r0 + MOE_BLOCK)[None, :], axis=0)
    flat = lambda a: a.reshape(-1).astype(i32)
    return dict(cnt=flat(cnt), src=flat(src), dst=flat(dst), loc=flat(loc), rbase=flat(rbase),
                block_e=block_e, nused=nused, r0=flat(r0), nvalid=flat(nvalid), tlo=flat(tlo), thi=flat(thi),
                nblk=flat(padded // MOE_BLOCK), blk0=flat(pad_start // MOE_BLOCK), n_slots=n_blocks * MOE_BLOCK)


def _stage1(x, positions, attn_norm_gain, w_in, q_norm_gain, k_norm_gain, conv_w, conv_out_gain):
    bsz, seq, _ = x.shape
    n = bsz * seq
    f32, bf16 = jnp.float32, jnp.bfloat16
    w = w_in[0]
    wqk = w[:, :2 * ATTN_WIDTH].astype(bf16)
    wvt = w[:, 2 * ATTN_WIDTH:3 * ATTN_WIDTH].T.astype(bf16)
    wc = w[:, 3 * ATTN_WIDTH:].astype(bf16)
    scale = DK ** -0.5 * LOG2E
    gqk = jnp.concatenate([jnp.tile(q_norm_gain[0].astype(f32), 2 * N_HEADS) * scale,
                           jnp.tile(k_norm_gain[0].astype(f32), 2 * N_HEADS)]).reshape(1, -1)
    freqs = (ROPE_THETA ** (-jnp.arange(0, ROT_DIM, 2, dtype=f32) / ROT_DIM)).reshape(SUBLANES, 1)
    return _inproj(x.reshape(n, D_MODEL), positions.reshape(1, n),
                   attn_norm_gain[0].reshape(1, -1).astype(f32), wqk, wvt, wc, gqk, freqs,
                   conv_w[0].astype(f32), conv_out_gain[0].reshape(1, -1).astype(f32), seq)


def kernel(x, positions, attn_norm_gain, w_in, q_norm_gain, k_norm_gain, lambda_q1, lambda_k1, lambda_q2, lambda_k2, subln_gain, conv_w, conv_out_gain, w_out, ffn_norm_gain, w_group_router, w_expert_router, w_gate, w_up, w_down):
    bsz, seq, _ = x.shape
    n = bsz * seq
    f32, bf16 = jnp.float32, jnp.bfloat16
    assert TM_OUT % T_SORT == 0 and seq % TM_IN == 0 and seq % TQ == 0 and TQ % KV_FULL == 0
    q, k, vt, conv_o = _stage1(x, positions, attn_norm_gain, w_in, q_norm_gain, k_norm_gain,
                                conv_w, conv_out_gain)
    attn_o = _attention(q, k, vt,
                        lambda_q1[0].reshape(1, -1).astype(f32), lambda_k1[0].reshape(1, -1).astype(f32),
                        lambda_q2[0].reshape(1, -1).astype(f32), lambda_k2[0].reshape(1, -1).astype(f32),
                        subln_gain[0].reshape(-1, 1).astype(f32), bsz, seq)

    wr = jnp.zeros((D_MODEL, ROUTER_ROWS), f32)
    wr = wr.at[:, 0:N_GROUPS].set(w_group_router[0].astype(f32))
    wr = wr.at[:, EXPERT_ROW0:EXPERT_ROW0 + N_EXPERTS].set(
        jnp.transpose(w_expert_router[0].astype(f32), (1, 0, 2)).reshape(D_MODEL, N_EXPERTS))
    wrh = wr.astype(bf16)
    wr2 = jnp.concatenate([wrh, (wr - wrh.astype(f32)).astype(bf16)], axis=1)
    h, xs, meta, cnt_out = _outproj(x.reshape(n, D_MODEL), attn_o, conv_o, w_out[0].astype(bf16),
                                    ffn_norm_gain[0].reshape(1, -1).astype(f32), wr2)

    tabs = _routing_tables(cnt_out, n // T_SORT, n)
    yg = _experts(tabs, xs, w_gate[0], w_up[0], w_down[0])
    out = _combine(tabs["cnt"], tabs["loc"], tabs["dst"], h, meta, yg)
    return out.reshape(x.shape)
```

```python
import functools
import math

import jax
import jax.numpy as jnp
from jax import lax
from jax.experimental import pallas as pl
from jax.experimental.pallas import tpu as pltpu

D_MODEL = 1024
N_HEADS = 4
DK = 64
DV = 128
ROT_DIM = 16
ROPE_THETA = 500000.0
ATTN_WIDTH = N_HEADS * DV
CONV_WIDTH = 512
NORM_EPS = 1e-6
LOG2E = 1.4426950408889634
LAMBDA_INIT = 0.8 - 0.6 * math.exp(-0.3 * 0)
N_GROUPS = 4
EPG = 8
N_EXPERTS = N_GROUPS * EPG
D_FF = 512
MOE_BLOCK = 256

LANES = 128
SUBLANES = 8
ROW_WORDS = D_MODEL // 2
ROW_CHUNKS = ROW_WORDS // LANES
PACKED_DTYPE = jnp.uint32

TM_IN = 512
TQ = 512
HQ = TQ // 2
KV_FULL = 512
SUM_ROWS = 16
T_SORT = 256
TM_OUT = 512
N_XBUF = 3
GATHER_UNROLL = 24
VMEM_LIMIT = 48 * 1024 * 1024


def _nt_dot(a, b):
    return lax.dot_general(a, b, (((1,), (1,)), ((), ())), preferred_element_type=jnp.float32)


def _dot(a, b):
    return jnp.dot(a, b, preferred_element_type=jnp.float32)


def _split3(x):
    h = x.astype(jnp.bfloat16)
    r = x - h.astype(jnp.float32)
    m = r.astype(jnp.bfloat16)
    l = (r - m.astype(jnp.float32)).astype(jnp.bfloat16)
    return h, m, l


def _split2(x):
    h = x.astype(jnp.bfloat16)
    l = (x - h.astype(jnp.float32)).astype(jnp.bfloat16)
    return h, l


def _inproj_kernel(x_ref, pos_ref, g1_ref, wqk_ref, wvt_ref, wc_ref, gqk_ref, freq_ref,
                   cw_ref, cg_ref,
                   q_ref, k_ref, vt_ref, conv_ref,
                   carry_ref, *, tiles_per_seq):
    tm = x_ref.shape[0]
    i = pl.program_id(0)

    x = x_ref[...]
    ms = jnp.mean(x * x, axis=-1, keepdims=True)
    hn = (x * lax.rsqrt(ms + NORM_EPS) * g1_ref[...]).astype(jnp.bfloat16)

    pos = pos_ref[...].astype(jnp.float32)
    ang = freq_ref[...] * pos
    lane_r = lax.broadcasted_iota(jnp.int32, (LANES, SUBLANES), 0)
    f_c = lax.broadcasted_iota(jnp.int32, (LANES, SUBLANES), 1)
    in_rot = (lane_r % DK) < ROT_DIM
    expand = jnp.where(in_rot & ((lane_r % (ROT_DIM // 2)) == f_c), 1.0, 0.0).astype(jnp.bfloat16)

    def to_rows(t):
        h, m, l = _split3(t)
        r = _dot(expand, h) + _dot(expand, m) + _dot(expand, l)
        return r.T

    cos_r = to_rows(jnp.cos(ang))
    sin_r = to_rows(jnp.sin(ang))
    lane = lax.broadcasted_iota(jnp.int32, (tm, LANES), 1)
    d = lane % DK
    cos_r = jnp.where(d < ROT_DIM, cos_r, 1.0)
    sin_lo = jnp.where(d < ROT_DIM // 2, -sin_r, 0.0)
    sin_hi = jnp.where((d >= ROT_DIM // 2) & (d < ROT_DIM), sin_r, 0.0)

    qk = _dot(hn, wqk_ref[...])
    seg_r = lax.broadcasted_iota(jnp.int32, (2 * LANES, 2 * LANES), 0) // DK
    seg_c = lax.broadcasted_iota(jnp.int32, (2 * LANES, 2 * LANES), 1) // DK
    seg_mean = jnp.where(seg_r == seg_c, 1.0 / DK, 0.0).astype(jnp.bfloat16)
    half = ROT_DIM // 2
    for c2 in range(4):
        blk = qk[:, c2 * 256:(c2 + 1) * 256]
        msq = _dot((blk * blk).astype(jnp.bfloat16), seg_mean)
        y2 = blk * lax.rsqrt(msq + NORM_EPS) * gqk_ref[:, c2 * 256:(c2 + 1) * 256]
        for c1 in range(2):
            c = c2 * 2 + c1
            y = y2[:, c1 * LANES:(c1 + 1) * LANES]
            rot = (y * cos_r
                   + pltpu.roll(y, LANES - half, 1) * sin_lo
                   + pltpu.roll(y, half, 1) * sin_hi)
            if c < N_HEADS:
                q_ref[:, c * LANES:(c + 1) * LANES] = rot.astype(jnp.bfloat16)
            else:
                h = c - N_HEADS
                k_ref[:, h * LANES:(h + 1) * LANES] = rot.astype(jnp.bfloat16)

    vt_ref[...] = _nt_dot(wvt_ref[...], hn).astype(jnp.bfloat16)

    cp = _dot(hn, wc_ref[...])
    cb = cp[:, :CONV_WIDTH]
    y = cp[:, CONV_WIDTH:2 * CONV_WIDTH] * cp[:, 2 * CONV_WIDTH:]

    @pl.when(i % tiles_per_seq == 0)
    def _():
        carry_ref[...] = jnp.zeros_like(carry_ref)

    prev = carry_ref[...]
    row = lax.broadcasted_iota(jnp.int32, (tm, CONV_WIDTH), 0)
    p1 = prev[SUBLANES - 1:SUBLANES, :]
    p2 = prev[SUBLANES - 2:SUBLANES - 1, :]
    y1 = jnp.where(row == 0, p1, pltpu.roll(y, 1, 0))
    y2 = jnp.where(row == 0, p2, jnp.where(row == 1, p1, pltpu.roll(y, 2, 0)))
    carry_ref[...] = y[tm - SUBLANES:, :]
    z = cw_ref[0:1, :] * y2 + cw_ref[1:2, :] * y1 + cw_ref[2:3, :] * y
    co = cb * z
    cms = jnp.mean(co * co, axis=-1, keepdims=True)
    conv_ref[...] = (co * lax.rsqrt(cms + NORM_EPS) * cg_ref[...]).astype(jnp.bfloat16)


def _inproj(x2, pos_row, g1, wqk, wvt, wc, gqk, freqs, cw, cg, seq):
    n = x2.shape[0]
    tm = TM_IN
    grid = (n // tm,)
    const = lambda i: (0, 0)
    return pl.pallas_call(
        functools.partial(_inproj_kernel, tiles_per_seq=seq // tm),
        grid=grid,
        in_specs=[
            pl.BlockSpec((tm, D_MODEL), lambda i: (i, 0)),
            pl.BlockSpec((1, tm), lambda i: (0, i)),
            pl.BlockSpec((1, D_MODEL), const),
            pl.BlockSpec((D_MODEL, 1024), const),
            pl.BlockSpec((512, D_MODEL), const),
            pl.BlockSpec((D_MODEL, 1536), const),
            pl.BlockSpec((1, 1024), const),
            pl.BlockSpec((SUBLANES, 1), const),
            pl.BlockSpec((3, CONV_WIDTH), const),
            pl.BlockSpec((1, CONV_WIDTH), const),
        ],
        out_specs=[
            pl.BlockSpec((tm, ATTN_WIDTH), lambda i: (i, 0)),
            pl.BlockSpec((tm, ATTN_WIDTH), lambda i: (i, 0)),
            pl.BlockSpec((ATTN_WIDTH, tm), lambda i: (0, i)),
            pl.BlockSpec((tm, CONV_WIDTH), lambda i: (i, 0)),
        ],
        out_shape=[
            jax.ShapeDtypeStruct((n, ATTN_WIDTH), jnp.bfloat16),
            jax.ShapeDtypeStruct((n, ATTN_WIDTH), jnp.bfloat16),
            jax.ShapeDtypeStruct((ATTN_WIDTH, n), jnp.bfloat16),
            jax.ShapeDtypeStruct((n, CONV_WIDTH), jnp.bfloat16),
        ],
        scratch_shapes=[pltpu.VMEM((SUBLANES, CONV_WIDTH), jnp.float32)],
        compiler_params=pltpu.CompilerParams(
            dimension_semantics=("arbitrary",), vmem_limit_bytes=VMEM_LIMIT),
        name="inproj",
    )(x2, pos_row, g1, wqk, wvt, wc, gqk, freqs, cw, cg)


def _attn_kernel(q_ref, k_ref, vt_ref, lq1_ref, lk1_ref, lq2_ref, lk2_ref, sg_ref, o_ref, acc_ref, *, nq):
    lam = (jnp.exp(jnp.sum(lq1_ref[...] * lk1_ref[...], axis=-1, keepdims=True))
           - jnp.exp(jnp.sum(lq2_ref[...] * lk2_ref[...], axis=-1, keepdims=True))
           + LAMBDA_INIT)
    tri = (lax.broadcasted_iota(jnp.int32, (HQ, HQ), 0)
           <= lax.broadcasted_iota(jnp.int32, (HQ, HQ), 1))
    row_d = lax.broadcasted_iota(jnp.int32, (LANES, HQ), 0)
    ones_rows = jnp.ones((SUM_ROWS, KV_FULL), jnp.bfloat16)

    def masked(s, n_tri):
        blocks = [jnp.where(tri, s[:, c * HQ:(c + 1) * HQ], -jnp.inf) for c in range(n_tri)]
        return jnp.concatenate(blocks + [s[:, n_tri * HQ:]], axis=1) if n_tri * HQ < s.shape[1] else \
            jnp.concatenate(blocks, axis=1)

    todo = {}
    for t in range(nq):
        lo = t * TQ
        todo[t] = ([(lo, lo + HQ, "mixed"), (lo + HQ, lo + TQ, "upper")]
                   + [(k0, k0 + KV_FULL, "full") for k0 in range(0, lo, KV_FULL)])
    tasks = []
    while any(todo.values()):
        for t in reversed(range(nq)):
            if todo[t]:
                tasks.append((t,) + todo[t].pop(0))
    last_task = {t: max(i for i, tk in enumerate(tasks) if tk[0] == t) for t in range(nq)}
    qzt, m_run = {}, {}

    def q_transposed(t):
        parts = []
        for half in range(2):
            r0 = t * TQ + half * HQ
            qt = q_ref[r0:r0 + HQ, :].astype(jnp.float32).T
            parts.append(jnp.where(row_d < DK, qt, 0.0))
            parts.append(jnp.where(row_d >= DK, qt, 0.0))
        return jnp.concatenate(parts, axis=1).astype(jnp.bfloat16)

    def scores(task):
        t, k0, k1, kind = task
        if t not in qzt:
            qzt[t] = q_transposed(t)
        kb = k_ref[k0:k1, :]
        if kind == "upper":
            return masked(_dot(kb, qzt[t][:, 2 * HQ:]), 2)
        s = _dot(kb, qzt[t])
        return masked(s, 2) if kind == "mixed" else s

    def softmax(task, s):
        t, _, _, kind = task
        mx = jnp.max(s, axis=0, keepdims=True)
        if kind == "mixed":
            m_run[t] = mx
            return jnp.exp2(s - mx).astype(jnp.bfloat16), None
        m_old = m_run[t][:, 2 * HQ:] if kind == "upper" else m_run[t]
        m_new = jnp.maximum(m_old, mx)
        alpha = jnp.exp2(m_old - m_new)
        p = jnp.exp2(s - m_new)
        m_run[t] = jnp.concatenate([m_run[t][:, :2 * HQ], m_new], axis=1) if kind == "upper" else m_new
        return p.astype(jnp.bfloat16), alpha

    def accumulate(task, p, alpha):
        t, k0, k1, kind = task
        vta = jnp.concatenate([vt_ref[:, k0:k1], ones_rows[:, :k1 - k0]], axis=0)
        pv = _dot(vta, p)
        if kind == "mixed":
            acc_ref[t] = pv
        elif kind == "upper":
            acc_ref[t, :, 2 * HQ:] = alpha * acc_ref[t, :, 2 * HQ:] + pv
        else:
            acc_ref[t] = alpha * acc_ref[t] + pv

    def finish(t):
        o_all = acc_ref[t, 0:DV, :] * (1.0 / acc_ref[t, DV:DV + 1, :])
        for half in range(2):
            o = (o_all[:, (2 * half) * HQ:(2 * half + 1) * HQ]
                 - lam * o_all[:, (2 * half + 1) * HQ:(2 * half + 2) * HQ])
            ms = jnp.mean(o * o, axis=0, keepdims=True)
            on = o * lax.rsqrt(ms + NORM_EPS) * sg_ref[...] * (1.0 - LAMBDA_INIT)
            r0 = t * TQ + half * HQ
            o_ref[r0:r0 + HQ, :] = on.T.astype(jnp.bfloat16)

    n = len(tasks)
    s_prev, p_prev = None, None
    for step in range(n + 2):
        if step >= 2:
            accumulate(tasks[step - 2], *p_prev)
            if last_task[tasks[step - 2][0]] == step - 2:
                finish(tasks[step - 2][0])
        if 1 <= step <= n:
            p_prev = softmax(tasks[step - 1], s_prev)
        if step < n:
            s_prev = scores(tasks[step])


def _attention(q, k, vt, lq1, lk1, lq2, lk2, sg_col, bsz, seq):
    n = q.shape[0]
    nq = seq // TQ
    vec = lambda b, h: (0, 0)
    return pl.pallas_call(
        functools.partial(_attn_kernel, nq=nq),
        grid=(bsz, N_HEADS),
        in_specs=[
            pl.BlockSpec((seq, DV), lambda b, h: (b, h)),
            pl.BlockSpec((seq, DV), lambda b, h: (b, h)),
            pl.BlockSpec((DV, seq), lambda b, h: (h, b)),
            pl.BlockSpec((1, DK), vec), pl.BlockSpec((1, DK), vec),
            pl.BlockSpec((1, DK), vec), pl.BlockSpec((1, DK), vec),
            pl.BlockSpec((DV, 1), vec),
        ],
        out_specs=pl.BlockSpec((seq, DV), lambda b, h: (b, h)),
        out_shape=jax.ShapeDtypeStruct((n, ATTN_WIDTH), jnp.bfloat16),
        scratch_shapes=[pltpu.VMEM((nq, DV + SUM_ROWS, 4 * HQ), jnp.float32)],
        compiler_params=pltpu.CompilerParams(
            dimension_semantics=("arbitrary", "arbitrary"), vmem_limit_bytes=VMEM_LIMIT),
        name="attn",
    )(q, k, vt, lq1, lk1, lq2, lk2, sg_col)


ROUTER_ROWS = 128
EXPERT_ROW0 = 32


def _outproj_kernel(x_ref, attn_ref, conv_ref, wo_ref, g2_ref, wr2_ref,
                    h_ref, xs_ref, meta_ref, cnt_ref):
    tm = x_ref.shape[0]
    t = T_SORT
    f32, bf16 = jnp.float32, jnp.bfloat16
    a = jnp.concatenate([attn_ref[...], conv_ref[...]], axis=1)
    h = x_ref[...] + _dot(a, wo_ref[...])
    h_ref[...] = h
    ms = jnp.mean(h * h, axis=-1, keepdims=True)
    hn = h * lax.rsqrt(ms + NORM_EPS) * g2_ref[...]
    hn_hi, hn_lo = _split2(hn)

    hh = _dot(hn_hi, wr2_ref[...])
    logits = hh[:, :ROUTER_ROWS] + hh[:, ROUTER_ROWS:] + _dot(hn_lo, wr2_ref[:, :ROUTER_ROWS])
    lt = logits.T
    row8 = lax.broadcasted_iota(jnp.int32, (SUBLANES, tm), 0).astype(f32)
    neg_inf = -jnp.inf

    def first_argmax(v):
        mx = jnp.max(v, axis=0, keepdims=True)
        idx = jnp.min(jnp.where(v == mx, row8, float(SUBLANES)), axis=0, keepdims=True)
        return mx, idx

    g_log = jnp.where(row8 < N_GROUPS, lt[0:SUBLANES, :], neg_inf)
    g_max, g_sel = first_argmax(g_log)
    g_gate = 1.0 / jnp.sum(jnp.exp(g_log - g_max), axis=0, keepdims=True)
    e_log = jnp.zeros((EPG, tm), f32)
    for g in range(N_GROUPS):
        rows = lt[EXPERT_ROW0 + g * EPG:EXPERT_ROW0 + (g + 1) * EPG, :]
        e_log = jnp.where(g_sel == float(g), rows, e_log)
    v1, i1 = first_argmax(e_log)
    v2, i2 = first_argmax(jnp.where(row8 == i1, neg_inf, e_log))
    tt = jnp.exp(v2 - v1)
    w1 = g_gate / (1.0 + tt)
    w2 = g_gate * tt / (1.0 + tt)
    e1 = g_sel * float(EPG) + i1
    e2 = g_sel * float(EPG) + i2

    row32 = lax.broadcasted_iota(jnp.int32, (N_EXPERTS, tm), 0).astype(f32)
    oh1 = row32 == e1
    oh2 = row32 == e2
    c = jnp.where(oh1 | oh2, 1.0, 0.0).astype(bf16)
    tok_r = lax.broadcasted_iota(jnp.int32, (tm, tm), 0)
    tok_c = lax.broadcasted_iota(jnp.int32, (tm, tm), 1)
    same_tile = (tok_r // t) == (tok_c // t)
    rank = _dot(c, jnp.where(same_tile & (tok_r < tok_c), 1.0, 0.0).astype(bf16))
    cnt_b = _dot(c, jnp.where(same_tile, 1.0, 0.0).astype(bf16))
    ex_r = lax.broadcasted_iota(jnp.int32, (N_EXPERTS, N_EXPERTS), 0)
    ex_c = lax.broadcasted_iota(jnp.int32, (N_EXPERTS, N_EXPERTS), 1)
    lower = jnp.where(ex_c < ex_r, 1.0, 0.0).astype(bf16)
    start_b = _dot(lower, cnt_b.astype(bf16))
    pos_e = start_b + rank
    p1 = jnp.sum(jnp.where(oh1, pos_e, 0.0), axis=0, keepdims=True)
    p2 = jnp.sum(jnp.where(oh2, pos_e, 0.0), axis=0, keepdims=True)

    srow = lax.broadcasted_iota(jnp.int32, (2 * t, t), 0).astype(f32)
    for s in range(tm // t):
        cols = slice(s * t, (s + 1) * t)
        perm = jnp.where((srow == p1[:, cols]) | (srow == p2[:, cols]), 1.0, 0.0).astype(bf16)
        xs = _dot(perm, hn_hi[cols, :])
        xs_ref[s * 2 * t:(s + 1) * 2 * t] = _pack_rows(xs)
        cnt_ref[s * N_EXPERTS:(s + 1) * N_EXPERTS, :] = cnt_b[:, s * t:s * t + LANES]

    meta = jnp.concatenate([p1, p2, w1, w2, jnp.zeros((LANES - 4, tm), f32)], axis=0)
    meta_ref[...] = meta.T


def _outproj(x2, attn_o, conv_o, wo, g2, wr2):
    n = x2.shape[0]
    tm = TM_OUT
    nsub = tm // T_SORT
    const = lambda i: (0, 0)

    return pl.pallas_call(
        _outproj_kernel,
        grid=(n // tm,),
        in_specs=[
            pl.BlockSpec((tm, D_MODEL), lambda i: (i, 0)),
            pl.BlockSpec((tm, ATTN_WIDTH), lambda i: (i, 0)),
            pl.BlockSpec((tm, CONV_WIDTH), lambda i: (i, 0)),
            pl.BlockSpec((D_MODEL, D_MODEL), const),
            pl.BlockSpec((1, D_MODEL), const),
            pl.BlockSpec((D_MODEL, 2 * ROUTER_ROWS), const),
        ],
        out_specs=[
            pl.BlockSpec((tm, D_MODEL), lambda i: (i, 0)),
            pl.BlockSpec((2 * tm, ROW_CHUNKS, LANES), lambda i: (i, 0, 0)),
            pl.BlockSpec((tm, LANES), lambda i: (i, 0)),
            pl.BlockSpec((nsub * N_EXPERTS, LANES), lambda i: (i, 0)),
        ],
        out_shape=[
            jax.ShapeDtypeStruct((n, D_MODEL), jnp.float32),
            jax.ShapeDtypeStruct((2 * n, ROW_CHUNKS, LANES), PACKED_DTYPE),
            jax.ShapeDtypeStruct((n, LANES), jnp.float32),
            jax.ShapeDtypeStruct((n // T_SORT * N_EXPERTS, LANES), jnp.float32),
        ],
        compiler_params=pltpu.CompilerParams(
            dimension_semantics=("arbitrary",), vmem_limit_bytes=VMEM_LIMIT),
        name="outproj",
    )(x2, attn_o, conv_o, wo, g2, wr2)


def _pack_rows(x):
    r = x.shape[0]
    w = pltpu.pack_elementwise([x[:, :ROW_WORDS], x[:, ROW_WORDS:]], packed_dtype=jnp.bfloat16)
    return pltpu.bitcast(w, PACKED_DTYPE).reshape(r, ROW_CHUNKS, LANES)


def _packed_zero_rows(r):
    z = jnp.zeros((r, ROW_CHUNKS, LANES), jnp.float32)
    w = pltpu.pack_elementwise([z, z], packed_dtype=jnp.bfloat16)
    return pltpu.bitcast(w, PACKED_DTYPE)


def _unpack_rows(u):
    r = u.shape[0]
    w = u.reshape(r, ROW_WORDS)
    lo = pltpu.unpack_elementwise(w, index=0, packed_dtype=jnp.bfloat16, unpacked_dtype=jnp.float32)
    hi = pltpu.unpack_elementwise(w, index=1, packed_dtype=jnp.bfloat16, unpacked_dtype=jnp.float32)
    return jnp.concatenate([lo, hi], axis=1).astype(jnp.bfloat16)


def _expert_kernel(nblk_ref, blk0_ref, be_ref, nused_ref, r0_ref, nvalid_ref, tlo_ref, thi_ref,
                   cnt_ref, src_ref, rbase_ref,
                   xs_hbm, wg_ref, wu_ref, wd_ref, y_hbm,
                   xbuf, ybuf, wg_s, wu_s, wd_s, sem, ysem):
    e = pl.program_id(0)
    nused = nused_ref[0]
    n_blocks = be_ref.shape[0]
    n_tiles = cnt_ref.shape[0] // N_EXPERTS

    def start_run(blk, slot, t, live):
        r0 = r0_ref[blk]
        k = t * N_EXPERTS + be_ref[blk]
        lo = jnp.maximum(rbase_ref[k], r0)
        hi = jnp.minimum(rbase_ref[k] + cnt_ref[k], r0 + MOE_BLOCK)
        rows = jnp.where(live, hi - lo, 0)

        @pl.when(rows > 0)
        def _():
            pltpu.make_async_copy(xs_hbm.at[pl.ds(src_ref[k] + lo - rbase_ref[k], rows)],
                                  xbuf.at[slot, pl.ds(lo - r0, rows)], sem.at[slot]).start()

    def gather_rolled(blk, slot, t_from):
        def run(t, carry):
            start_run(blk, slot, t, True)
            return carry
        lax.fori_loop(t_from, thi_ref[blk], run, 0)

    def gather_unrolled(blk, slot, live):
        blk = jnp.minimum(blk, n_blocks - 1)
        for r in range(GATHER_UNROLL):
            t = tlo_ref[blk] + r
            start_run(blk, slot, jnp.minimum(t, n_tiles - 1), live & (t < thi_ref[blk]))

        @pl.when(live & (tlo_ref[blk] + GATHER_UNROLL < thi_ref[blk]))
        def _():
            gather_rolled(blk, slot, tlo_ref[blk] + GATHER_UNROLL)

    def y_copy(blk, slot):
        return pltpu.make_async_copy(ybuf.at[slot], y_hbm.at[pl.ds(blk * MOE_BLOCK, MOE_BLOCK)], ysem.at[slot])

    @pl.when(e == 0)
    def _():
        for slot in range(N_XBUF):
            xbuf[slot] = _packed_zero_rows(MOE_BLOCK)
        gather_rolled(0, 0, tlo_ref[0])

        @pl.when(nused > 1)
        def _():
            gather_rolled(1, 1, tlo_ref[1])
        for slot in range(2):
            ybuf[slot] = _packed_zero_rows(MOE_BLOCK)
            y_copy(slot, slot).start()

    @pl.when(nblk_ref[e] > 0)
    def _():
        wg_s[...] = wg_ref[0].astype(jnp.bfloat16)
        wu_s[...] = wu_ref[0].astype(jnp.bfloat16)
        wd_s[...] = wd_ref[0].astype(jnp.bfloat16)

        def block(c, carry):
            b = blk0_ref[e] + c
            slot = b % N_XBUF
            yslot = b % 2
            nv = nvalid_ref[b]
            y_copy(b, yslot).wait()
            pltpu.make_async_copy(xs_hbm.at[pl.ds(0, nv)], xbuf.at[slot, pl.ds(0, nv)], sem.at[slot]).wait()

            x = _unpack_rows(xbuf[slot])
            g = _dot(x, wg_s[...])
            u = _dot(x, wu_s[...])
            act = (g / (1.0 + jnp.exp(-g)) * u).astype(jnp.bfloat16)
            y = _dot(act, wd_s[...])
            ybuf[yslot] = _pack_rows(y)
            y_copy(b, yslot).start()

            gather_unrolled(b + 2, (b + 2) % N_XBUF, b + 2 < nused)
            return carry
        lax.fori_loop(0, nblk_ref[e], block, 0)

    @pl.when(e == N_EXPERTS - 1)
    def _():
        for slot in range(2):
            y_copy(slot, slot).wait()
        ybuf[0] = _packed_zero_rows(MOE_BLOCK)

        def zero_block(b, carry):
            y_copy(b, 0).start()
            y_copy(b, 0).wait()
            return carry
        lax.fori_loop(nused, n_blocks, zero_block, 0)


def _experts(tabs, xs, w_gate, w_up, w_down):
    n_slots = tabs["n_slots"]
    bf16 = jnp.bfloat16
    w_map = lambda e, *_: (e, 0, 0)
    return pl.pallas_call(
        _expert_kernel,
        grid_spec=pltpu.PrefetchScalarGridSpec(
            num_scalar_prefetch=11,
            grid=(N_EXPERTS,),
            in_specs=[
                pl.BlockSpec(memory_space=pl.ANY),
                pl.BlockSpec((1, D_MODEL, D_FF), w_map),
                pl.BlockSpec((1, D_MODEL, D_FF), w_map),
                pl.BlockSpec((1, D_FF, D_MODEL), w_map),
            ],
            out_specs=pl.BlockSpec(memory_space=pl.ANY),
            scratch_shapes=[pltpu.VMEM((N_XBUF, MOE_BLOCK, ROW_CHUNKS, LANES), PACKED_DTYPE),
                            pltpu.VMEM((2, MOE_BLOCK, ROW_CHUNKS, LANES), PACKED_DTYPE),
                            pltpu.VMEM((D_MODEL, D_FF), bf16),
                            pltpu.VMEM((D_MODEL, D_FF), bf16),
                            pltpu.VMEM((D_FF, D_MODEL), bf16),
                            pltpu.SemaphoreType.DMA((N_XBUF,)),
                            pltpu.SemaphoreType.DMA((2,))],
        ),
        out_shape=jax.ShapeDtypeStruct((n_slots, ROW_CHUNKS, LANES), PACKED_DTYPE),
        compiler_params=pltpu.CompilerParams(
            dimension_semantics=("arbitrary",), vmem_limit_bytes=VMEM_LIMIT),
        name="experts",
    )(tabs["nblk"], tabs["blk0"], tabs["block_e"], tabs["nused"], tabs["r0"], tabs["nvalid"], tabs["tlo"],
      tabs["thi"], tabs["cnt"], tabs["src"], tabs["rbase"], xs, w_gate, w_up, w_down)


def _combine_kernel(cnt_ref, loc_ref, dst_ref, h_ref, meta_ref, yg_hbm, o_ref, ybuf, sem, *, n_steps, nsub):
    i = pl.program_id(0)
    t = T_SORT
    f32, bf16 = jnp.float32, jnp.bfloat16

    slot_in = i % 2
    tile0 = jnp.minimum(i, n_steps - 1) * nsub
    for s in range(nsub):
        for e in range(N_EXPERTS):
            k = (tile0 + s) * N_EXPERTS + e
            rows = jnp.where(i < n_steps, cnt_ref[k], 0)

            @pl.when(rows > 0)
            def _():
                pltpu.make_async_copy(yg_hbm.at[pl.ds(dst_ref[k], rows)],
                                      ybuf.at[slot_in, pl.ds(s * 2 * t + loc_ref[k], rows)],
                                      sem.at[slot_in]).start()

    @pl.when(i > 0)
    def _():
        slot = (i - 1) % 2
        pltpu.make_async_copy(yg_hbm.at[pl.ds(0, nsub * 2 * t)], ybuf.at[slot], sem.at[slot]).wait()
        lane = lax.broadcasted_iota(jnp.int32, (t, 2 * t), 1).astype(f32)
        for s in range(nsub):
            y = _unpack_rows(ybuf[slot, s * 2 * t:(s + 1) * 2 * t])
            meta = meta_ref[s * t:(s + 1) * t, :]
            pick1 = jnp.where(lane == meta[:, 0:1], 1.0, 0.0).astype(bf16)
            pick2 = jnp.where(lane == meta[:, 1:2], 1.0, 0.0).astype(bf16)
            y1 = _dot(pick1, y)
            y2 = _dot(pick2, y)
            o_ref[s * t:(s + 1) * t, :] = h_ref[s * t:(s + 1) * t, :] + (meta[:, 2:3] * y1 + meta[:, 3:4] * y2)


def _combine(cnt, loc, dst, h, meta, yg):
    n = h.shape[0]
    tm = TM_OUT
    nsub = tm // T_SORT
    n_steps = n // tm
    prev = lambda i, *_: (jnp.maximum(i - 1, 0), 0)
    return pl.pallas_call(
        functools.partial(_combine_kernel, n_steps=n_steps, nsub=nsub),
        grid_spec=pltpu.PrefetchScalarGridSpec(
            num_scalar_prefetch=3,
            grid=(n_steps + 1,),
            in_specs=[
                pl.BlockSpec((tm, D_MODEL), prev),
                pl.BlockSpec((tm, LANES), prev),
                pl.BlockSpec(memory_space=pl.ANY),
            ],
            out_specs=pl.BlockSpec((tm, D_MODEL), prev),
            scratch_shapes=[pltpu.VMEM((2, nsub * 2 * T_SORT, ROW_CHUNKS, LANES), PACKED_DTYPE),
                            pltpu.SemaphoreType.DMA((2,))],
        ),
        out_shape=jax.ShapeDtypeStruct((n, D_MODEL), jnp.float32),
        compiler_params=pltpu.CompilerParams(
            dimension_semantics=("arbitrary",), vmem_limit_bytes=VMEM_LIMIT),
        name="combine",
    )(cnt, loc, dst, h, meta, yg)


def _routing_tables(cnt_out, n_tiles, n_tok):
    i32 = jnp.int32
    cnt = cnt_out.reshape(n_tiles, N_EXPERTS, LANES)[:, :, 0].astype(i32)
    count = jnp.sum(cnt, axis=0)
    padded = ((count + MOE_BLOCK - 1) // MOE_BLOCK) * MOE_BLOCK
    pad_end = jnp.cumsum(padded)
    pad_start = pad_end - padded
    run_end = jnp.cumsum(cnt, axis=0)
    rbase = run_end - cnt
    dst = pad_start[None, :] + rbase
    loc = jnp.cumsum(cnt, axis=1) - cnt
    src = loc + (jnp.arange(n_tiles, dtype=i32) * (2 * T_SORT))[:, None]
    n_blocks = (2 * n_tok) // MOE_BLOCK + N_EXPERTS
    block_start = jnp.arange(n_blocks, dtype=i32) * MOE_BLOCK
    block_e = jnp.minimum(jnp.sum(pad_end[None, :] <= block_start[:, None], axis=1), N_EXPERTS - 1).astype(i32)
    nused = (pad_end[-1] // MOE_BLOCK).astype(i32).reshape(1)
    ex = jnp.arange(N_EXPERTS, dtype=i32)
    sel = (block_e[None, :] == ex[:, None]).astype(i32)
    pick = lambda per_expert: jnp.sum(per_expert[..., :, None] * sel, axis=-2)
    r0 = block_start - pick(pad_start)
    nvalid = jnp.clip(pick(count) - r0, 0, MOE_BLOCK)
    tlo = jnp.sum(pick(run_end) <= r0[None, :], axis=0)
    thi = jnp.sum(pick(rbase) < (r0 + MOE_BLOCK)[None, :], axis=0)
    flat = lambda a: a.reshape(-1).astype(i32)
    return dict(cnt=flat(cnt), src=flat(src), dst=flat(dst), loc=flat(loc), rbase=flat(rbase),
                block_e=block_e, nused=nused, r0=flat(r0), nvalid=flat(nvalid), tlo=flat(tlo), thi=flat(thi),
                nblk=flat(padded // MOE_BLOCK), blk0=flat(pad_start // MOE_BLOCK), n_slots=n_blocks * MOE_BLOCK)


def _stage1(x, positions, attn_norm_gain, w_in, q_norm_gain, k_norm_gain, conv_w, conv_out_gain):
    bsz, seq, _ = x.shape
    n = bsz * seq
    f32, bf16 = jnp.float32, jnp.bfloat16
    w = w_in[0]
    wqk = w[:, :2 * ATTN_WIDTH].astype(bf16)
    wvt = w[:, 2 * ATTN_WIDTH:3 * ATTN_WIDTH].T.astype(bf16)
    wc = w[:, 3 * ATTN_WIDTH:].astype(bf16)
    scale = DK ** -0.5 * LOG2E
    gqk = jnp.concatenate([jnp.tile(q_norm_gain[0].astype(f32), 2 * N_HEADS) * scale,
                           jnp.tile(k_norm_gain[0].astype(f32), 2 * N_HEADS)]).reshape(1, -1)
    freqs = (ROPE_THETA ** (-jnp.arange(0, ROT_DIM, 2, dtype=f32) / ROT_DIM)).reshape(SUBLANES, 1)
    return _inproj(x.reshape(n, D_MODEL), positions.reshape(1, n),
                   attn_norm_gain[0].reshape(1, -1).astype(f32), wqk, wvt, wc, gqk, freqs,
                   conv_w[0].astype(f32), conv_out_gain[0].reshape(1, -1).astype(f32), seq)


def kernel(x, positions, attn_norm_gain, w_in, q_norm_gain, k_norm_gain, lambda_q1, lambda_k1, lambda_q2, lambda_k2, subln_gain, conv_w, conv_out_gain, w_out, ffn_norm_gain, w_group_router, w_expert_router, w_gate, w_up, w_down):
    bsz, seq, _ = x.shape
    n = bsz * seq
    f32, bf16 = jnp.float32, jnp.bfloat16
    assert TM_OUT % T_SORT == 0 and seq % TM_IN == 0 and seq % TQ == 0 and TQ % KV_FULL == 0
    q, k, vt, conv_o = _stage1(x, positions, attn_norm_gain, w_in, q_norm_gain, k_norm_gain,
                                conv_w, conv_out_gain)
    attn_o = _attention(q, k, vt,
                        lambda_q1[0].reshape(1, -1).astype(f32), lambda_k1[0].reshape(1, -1).astype(f32),
                        lambda_q2[0].reshape(1, -1).astype(f32), lambda_k2[0].reshape(1, -1).astype(f32),
                        subln_gain[0].reshape(-1, 1).astype(f32), bsz, seq)

    wr = jnp.zeros((D_MODEL, ROUTER_ROWS), f32)
    wr = wr.at[:, 0:N_GROUPS].set(w_group_router[0].astype(f32))
    wr = wr.at[:, EXPERT_ROW0:EXPERT_ROW0 + N_EXPERTS].set(
        jnp.transpose(w_expert_router[0].astype(f32), (1, 0, 2)).reshape(D_MODEL, N_EXPERTS))
    wrh = wr.astype(bf16)
    wr2 = jnp.concatenate([wrh, (wr - wrh.astype(f32)).astype(bf16)], axis=1)
    h, xs, meta, cnt_out = _outproj(x.reshape(n, D_MODEL), attn_o, conv_o, w_out[0].astype(bf16),
                                    ffn_norm_gain[0].reshape(1, -1).astype(f32), wr2)

    tabs = _routing_tables(cnt_out, n // T_SORT, n)
    yg = _experts(tabs, xs, w_gate[0], w_up[0], w_down[0])
    out = _combine(tabs["cnt"], tabs["loc"], tabs["dst"], h, meta, yg)
    return out.reshape(x.shape)
```

```python
import functools
import math

import jax
import jax.numpy as jnp
from jax import lax
from jax.experimental import pallas as pl
from jax.experimental.pallas import tpu as pltpu

D_MODEL = 1024
N_HEADS = 4
DK = 64
DV = 128
ROT_DIM = 16
ROPE_THETA = 500000.0
ATTN_WIDTH = N_HEADS * DV
CONV_WIDTH = 512
NORM_EPS = 1e-6
LOG2E = 1.4426950408889634
LAMBDA_INIT = 0.8 - 0.6 * math.exp(-0.3 * 0)
N_GROUPS = 4
EPG = 8
N_EXPERTS = N_GROUPS * EPG
D_FF = 512
MOE_BLOCK = 256

LANES = 128
SUBLANES = 8
ROW_WORDS = D_MODEL // 2
ROW_CHUNKS = ROW_WORDS // LANES
PACKED_DTYPE = jnp.uint32

TM_IN = 1024
TQ = 512
HQ = TQ // 2
KV_FULL = 512
SUM_ROWS = 16
T_SORT = 256
TM_OUT = 512
N_XBUF = 4
GATHER_UNROLL = 24
VMEM_LIMIT = 48 * 1024 * 1024


def _nt_dot(a, b):
    return lax.dot_general(a, b, (((1,), (1,)), ((), ())), preferred_element_type=jnp.float32)


def _dot(a, b):
    return jnp.dot(a, b, preferred_element_type=jnp.float32)


def _split3(x):
    h = x.astype(jnp.bfloat16)
    r = x - h.astype(jnp.float32)
    m = r.astype(jnp.bfloat16)
    l = (r - m.astype(jnp.float32)).astype(jnp.bfloat16)
    return h, m, l


def _split2(x):
    h = x.astype(jnp.bfloat16)
    l = (x - h.astype(jnp.float32)).astype(jnp.bfloat16)
    return h, l


def _inproj_kernel(x_ref, pos_ref, g1_ref, wqk_ref, wvt_ref, wc_ref, gqk_ref, freq_ref,
                   cw_ref, cg_ref,
                   q_ref, k_ref, vt_ref, conv_ref,
                   carry_ref, *, tiles_per_seq):
    tm = x_ref.shape[0]
    i = pl.program_id(0)

    x = x_ref[...]
    ms = jnp.mean(x * x, axis=-1, keepdims=True)
    hn = (x * lax.rsqrt(ms + NORM_EPS) * g1_ref[...]).astype(jnp.bfloat16)

    pos = pos_ref[...].astype(jnp.float32)
    ang = freq_ref[...] * pos
    lane_r = lax.broadcasted_iota(jnp.int32, (LANES, SUBLANES), 0)
    f_c = lax.broadcasted_iota(jnp.int32, (LANES, SUBLANES), 1)
    in_rot = (lane_r % DK) < ROT_DIM
    expand = jnp.where(in_rot & ((lane_r % (ROT_DIM // 2)) == f_c), 1.0, 0.0).astype(jnp.bfloat16)

    def to_rows(t):
        h, m, l = _split3(t)
        r = _dot(expand, h) + _dot(expand, m) + _dot(expand, l)
        return r.T

    cos_r = to_rows(jnp.cos(ang))
    sin_r = to_rows(jnp.sin(ang))
    lane = lax.broadcasted_iota(jnp.int32, (tm, LANES), 1)
    d = lane % DK
    cos_r = jnp.where(d < ROT_DIM, cos_r, 1.0)
    sin_lo = jnp.where(d < ROT_DIM // 2, -sin_r, 0.0)
    sin_hi = jnp.where((d >= ROT_DIM // 2) & (d < ROT_DIM), sin_r, 0.0)

    qk = _dot(hn, wqk_ref[...])
    seg_r = lax.broadcasted_iota(jnp.int32, (2 * LANES, 2 * LANES), 0) // DK
    seg_c = lax.broadcasted_iota(jnp.int32, (2 * LANES, 2 * LANES), 1) // DK
    seg_mean = jnp.where(seg_r == seg_c, 1.0 / DK, 0.0).astype(jnp.bfloat16)
    half = ROT_DIM // 2
    for c2 in range(4):
        blk = qk[:, c2 * 256:(c2 + 1) * 256]
        msq = _dot((blk * blk).astype(jnp.bfloat16), seg_mean)
        y2 = blk * lax.rsqrt(msq + NORM_EPS) * gqk_ref[:, c2 * 256:(c2 + 1) * 256]
        for c1 in range(2):
            c = c2 * 2 + c1
            y = y2[:, c1 * LANES:(c1 + 1) * LANES]
            rot = (y * cos_r
                   + pltpu.roll(y, LANES - half, 1) * sin_lo
                   + pltpu.roll(y, half, 1) * sin_hi)
            if c < N_HEADS:
                q_ref[:, c * LANES:(c + 1) * LANES] = rot.astype(jnp.bfloat16)
            else:
                h = c - N_HEADS
                k_ref[:, h * LANES:(h + 1) * LANES] = rot.astype(jnp.bfloat16)

    vt_ref[...] = _nt_dot(wvt_ref[...], hn).astype(jnp.bfloat16)

    cp = _dot(hn, wc_ref[...])
    cb = cp[:, :CONV_WIDTH]
    y = cp[:, CONV_WIDTH:2 * CONV_WIDTH] * cp[:, 2 * CONV_WIDTH:]

    @pl.when(i % tiles_per_seq == 0)
    def _():
        carry_ref[...] = jnp.zeros_like(carry_ref)

    prev = carry_ref[...]
    row = lax.broadcasted_iota(jnp.int32, (tm, CONV_WIDTH), 0)
    p1 = prev[SUBLANES - 1:SUBLANES, :]
    p2 = prev[SUBLANES - 2:SUBLANES - 1, :]
    y1 = jnp.where(row == 0, p1, pltpu.roll(y, 1, 0))
    y2 = jnp.where(row == 0, p2, jnp.where(row == 1, p1, pltpu.roll(y, 2, 0)))
    carry_ref[...] = y[tm - SUBLANES:, :]
    z = cw_ref[0:1, :] * y2 + cw_ref[1:2, :] * y1 + cw_ref[2:3, :] * y
    co = cb * z
    cms = jnp.mean(co * co, axis=-1, keepdims=True)
    conv_ref[...] = (co * lax.rsqrt(cms + NORM_EPS) * cg_ref[...]).astype(jnp.bfloat16)


def _inproj(x2, pos_row, g1, wqk, wvt, wc, gqk, freqs, cw, cg, seq):
    n = x2.shape[0]
    tm = TM_IN
    grid = (n // tm,)
    const = lambda i: (0, 0)
    return pl.pallas_call(
        functools.partial(_inproj_kernel, tiles_per_seq=seq // tm),
        grid=grid,
        in_specs=[
            pl.BlockSpec((tm, D_MODEL), lambda i: (i, 0)),
            pl.BlockSpec((1, tm), lambda i: (0, i)),
            pl.BlockSpec((1, D_MODEL), const),
            pl.BlockSpec((D_MODEL, 1024), const),
            pl.BlockSpec((512, D_MODEL), const),
            pl.BlockSpec((D_MODEL, 1536), const),
            pl.BlockSpec((1, 1024), const),
            pl.BlockSpec((SUBLANES, 1), const),
            pl.BlockSpec((3, CONV_WIDTH), const),
            pl.BlockSpec((1, CONV_WIDTH), const),
        ],
        out_specs=[
            pl.BlockSpec((tm, ATTN_WIDTH), lambda i: (i, 0)),
            pl.BlockSpec((tm, ATTN_WIDTH), lambda i: (i, 0)),
            pl.BlockSpec((ATTN_WIDTH, tm), lambda i: (0, i)),
            pl.BlockSpec((tm, CONV_WIDTH), lambda i: (i, 0)),
        ],
        out_shape=[
            jax.ShapeDtypeStruct((n, ATTN_WIDTH), jnp.bfloat16),
            jax.ShapeDtypeStruct((n, ATTN_WIDTH), jnp.bfloat16),
            jax.ShapeDtypeStruct((ATTN_WIDTH, n), jnp.bfloat16),
            jax.ShapeDtypeStruct((n, CONV_WIDTH), jnp.bfloat16),
        ],
        scratch_shapes=[pltpu.VMEM((SUBLANES, CONV_WIDTH), jnp.float32)],
        compiler_params=pltpu.CompilerParams(
            dimension_semantics=("arbitrary",), vmem_limit_bytes=VMEM_LIMIT),
        name="inproj",
    )(x2, pos_row, g1, wqk, wvt, wc, gqk, freqs, cw, cg)


def _attn_kernel(q_ref, k_ref, vt_ref, lq1_ref, lk1_ref, lq2_ref, lk2_ref, sg_ref, o_ref, acc_ref, *, nq):
    lam = (jnp.exp(jnp.sum(lq1_ref[...] * lk1_ref[...], axis=-1, keepdims=True))
           - jnp.exp(jnp.sum(lq2_ref[...] * lk2_ref[...], axis=-1, keepdims=True))
           + LAMBDA_INIT)
    tri = (lax.broadcasted_iota(jnp.int32, (HQ, HQ), 0)
           <= lax.broadcasted_iota(jnp.int32, (HQ, HQ), 1))
    row_d = lax.broadcasted_iota(jnp.int32, (LANES, HQ), 0)
    ones_rows = jnp.ones((SUM_ROWS, KV_FULL), jnp.bfloat16)

    def masked(s, n_tri):
        blocks = [jnp.where(tri, s[:, c * HQ:(c + 1) * HQ], -jnp.inf) for c in range(n_tri)]
        return jnp.concatenate(blocks + [s[:, n_tri * HQ:]], axis=1) if n_tri * HQ < s.shape[1] else \
            jnp.concatenate(blocks, axis=1)

    todo = {}
    for t in range(nq):
        lo = t * TQ
        todo[t] = ([(lo, lo + HQ, "mixed"), (lo + HQ, lo + TQ, "upper")]
                   + [(k0, k0 + KV_FULL, "full") for k0 in range(0, lo, KV_FULL)])
    tasks = []
    while any(todo.values()):
        for t in reversed(range(nq)):
            if todo[t]:
                tasks.append((t,) + todo[t].pop(0))
    last_task = {t: max(i for i, tk in enumerate(tasks) if tk[0] == t) for t in range(nq)}
    qzt, m_run = {}, {}

    def q_transposed(t):
        parts = []
        for half in range(2):
            r0 = t * TQ + half * HQ
            qt = q_ref[r0:r0 + HQ, :].astype(jnp.float32).T
            parts.append(jnp.where(row_d < DK, qt, 0.0))
            parts.append(jnp.where(row_d >= DK, qt, 0.0))
        return jnp.concatenate(parts, axis=1).astype(jnp.bfloat16)

    def scores(task):
        t, k0, k1, kind = task
        if t not in qzt:
            qzt[t] = q_transposed(t)
        kb = k_ref[k0:k1, :]
        if kind == "upper":
            return masked(_dot(kb, qzt[t][:, 2 * HQ:]), 2)
        s = _dot(kb, qzt[t])
        return masked(s, 2) if kind == "mixed" else s

    def softmax(task, s):
        t, _, _, kind = task
        mx = jnp.max(s, axis=0, keepdims=True)
        if kind == "mixed":
            m_run[t] = mx
            return jnp.exp2(s - mx).astype(jnp.bfloat16), None
        m_old = m_run[t][:, 2 * HQ:] if kind == "upper" else m_run[t]
        m_new = jnp.maximum(m_old, mx)
        alpha = jnp.exp2(m_old - m_new)
        p = jnp.exp2(s - m_new)
        m_run[t] = jnp.concatenate([m_run[t][:, :2 * HQ], m_new], axis=1) if kind == "upper" else m_new
        return p.astype(jnp.bfloat16), alpha

    def accumulate(task, p, alpha):
        t, k0, k1, kind = task
        vta = jnp.concatenate([vt_ref[:, k0:k1], ones_rows[:, :k1 - k0]], axis=0)
        pv = _dot(vta, p)
        if kind == "mixed":
            acc_ref[t] = pv
        elif kind == "upper":
            acc_ref[t, :, 2 * HQ:] = alpha * acc_ref[t, :, 2 * HQ:] + pv
        else:
            acc_ref[t] = alpha * acc_ref[t] + pv

    def finish(t):
        o_all = acc_ref[t, 0:DV, :] * (1.0 / acc_ref[t, DV:DV + 1, :])
        for half in range(2):
            o = (o_all[:, (2 * half) * HQ:(2 * half + 1) * HQ]
                 - lam * o_all[:, (2 * half + 1) * HQ:(2 * half + 2) * HQ])
            ms = jnp.mean(o * o, axis=0, keepdims=True)
            on = o * lax.rsqrt(ms + NORM_EPS) * sg_ref[...] * (1.0 - LAMBDA_INIT)
            r0 = t * TQ + half * HQ
            o_ref[r0:r0 + HQ, :] = on.T.astype(jnp.bfloat16)

    n = len(tasks)
    s_prev, p_prev = None, None
    for step in range(n + 2):
        if step >= 2:
            accumulate(tasks[step - 2], *p_prev)
            if last_task[tasks[step - 2][0]] == step - 2:
                finish(tasks[step - 2][0])
        if 1 <= step <= n:
            p_prev = softmax(tasks[step - 1], s_prev)
        if step < n:
            s_prev = scores(tasks[step])


def _attention(q, k, vt, lq1, lk1, lq2, lk2, sg_col, bsz, seq):
    n = q.shape[0]
    nq = seq // TQ
    vec = lambda b, h: (0, 0)
    return pl.pallas_call(
        functools.partial(_attn_kernel, nq=nq),
        grid=(bsz, N_HEADS),
        in_specs=[
            pl.BlockSpec((seq, DV), lambda b, h: (b, h)),
            pl.BlockSpec((seq, DV), lambda b, h: (b, h)),
            pl.BlockSpec((DV, seq), lambda b, h: (h, b)),
            pl.BlockSpec((1, DK), vec), pl.BlockSpec((1, DK), vec),
            pl.BlockSpec((1, DK), vec), pl.BlockSpec((1, DK), vec),
            pl.BlockSpec((DV, 1), vec),
        ],
        out_specs=pl.BlockSpec((seq, DV), lambda b, h: (b, h)),
        out_shape=jax.ShapeDtypeStruct((n, ATTN_WIDTH), jnp.bfloat16),
        scratch_shapes=[pltpu.VMEM((nq, DV + SUM_ROWS, 4 * HQ), jnp.float32)],
        compiler_params=pltpu.CompilerParams(
            dimension_semantics=("arbitrary", "arbitrary"), vmem_limit_bytes=VMEM_LIMIT),
        name="attn",
    )(q, k, vt, lq1, lk1, lq2, lk2, sg_col)


ROUTER_ROWS = 128
EXPERT_ROW0 = 32


def _outproj_kernel(x_ref, attn_ref, conv_ref, wo_ref, g2_ref, wr2_ref,
                    h_ref, xs_ref, meta_ref, cnt_ref):
    tm = x_ref.shape[0]
    t = T_SORT
    f32, bf16 = jnp.float32, jnp.bfloat16
    a = jnp.concatenate([attn_ref[...], conv_ref[...]], axis=1)
    h = x_ref[...] + _dot(a, wo_ref[...])
    h_ref[...] = h
    ms = jnp.mean(h * h, axis=-1, keepdims=True)
    hn = h * lax.rsqrt(ms + NORM_EPS) * g2_ref[...]
    hn_hi, hn_lo = _split2(hn)

    hh = _dot(hn_hi, wr2_ref[...])
    logits = hh[:, :ROUTER_ROWS] + hh[:, ROUTER_ROWS:] + _dot(hn_lo, wr2_ref[:, :ROUTER_ROWS])
    lt = logits.T
    row8 = lax.broadcasted_iota(jnp.int32, (SUBLANES, tm), 0).astype(f32)
    neg_inf = -jnp.inf

    def first_argmax(v):
        mx = jnp.max(v, axis=0, keepdims=True)
        idx = jnp.min(jnp.where(v == mx, row8, float(SUBLANES)), axis=0, keepdims=True)
        return mx, idx

    g_log = jnp.where(row8 < N_GROUPS, lt[0:SUBLANES, :], neg_inf)
    g_max, g_sel = first_argmax(g_log)
    g_gate = 1.0 / jnp.sum(jnp.exp(g_log - g_max), axis=0, keepdims=True)
    e_log = jnp.zeros((EPG, tm), f32)
    for g in range(N_GROUPS):
        rows = lt[EXPERT_ROW0 + g * EPG:EXPERT_ROW0 + (g + 1) * EPG, :]
        e_log = jnp.where(g_sel == float(g), rows, e_log)
    v1, i1 = first_argmax(e_log)
    v2, i2 = first_argmax(jnp.where(row8 == i1, neg_inf, e_log))
    tt = jnp.exp(v2 - v1)
    w1 = g_gate / (1.0 + tt)
    w2 = g_gate * tt / (1.0 + tt)
    e1 = g_sel * float(EPG) + i1
    e2 = g_sel * float(EPG) + i2

    row32 = lax.broadcasted_iota(jnp.int32, (N_EXPERTS, tm), 0).astype(f32)
    oh1 = row32 == e1
    oh2 = row32 == e2
    c = jnp.where(oh1 | oh2, 1.0, 0.0).astype(bf16)
    tok_r = lax.broadcasted_iota(jnp.int32, (tm, tm), 0)
    tok_c = lax.broadcasted_iota(jnp.int32, (tm, tm), 1)
    same_tile = (tok_r // t) == (tok_c // t)
    rank = _dot(c, jnp.where(same_tile & (tok_r < tok_c), 1.0, 0.0).astype(bf16))
    cnt_b = _dot(c, jnp.where(same_tile, 1.0, 0.0).astype(bf16))
    ex_r = lax.broadcasted_iota(jnp.int32, (N_EXPERTS, N_EXPERTS), 0)
    ex_c = lax.broadcasted_iota(jnp.int32, (N_EXPERTS, N_EXPERTS), 1)
    lower = jnp.where(ex_c < ex_r, 1.0, 0.0).astype(bf16)
    start_b = _dot(lower, cnt_b.astype(bf16))
    pos_e = start_b + rank
    p1 = jnp.sum(jnp.where(oh1, pos_e, 0.0), axis=0, keepdims=True)
    p2 = jnp.sum(jnp.where(oh2, pos_e, 0.0), axis=0, keepdims=True)

    srow = lax.broadcasted_iota(jnp.int32, (2 * t, t), 0).astype(f32)
    for s in range(tm // t):
        cols = slice(s * t, (s + 1) * t)
        perm = jnp.where((srow == p1[:, cols]) | (srow == p2[:, cols]), 1.0, 0.0).astype(bf16)
        xs = _dot(perm, hn_hi[cols, :])
        xs_ref[s * 2 * t:(s + 1) * 2 * t] = _pack_rows(xs)
        cnt_ref[s * N_EXPERTS:(s + 1) * N_EXPERTS, :] = cnt_b[:, s * t:s * t + LANES]

    meta = jnp.concatenate([p1, p2, w1, w2, jnp.zeros((LANES - 4, tm), f32)], axis=0)
    meta_ref[...] = meta.T


def _outproj(x2, attn_o, conv_o, wo, g2, wr2):
    n = x2.shape[0]
    tm = TM_OUT
    nsub = tm // T_SORT
    const = lambda i: (0, 0)

    return pl.pallas_call(
        _outproj_kernel,
        grid=(n // tm,),
        in_specs=[
            pl.BlockSpec((tm, D_MODEL), lambda i: (i, 0)),
            pl.BlockSpec((tm, ATTN_WIDTH), lambda i: (i, 0)),
            pl.BlockSpec((tm, CONV_WIDTH), lambda i: (i, 0)),
            pl.BlockSpec((D_MODEL, D_MODEL), const),
            pl.BlockSpec((1, D_MODEL), const),
            pl.BlockSpec((D_MODEL, 2 * ROUTER_ROWS), const),
        ],
        out_specs=[
            pl.BlockSpec((tm, D_MODEL), lambda i: (i, 0)),
            pl.BlockSpec((2 * tm, ROW_CHUNKS, LANES), lambda i: (i, 0, 0)),
            pl.BlockSpec((tm, LANES), lambda i: (i, 0)),
            pl.BlockSpec((nsub * N_EXPERTS, LANES), lambda i: (i, 0)),
        ],
        out_shape=[
            jax.ShapeDtypeStruct((n, D_MODEL), jnp.float32),
            jax.ShapeDtypeStruct((2 * n, ROW_CHUNKS, LANES), PACKED_DTYPE),
            jax.ShapeDtypeStruct((n, LANES), jnp.float32),
            jax.ShapeDtypeStruct((n // T_SORT * N_EXPERTS, LANES), jnp.float32),
        ],
        compiler_params=pltpu.CompilerParams(
            dimension_semantics=("arbitrary",), vmem_limit_bytes=VMEM_LIMIT),
        name="outproj",
    )(x2, attn_o, conv_o, wo, g2, wr2)


def _pack_rows(x):
    r = x.shape[0]
    w = pltpu.pack_elementwise([x[:, :ROW_WORDS], x[:, ROW_WORDS:]], packed_dtype=jnp.bfloat16)
    return pltpu.bitcast(w, PACKED_DTYPE).reshape(r, ROW_CHUNKS, LANES)


def _packed_zero_rows(r):
    z = jnp.zeros((r, ROW_CHUNKS, LANES), jnp.float32)
    w = pltpu.pack_elementwise([z, z], packed_dtype=jnp.bfloat16)
    return pltpu.bitcast(w, PACKED_DTYPE)


def _unpack_rows(u):
    r = u.shape[0]
    w = u.reshape(r, ROW_WORDS)
    lo = pltpu.unpack_elementwise(w, index=0, packed_dtype=jnp.bfloat16, unpacked_dtype=jnp.float32)
    hi = pltpu.unpack_elementwise(w, index=1, packed_dtype=jnp.bfloat16, unpacked_dtype=jnp.float32)
    return jnp.concatenate([lo, hi], axis=1).astype(jnp.bfloat16)


def _expert_kernel(nblk_ref, blk0_ref, be_ref, nused_ref, r0_ref, nvalid_ref, tlo_ref, thi_ref,
                   cnt_ref, src_ref, rbase_ref,
                   xs_hbm, wg_ref, wu_ref, wd_ref, y_hbm,
                   xbuf, ybuf, wg_s, wu_s, wd_s, sem, ysem):
    e = pl.program_id(0)
    nused = nused_ref[0]
    n_blocks = be_ref.shape[0]
    n_tiles = cnt_ref.shape[0] // N_EXPERTS

    def start_run(blk, slot, t, live):
        r0 = r0_ref[blk]
        k = t * N_EXPERTS + be_ref[blk]
        lo = jnp.maximum(rbase_ref[k], r0)
        hi = jnp.minimum(rbase_ref[k] + cnt_ref[k], r0 + MOE_BLOCK)
        rows = jnp.where(live, hi - lo, 0)

        @pl.when(rows > 0)
        def _():
            pltpu.make_async_copy(xs_hbm.at[pl.ds(src_ref[k] + lo - rbase_ref[k], rows)],
                                  xbuf.at[slot, pl.ds(lo - r0, rows)], sem.at[slot]).start()

    def gather_rolled(blk, slot, t_from):
        def run(t, carry):
            start_run(blk, slot, t, True)
            return carry
        lax.fori_loop(t_from, thi_ref[blk], run, 0)

    def gather_unrolled(blk, slot, live):
        blk = jnp.minimum(blk, n_blocks - 1)
        for r in range(GATHER_UNROLL):
            t = tlo_ref[blk] + r
            start_run(blk, slot, jnp.minimum(t, n_tiles - 1), live & (t < thi_ref[blk]))

        @pl.when(live & (tlo_ref[blk] + GATHER_UNROLL < thi_ref[blk]))
        def _():
            gather_rolled(blk, slot, tlo_ref[blk] + GATHER_UNROLL)

    def y_copy(blk, slot):
        return pltpu.make_async_copy(ybuf.at[slot], y_hbm.at[pl.ds(blk * MOE_BLOCK, MOE_BLOCK)], ysem.at[slot])

    @pl.when(e == 0)
    def _():
        for slot in range(N_XBUF):
            xbuf[slot] = _packed_zero_rows(MOE_BLOCK)
        gather_rolled(0, 0, tlo_ref[0])
        for blk in range(1, N_XBUF - 1):
            @pl.when(nused > blk)
            def _():
                gather_rolled(blk, blk, tlo_ref[blk])
        for slot in range(2):
            ybuf[slot] = _packed_zero_rows(MOE_BLOCK)
            y_copy(slot, slot).start()

    @pl.when(nblk_ref[e] > 0)
    def _():
        wg_s[...] = wg_ref[0].astype(jnp.bfloat16)
        wu_s[...] = wu_ref[0].astype(jnp.bfloat16)
        wd_s[...] = wd_ref[0].astype(jnp.bfloat16)

        def block(c, carry):
            b = blk0_ref[e] + c
            slot = b % N_XBUF
            yslot = b % 2
            nv = nvalid_ref[b]
            y_copy(b, yslot).wait()
            pltpu.make_async_copy(xs_hbm.at[pl.ds(0, nv)], xbuf.at[slot, pl.ds(0, nv)], sem.at[slot]).wait()

            x = _unpack_rows(xbuf[slot])
            g = _dot(x, wg_s[...])
            u = _dot(x, wu_s[...])
            act = (g / (1.0 + jnp.exp(-g)) * u).astype(jnp.bfloat16)
            y = _dot(act, wd_s[...])
            ybuf[yslot] = _pack_rows(y)
            y_copy(b, yslot).start()

            ahead = b + N_XBUF - 1
            gather_unrolled(ahead, ahead % N_XBUF, ahead < nused)
            return carry
        lax.fori_loop(0, nblk_ref[e], block, 0)

    @pl.when(e == N_EXPERTS - 1)
    def _():
        for slot in range(2):
            y_copy(slot, slot).wait()
        ybuf[0] = _packed_zero_rows(MOE_BLOCK)

        def zero_block(b, carry):
            y_copy(b, 0).start()
            y_copy(b, 0).wait()
            return carry
        lax.fori_loop(nused, n_blocks, zero_block, 0)


def _experts(tabs, xs, w_gate, w_up, w_down):
    n_slots = tabs["n_slots"]
    bf16 = jnp.bfloat16
    w_map = lambda e, *_: (e, 0, 0)
    return pl.pallas_call(
        _expert_kernel,
        grid_spec=pltpu.PrefetchScalarGridSpec(
            num_scalar_prefetch=11,
            grid=(N_EXPERTS,),
            in_specs=[
                pl.BlockSpec(memory_space=pl.ANY),
                pl.BlockSpec((1, D_MODEL, D_FF), w_map),
                pl.BlockSpec((1, D_MODEL, D_FF), w_map),
                pl.BlockSpec((1, D_FF, D_MODEL), w_map),
            ],
            out_specs=pl.BlockSpec(memory_space=pl.ANY),
            scratch_shapes=[pltpu.VMEM((N_XBUF, MOE_BLOCK, ROW_CHUNKS, LANES), PACKED_DTYPE),
                            pltpu.VMEM((2, MOE_BLOCK, ROW_CHUNKS, LANES), PACKED_DTYPE),
                            pltpu.VMEM((D_MODEL, D_FF), bf16),
                            pltpu.VMEM((D_MODEL, D_FF), bf16),
                            pltpu.VMEM((D_FF, D_MODEL), bf16),
                            pltpu.SemaphoreType.DMA((N_XBUF,)),
                            pltpu.SemaphoreType.DMA((2,))],
        ),
        out_shape=jax.ShapeDtypeStruct((n_slots, ROW_CHUNKS, LANES), PACKED_DTYPE),
        compiler_params=pltpu.CompilerParams(
            dimension_semantics=("arbitrary",), vmem_limit_bytes=VMEM_LIMIT),
        name="experts",
    )(tabs["nblk"], tabs["blk0"], tabs["block_e"], tabs["nused"], tabs["r0"], tabs["nvalid"], tabs["tlo"],
      tabs["thi"], tabs["cnt"], tabs["src"], tabs["rbase"], xs, w_gate, w_up, w_down)


def _combine_kernel(cnt_ref, loc_ref, dst_ref, h_ref, meta_ref, yg_hbm, o_ref, ybuf, sem, *, n_steps, nsub):
    i = pl.program_id(0)
    t = T_SORT
    f32, bf16 = jnp.float32, jnp.bfloat16

    slot_in = i % 2
    tile0 = jnp.minimum(i, n_steps - 1) * nsub
    for s in range(nsub):
        for e in range(N_EXPERTS):
            k = (tile0 + s) * N_EXPERTS + e
            rows = jnp.where(i < n_steps, cnt_ref[k], 0)

            @pl.when(rows > 0)
            def _():
                pltpu.make_async_copy(yg_hbm.at[pl.ds(dst_ref[k], rows)],
                                      ybuf.at[slot_in, pl.ds(s * 2 * t + loc_ref[k], rows)],
                                      sem.at[slot_in]).start()

    @pl.when(i > 0)
    def _():
        slot = (i - 1) % 2
        pltpu.make_async_copy(yg_hbm.at[pl.ds(0, nsub * 2 * t)], ybuf.at[slot], sem.at[slot]).wait()
        lane = lax.broadcasted_iota(jnp.int32, (t, 2 * t), 1).astype(f32)
        for s in range(nsub):
            y = _unpack_rows(ybuf[slot, s * 2 * t:(s + 1) * 2 * t])
            meta = meta_ref[s * t:(s + 1) * t, :]
            pick1 = jnp.where(lane == meta[:, 0:1], 1.0, 0.0).astype(bf16)
            pick2 = jnp.where(lane == meta[:, 1:2], 1.0, 0.0).astype(bf16)
            y1 = _dot(pick1, y)
            y2 = _dot(pick2, y)
            o_ref[s * t:(s + 1) * t, :] = h_ref[s * t:(s + 1) * t, :] + (meta[:, 2:3] * y1 + meta[:, 3:4] * y2)


def _combine(cnt, loc, dst, h, meta, yg):
    n = h.shape[0]
    tm = TM_OUT
    nsub = tm // T_SORT
    n_steps = n // tm
    prev = lambda i, *_: (jnp.maximum(i - 1, 0), 0)
    return pl.pallas_call(
        functools.partial(_combine_kernel, n_steps=n_steps, nsub=nsub),
        grid_spec=pltpu.PrefetchScalarGridSpec(
            num_scalar_prefetch=3,
            grid=(n_steps + 1,),
            in_specs=[
                pl.BlockSpec((tm, D_MODEL), prev),
                pl.BlockSpec((tm, LANES), prev),
                pl.BlockSpec(memory_space=pl.ANY),
            ],
            out_specs=pl.BlockSpec((tm, D_MODEL), prev),
            scratch_shapes=[pltpu.VMEM((2, nsub * 2 * T_SORT, ROW_CHUNKS, LANES), PACKED_DTYPE),
                            pltpu.SemaphoreType.DMA((2,))],
        ),
        out_shape=jax.ShapeDtypeStruct((n, D_MODEL), jnp.float32),
        compiler_params=pltpu.CompilerParams(
            dimension_semantics=("arbitrary",), vmem_limit_bytes=VMEM_LIMIT),
        name="combine",
    )(cnt, loc, dst, h, meta, yg)


def _routing_tables(cnt_out, n_tiles, n_tok):
    i32 = jnp.int32
    cnt = cnt_out.reshape(n_tiles, N_EXPERTS, LANES)[:, :, 0].astype(i32)
    count = jnp.sum(cnt, axis=0)
    padded = ((count + MOE_BLOCK - 1) // MOE_BLOCK) * MOE_BLOCK
    pad_end = jnp.cumsum(padded)
    pad_start = pad_end - padded
    run_end = jnp.cumsum(cnt, axis=0)
    rbase = run_end - cnt
    dst = pad_start[None, :] + rbase
    loc = jnp.cumsum(cnt, axis=1) - cnt
    src = loc + (jnp.arange(n_tiles, dtype=i32) * (2 * T_SORT))[:, None]
    n_blocks = (2 * n_tok) // MOE_BLOCK + N_EXPERTS
    block_start = jnp.arange(n_blocks, dtype=i32) * MOE_BLOCK
    block_e = jnp.minimum(jnp.sum(pad_end[None, :] <= block_start[:, None], axis=1), N_EXPERTS - 1).astype(i32)
    nused = (pad_end[-1] // MOE_BLOCK).astype(i32).reshape(1)
    ex = jnp.arange(N_EXPERTS, dtype=i32)
    sel = (block_e[None, :] == ex[:, None]).astype(i32)
    pick = lambda per_expert: jnp.sum(per_expert[..., :, None] * sel, axis=-2)
    r0 = block_start - pick(pad_start)
    nvalid = jnp.clip(pick(count) - r0, 0, MOE_BLOCK)
    tlo = jnp.sum(pick(run_end) <= r0[None, :], axis=0)
    thi = jnp.sum(pick(rbase) < (r0 + MOE_BLOCK)[None, :], axis=0)
    flat = lambda a: a.reshape(-1).astype(i32)
    return dict(cnt=flat(cnt), src=flat(src), dst=flat(dst), loc=flat(loc), rbase=flat(rbase),
                block_e=block_e, nused=nused, r0=flat(r0), nvalid=flat(nvalid), tlo=flat(tlo), thi=flat(thi),
                nblk=flat(padded // MOE_BLOCK), blk0=flat(pad_start // MOE_BLOCK), n_slots=n_blocks * MOE_BLOCK)


def _stage1(x, positions, attn_norm_gain, w_in, q_norm_gain, k_norm_gain, conv_w, conv_out_gain):
    bsz, seq, _ = x.shape
    n = bsz * seq
    f32, bf16 = jnp.float32, jnp.bfloat16
    w = w_in[0]
    wqk = w[:, :2 * ATTN_WIDTH].astype(bf16)
    wvt = w[:, 2 * ATTN_WIDTH:3 * ATTN_WIDTH].T.astype(bf16)
    wc = w[:, 3 * ATTN_WIDTH:].astype(bf16)
    scale = DK ** -0.5 * LOG2E
    gqk = jnp.concatenate([jnp.tile(q_norm_gain[0].astype(f32), 2 * N_HEADS) * scale,
                           jnp.tile(k_norm_gain[0].astype(f32), 2 * N_HEADS)]).reshape(1, -1)
    freqs = (ROPE_THETA ** (-jnp.arange(0, ROT_DIM, 2, dtype=f32) / ROT_DIM)).reshape(SUBLANES, 1)
    return _inproj(x.reshape(n, D_MODEL), positions.reshape(1, n),
                   attn_norm_gain[0].reshape(1, -1).astype(f32), wqk, wvt, wc, gqk, freqs,
                   conv_w[0].astype(f32), conv_out_gain[0].reshape(1, -1).astype(f32), seq)


def kernel(x, positions, attn_norm_gain, w_in, q_norm_gain, k_norm_gain, lambda_q1, lambda_k1, lambda_q2, lambda_k2, subln_gain, conv_w, conv_out_gain, w_out, ffn_norm_gain, w_group_router, w_expert_router, w_gate, w_up, w_down):
    bsz, seq, _ = x.shape
    n = bsz * seq
    f32, bf16 = jnp.float32, jnp.bfloat16
    assert TM_OUT % T_SORT == 0 and seq % TM_IN == 0 and seq % TQ == 0 and TQ % KV_FULL == 0
    q, k, vt, conv_o = _stage1(x, positions, attn_norm_gain, w_in, q_norm_gain, k_norm_gain,
                                conv_w, conv_out_gain)
    attn_o = _attention(q, k, vt,
                        lambda_q1[0].reshape(1, -1).astype(f32), lambda_k1[0].reshape(1, -1).astype(f32),
                        lambda_q2[0].reshape(1, -1).astype(f32), lambda_k2[0].reshape(1, -1).astype(f32),
                        subln_gain[0].reshape(-1, 1).astype(f32), bsz, seq)

    wr = jnp.zeros((D_MODEL, ROUTER_ROWS), f32)
    wr = wr.at[:, 0:N_GROUPS].set(w_group_router[0].astype(f32))
    wr = wr.at[:, EXPERT_ROW0:EXPERT_ROW0 + N_EXPERTS].set(
        jnp.transpose(w_expert_router[0].astype(f32), (1, 0, 2)).reshape(D_MODEL, N_EXPERTS))
    wrh = wr.astype(bf16)
    wr2 = jnp.concatenate([wrh, (wr - wrh.astype(f32)).astype(bf16)], axis=1)
    h, xs, meta, cnt_out = _outproj(x.reshape(n, D_MODEL), attn_o, conv_o, w_out[0].astype(bf16),
                                    ffn_norm_gain[0].reshape(1, -1).astype(f32), wr2)

    tabs = _routing_tables(cnt_out, n // T_SORT, n)
    yg = _experts(tabs, xs, w_gate[0], w_up[0], w_down[0])
    out = _combine(tabs["cnt"], tabs["loc"], tabs["dst"], h, meta, yg)
    return out.reshape(x.shape)
```

```python
import functools
import math

import jax
import jax.numpy as jnp
from jax import lax
from jax.experimental import pallas as pl
from jax.experimental.pallas import tpu as pltpu

D_MODEL = 1024
N_HEADS = 4
DK = 64
DV = 128
ROT_DIM = 16
ROPE_THETA = 500000.0
ATTN_WIDTH = N_HEADS * DV
CONV_WIDTH = 512
NORM_EPS = 1e-6
LOG2E = 1.4426950408889634
LAMBDA_INIT = 0.8 - 0.6 * math.exp(-0.3 * 0)
N_GROUPS = 4
EPG = 8
N_EXPERTS = N_GROUPS * EPG
D_FF = 512
MOE_BLOCK = 256

LANES = 128
SUBLANES = 8
ROW_WORDS = D_MODEL // 2
ROW_CHUNKS = ROW_WORDS // LANES
PACKED_DTYPE = jnp.uint32

TM_IN = 1024
TQ = 512
HQ = TQ // 2
KV_FULL = 512
SUM_ROWS = 16
T_SORT = 256
TM_OUT = 512
N_XBUF = 3
GATHER_UNROLL = 24
VMEM_LIMIT = 48 * 1024 * 1024


def _nt_dot(a, b):
    return lax.dot_general(a, b, (((1,), (1,)), ((), ())), preferred_element_type=jnp.float32)


def _dot(a, b):
    return jnp.dot(a, b, preferred_element_type=jnp.float32)


def _split3(x):
    h = x.astype(jnp.bfloat16)
    r = x - h.astype(jnp.float32)
    m = r.astype(jnp.bfloat16)
    l = (r - m.astype(jnp.float32)).astype(jnp.bfloat16)
    return h, m, l


def _split2(x):
    h = x.astype(jnp.bfloat16)
    l = (x - h.astype(jnp.float32)).astype(jnp.bfloat16)
    return h, l


def _inproj_kernel(x_ref, pos_ref, g1_ref, wqk_ref, wvt_ref, wc_ref, gqk_ref, freq_ref,
                   cw_ref, cg_ref,
                   q_ref, k_ref, vt_ref, conv_ref,
                   carry_ref, *, tiles_per_seq):
    tm = x_ref.shape[0]
    i = pl.program_id(0)

    x = x_ref[...]
    ms = jnp.mean(x * x, axis=-1, keepdims=True)
    hn = (x * lax.rsqrt(ms + NORM_EPS) * g1_ref[...]).astype(jnp.bfloat16)

    pos = pos_ref[...].astype(jnp.float32)
    ang = freq_ref[...] * pos
    lane_r = lax.broadcasted_iota(jnp.int32, (LANES, SUBLANES), 0)
    f_c = lax.broadcasted_iota(jnp.int32, (LANES, SUBLANES), 1)
    in_rot = (lane_r % DK) < ROT_DIM
    expand = jnp.where(in_rot & ((lane_r % (ROT_DIM // 2)) == f_c), 1.0, 0.0).astype(jnp.bfloat16)

    def to_rows(t):
        h, m, l = _split3(t)
        r = _dot(expand, h) + _dot(expand, m) + _dot(expand, l)
        return r.T

    cos_r = to_rows(jnp.cos(ang))
    sin_r = to_rows(jnp.sin(ang))
    lane = lax.broadcasted_iota(jnp.int32, (tm, LANES), 1)
    d = lane % DK
    cos_r = jnp.where(d < ROT_DIM, cos_r, 1.0)
    sin_lo = jnp.where(d < ROT_DIM // 2, -sin_r, 0.0)
    sin_hi = jnp.where((d >= ROT_DIM // 2) & (d < ROT_DIM), sin_r, 0.0)

    qk = _dot(hn, wqk_ref[...])
    seg_r = lax.broadcasted_iota(jnp.int32, (2 * LANES, 2 * LANES), 0) // DK
    seg_c = lax.broadcasted_iota(jnp.int32, (2 * LANES, 2 * LANES), 1) // DK
    seg_mean = jnp.where(seg_r == seg_c, 1.0 / DK, 0.0).astype(jnp.bfloat16)
    half = ROT_DIM // 2
    for c2 in range(4):
        blk = qk[:, c2 * 256:(c2 + 1) * 256]
        msq = _dot((blk * blk).astype(jnp.bfloat16), seg_mean)
        y2 = blk * lax.rsqrt(msq + NORM_EPS) * gqk_ref[:, c2 * 256:(c2 + 1) * 256]
        for c1 in range(2):
            c = c2 * 2 + c1
            y = y2[:, c1 * LANES:(c1 + 1) * LANES]
            rot = (y * cos_r
                   + pltpu.roll(y, LANES - half, 1) * sin_lo
                   + pltpu.roll(y, half, 1) * sin_hi)
            if c < N_HEADS:
                q_ref[:, c * LANES:(c + 1) * LANES] = rot.astype(jnp.bfloat16)
            else:
                h = c - N_HEADS
                k_ref[:, h * LANES:(h + 1) * LANES] = rot.astype(jnp.bfloat16)

    vt_ref[...] = _nt_dot(wvt_ref[...], hn).astype(jnp.bfloat16)

    cp = _dot(hn, wc_ref[...])
    cb = cp[:, :CONV_WIDTH]
    y = cp[:, CONV_WIDTH:2 * CONV_WIDTH] * cp[:, 2 * CONV_WIDTH:]

    @pl.when(i % tiles_per_seq == 0)
    def _():
        carry_ref[...] = jnp.zeros_like(carry_ref)

    prev = carry_ref[...]
    row = lax.broadcasted_iota(jnp.int32, (tm, CONV_WIDTH), 0)
    p1 = prev[SUBLANES - 1:SUBLANES, :]
    p2 = prev[SUBLANES - 2:SUBLANES - 1, :]
    y1 = jnp.where(row == 0, p1, pltpu.roll(y, 1, 0))
    y2 = jnp.where(row == 0, p2, jnp.where(row == 1, p1, pltpu.roll(y, 2, 0)))
    carry_ref[...] = y[tm - SUBLANES:, :]
    z = cw_ref[0:1, :] * y2 + cw_ref[1:2, :] * y1 + cw_ref[2:3, :] * y
    co = cb * z
    cms = jnp.mean(co * co, axis=-1, keepdims=True)
    conv_ref[...] = (co * lax.rsqrt(cms + NORM_EPS) * cg_ref[...]).astype(jnp.bfloat16)


def _inproj(x2, pos_row, g1, wqk, wvt, wc, gqk, freqs, cw, cg, seq):
    n = x2.shape[0]
    tm = TM_IN
    grid = (n // tm,)
    const = lambda i: (0, 0)
    return pl.pallas_call(
        functools.partial(_inproj_kernel, tiles_per_seq=seq // tm),
        grid=grid,
        in_specs=[
            pl.BlockSpec((tm, D_MODEL), lambda i: (i, 0)),
            pl.BlockSpec((1, tm), lambda i: (0, i)),
            pl.BlockSpec((1, D_MODEL), const),
            pl.BlockSpec((D_MODEL, 1024), const),
            pl.BlockSpec((512, D_MODEL), const),
            pl.BlockSpec((D_MODEL, 1536), const),
            pl.BlockSpec((1, 1024), const),
            pl.BlockSpec((SUBLANES, 1), const),
            pl.BlockSpec((3, CONV_WIDTH), const),
            pl.BlockSpec((1, CONV_WIDTH), const),
        ],
        out_specs=[
            pl.BlockSpec((tm, ATTN_WIDTH), lambda i: (i, 0)),
            pl.BlockSpec((tm, ATTN_WIDTH), lambda i: (i, 0)),
            pl.BlockSpec((ATTN_WIDTH, tm), lambda i: (0, i)),
            pl.BlockSpec((tm, CONV_WIDTH), lambda i: (i, 0)),
        ],
        out_shape=[
            jax.ShapeDtypeStruct((n, ATTN_WIDTH), jnp.bfloat16),
            jax.ShapeDtypeStruct((n, ATTN_WIDTH), jnp.bfloat16),
            jax.ShapeDtypeStruct((ATTN_WIDTH, n), jnp.bfloat16),
            jax.ShapeDtypeStruct((n, CONV_WIDTH), jnp.bfloat16),
        ],
        scratch_shapes=[pltpu.VMEM((SUBLANES, CONV_WIDTH), jnp.float32)],
        compiler_params=pltpu.CompilerParams(
            dimension_semantics=("arbitrary",), vmem_limit_bytes=VMEM_LIMIT),
        name="inproj",
    )(x2, pos_row, g1, wqk, wvt, wc, gqk, freqs, cw, cg)


def _attn_kernel(q_ref, k_ref, vt_ref, lq1_ref, lk1_ref, lq2_ref, lk2_ref, sg_ref, o_ref, acc_ref, *, nq):
    lam = (jnp.exp(jnp.sum(lq1_ref[...] * lk1_ref[...], axis=-1, keepdims=True))
           - jnp.exp(jnp.sum(lq2_ref[...] * lk2_ref[...], axis=-1, keepdims=True))
           + LAMBDA_INIT)
    tri = (lax.broadcasted_iota(jnp.int32, (HQ, HQ), 0)
           <= lax.broadcasted_iota(jnp.int32, (HQ, HQ), 1))
    row_d = lax.broadcasted_iota(jnp.int32, (LANES, HQ), 0)
    ones_rows = jnp.ones((SUM_ROWS, KV_FULL), jnp.bfloat16)

    def masked(s, n_tri):
        blocks = [jnp.where(tri, s[:, c * HQ:(c + 1) * HQ], -jnp.inf) for c in range(n_tri)]
        return jnp.concatenate(blocks + [s[:, n_tri * HQ:]], axis=1) if n_tri * HQ < s.shape[1] else \
            jnp.concatenate(blocks, axis=1)

    todo = {}
    for t in range(nq):
        lo = t * TQ
        todo[t] = ([(lo, lo + HQ, "mixed"), (lo + HQ, lo + TQ, "upper")]
                   + [(k0, k0 + KV_FULL, "full") for k0 in range(0, lo, KV_FULL)])
    tasks = []
    while any(todo.values()):
        for t in reversed(range(nq)):
            if todo[t]:
                tasks.append((t,) + todo[t].pop(0))
    last_task = {t: max(i for i, tk in enumerate(tasks) if tk[0] == t) for t in range(nq)}
    qzt, m_run = {}, {}

    def q_transposed(t):
        parts = []
        for half in range(2):
            r0 = t * TQ + half * HQ
            qt = q_ref[r0:r0 + HQ, :].astype(jnp.float32).T
            parts.append(jnp.where(row_d < DK, qt, 0.0))
            parts.append(jnp.where(row_d >= DK, qt, 0.0))
        return jnp.concatenate(parts, axis=1).astype(jnp.bfloat16)

    def scores(task):
        t, k0, k1, kind = task
        if t not in qzt:
            qzt[t] = q_transposed(t)
        kb = k_ref[k0:k1, :]
        if kind == "upper":
            return masked(_dot(kb, qzt[t][:, 2 * HQ:]), 2)
        s = _dot(kb, qzt[t])
        return masked(s, 2) if kind == "mixed" else s

    def softmax(task, s):
        t, _, _, kind = task
        mx = jnp.max(s, axis=0, keepdims=True)
        if kind == "mixed":
            m_run[t] = mx
            return jnp.exp2(s - mx).astype(jnp.bfloat16), None
        m_old = m_run[t][:, 2 * HQ:] if kind == "upper" else m_run[t]
        m_new = jnp.maximum(m_old, mx)
        alpha = jnp.exp2(m_old - m_new)
        p = jnp.exp2(s - m_new)
        m_run[t] = jnp.concatenate([m_run[t][:, :2 * HQ], m_new], axis=1) if kind == "upper" else m_new
        return p.astype(jnp.bfloat16), alpha

    def accumulate(task, p, alpha):
        t, k0, k1, kind = task
        vta = jnp.concatenate([vt_ref[:, k0:k1], ones_rows[:, :k1 - k0]], axis=0)
        pv = _dot(vta, p)
        if kind == "mixed":
            acc_ref[t] = pv
        elif kind == "upper":
            acc_ref[t, :, 2 * HQ:] = alpha * acc_ref[t, :, 2 * HQ:] + pv
        else:
            acc_ref[t] = alpha * acc_ref[t] + pv

    def finish(t):
        o_all = acc_ref[t, 0:DV, :] * (1.0 / acc_ref[t, DV:DV + 1, :])
        for half in range(2):
            o = (o_all[:, (2 * half) * HQ:(2 * half + 1) * HQ]
                 - lam * o_all[:, (2 * half + 1) * HQ:(2 * half + 2) * HQ])
            ms = jnp.mean(o * o, axis=0, keepdims=True)
            on = o * lax.rsqrt(ms + NORM_EPS) * sg_ref[...] * (1.0 - LAMBDA_INIT)
            r0 = t * TQ + half * HQ
            o_ref[r0:r0 + HQ, :] = on.T.astype(jnp.bfloat16)

    n = len(tasks)
    s_prev, p_prev = None, None
    for step in range(n + 2):
        if step >= 2:
            accumulate(tasks[step - 2], *p_prev)
            if last_task[tasks[step - 2][0]] == step - 2:
                finish(tasks[step - 2][0])
        if 1 <= step <= n:
            p_prev = softmax(tasks[step - 1], s_prev)
        if step < n:
            s_prev = scores(tasks[step])


def _attention(q, k, vt, lq1, lk1, lq2, lk2, sg_col, bsz, seq):
    n = q.shape[0]
    nq = seq // TQ
    vec = lambda b, h: (0, 0)
    return pl.pallas_call(
        functools.partial(_attn_kernel, nq=nq),
        grid=(bsz, N_HEADS),
        in_specs=[
            pl.BlockSpec((seq, DV), lambda b, h: (b, h)),
            pl.BlockSpec((seq, DV), lambda b, h: (b, h)),
            pl.BlockSpec((DV, seq), lambda b, h: (h, b)),
            pl.BlockSpec((1, DK), vec), pl.BlockSpec((1, DK), vec),
            pl.BlockSpec((1, DK), vec), pl.BlockSpec((1, DK), vec),
            pl.BlockSpec((DV, 1), vec),
        ],
        out_specs=pl.BlockSpec((seq, DV), lambda b, h: (b, h)),
        out_shape=jax.ShapeDtypeStruct((n, ATTN_WIDTH), jnp.bfloat16),
        scratch_shapes=[pltpu.VMEM((nq, DV + SUM_ROWS, 4 * HQ), jnp.float32)],
        compiler_params=pltpu.CompilerParams(
            dimension_semantics=("arbitrary", "arbitrary"), vmem_limit_bytes=VMEM_LIMIT),
        name="attn",
    )(q, k, vt, lq1, lk1, lq2, lk2, sg_col)


ROUTER_ROWS = 128
EXPERT_ROW0 = 32


def _outproj_kernel(x_ref, attn_ref, conv_ref, wo_ref, g2_ref, wr2_ref,
                    h_ref, xs_ref, meta_ref, cnt_ref):
    tm = x_ref.shape[0]
    t = T_SORT
    f32, bf16 = jnp.float32, jnp.bfloat16
    a = jnp.concatenate([attn_ref[...], conv_ref[...]], axis=1)
    h = x_ref[...] + _dot(a, wo_ref[...])
    h_ref[...] = h
    ms = jnp.mean(h * h, axis=-1, keepdims=True)
    hn = h * lax.rsqrt(ms + NORM_EPS) * g2_ref[...]
    hn_hi, hn_lo = _split2(hn)

    hh = _dot(hn_hi, wr2_ref[...])
    logits = hh[:, :ROUTER_ROWS] + hh[:, ROUTER_ROWS:] + _dot(hn_lo, wr2_ref[:, :ROUTER_ROWS])
    lt = logits.T
    row8 = lax.broadcasted_iota(jnp.int32, (SUBLANES, tm), 0).astype(f32)
    neg_inf = -jnp.inf

    def first_argmax(v):
        mx = jnp.max(v, axis=0, keepdims=True)
        idx = jnp.min(jnp.where(v == mx, row8, float(SUBLANES)), axis=0, keepdims=True)
        return mx, idx

    g_log = jnp.where(row8 < N_GROUPS, lt[0:SUBLANES, :], neg_inf)
    g_max, g_sel = first_argmax(g_log)
    g_gate = 1.0 / jnp.sum(jnp.exp(g_log - g_max), axis=0, keepdims=True)
    e_log = jnp.zeros((EPG, tm), f32)
    for g in range(N_GROUPS):
        rows = lt[EXPERT_ROW0 + g * EPG:EXPERT_ROW0 + (g + 1) * EPG, :]
        e_log = jnp.where(g_sel == float(g), rows, e_log)
    v1, i1 = first_argmax(e_log)
    v2, i2 = first_argmax(jnp.where(row8 == i1, neg_inf, e_log))
    tt = jnp.exp(v2 - v1)
    w1 = g_gate / (1.0 + tt)
    w2 = g_gate * tt / (1.0 + tt)
    e1 = g_sel * float(EPG) + i1
    e2 = g_sel * float(EPG) + i2

    row32 = lax.broadcasted_iota(jnp.int32, (N_EXPERTS, tm), 0).astype(f32)
    oh1 = row32 == e1
    oh2 = row32 == e2
    c = jnp.where(oh1 | oh2, 1.0, 0.0).astype(bf16)
    tok_r = lax.broadcasted_iota(jnp.int32, (tm, tm), 0)
    tok_c = lax.broadcasted_iota(jnp.int32, (tm, tm), 1)
    same_tile = (tok_r // t) == (tok_c // t)
    rank = _dot(c, jnp.where(same_tile & (tok_r < tok_c), 1.0, 0.0).astype(bf16))
    cnt_b = _dot(c, jnp.where(same_tile, 1.0, 0.0).astype(bf16))
    ex_r = lax.broadcasted_iota(jnp.int32, (N_EXPERTS, N_EXPERTS), 0)
    ex_c = lax.broadcasted_iota(jnp.int32, (N_EXPERTS, N_EXPERTS), 1)
    lower = jnp.where(ex_c < ex_r, 1.0, 0.0).astype(bf16)
    start_b = _dot(lower, cnt_b.astype(bf16))
    pos_e = start_b + rank
    p1 = jnp.sum(jnp.where(oh1, pos_e, 0.0), axis=0, keepdims=True)
    p2 = jnp.sum(jnp.where(oh2, pos_e, 0.0), axis=0, keepdims=True)

    srow = lax.broadcasted_iota(jnp.int32, (2 * t, t), 0).astype(f32)
    for s in range(tm // t):
        cols = slice(s * t, (s + 1) * t)
        perm = jnp.where((srow == p1[:, cols]) | (srow == p2[:, cols]), 1.0, 0.0).astype(bf16)
        xs = _dot(perm, hn_hi[cols, :])
        xs_ref[s * 2 * t:(s + 1) * 2 * t] = _pack_rows(xs)
        cnt_ref[s * N_EXPERTS:(s + 1) * N_EXPERTS, :] = cnt_b[:, s * t:s * t + LANES]

    meta = jnp.concatenate([p1, p2, w1, w2, jnp.zeros((LANES - 4, tm), f32)], axis=0)
    meta_ref[...] = meta.T


def _outproj(x2, attn_o, conv_o, wo, g2, wr2):
    n = x2.shape[0]
    tm = TM_OUT
    nsub = tm // T_SORT
    const = lambda i: (0, 0)

    return pl.pallas_call(
        _outproj_kernel,
        grid=(n // tm,),
        in_specs=[
            pl.BlockSpec((tm, D_MODEL), lambda i: (i, 0)),
            pl.BlockSpec((tm, ATTN_WIDTH), lambda i: (i, 0)),
            pl.BlockSpec((tm, CONV_WIDTH), lambda i: (i, 0)),
            pl.BlockSpec((D_MODEL, D_MODEL), const),
            pl.BlockSpec((1, D_MODEL), const),
            pl.BlockSpec((D_MODEL, 2 * ROUTER_ROWS), const),
        ],
        out_specs=[
            pl.BlockSpec((tm, D_MODEL), lambda i: (i, 0)),
            pl.BlockSpec((2 * tm, ROW_CHUNKS, LANES), lambda i: (i, 0, 0)),
            pl.BlockSpec((tm, LANES), lambda i: (i, 0)),
            pl.BlockSpec((nsub * N_EXPERTS, LANES), lambda i: (i, 0)),
        ],
        out_shape=[
            jax.ShapeDtypeStruct((n, D_MODEL), jnp.float32),
            jax.ShapeDtypeStruct((2 * n, ROW_CHUNKS, LANES), PACKED_DTYPE),
            jax.ShapeDtypeStruct((n, LANES), jnp.float32),
            jax.ShapeDtypeStruct((n // T_SORT * N_EXPERTS, LANES), jnp.float32),
        ],
        compiler_params=pltpu.CompilerParams(
            dimension_semantics=("arbitrary",), vmem_limit_bytes=VMEM_LIMIT),
        name="outproj",
    )(x2, attn_o, conv_o, wo, g2, wr2)


def _pack_rows(x):
    r = x.shape[0]
    w = pltpu.pack_elementwise([x[:, :ROW_WORDS], x[:, ROW_WORDS:]], packed_dtype=jnp.bfloat16)
    return pltpu.bitcast(w, PACKED_DTYPE).reshape(r, ROW_CHUNKS, LANES)


def _packed_zero_rows(r):
    z = jnp.zeros((r, ROW_CHUNKS, LANES), jnp.float32)
    w = pltpu.pack_elementwise([z, z], packed_dtype=jnp.bfloat16)
    return pltpu.bitcast(w, PACKED_DTYPE)


def _unpack_rows(u):
    r = u.shape[0]
    w = u.reshape(r, ROW_WORDS)
    lo = pltpu.unpack_elementwise(w, index=0, packed_dtype=jnp.bfloat16, unpacked_dtype=jnp.float32)
    hi = pltpu.unpack_elementwise(w, index=1, packed_dtype=jnp.bfloat16, unpacked_dtype=jnp.float32)
    return jnp.concatenate([lo, hi], axis=1).astype(jnp.bfloat16)


def _expert_kernel(nblk_ref, blk0_ref, be_ref, nused_ref, r0_ref, nvalid_ref, tlo_ref, thi_ref,
                   cnt_ref, src_ref, rbase_ref,
                   xs_hbm, wg_ref, wu_ref, wd_ref, y_hbm,
                   xbuf, ybuf, wg_s, wu_s, wd_s, sem, ysem):
    e = pl.program_id(0)
    nused = nused_ref[0]
    n_blocks = be_ref.shape[0]
    n_tiles = cnt_ref.shape[0] // N_EXPERTS

    def start_run(blk, slot, t, live, priority=0):
        r0 = r0_ref[blk]
        k = t * N_EXPERTS + be_ref[blk]
        lo = jnp.maximum(rbase_ref[k], r0)
        hi = jnp.minimum(rbase_ref[k] + cnt_ref[k], r0 + MOE_BLOCK)
        rows = jnp.where(live, hi - lo, 0)

        @pl.when(rows > 0)
        def _():
            pltpu.make_async_copy(xs_hbm.at[pl.ds(src_ref[k] + lo - rbase_ref[k], rows)],
                                  xbuf.at[slot, pl.ds(lo - r0, rows)], sem.at[slot]).start(priority=priority)

    def gather_rolled(blk, slot, t_from):
        def run(t, carry):
            start_run(blk, slot, t, True)
            return carry
        lax.fori_loop(t_from, thi_ref[blk], run, 0)

    def gather_unrolled(blk, slot, live):
        blk = jnp.minimum(blk, n_blocks - 1)
        for r in range(GATHER_UNROLL):
            t = tlo_ref[blk] + r
            start_run(blk, slot, jnp.minimum(t, n_tiles - 1), live & (t < thi_ref[blk]), priority=r % 2)

        @pl.when(live & (tlo_ref[blk] + GATHER_UNROLL < thi_ref[blk]))
        def _():
            gather_rolled(blk, slot, tlo_ref[blk] + GATHER_UNROLL)

    def y_copy(blk, slot):
        return pltpu.make_async_copy(ybuf.at[slot], y_hbm.at[pl.ds(blk * MOE_BLOCK, MOE_BLOCK)], ysem.at[slot])

    @pl.when(e == 0)
    def _():
        for slot in range(N_XBUF):
            xbuf[slot] = _packed_zero_rows(MOE_BLOCK)
        gather_rolled(0, 0, tlo_ref[0])
        for blk in range(1, N_XBUF - 1):
            @pl.when(nused > blk)
            def _():
                gather_rolled(blk, blk, tlo_ref[blk])
        for slot in range(2):
            ybuf[slot] = _packed_zero_rows(MOE_BLOCK)
            y_copy(slot, slot).start()

    @pl.when(nblk_ref[e] > 0)
    def _():
        wg_s[...] = wg_ref[0].astype(jnp.bfloat16)
        wu_s[...] = wu_ref[0].astype(jnp.bfloat16)
        wd_s[...] = wd_ref[0].astype(jnp.bfloat16)

        def block(c, carry):
            b = blk0_ref[e] + c
            slot = b % N_XBUF
            yslot = b % 2
            nv = nvalid_ref[b]
            y_copy(b, yslot).wait()
            pltpu.make_async_copy(xs_hbm.at[pl.ds(0, nv)], xbuf.at[slot, pl.ds(0, nv)], sem.at[slot]).wait()

            x = _unpack_rows(xbuf[slot])
            g = _dot(x, wg_s[...])
            u = _dot(x, wu_s[...])
            act = (g / (1.0 + jnp.exp(-g)) * u).astype(jnp.bfloat16)
            y = _dot(act, wd_s[...])
            ybuf[yslot] = _pack_rows(y)
            y_copy(b, yslot).start()

            ahead = b + N_XBUF - 1
            gather_unrolled(ahead, ahead % N_XBUF, ahead < nused)
            return carry
        lax.fori_loop(0, nblk_ref[e], block, 0)

    @pl.when(e == N_EXPERTS - 1)
    def _():
        for slot in range(2):
            y_copy(slot, slot).wait()
        ybuf[0] = _packed_zero_rows(MOE_BLOCK)

        def zero_block(b, carry):
            y_copy(b, 0).start()
            y_copy(b, 0).wait()
            return carry
        lax.fori_loop(nused, n_blocks, zero_block, 0)


def _experts(tabs, xs, w_gate, w_up, w_down):
    n_slots = tabs["n_slots"]
    bf16 = jnp.bfloat16
    w_map = lambda e, *_: (e, 0, 0)
    return pl.pallas_call(
        _expert_kernel,
        grid_spec=pltpu.PrefetchScalarGridSpec(
            num_scalar_prefetch=11,
            grid=(N_EXPERTS,),
            in_specs=[
                pl.BlockSpec(memory_space=pl.ANY),
                pl.BlockSpec((1, D_MODEL, D_FF), w_map),
                pl.BlockSpec((1, D_MODEL, D_FF), w_map),
                pl.BlockSpec((1, D_FF, D_MODEL), w_map),
            ],
            out_specs=pl.BlockSpec(memory_space=pl.ANY),
            scratch_shapes=[pltpu.VMEM((N_XBUF, MOE_BLOCK, ROW_CHUNKS, LANES), PACKED_DTYPE),
                            pltpu.VMEM((2, MOE_BLOCK, ROW_CHUNKS, LANES), PACKED_DTYPE),
                            pltpu.VMEM((D_MODEL, D_FF), bf16),
                            pltpu.VMEM((D_MODEL, D_FF), bf16),
                            pltpu.VMEM((D_FF, D_MODEL), bf16),
                            pltpu.SemaphoreType.DMA((N_XBUF,)),
                            pltpu.SemaphoreType.DMA((2,))],
        ),
        out_shape=jax.ShapeDtypeStruct((n_slots, ROW_CHUNKS, LANES), PACKED_DTYPE),
        compiler_params=pltpu.CompilerParams(
            dimension_semantics=("arbitrary",), vmem_limit_bytes=VMEM_LIMIT),
        name="experts",
    )(tabs["nblk"], tabs["blk0"], tabs["block_e"], tabs["nused"], tabs["r0"], tabs["nvalid"], tabs["tlo"],
      tabs["thi"], tabs["cnt"], tabs["src"], tabs["rbase"], xs, w_gate, w_up, w_down)


def _combine_kernel(cnt_ref, loc_ref, dst_ref, h_ref, meta_ref, yg_hbm, o_ref, ybuf, sem, *, n_steps, nsub):
    i = pl.program_id(0)
    t = T_SORT
    f32, bf16 = jnp.float32, jnp.bfloat16

    slot_in = i % 2
    tile0 = jnp.minimum(i, n_steps - 1) * nsub
    for s in range(nsub):
        for e in range(N_EXPERTS):
            k = (tile0 + s) * N_EXPERTS + e
            rows = jnp.where(i < n_steps, cnt_ref[k], 0)

            @pl.when(rows > 0)
            def _():
                pltpu.make_async_copy(yg_hbm.at[pl.ds(dst_ref[k], rows)],
                                      ybuf.at[slot_in, pl.ds(s * 2 * t + loc_ref[k], rows)],
                                      sem.at[slot_in]).start(priority=e % 2)

    @pl.when(i > 0)
    def _():
        slot = (i - 1) % 2
        pltpu.make_async_copy(yg_hbm.at[pl.ds(0, nsub * 2 * t)], ybuf.at[slot], sem.at[slot]).wait()
        lane = lax.broadcasted_iota(jnp.int32, (t, 2 * t), 1).astype(f32)
        for s in range(nsub):
            y = _unpack_rows(ybuf[slot, s * 2 * t:(s + 1) * 2 * t])
            meta = meta_ref[s * t:(s + 1) * t, :]
            pick1 = jnp.where(lane == meta[:, 0:1], 1.0, 0.0).astype(bf16)
            pick2 = jnp.where(lane == meta[:, 1:2], 1.0, 0.0).astype(bf16)
            y1 = _dot(pick1, y)
            y2 = _dot(pick2, y)
            o_ref[s * t:(s + 1) * t, :] = h_ref[s * t:(s + 1) * t, :] + (meta[:, 2:3] * y1 + meta[:, 3:4] * y2)


def _combine(cnt, loc, dst, h, meta, yg):
    n = h.shape[0]
    tm = TM_OUT
    nsub = tm // T_SORT
    n_steps = n // tm
    prev = lambda i, *_: (jnp.maximum(i - 1, 0), 0)
    return pl.pallas_call(
        functools.partial(_combine_kernel, n_steps=n_steps, nsub=nsub),
        grid_spec=pltpu.PrefetchScalarGridSpec(
            num_scalar_prefetch=3,
            grid=(n_steps + 1,),
            in_specs=[
                pl.BlockSpec((tm, D_MODEL), prev),
                pl.BlockSpec((tm, LANES), prev),
                pl.BlockSpec(memory_space=pl.ANY),
            ],
            out_specs=pl.BlockSpec((tm, D_MODEL), prev),
            scratch_shapes=[pltpu.VMEM((2, nsub * 2 * T_SORT, ROW_CHUNKS, LANES), PACKED_DTYPE),
                            pltpu.SemaphoreType.DMA((2,))],
        ),
        out_shape=jax.ShapeDtypeStruct((n, D_MODEL), jnp.float32),
        compiler_params=pltpu.CompilerParams(
            dimension_semantics=("arbitrary",), vmem_limit_bytes=VMEM_LIMIT),
        name="combine",
    )(cnt, loc, dst, h, meta, yg)


def _routing_tables(cnt_out, n_tiles, n_tok):
    i32 = jnp.int32
    cnt = cnt_out.reshape(n_tiles, N_EXPERTS, LANES)[:, :, 0].astype(i32)
    count = jnp.sum(cnt, axis=0)
    padded = ((count + MOE_BLOCK - 1) // MOE_BLOCK) * MOE_BLOCK
    pad_end = jnp.cumsum(padded)
    pad_start = pad_end - padded
    run_end = jnp.cumsum(cnt, axis=0)
    rbase = run_end - cnt
    dst = pad_start[None, :] + rbase
    loc = jnp.cumsum(cnt, axis=1) - cnt
    src = loc + (jnp.arange(n_tiles, dtype=i32) * (2 * T_SORT))[:, None]
    n_blocks = (2 * n_tok) // MOE_BLOCK + N_EXPERTS
    block_start = jnp.arange(n_blocks, dtype=i32) * MOE_BLOCK
    block_e = jnp.minimum(jnp.sum(pad_end[None, :] <= block_start[:, None], axis=1), N_EXPERTS - 1).astype(i32)
    nused = (pad_end[-1] // MOE_BLOCK).astype(i32).reshape(1)
    ex = jnp.arange(N_EXPERTS, dtype=i32)
    sel = (block_e[None, :] == ex[:, None]).astype(i32)
    pick = lambda per_expert: jnp.sum(per_expert[..., :, None] * sel, axis=-2)
    r0 = block_start - pick(pad_start)
    nvalid = jnp.clip(pick(count) - r0, 0, MOE_BLOCK)
    tlo = jnp.sum(pick(run_end) <= r0[None, :], axis=0)
    thi = jnp.sum(pick(rbase) < (r0 + MOE_BLOCK)[None, :], axis=0)
    flat = lambda a: a.reshape(-1).astype(i32)
    return dict(cnt=flat(cnt), src=flat(src), dst=flat(dst), loc=flat(loc), rbase=flat(rbase),
                block_e=block_e, nused=nused, r0=flat(r0), nvalid=flat(nvalid), tlo=flat(tlo), thi=flat(thi),
                nblk=flat(padded // MOE_BLOCK), blk0=flat(pad_start // MOE_BLOCK), n_slots=n_blocks * MOE_BLOCK)


def _stage1(x, positions, attn_norm_gain, w_in, q_norm_gain, k_norm_gain, conv_w, conv_out_gain):
    bsz, seq, _ = x.shape
    n = bsz * seq
    f32, bf16 = jnp.float32, jnp.bfloat16
    w = w_in[0]
    wqk = w[:, :2 * ATTN_WIDTH].astype(bf16)
    wvt = w[:, 2 * ATTN_WIDTH:3 * ATTN_WIDTH].T.astype(bf16)
    wc = w[:, 3 * ATTN_WIDTH:].astype(bf16)
    scale = DK ** -0.5 * LOG2E
    gqk = jnp.concatenate([jnp.tile(q_norm_gain[0].astype(f32), 2 * N_HEADS) * scale,
                           jnp.tile(k_norm_gain[0].astype(f32), 2 * N_HEADS)]).reshape(1, -1)
    freqs = (ROPE_THETA ** (-jnp.arange(0, ROT_DIM, 2, dtype=f32) / ROT_DIM)).reshape(SUBLANES, 1)
    return _inproj(x.reshape(n, D_MODEL), positions.reshape(1, n),
                   attn_norm_gain[0].reshape(1, -1).astype(f32), wqk, wvt, wc, gqk, freqs,
                   conv_w[0].astype(f32), conv_out_gain[0].reshape(1, -1).astype(f32), seq)


def kernel(x, positions, attn_norm_gain, w_in, q_norm_gain, k_norm_gain, lambda_q1, lambda_k1, lambda_q2, lambda_k2, subln_gain, conv_w, conv_out_gain, w_out, ffn_norm_gain, w_group_router, w_expert_router, w_gate, w_up, w_down):
    bsz, seq, _ = x.shape
    n = bsz * seq
    f32, bf16 = jnp.float32, jnp.bfloat16
    assert TM_OUT % T_SORT == 0 and seq % TM_IN == 0 and seq % TQ == 0 and TQ % KV_FULL == 0
    q, k, vt, conv_o = _stage1(x, positions, attn_norm_gain, w_in, q_norm_gain, k_norm_gain,
                                conv_w, conv_out_gain)
    attn_o = _attention(q, k, vt,
                        lambda_q1[0].reshape(1, -1).astype(f32), lambda_k1[0].reshape(1, -1).astype(f32),
                        lambda_q2[0].reshape(1, -1).astype(f32), lambda_k2[0].reshape(1, -1).astype(f32),
                        subln_gain[0].reshape(-1, 1).astype(f32), bsz, seq)

    wr = jnp.zeros((D_MODEL, ROUTER_ROWS), f32)
    wr = wr.at[:, 0:N_GROUPS].set(w_group_router[0].astype(f32))
    wr = wr.at[:, EXPERT_ROW0:EXPERT_ROW0 + N_EXPERTS].set(
        jnp.transpose(w_expert_router[0].astype(f32), (1, 0, 2)).reshape(D_MODEL, N_EXPERTS))
    wrh = wr.astype(bf16)
    wr2 = jnp.concatenate([wrh, (wr - wrh.astype(f32)).astype(bf16)], axis=1)
    h, xs, meta, cnt_out = _outproj(x.reshape(n, D_MODEL), attn_o, conv_o, w_out[0].astype(bf16),
                                    ffn_norm_gain[0].reshape(1, -1).astype(f32), wr2)

    tabs = _routing_tables(cnt_out, n // T_SORT, n)
    yg = _experts(tabs, xs, w_gate[0], w_up[0], w_down[0])
    out = _combine(tabs["cnt"], tabs["loc"], tabs["dst"], h, meta, yg)
    return out.reshape(x.shape)
```

```python
import functools
import math

import jax
import jax.numpy as jnp
from jax import lax
from jax.experimental import pallas as pl
from jax.experimental.pallas import tpu as pltpu

D_MODEL = 1024
N_HEADS = 4
DK = 64
DV = 128
ROT_DIM = 16
ROPE_THETA = 500000.0
ATTN_WIDTH = N_HEADS * DV
CONV_WIDTH = 512
NORM_EPS = 1e-6
LOG2E = 1.4426950408889634
LAMBDA_INIT = 0.8 - 0.6 * math.exp(-0.3 * 0)
N_GROUPS = 4
EPG = 8
N_EXPERTS = N_GROUPS * EPG
D_FF = 512
MOE_BLOCK = 256

LANES = 128
SUBLANES = 8
ROW_WORDS = D_MODEL // 2
ROW_CHUNKS = ROW_WORDS // LANES
PACKED_DTYPE = jnp.uint32

TM_IN = 1024
TQ = 512
HQ = TQ // 2
KV_FULL = 512
SUM_ROWS = 16
T_SORT = 256
TM_OUT = 512
N_XBUF = 3
GATHER_UNROLL = 24
GATHER_PARTS = 3
VMEM_LIMIT = 48 * 1024 * 1024


def _nt_dot(a, b):
    return lax.dot_general(a, b, (((1,), (1,)), ((), ())), preferred_element_type=jnp.float32)


def _dot(a, b):
    return jnp.dot(a, b, preferred_element_type=jnp.float32)


def _split3(x):
    h = x.astype(jnp.bfloat16)
    r = x - h.astype(jnp.float32)
    m = r.astype(jnp.bfloat16)
    l = (r - m.astype(jnp.float32)).astype(jnp.bfloat16)
    return h, m, l


def _split2(x):
    h = x.astype(jnp.bfloat16)
    l = (x - h.astype(jnp.float32)).astype(jnp.bfloat16)
    return h, l


def _inproj_kernel(x_ref, pos_ref, g1_ref, wqk_ref, wvt_ref, wc_ref, gqk_ref, freq_ref,
                   cw_ref, cg_ref,
                   q_ref, k_ref, vt_ref, conv_ref,
                   carry_ref, *, tiles_per_seq):
    tm = x_ref.shape[0]
    i = pl.program_id(0)

    x = x_ref[...]
    ms = jnp.mean(x * x, axis=-1, keepdims=True)
    hn = (x * lax.rsqrt(ms + NORM_EPS) * g1_ref[...]).astype(jnp.bfloat16)

    pos = pos_ref[...].astype(jnp.float32)
    ang = freq_ref[...] * pos
    lane_r = lax.broadcasted_iota(jnp.int32, (LANES, SUBLANES), 0)
    f_c = lax.broadcasted_iota(jnp.int32, (LANES, SUBLANES), 1)
    in_rot = (lane_r % DK) < ROT_DIM
    expand = jnp.where(in_rot & ((lane_r % (ROT_DIM // 2)) == f_c), 1.0, 0.0).astype(jnp.bfloat16)

    def to_rows(t):
        h, m, l = _split3(t)
        r = _dot(expand, h) + _dot(expand, m) + _dot(expand, l)
        return r.T

    cos_r = to_rows(jnp.cos(ang))
    sin_r = to_rows(jnp.sin(ang))
    lane = lax.broadcasted_iota(jnp.int32, (tm, LANES), 1)
    d = lane % DK
    cos_r = jnp.where(d < ROT_DIM, cos_r, 1.0)
    sin_lo = jnp.where(d < ROT_DIM // 2, -sin_r, 0.0)
    sin_hi = jnp.where((d >= ROT_DIM // 2) & (d < ROT_DIM), sin_r, 0.0)

    qk = _dot(hn, wqk_ref[...])
    seg_r = lax.broadcasted_iota(jnp.int32, (2 * LANES, 2 * LANES), 0) // DK
    seg_c = lax.broadcasted_iota(jnp.int32, (2 * LANES, 2 * LANES), 1) // DK
    seg_mean = jnp.where(seg_r == seg_c, 1.0 / DK, 0.0).astype(jnp.bfloat16)
    half = ROT_DIM // 2
    for c2 in range(4):
        blk = qk[:, c2 * 256:(c2 + 1) * 256]
        msq = _dot((blk * blk).astype(jnp.bfloat16), seg_mean)
        y2 = blk * lax.rsqrt(msq + NORM_EPS) * gqk_ref[:, c2 * 256:(c2 + 1) * 256]
        for c1 in range(2):
            c = c2 * 2 + c1
            y = y2[:, c1 * LANES:(c1 + 1) * LANES]
            rot = (y * cos_r
                   + pltpu.roll(y, LANES - half, 1) * sin_lo
                   + pltpu.roll(y, half, 1) * sin_hi)
            if c < N_HEADS:
                q_ref[:, c * LANES:(c + 1) * LANES] = rot.astype(jnp.bfloat16)
            else:
                h = c - N_HEADS
                k_ref[:, h * LANES:(h + 1) * LANES] = rot.astype(jnp.bfloat16)

    vt_ref[...] = _nt_dot(wvt_ref[...], hn).astype(jnp.bfloat16)

    cp = _dot(hn, wc_ref[...])
    cb = cp[:, :CONV_WIDTH]
    y = cp[:, CONV_WIDTH:2 * CONV_WIDTH] * cp[:, 2 * CONV_WIDTH:]

    @pl.when(i % tiles_per_seq == 0)
    def _():
        carry_ref[...] = jnp.zeros_like(carry_ref)

    prev = carry_ref[...]
    row = lax.broadcasted_iota(jnp.int32, (tm, CONV_WIDTH), 0)
    p1 = prev[SUBLANES - 1:SUBLANES, :]
    p2 = prev[SUBLANES - 2:SUBLANES - 1, :]
    y1 = jnp.where(row == 0, p1, pltpu.roll(y, 1, 0))
    y2 = jnp.where(row == 0, p2, jnp.where(row == 1, p1, pltpu.roll(y, 2, 0)))
    carry_ref[...] = y[tm - SUBLANES:, :]
    z = cw_ref[0:1, :] * y2 + cw_ref[1:2, :] * y1 + cw_ref[2:3, :] * y
    co = cb * z
    cms = jnp.mean(co * co, axis=-1, keepdims=True)
    conv_ref[...] = (co * lax.rsqrt(cms + NORM_EPS) * cg_ref[...]).astype(jnp.bfloat16)


def _inproj(x2, pos_row, g1, wqk, wvt, wc, gqk, freqs, cw, cg, seq):
    n = x2.shape[0]
    tm = TM_IN
    grid = (n // tm,)
    const = lambda i: (0, 0)
    return pl.pallas_call(
        functools.partial(_inproj_kernel, tiles_per_seq=seq // tm),
        grid=grid,
        in_specs=[
            pl.BlockSpec((tm, D_MODEL), lambda i: (i, 0)),
            pl.BlockSpec((1, tm), lambda i: (0, i)),
            pl.BlockSpec((1, D_MODEL), const),
            pl.BlockSpec((D_MODEL, 1024), const),
            pl.BlockSpec((512, D_MODEL), const),
            pl.BlockSpec((D_MODEL, 1536), const),
            pl.BlockSpec((1, 1024), const),
            pl.BlockSpec((SUBLANES, 1), const),
            pl.BlockSpec((3, CONV_WIDTH), const),
            pl.BlockSpec((1, CONV_WIDTH), const),
        ],
        out_specs=[
            pl.BlockSpec((tm, ATTN_WIDTH), lambda i: (i, 0)),
            pl.BlockSpec((tm, ATTN_WIDTH), lambda i: (i, 0)),
            pl.BlockSpec((ATTN_WIDTH, tm), lambda i: (0, i)),
            pl.BlockSpec((tm, CONV_WIDTH), lambda i: (i, 0)),
        ],
        out_shape=[
            jax.ShapeDtypeStruct((n, ATTN_WIDTH), jnp.bfloat16),
            jax.ShapeDtypeStruct((n, ATTN_WIDTH), jnp.bfloat16),
            jax.ShapeDtypeStruct((ATTN_WIDTH, n), jnp.bfloat16),
            jax.ShapeDtypeStruct((n, CONV_WIDTH), jnp.bfloat16),
        ],
        scratch_shapes=[pltpu.VMEM((SUBLANES, CONV_WIDTH), jnp.float32)],
        compiler_params=pltpu.CompilerParams(
            dimension_semantics=("arbitrary",), vmem_limit_bytes=VMEM_LIMIT),
        name="inproj",
    )(x2, pos_row, g1, wqk, wvt, wc, gqk, freqs, cw, cg)


def _attn_kernel(q_ref, k_ref, vt_ref, lq1_ref, lk1_ref, lq2_ref, lk2_ref, sg_ref, o_ref, acc_ref, *, nq):
    lam = (jnp.exp(jnp.sum(lq1_ref[...] * lk1_ref[...], axis=-1, keepdims=True))
           - jnp.exp(jnp.sum(lq2_ref[...] * lk2_ref[...], axis=-1, keepdims=True))
           + LAMBDA_INIT)
    tri = (lax.broadcasted_iota(jnp.int32, (HQ, HQ), 0)
           <= lax.broadcasted_iota(jnp.int32, (HQ, HQ), 1))
    row_d = lax.broadcasted_iota(jnp.int32, (LANES, HQ), 0)
    ones_rows = jnp.ones((SUM_ROWS, KV_FULL), jnp.bfloat16)

    def masked(s, n_tri):
        blocks = [jnp.where(tri, s[:, c * HQ:(c + 1) * HQ], -jnp.inf) for c in range(n_tri)]
        return jnp.concatenate(blocks + [s[:, n_tri * HQ:]], axis=1) if n_tri * HQ < s.shape[1] else \
            jnp.concatenate(blocks, axis=1)

    todo = {}
    for t in range(nq):
        lo = t * TQ
        todo[t] = ([(lo, lo + HQ, "mixed"), (lo + HQ, lo + TQ, "upper")]
                   + [(k0, k0 + KV_FULL, "full") for k0 in range(0, lo, KV_FULL)])
    tasks = []
    while any(todo.values()):
        for t in reversed(range(nq)):
            if todo[t]:
                tasks.append((t,) + todo[t].pop(0))
    last_task = {t: max(i for i, tk in enumerate(tasks) if tk[0] == t) for t in range(nq)}
    qzt, m_run = {}, {}

    def q_transposed(t):
        parts = []
        for half in range(2):
            r0 = t * TQ + half * HQ
            qt = q_ref[r0:r0 + HQ, :].astype(jnp.float32).T
            parts.append(jnp.where(row_d < DK, qt, 0.0))
            parts.append(jnp.where(row_d >= DK, qt, 0.0))
        return jnp.concatenate(parts, axis=1).astype(jnp.bfloat16)

    def scores(task):
        t, k0, k1, kind = task
        if t not in qzt:
            qzt[t] = q_transposed(t)
        kb = k_ref[k0:k1, :]
        if kind == "upper":
            return masked(_dot(kb, qzt[t][:, 2 * HQ:]), 2)
        s = _dot(kb, qzt[t])
        return masked(s, 2) if kind == "mixed" else s

    def softmax(task, s):
        t, _, _, kind = task
        mx = jnp.max(s, axis=0, keepdims=True)
        if kind == "mixed":
            m_run[t] = mx
            return jnp.exp2(s - mx).astype(jnp.bfloat16), None
        m_old = m_run[t][:, 2 * HQ:] if kind == "upper" else m_run[t]
        m_new = jnp.maximum(m_old, mx)
        alpha = jnp.exp2(m_old - m_new)
        p = jnp.exp2(s - m_new)
        m_run[t] = jnp.concatenate([m_run[t][:, :2 * HQ], m_new], axis=1) if kind == "upper" else m_new
        return p.astype(jnp.bfloat16), alpha

    def accumulate(task, p, alpha):
        t, k0, k1, kind = task
        vta = jnp.concatenate([vt_ref[:, k0:k1], ones_rows[:, :k1 - k0]], axis=0)
        pv = _dot(vta, p)
        if kind == "mixed":
            acc_ref[t] = pv
        elif kind == "upper":
            acc_ref[t, :, 2 * HQ:] = alpha * acc_ref[t, :, 2 * HQ:] + pv
        else:
            acc_ref[t] = alpha * acc_ref[t] + pv

    def finish(t):
        o_all = acc_ref[t, 0:DV, :] * (1.0 / acc_ref[t, DV:DV + 1, :])
        for half in range(2):
            o = (o_all[:, (2 * half) * HQ:(2 * half + 1) * HQ]
                 - lam * o_all[:, (2 * half + 1) * HQ:(2 * half + 2) * HQ])
            ms = jnp.mean(o * o, axis=0, keepdims=True)
            on = o * lax.rsqrt(ms + NORM_EPS) * sg_ref[...] * (1.0 - LAMBDA_INIT)
            r0 = t * TQ + half * HQ
            o_ref[r0:r0 + HQ, :] = on.T.astype(jnp.bfloat16)

    n = len(tasks)
    s_prev, p_prev = None, None
    for step in range(n + 2):
        if step >= 2:
            accumulate(tasks[step - 2], *p_prev)
            if last_task[tasks[step - 2][0]] == step - 2:
                finish(tasks[step - 2][0])
        if 1 <= step <= n:
            p_prev = softmax(tasks[step - 1], s_prev)
        if step < n:
            s_prev = scores(tasks[step])


def _attention(q, k, vt, lq1, lk1, lq2, lk2, sg_col, bsz, seq):
    n = q.shape[0]
    nq = seq // TQ
    vec = lambda b, h: (0, 0)
    return pl.pallas_call(
        functools.partial(_attn_kernel, nq=nq),
        grid=(bsz, N_HEADS),
        in_specs=[
            pl.BlockSpec((seq, DV), lambda b, h: (b, h)),
            pl.BlockSpec((seq, DV), lambda b, h: (b, h)),
            pl.BlockSpec((DV, seq), lambda b, h: (h, b)),
            pl.BlockSpec((1, DK), vec), pl.BlockSpec((1, DK), vec),
            pl.BlockSpec((1, DK), vec), pl.BlockSpec((1, DK), vec),
            pl.BlockSpec((DV, 1), vec),
        ],
        out_specs=pl.BlockSpec((seq, DV), lambda b, h: (b, h)),
        out_shape=jax.ShapeDtypeStruct((n, ATTN_WIDTH), jnp.bfloat16),
        scratch_shapes=[pltpu.VMEM((nq, DV + SUM_ROWS, 4 * HQ), jnp.float32)],
        compiler_params=pltpu.CompilerParams(
            dimension_semantics=("arbitrary", "arbitrary"), vmem_limit_bytes=VMEM_LIMIT),
        name="attn",
    )(q, k, vt, lq1, lk1, lq2, lk2, sg_col)


ROUTER_ROWS = 128
EXPERT_ROW0 = 32


def _outproj_kernel(x_ref, attn_ref, conv_ref, wo_ref, g2_ref, wr2_ref,
                    h_ref, xs_ref, meta_ref, cnt_ref):
    tm = x_ref.shape[0]
    t = T_SORT
    f32, bf16 = jnp.float32, jnp.bfloat16
    a = jnp.concatenate([attn_ref[...], conv_ref[...]], axis=1)
    h = x_ref[...] + _dot(a, wo_ref[...])
    h_ref[...] = h
    ms = jnp.mean(h * h, axis=-1, keepdims=True)
    hn = h * lax.rsqrt(ms + NORM_EPS) * g2_ref[...]
    hn_hi, hn_lo = _split2(hn)

    hh = _dot(hn_hi, wr2_ref[...])
    logits = hh[:, :ROUTER_ROWS] + hh[:, ROUTER_ROWS:] + _dot(hn_lo, wr2_ref[:, :ROUTER_ROWS])
    lt = logits.T
    row8 = lax.broadcasted_iota(jnp.int32, (SUBLANES, tm), 0).astype(f32)
    neg_inf = -jnp.inf

    def first_argmax(v):
        mx = jnp.max(v, axis=0, keepdims=True)
        idx = jnp.min(jnp.where(v == mx, row8, float(SUBLANES)), axis=0, keepdims=True)
        return mx, idx

    g_log = jnp.where(row8 < N_GROUPS, lt[0:SUBLANES, :], neg_inf)
    g_max, g_sel = first_argmax(g_log)
    g_gate = 1.0 / jnp.sum(jnp.exp(g_log - g_max), axis=0, keepdims=True)
    e_log = jnp.zeros((EPG, tm), f32)
    for g in range(N_GROUPS):
        rows = lt[EXPERT_ROW0 + g * EPG:EXPERT_ROW0 + (g + 1) * EPG, :]
        e_log = jnp.where(g_sel == float(g), rows, e_log)
    v1, i1 = first_argmax(e_log)
    v2, i2 = first_argmax(jnp.where(row8 == i1, neg_inf, e_log))
    tt = jnp.exp(v2 - v1)
    w1 = g_gate / (1.0 + tt)
    w2 = g_gate * tt / (1.0 + tt)
    e1 = g_sel * float(EPG) + i1
    e2 = g_sel * float(EPG) + i2

    row32 = lax.broadcasted_iota(jnp.int32, (N_EXPERTS, tm), 0).astype(f32)
    oh1 = row32 == e1
    oh2 = row32 == e2
    c = jnp.where(oh1 | oh2, 1.0, 0.0).astype(bf16)
    tok_r = lax.broadcasted_iota(jnp.int32, (tm, tm), 0)
    tok_c = lax.broadcasted_iota(jnp.int32, (tm, tm), 1)
    same_tile = (tok_r // t) == (tok_c // t)
    rank = _dot(c, jnp.where(same_tile & (tok_r < tok_c), 1.0, 0.0).astype(bf16))
    cnt_b = _dot(c, jnp.where(same_tile, 1.0, 0.0).astype(bf16))
    ex_r = lax.broadcasted_iota(jnp.int32, (N_EXPERTS, N_EXPERTS), 0)
    ex_c = lax.broadcasted_iota(jnp.int32, (N_EXPERTS, N_EXPERTS), 1)
    lower = jnp.where(ex_c < ex_r, 1.0, 0.0).astype(bf16)
    start_b = _dot(lower, cnt_b.astype(bf16))
    pos_e = start_b + rank
    p1 = jnp.sum(jnp.where(oh1, pos_e, 0.0), axis=0, keepdims=True)
    p2 = jnp.sum(jnp.where(oh2, pos_e, 0.0), axis=0, keepdims=True)

    srow = lax.broadcasted_iota(jnp.int32, (2 * t, t), 0).astype(f32)
    for s in range(tm // t):
        cols = slice(s * t, (s + 1) * t)
        perm = jnp.where((srow == p1[:, cols]) | (srow == p2[:, cols]), 1.0, 0.0).astype(bf16)
        xs = _dot(perm, hn_hi[cols, :])
        xs_ref[s * 2 * t:(s + 1) * 2 * t] = _pack_rows(xs)
        cnt_ref[s * N_EXPERTS:(s + 1) * N_EXPERTS, :] = cnt_b[:, s * t:s * t + LANES]

    meta = jnp.concatenate([p1, p2, w1, w2, jnp.zeros((LANES - 4, tm), f32)], axis=0)
    meta_ref[...] = meta.T


def _outproj(x2, attn_o, conv_o, wo, g2, wr2):
    n = x2.shape[0]
    tm = TM_OUT
    nsub = tm // T_SORT
    const = lambda i: (0, 0)

    return pl.pallas_call(
        _outproj_kernel,
        grid=(n // tm,),
        in_specs=[
            pl.BlockSpec((tm, D_MODEL), lambda i: (i, 0)),
            pl.BlockSpec((tm, ATTN_WIDTH), lambda i: (i, 0)),
            pl.BlockSpec((tm, CONV_WIDTH), lambda i: (i, 0)),
            pl.BlockSpec((D_MODEL, D_MODEL), const),
            pl.BlockSpec((1, D_MODEL), const),
            pl.BlockSpec((D_MODEL, 2 * ROUTER_ROWS), const),
        ],
        out_specs=[
            pl.BlockSpec((tm, D_MODEL), lambda i: (i, 0)),
            pl.BlockSpec((2 * tm, ROW_CHUNKS, LANES), lambda i: (i, 0, 0)),
            pl.BlockSpec((tm, LANES), lambda i: (i, 0)),
            pl.BlockSpec((nsub * N_EXPERTS, LANES), lambda i: (i, 0)),
        ],
        out_shape=[
            jax.ShapeDtypeStruct((n, D_MODEL), jnp.float32),
            jax.ShapeDtypeStruct((2 * n, ROW_CHUNKS, LANES), PACKED_DTYPE),
            jax.ShapeDtypeStruct((n, LANES), jnp.float32),
            jax.ShapeDtypeStruct((n // T_SORT * N_EXPERTS, LANES), jnp.float32),
        ],
        compiler_params=pltpu.CompilerParams(
            dimension_semantics=("arbitrary",), vmem_limit_bytes=VMEM_LIMIT),
        name="outproj",
    )(x2, attn_o, conv_o, wo, g2, wr2)


def _pack_rows(x):
    r = x.shape[0]
    w = pltpu.pack_elementwise([x[:, :ROW_WORDS], x[:, ROW_WORDS:]], packed_dtype=jnp.bfloat16)
    return pltpu.bitcast(w, PACKED_DTYPE).reshape(r, ROW_CHUNKS, LANES)


def _packed_zero_rows(r):
    z = jnp.zeros((r, ROW_CHUNKS, LANES), jnp.float32)
    w = pltpu.pack_elementwise([z, z], packed_dtype=jnp.bfloat16)
    return pltpu.bitcast(w, PACKED_DTYPE)


def _unpack_rows(u):
    r = u.shape[0]
    w = u.reshape(r, ROW_WORDS)
    lo = pltpu.unpack_elementwise(w, index=0, packed_dtype=jnp.bfloat16, unpacked_dtype=jnp.float32)
    hi = pltpu.unpack_elementwise(w, index=1, packed_dtype=jnp.bfloat16, unpacked_dtype=jnp.float32)
    return jnp.concatenate([lo, hi], axis=1).astype(jnp.bfloat16)


def _expert_kernel(nblk_ref, blk0_ref, be_ref, nused_ref, r0_ref, nvalid_ref, tlo_ref, thi_ref,
                   cnt_ref, src_ref, rbase_ref,
                   xs_hbm, wg_ref, wu_ref, wd_ref, y_hbm,
                   xbuf, ybuf, wg_s, wu_s, wd_s, sem, ysem):
    e = pl.program_id(0)
    nused = nused_ref[0]
    n_blocks = be_ref.shape[0]
    n_tiles = cnt_ref.shape[0] // N_EXPERTS

    def start_run(blk, slot, t, live, priority=0):
        r0 = r0_ref[blk]
        k = t * N_EXPERTS + be_ref[blk]
        lo = jnp.maximum(rbase_ref[k], r0)
        hi = jnp.minimum(rbase_ref[k] + cnt_ref[k], r0 + MOE_BLOCK)
        rows = jnp.where(live, hi - lo, 0)

        @pl.when(rows > 0)
        def _():
            pltpu.make_async_copy(xs_hbm.at[pl.ds(src_ref[k] + lo - rbase_ref[k], rows)],
                                  xbuf.at[slot, pl.ds(lo - r0, rows)], sem.at[slot]).start(priority=priority)

    def gather_rolled(blk, slot, t_from):
        def run(t, carry):
            start_run(blk, slot, t, True)
            return carry
        lax.fori_loop(t_from, thi_ref[blk], run, 0)

    def gather_unrolled(blk, slot, live, part):
        blk = jnp.minimum(blk, n_blocks - 1)
        per = GATHER_UNROLL // GATHER_PARTS
        for r in range(part * per, (part + 1) * per):
            t = tlo_ref[blk] + r
            start_run(blk, slot, jnp.minimum(t, n_tiles - 1), live & (t < thi_ref[blk]), priority=r % 2)
        if part == GATHER_PARTS - 1:
            @pl.when(live & (tlo_ref[blk] + GATHER_UNROLL < thi_ref[blk]))
            def _():
                gather_rolled(blk, slot, tlo_ref[blk] + GATHER_UNROLL)

    def y_copy(blk, slot):
        return pltpu.make_async_copy(ybuf.at[slot], y_hbm.at[pl.ds(blk * MOE_BLOCK, MOE_BLOCK)], ysem.at[slot])

    @pl.when(e == 0)
    def _():
        for slot in range(N_XBUF):
            xbuf[slot] = _packed_zero_rows(MOE_BLOCK)
        gather_rolled(0, 0, tlo_ref[0])
        for blk in range(1, N_XBUF - 1):
            @pl.when(nused > blk)
            def _():
                gather_rolled(blk, blk, tlo_ref[blk])
        for slot in range(2):
            ybuf[slot] = _packed_zero_rows(MOE_BLOCK)
            y_copy(slot, slot).start()

    @pl.when(nblk_ref[e] > 0)
    def _():
        wg_s[...] = wg_ref[0].astype(jnp.bfloat16)
        wu_s[...] = wu_ref[0].astype(jnp.bfloat16)
        wd_s[...] = wd_ref[0].astype(jnp.bfloat16)

        def block(c, carry):
            b = blk0_ref[e] + c
            slot = b % N_XBUF
            yslot = b % 2
            nv = nvalid_ref[b]
            y_copy(b, yslot).wait()
            pltpu.make_async_copy(xs_hbm.at[pl.ds(0, nv)], xbuf.at[slot, pl.ds(0, nv)], sem.at[slot]).wait()

            ahead = b + N_XBUF - 1
            x = _unpack_rows(xbuf[slot])
            g = _dot(x, wg_s[...])
            gather_unrolled(ahead, ahead % N_XBUF, ahead < nused, 0)
            u = _dot(x, wu_s[...])
            act = (g / (1.0 + jnp.exp(-g)) * u).astype(jnp.bfloat16)
            gather_unrolled(ahead, ahead % N_XBUF, ahead < nused, 1)
            y = _dot(act, wd_s[...])
            ybuf[yslot] = _pack_rows(y)
            y_copy(b, yslot).start()
            gather_unrolled(ahead, ahead % N_XBUF, ahead < nused, 2)
            return carry
        lax.fori_loop(0, nblk_ref[e], block, 0)

    @pl.when(e == N_EXPERTS - 1)
    def _():
        for slot in range(2):
            y_copy(slot, slot).wait()
        ybuf[0] = _packed_zero_rows(MOE_BLOCK)

        def zero_block(b, carry):
            y_copy(b, 0).start()
            y_copy(b, 0).wait()
            return carry
        lax.fori_loop(nused, n_blocks, zero_block, 0)


def _experts(tabs, xs, w_gate, w_up, w_down):
    n_slots = tabs["n_slots"]
    bf16 = jnp.bfloat16
    w_map = lambda e, *_: (e, 0, 0)
    return pl.pallas_call(
        _expert_kernel,
        grid_spec=pltpu.PrefetchScalarGridSpec(
            num_scalar_prefetch=11,
            grid=(N_EXPERTS,),
            in_specs=[
                pl.BlockSpec(memory_space=pl.ANY),
                pl.BlockSpec((1, D_MODEL, D_FF), w_map),
                pl.BlockSpec((1, D_MODEL, D_FF), w_map),
                pl.BlockSpec((1, D_FF, D_MODEL), w_map),
            ],
            out_specs=pl.BlockSpec(memory_space=pl.ANY),
            scratch_shapes=[pltpu.VMEM((N_XBUF, MOE_BLOCK, ROW_CHUNKS, LANES), PACKED_DTYPE),
                            pltpu.VMEM((2, MOE_BLOCK, ROW_CHUNKS, LANES), PACKED_DTYPE),
                            pltpu.VMEM((D_MODEL, D_FF), bf16),
                            pltpu.VMEM((D_MODEL, D_FF), bf16),
                            pltpu.VMEM((D_FF, D_MODEL), bf16),
                            pltpu.SemaphoreType.DMA((N_XBUF,)),
                            pltpu.SemaphoreType.DMA((2,))],
        ),
        out_shape=jax.ShapeDtypeStruct((n_slots, ROW_CHUNKS, LANES), PACKED_DTYPE),
        compiler_params=pltpu.CompilerParams(
            dimension_semantics=("arbitrary",), vmem_limit_bytes=VMEM_LIMIT),
        name="experts",
    )(tabs["nblk"], tabs["blk0"], tabs["block_e"], tabs["nused"], tabs["r0"], tabs["nvalid"], tabs["tlo"],
      tabs["thi"], tabs["cnt"], tabs["src"], tabs["rbase"], xs, w_gate, w_up, w_down)


def _combine_kernel(cnt_ref, loc_ref, dst_ref, h_ref, meta_ref, yg_hbm, o_ref, ybuf, sem, *, n_steps, nsub):
    i = pl.program_id(0)
    t = T_SORT
    f32, bf16 = jnp.float32, jnp.bfloat16

    slot_in = i % 2
    tile0 = jnp.minimum(i, n_steps - 1) * nsub
    for s in range(nsub):
        for e in range(N_EXPERTS):
            k = (tile0 + s) * N_EXPERTS + e
            rows = jnp.where(i < n_steps, cnt_ref[k], 0)

            @pl.when(rows > 0)
            def _():
                pltpu.make_async_copy(yg_hbm.at[pl.ds(dst_ref[k], rows)],
                                      ybuf.at[slot_in, pl.ds(s * 2 * t + loc_ref[k], rows)],
                                      sem.at[slot_in]).start(priority=e % 2)

    @pl.when(i > 0)
    def _():
        slot = (i - 1) % 2
        pltpu.make_async_copy(yg_hbm.at[pl.ds(0, nsub * 2 * t)], ybuf.at[slot], sem.at[slot]).wait()
        lane = lax.broadcasted_iota(jnp.int32, (t, 2 * t), 1).astype(f32)
        for s in range(nsub):
            y = _unpack_rows(ybuf[slot, s * 2 * t:(s + 1) * 2 * t])
            meta = meta_ref[s * t:(s + 1) * t, :]
            pick1 = jnp.where(lane == meta[:, 0:1], 1.0, 0.0).astype(bf16)
            pick2 = jnp.where(lane == meta[:, 1:2], 1.0, 0.0).astype(bf16)
            y1 = _dot(pick1, y)
            y2 = _dot(pick2, y)
            o_ref[s * t:(s + 1) * t, :] = h_ref[s * t:(s + 1) * t, :] + (meta[:, 2:3] * y1 + meta[:, 3:4] * y2)


def _combine(cnt, loc, dst, h, meta, yg):
    n = h.shape[0]
    tm = TM_OUT
    nsub = tm // T_SORT
    n_steps = n // tm
    prev = lambda i, *_: (jnp.maximum(i - 1, 0), 0)
    return pl.pallas_call(
        functools.partial(_combine_kernel, n_steps=n_steps, nsub=nsub),
        grid_spec=pltpu.PrefetchScalarGridSpec(
            num_scalar_prefetch=3,
            grid=(n_steps + 1,),
            in_specs=[
                pl.BlockSpec((tm, D_MODEL), prev),
                pl.BlockSpec((tm, LANES), prev),
                pl.BlockSpec(memory_space=pl.ANY),
            ],
            out_specs=pl.BlockSpec((tm, D_MODEL), prev),
            scratch_shapes=[pltpu.VMEM((2, nsub * 2 * T_SORT, ROW_CHUNKS, LANES), PACKED_DTYPE),
                            pltpu.SemaphoreType.DMA((2,))],
        ),
        out_shape=jax.ShapeDtypeStruct((n, D_MODEL), jnp.float32),
        compiler_params=pltpu.CompilerParams(
            dimension_semantics=("arbitrary",), vmem_limit_bytes=VMEM_LIMIT),
        name="combine",
    )(cnt, loc, dst, h, meta, yg)


def _routing_tables(cnt_out, n_tiles, n_tok):
    i32 = jnp.int32
    cnt = cnt_out.reshape(n_tiles, N_EXPERTS, LANES)[:, :, 0].astype(i32)
    count = jnp.sum(cnt, axis=0)
    padded = ((count + MOE_BLOCK - 1) // MOE_BLOCK) * MOE_BLOCK
    pad_end = jnp.cumsum(padded)
    pad_start = pad_end - padded
    run_end = jnp.cumsum(cnt, axis=0)
    rbase = run_end - cnt
    dst = pad_start[None, :] + rbase
    loc = jnp.cumsum(cnt, axis=1) - cnt
    src = loc + (jnp.arange(n_tiles, dtype=i32) * (2 * T_SORT))[:, None]
    n_blocks = (2 * n_tok) // MOE_BLOCK + N_EXPERTS
    block_start = jnp.arange(n_blocks, dtype=i32) * MOE_BLOCK
    block_e = jnp.minimum(jnp.sum(pad_end[None, :] <= block_start[:, None], axis=1), N_EXPERTS - 1).astype(i32)
    nused = (pad_end[-1] // MOE_BLOCK).astype(i32).reshape(1)
    ex = jnp.arange(N_EXPERTS, dtype=i32)
    sel = (block_e[None, :] == ex[:, None]).astype(i32)
    pick = lambda per_expert: jnp.sum(per_expert[..., :, None] * sel, axis=-2)
    r0 = block_start - pick(pad_start)
    nvalid = jnp.clip(pick(count) - r0, 0, MOE_BLOCK)
    tlo = jnp.sum(pick(run_end) <= r0[None, :], axis=0)
    thi = jnp.sum(pick(rbase) < (r0 + MOE_BLOCK)[None, :], axis=0)
    flat = lambda a: a.reshape(-1).astype(i32)
    return dict(cnt=flat(cnt), src=flat(src), dst=flat(dst), loc=flat(loc), rbase=flat(rbase),
                block_e=block_e, nused=nused, r0=flat(r0), nvalid=flat(nvalid), tlo=flat(tlo), thi=flat(thi),
                nblk=flat(padded // MOE_BLOCK), blk0=flat(pad_start // MOE_BLOCK), n_slots=n_blocks * MOE_BLOCK)


def _stage1(x, positions, attn_norm_gain, w_in, q_norm_gain, k_norm_gain, conv_w, conv_out_gain):
    bsz, seq, _ = x.shape
    n = bsz * seq
    f32, bf16 = jnp.float32, jnp.bfloat16
    w = w_in[0]
    wqk = w[:, :2 * ATTN_WIDTH].astype(bf16)
    wvt = w[:, 2 * ATTN_WIDTH:3 * ATTN_WIDTH].T.astype(bf16)
    wc = w[:, 3 * ATTN_WIDTH:].astype(bf16)
    scale = DK ** -0.5 * LOG2E
    gqk = jnp.concatenate([jnp.tile(q_norm_gain[0].astype(f32), 2 * N_HEADS) * scale,
                           jnp.tile(k_norm_gain[0].astype(f32), 2 * N_HEADS)]).reshape(1, -1)
    freqs = (ROPE_THETA ** (-jnp.arange(0, ROT_DIM, 2, dtype=f32) / ROT_DIM)).reshape(SUBLANES, 1)
    return _inproj(x.reshape(n, D_MODEL), positions.reshape(1, n),
                   attn_norm_gain[0].reshape(1, -1).astype(f32), wqk, wvt, wc, gqk, freqs,
                   conv_w[0].astype(f32), conv_out_gain[0].reshape(1, -1).astype(f32), seq)


def kernel(x, positions, attn_norm_gain, w_in, q_norm_gain, k_norm_gain, lambda_q1, lambda_k1, lambda_q2, lambda_k2, subln_gain, conv_w, conv_out_gain, w_out, ffn_norm_gain, w_group_router, w_expert_router, w_gate, w_up, w_down):
    bsz, seq, _ = x.shape
    n = bsz * seq
    f32, bf16 = jnp.float32, jnp.bfloat16
    assert TM_OUT % T_SORT == 0 and seq % TM_IN == 0 and seq % TQ == 0 and TQ % KV_FULL == 0
    q, k, vt, conv_o = _stage1(x, positions, attn_norm_gain, w_in, q_norm_gain, k_norm_gain,
                                conv_w, conv_out_gain)
    attn_o = _attention(q, k, vt,
                        lambda_q1[0].reshape(1, -1).astype(f32), lambda_k1[0].reshape(1, -1).astype(f32),
                        lambda_q2[0].reshape(1, -1).astype(f32), lambda_k2[0].reshape(1, -1).astype(f32),
                        subln_gain[0].reshape(-1, 1).astype(f32), bsz, seq)

    wr = jnp.zeros((D_MODEL, ROUTER_ROWS), f32)
    wr = wr.at[:, 0:N_GROUPS].set(w_group_router[0].astype(f32))
    wr = wr.at[:, EXPERT_ROW0:EXPERT_ROW0 + N_EXPERTS].set(
        jnp.transpose(w_expert_router[0].astype(f32), (1, 0, 2)).reshape(D_MODEL, N_EXPERTS))
    wrh = wr.astype(bf16)
    wr2 = jnp.concatenate([wrh, (wr - wrh.astype(f32)).astype(bf16)], axis=1)
    h, xs, meta, cnt_out = _outproj(x.reshape(n, D_MODEL), attn_o, conv_o, w_out[0].astype(bf16),
                                    ffn_norm_gain[0].reshape(1, -1).astype(f32), wr2)

    tabs = _routing_tables(cnt_out, n // T_SORT, n)
    yg = _experts(tabs, xs, w_gate[0], w_up[0], w_down[0])
    out = _combine(tabs["cnt"], tabs["loc"], tabs["dst"], h, meta, yg)
    return out.reshape(x.shape)
```

```python
import functools
import math

import jax
import jax.numpy as jnp
from jax import lax
from jax.experimental import pallas as pl
from jax.experimental.pallas import tpu as pltpu

D_MODEL = 1024
N_HEADS = 4
DK = 64
DV = 128
ROT_DIM = 16
ROPE_THETA = 500000.0
ATTN_WIDTH = N_HEADS * DV
CONV_WIDTH = 512
NORM_EPS = 1e-6
LOG2E = 1.4426950408889634
LAMBDA_INIT = 0.8 - 0.6 * math.exp(-0.3 * 0)
N_GROUPS = 4
EPG = 8
N_EXPERTS = N_GROUPS * EPG
D_FF = 512
MOE_BLOCK = 256

LANES = 128
SUBLANES = 8
ROW_WORDS = D_MODEL // 2
ROW_CHUNKS = ROW_WORDS // LANES
PACKED_DTYPE = jnp.uint32

TM_IN = 1024
TQ = 512
HQ = TQ // 2
KV_FULL = 512
SUM_ROWS = 16
T_SORT = 256
TM_OUT = 512
N_XBUF = 3
GATHER_UNROLL = 24
GATHER_PARTS = 3
VMEM_LIMIT = 48 * 1024 * 1024


def _nt_dot(a, b):
    return lax.dot_general(a, b, (((1,), (1,)), ((), ())), preferred_element_type=jnp.float32)


def _dot(a, b):
    return jnp.dot(a, b, preferred_element_type=jnp.float32)


def _split3(x):
    h = x.astype(jnp.bfloat16)
    r = x - h.astype(jnp.float32)
    m = r.astype(jnp.bfloat16)
    l = (r - m.astype(jnp.float32)).astype(jnp.bfloat16)
    return h, m, l


def _split2(x):
    h = x.astype(jnp.bfloat16)
    l = (x - h.astype(jnp.float32)).astype(jnp.bfloat16)
    return h, l


def _inproj_kernel(x_ref, pos_ref, g1_ref, wqk_ref, wvt_ref, wc_ref, gqk_ref, freq_ref,
                   cw_ref, cg_ref,
                   q_ref, k_ref, vt_ref, conv_ref,
                   carry_ref, *, tiles_per_seq):
    tm = x_ref.shape[0]
    i = pl.program_id(0)

    x = x_ref[...]
    ms = jnp.mean(x * x, axis=-1, keepdims=True)
    hn = (x * lax.rsqrt(ms + NORM_EPS) * g1_ref[...]).astype(jnp.bfloat16)

    pos = pos_ref[...].astype(jnp.float32)
    ang = freq_ref[...] * pos
    lane_r = lax.broadcasted_iota(jnp.int32, (LANES, SUBLANES), 0)
    f_c = lax.broadcasted_iota(jnp.int32, (LANES, SUBLANES), 1)
    in_rot = (lane_r % DK) < ROT_DIM
    expand = jnp.where(in_rot & ((lane_r % (ROT_DIM // 2)) == f_c), 1.0, 0.0).astype(jnp.bfloat16)

    def to_rows(t):
        h, m, l = _split3(t)
        r = _dot(expand, h) + _dot(expand, m) + _dot(expand, l)
        return r.T

    cos_r = to_rows(jnp.cos(ang))
    sin_r = to_rows(jnp.sin(ang))
    lane = lax.broadcasted_iota(jnp.int32, (tm, LANES), 1)
    d = lane % DK
    cos_r = jnp.where(d < ROT_DIM, cos_r, 1.0)
    sin_lo = jnp.where(d < ROT_DIM // 2, -sin_r, 0.0)
    sin_hi = jnp.where((d >= ROT_DIM // 2) & (d < ROT_DIM), sin_r, 0.0)

    qk = _dot(hn, wqk_ref[...])
    seg_r = lax.broadcasted_iota(jnp.int32, (2 * LANES, 2 * LANES), 0) // DK
    seg_c = lax.broadcasted_iota(jnp.int32, (2 * LANES, 2 * LANES), 1) // DK
    seg_mean = jnp.where(seg_r == seg_c, 1.0 / DK, 0.0).astype(jnp.bfloat16)
    half = ROT_DIM // 2
    for c2 in range(4):
        blk = qk[:, c2 * 256:(c2 + 1) * 256]
        msq = _dot((blk * blk).astype(jnp.bfloat16), seg_mean)
        y2 = blk * lax.rsqrt(msq + NORM_EPS) * gqk_ref[:, c2 * 256:(c2 + 1) * 256]
        for c1 in range(2):
            c = c2 * 2 + c1
            y = y2[:, c1 * LANES:(c1 + 1) * LANES]
            rot = (y * cos_r
                   + pltpu.roll(y, LANES - half, 1) * sin_lo
                   + pltpu.roll(y, half, 1) * sin_hi)
            if c < N_HEADS:
                q_ref[:, c * LANES:(c + 1) * LANES] = rot.astype(jnp.bfloat16)
            else:
                h = c - N_HEADS
                k_ref[:, h * LANES:(h + 1) * LANES] = rot.astype(jnp.bfloat16)

    vt_ref[...] = _nt_dot(wvt_ref[...], hn).astype(jnp.bfloat16)

    cp = _dot(hn, wc_ref[...])
    cb = cp[:, :CONV_WIDTH]
    y = cp[:, CONV_WIDTH:2 * CONV_WIDTH] * cp[:, 2 * CONV_WIDTH:]

    @pl.when(i % tiles_per_seq == 0)
    def _():
        carry_ref[...] = jnp.zeros_like(carry_ref)

    prev = carry_ref[...]
    row = lax.broadcasted_iota(jnp.int32, (tm, CONV_WIDTH), 0)
    p1 = prev[SUBLANES - 1:SUBLANES, :]
    p2 = prev[SUBLANES - 2:SUBLANES - 1, :]
    y1 = jnp.where(row == 0, p1, pltpu.roll(y, 1, 0))
    y2 = jnp.where(row == 0, p2, jnp.where(row == 1, p1, pltpu.roll(y, 2, 0)))
    carry_ref[...] = y[tm - SUBLANES:, :]
    z = cw_ref[0:1, :] * y2 + cw_ref[1:2, :] * y1 + cw_ref[2:3, :] * y
    co = cb * z
    cms = jnp.mean(co * co, axis=-1, keepdims=True)
    conv_ref[...] = (co * lax.rsqrt(cms + NORM_EPS) * cg_ref[...]).astype(jnp.bfloat16)


def _inproj(x2, pos_row, g1, wqk, wvt, wc, gqk, freqs, cw, cg, seq):
    n = x2.shape[0]
    tm = TM_IN
    grid = (n // tm,)
    const = lambda i: (0, 0)
    return pl.pallas_call(
        functools.partial(_inproj_kernel, tiles_per_seq=seq // tm),
        grid=grid,
        in_specs=[
            pl.BlockSpec((tm, D_MODEL), lambda i: (i, 0)),
            pl.BlockSpec((1, tm), lambda i: (0, i)),
            pl.BlockSpec((1, D_MODEL), const),
            pl.BlockSpec((D_MODEL, 2 * ATTN_WIDTH), const),
            pl.BlockSpec((ATTN_WIDTH, D_MODEL), const),
            pl.BlockSpec((D_MODEL, 3 * CONV_WIDTH), lambda i: (0, 1)),
            pl.BlockSpec((1, 1024), const),
            pl.BlockSpec((SUBLANES, 1), const),
            pl.BlockSpec((3, CONV_WIDTH), const),
            pl.BlockSpec((1, CONV_WIDTH), const),
        ],
        out_specs=[
            pl.BlockSpec((tm, ATTN_WIDTH), lambda i: (i, 0)),
            pl.BlockSpec((tm, ATTN_WIDTH), lambda i: (i, 0)),
            pl.BlockSpec((ATTN_WIDTH, tm), lambda i: (0, i)),
            pl.BlockSpec((tm, CONV_WIDTH), lambda i: (i, 0)),
        ],
        out_shape=[
            jax.ShapeDtypeStruct((n, ATTN_WIDTH), jnp.bfloat16),
            jax.ShapeDtypeStruct((n, ATTN_WIDTH), jnp.bfloat16),
            jax.ShapeDtypeStruct((ATTN_WIDTH, n), jnp.bfloat16),
            jax.ShapeDtypeStruct((n, CONV_WIDTH), jnp.bfloat16),
        ],
        scratch_shapes=[pltpu.VMEM((SUBLANES, CONV_WIDTH), jnp.float32)],
        compiler_params=pltpu.CompilerParams(
            dimension_semantics=("arbitrary",), vmem_limit_bytes=VMEM_LIMIT),
        name="inproj",
    )(x2, pos_row, g1, wqk, wvt, wc, gqk, freqs, cw, cg)


def _attn_kernel(q_ref, k_ref, vt_ref, lq1_ref, lk1_ref, lq2_ref, lk2_ref, sg_ref, o_ref, acc_ref, *, nq):
    lam = (jnp.exp(jnp.sum(lq1_ref[...] * lk1_ref[...], axis=-1, keepdims=True))
           - jnp.exp(jnp.sum(lq2_ref[...] * lk2_ref[...], axis=-1, keepdims=True))
           + LAMBDA_INIT)
    tri = (lax.broadcasted_iota(jnp.int32, (HQ, HQ), 0)
           <= lax.broadcasted_iota(jnp.int32, (HQ, HQ), 1))
    row_d = lax.broadcasted_iota(jnp.int32, (LANES, HQ), 0)
    ones_rows = jnp.ones((SUM_ROWS, KV_FULL), jnp.bfloat16)

    def masked(s, n_tri):
        blocks = [jnp.where(tri, s[:, c * HQ:(c + 1) * HQ], -jnp.inf) for c in range(n_tri)]
        return jnp.concatenate(blocks + [s[:, n_tri * HQ:]], axis=1) if n_tri * HQ < s.shape[1] else \
            jnp.concatenate(blocks, axis=1)

    todo = {}
    for t in range(nq):
        lo = t * TQ
        todo[t] = ([(lo, lo + HQ, "mixed"), (lo + HQ, lo + TQ, "upper")]
                   + [(k0, k0 + KV_FULL, "full") for k0 in range(0, lo, KV_FULL)])
    tasks = []
    while any(todo.values()):
        for t in reversed(range(nq)):
            if todo[t]:
                tasks.append((t,) + todo[t].pop(0))
    last_task = {t: max(i for i, tk in enumerate(tasks) if tk[0] == t) for t in range(nq)}
    qzt, m_run = {}, {}

    def q_transposed(t):
        parts = []
        for half in range(2):
            r0 = t * TQ + half * HQ
            qt = q_ref[r0:r0 + HQ, :].astype(jnp.float32).T
            parts.append(jnp.where(row_d < DK, qt, 0.0))
            parts.append(jnp.where(row_d >= DK, qt, 0.0))
        return jnp.concatenate(parts, axis=1).astype(jnp.bfloat16)

    def scores(task):
        t, k0, k1, kind = task
        if t not in qzt:
            qzt[t] = q_transposed(t)
        kb = k_ref[k0:k1, :]
        if kind == "upper":
            return masked(_dot(kb, qzt[t][:, 2 * HQ:]), 2)
        s = _dot(kb, qzt[t])
        return masked(s, 2) if kind == "mixed" else s

    def softmax(task, s):
        t, _, _, kind = task
        mx = jnp.max(s, axis=0, keepdims=True)
        if kind == "mixed":
            m_run[t] = mx
            return jnp.exp2(s - mx).astype(jnp.bfloat16), None
        m_old = m_run[t][:, 2 * HQ:] if kind == "upper" else m_run[t]
        m_new = jnp.maximum(m_old, mx)
        alpha = jnp.exp2(m_old - m_new)
        p = jnp.exp2(s - m_new)
        m_run[t] = jnp.concatenate([m_run[t][:, :2 * HQ], m_new], axis=1) if kind == "upper" else m_new
        return p.astype(jnp.bfloat16), alpha

    def accumulate(task, p, alpha):
        t, k0, k1, kind = task
        vta = jnp.concatenate([vt_ref[:, k0:k1], ones_rows[:, :k1 - k0]], axis=0)
        pv = _dot(vta, p)
        if kind == "mixed":
            acc_ref[t] = pv
        elif kind == "upper":
            acc_ref[t, :, 2 * HQ:] = alpha * acc_ref[t, :, 2 * HQ:] + pv
        else:
            acc_ref[t] = alpha * acc_ref[t] + pv

    def finish(t):
        o_all = acc_ref[t, 0:DV, :] * (1.0 / acc_ref[t, DV:DV + 1, :])
        for half in range(2):
            o = (o_all[:, (2 * half) * HQ:(2 * half + 1) * HQ]
                 - lam * o_all[:, (2 * half + 1) * HQ:(2 * half + 2) * HQ])
            ms = jnp.mean(o * o, axis=0, keepdims=True)
            on = o * lax.rsqrt(ms + NORM_EPS) * sg_ref[...] * (1.0 - LAMBDA_INIT)
            r0 = t * TQ + half * HQ
            o_ref[r0:r0 + HQ, :] = on.T.astype(jnp.bfloat16)

    n = len(tasks)
    s_prev, p_prev = None, None
    for step in range(n + 2):
        if step >= 2:
            accumulate(tasks[step - 2], *p_prev)
            if last_task[tasks[step - 2][0]] == step - 2:
                finish(tasks[step - 2][0])
        if 1 <= step <= n:
            p_prev = softmax(tasks[step - 1], s_prev)
        if step < n:
            s_prev = scores(tasks[step])


def _attention(q, k, vt, lq1, lk1, lq2, lk2, sg_col, bsz, seq):
    n = q.shape[0]
    nq = seq // TQ
    vec = lambda b, h: (0, 0)
    return pl.pallas_call(
        functools.partial(_attn_kernel, nq=nq),
        grid=(bsz, N_HEADS),
        in_specs=[
            pl.BlockSpec((seq, DV), lambda b, h: (b, h)),
            pl.BlockSpec((seq, DV), lambda b, h: (b, h)),
            pl.BlockSpec((DV, seq), lambda b, h: (h, b)),
            pl.BlockSpec((1, DK), vec), pl.BlockSpec((1, DK), vec),
            pl.BlockSpec((1, DK), vec), pl.BlockSpec((1, DK), vec),
            pl.BlockSpec((DV, 1), vec),
        ],
        out_specs=pl.BlockSpec((seq, DV), lambda b, h: (b, h)),
        out_shape=jax.ShapeDtypeStruct((n, ATTN_WIDTH), jnp.bfloat16),
        scratch_shapes=[pltpu.VMEM((nq, DV + SUM_ROWS, 4 * HQ), jnp.float32)],
        compiler_params=pltpu.CompilerParams(
            dimension_semantics=("arbitrary", "arbitrary"), vmem_limit_bytes=VMEM_LIMIT),
        name="attn",
    )(q, k, vt, lq1, lk1, lq2, lk2, sg_col)


ROUTER_ROWS = 128
EXPERT_ROW0 = 32


def _outproj_kernel(x_ref, attn_ref, conv_ref, wo_ref, g2_ref, wr2_ref,
                    h_ref, xs_ref, meta_ref, cnt_ref):
    tm = x_ref.shape[0]
    t = T_SORT
    f32, bf16 = jnp.float32, jnp.bfloat16
    a = jnp.concatenate([attn_ref[...], conv_ref[...]], axis=1)
    h = x_ref[...] + _dot(a, wo_ref[...])
    h_ref[...] = h
    ms = jnp.mean(h * h, axis=-1, keepdims=True)
    hn = h * lax.rsqrt(ms + NORM_EPS) * g2_ref[...]
    hn_hi, hn_lo = _split2(hn)

    hh = _dot(hn_hi, wr2_ref[...])
    logits = hh[:, :ROUTER_ROWS] + hh[:, ROUTER_ROWS:] + _dot(hn_lo, wr2_ref[:, :ROUTER_ROWS])
    lt = logits.T
    row8 = lax.broadcasted_iota(jnp.int32, (SUBLANES, tm), 0).astype(f32)
    neg_inf = -jnp.inf

    def first_argmax(v):
        mx = jnp.max(v, axis=0, keepdims=True)
        idx = jnp.min(jnp.where(v == mx, row8, float(SUBLANES)), axis=0, keepdims=True)
        return mx, idx

    g_log = jnp.where(row8 < N_GROUPS, lt[0:SUBLANES, :], neg_inf)
    g_max, g_sel = first_argmax(g_log)
    g_gate = 1.0 / jnp.sum(jnp.exp(g_log - g_max), axis=0, keepdims=True)
    e_log = jnp.zeros((EPG, tm), f32)
    for g in range(N_GROUPS):
        rows = lt[EXPERT_ROW0 + g * EPG:EXPERT_ROW0 + (g + 1) * EPG, :]
        e_log = jnp.where(g_sel == float(g), rows, e_log)
    v1, i1 = first_argmax(e_log)
    v2, i2 = first_argmax(jnp.where(row8 == i1, neg_inf, e_log))
    tt = jnp.exp(v2 - v1)
    w1 = g_gate / (1.0 + tt)
    w2 = g_gate * tt / (1.0 + tt)
    e1 = g_sel * float(EPG) + i1
    e2 = g_sel * float(EPG) + i2

    row32 = lax.broadcasted_iota(jnp.int32, (N_EXPERTS, tm), 0).astype(f32)
    oh1 = row32 == e1
    oh2 = row32 == e2
    c = jnp.where(oh1 | oh2, 1.0, 0.0).astype(bf16)
    tok_r = lax.broadcasted_iota(jnp.int32, (tm, tm), 0)
    tok_c = lax.broadcasted_iota(jnp.int32, (tm, tm), 1)
    same_tile = (tok_r // t) == (tok_c // t)
    rank = _dot(c, jnp.where(same_tile & (tok_r < tok_c), 1.0, 0.0).astype(bf16))
    cnt_b = _dot(c, jnp.where(same_tile, 1.0, 0.0).astype(bf16))
    ex_r = lax.broadcasted_iota(jnp.int32, (N_EXPERTS, N_EXPERTS), 0)
    ex_c = lax.broadcasted_iota(jnp.int32, (N_EXPERTS, N_EXPERTS), 1)
    lower = jnp.where(ex_c < ex_r, 1.0, 0.0).astype(bf16)
    start_b = _dot(lower, cnt_b.astype(bf16))
    pos_e = start_b + rank
    p1 = jnp.sum(jnp.where(oh1, pos_e, 0.0), axis=0, keepdims=True)
    p2 = jnp.sum(jnp.where(oh2, pos_e, 0.0), axis=0, keepdims=True)

    srow = lax.broadcasted_iota(jnp.int32, (2 * t, t), 0).astype(f32)
    for s in range(tm // t):
        cols = slice(s * t, (s + 1) * t)
        perm = jnp.where((srow == p1[:, cols]) | (srow == p2[:, cols]), 1.0, 0.0).astype(bf16)
        xs = _dot(perm, hn_hi[cols, :])
        xs_ref[s * 2 * t:(s + 1) * 2 * t] = _pack_rows(xs)
        cnt_ref[s * N_EXPERTS:(s + 1) * N_EXPERTS, :] = cnt_b[:, s * t:s * t + LANES]

    meta = jnp.concatenate([p1, p2, w1, w2, jnp.zeros((LANES - 4, tm), f32)], axis=0)
    meta_ref[...] = meta.T


def _outproj(x2, attn_o, conv_o, wo, g2, wr2):
    n = x2.shape[0]
    tm = TM_OUT
    nsub = tm // T_SORT
    const = lambda i: (0, 0)

    return pl.pallas_call(
        _outproj_kernel,
        grid=(n // tm,),
        in_specs=[
            pl.BlockSpec((tm, D_MODEL), lambda i: (i, 0)),
            pl.BlockSpec((tm, ATTN_WIDTH), lambda i: (i, 0)),
            pl.BlockSpec((tm, CONV_WIDTH), lambda i: (i, 0)),
            pl.BlockSpec((D_MODEL, D_MODEL), const),
            pl.BlockSpec((1, D_MODEL), const),
            pl.BlockSpec((D_MODEL, 2 * ROUTER_ROWS), const),
        ],
        out_specs=[
            pl.BlockSpec((tm, D_MODEL), lambda i: (i, 0)),
            pl.BlockSpec((2 * tm, ROW_CHUNKS, LANES), lambda i: (i, 0, 0)),
            pl.BlockSpec((tm, LANES), lambda i: (i, 0)),
            pl.BlockSpec((nsub * N_EXPERTS, LANES), lambda i: (i, 0)),
        ],
        out_shape=[
            jax.ShapeDtypeStruct((n, D_MODEL), jnp.float32),
            jax.ShapeDtypeStruct((2 * n, ROW_CHUNKS, LANES), PACKED_DTYPE),
            jax.ShapeDtypeStruct((n, LANES), jnp.float32),
            jax.ShapeDtypeStruct((n // T_SORT * N_EXPERTS, LANES), jnp.float32),
        ],
        compiler_params=pltpu.CompilerParams(
            dimension_semantics=("arbitrary",), vmem_limit_bytes=VMEM_LIMIT),
        name="outproj",
    )(x2, attn_o, conv_o, wo, g2, wr2)


def _pack_rows(x):
    r = x.shape[0]
    w = pltpu.pack_elementwise([x[:, :ROW_WORDS], x[:, ROW_WORDS:]], packed_dtype=jnp.bfloat16)
    return pltpu.bitcast(w, PACKED_DTYPE).reshape(r, ROW_CHUNKS, LANES)


def _packed_zero_rows(r):
    z = jnp.zeros((r, ROW_CHUNKS, LANES), jnp.float32)
    w = pltpu.pack_elementwise([z, z], packed_dtype=jnp.bfloat16)
    return pltpu.bitcast(w, PACKED_DTYPE)


def _unpack_rows(u):
    r = u.shape[0]
    w = u.reshape(r, ROW_WORDS)
    lo = pltpu.unpack_elementwise(w, index=0, packed_dtype=jnp.bfloat16, unpacked_dtype=jnp.float32)
    hi = pltpu.unpack_elementwise(w, index=1, packed_dtype=jnp.bfloat16, unpacked_dtype=jnp.float32)
    return jnp.concatenate([lo, hi], axis=1).astype(jnp.bfloat16)


def _expert_kernel(nblk_ref, blk0_ref, be_ref, nused_ref, r0_ref, nvalid_ref, tlo_ref, thi_ref,
                   cnt_ref, src_ref, rbase_ref,
                   xs_hbm, wg_ref, wu_ref, wd_ref, y_hbm,
                   xbuf, ybuf, wg_s, wu_s, wd_s, sem, ysem):
    e = pl.program_id(0)
    nused = nused_ref[0]
    n_blocks = be_ref.shape[0]
    n_tiles = cnt_ref.shape[0] // N_EXPERTS

    def start_run(blk, slot, t, live, priority=0):
        r0 = r0_ref[blk]
        k = t * N_EXPERTS + be_ref[blk]
        lo = jnp.maximum(rbase_ref[k], r0)
        hi = jnp.minimum(rbase_ref[k] + cnt_ref[k], r0 + MOE_BLOCK)
        rows = jnp.where(live, hi - lo, 0)

        @pl.when(rows > 0)
        def _():
            pltpu.make_async_copy(xs_hbm.at[pl.ds(src_ref[k] + lo - rbase_ref[k], rows)],
                                  xbuf.at[slot, pl.ds(lo - r0, rows)], sem.at[slot]).start(priority=priority)

    def gather_rolled(blk, slot, t_from):
        def run(t, carry):
            start_run(blk, slot, t, True)
            return carry
        lax.fori_loop(t_from, thi_ref[blk], run, 0)

    def gather_unrolled(blk, slot, live, part):
        blk = jnp.minimum(blk, n_blocks - 1)
        per = GATHER_UNROLL // GATHER_PARTS
        for r in range(part * per, (part + 1) * per):
            t = tlo_ref[blk] + r
            start_run(blk, slot, jnp.minimum(t, n_tiles - 1), live & (t < thi_ref[blk]), priority=r % 2)
        if part == GATHER_PARTS - 1:
            @pl.when(live & (tlo_ref[blk] + GATHER_UNROLL < thi_ref[blk]))
            def _():
                gather_rolled(blk, slot, tlo_ref[blk] + GATHER_UNROLL)

    def y_copy(blk, slot):
        return pltpu.make_async_copy(ybuf.at[slot], y_hbm.at[pl.ds(blk * MOE_BLOCK, MOE_BLOCK)], ysem.at[slot])

    @pl.when(e == 0)
    def _():
        for slot in range(N_XBUF):
            xbuf[slot] = _packed_zero_rows(MOE_BLOCK)
        gather_rolled(0, 0, tlo_ref[0])
        for blk in range(1, N_XBUF - 1):
            @pl.when(nused > blk)
            def _():
                gather_rolled(blk, blk, tlo_ref[blk])
        for slot in range(2):
            ybuf[slot] = _packed_zero_rows(MOE_BLOCK)
            y_copy(slot, slot).start()

    @pl.when(nblk_ref[e] > 0)
    def _():
        wg_s[...] = wg_ref[0].astype(jnp.bfloat16)
        wu_s[...] = wu_ref[0].astype(jnp.bfloat16)
        wd_s[...] = wd_ref[0].astype(jnp.bfloat16)

        def block(c, carry):
            b = blk0_ref[e] + c
            slot = b % N_XBUF
            yslot = b % 2
            nv = nvalid_ref[b]
            y_copy(b, yslot).wait()
            pltpu.make_async_copy(xs_hbm.at[pl.ds(0, nv)], xbuf.at[slot, pl.ds(0, nv)], sem.at[slot]).wait()

            ahead = b + N_XBUF - 1
            x = _unpack_rows(xbuf[slot])
            g = _dot(x, wg_s[...])
            gather_unrolled(ahead, ahead % N_XBUF, ahead < nused, 0)
            u = _dot(x, wu_s[...])
            act = (g / (1.0 + jnp.exp(-g)) * u).astype(jnp.bfloat16)
            gather_unrolled(ahead, ahead % N_XBUF, ahead < nused, 1)
            y = _dot(act, wd_s[...])
            ybuf[yslot] = _pack_rows(y)
            y_copy(b, yslot).start()
            gather_unrolled(ahead, ahead % N_XBUF, ahead < nused, 2)
            return carry
        lax.fori_loop(0, nblk_ref[e], block, 0)

    @pl.when(e == N_EXPERTS - 1)
    def _():
        for slot in range(2):
            y_copy(slot, slot).wait()
        ybuf[0] = _packed_zero_rows(MOE_BLOCK)

        def zero_block(b, carry):
            y_copy(b, 0).start()
            y_copy(b, 0).wait()
            return carry
        lax.fori_loop(nused, n_blocks, zero_block, 0)


def _experts(tabs, xs, w_gate, w_up, w_down):
    n_slots = tabs["n_slots"]
    bf16 = jnp.bfloat16
    w_map = lambda e, *_: (e, 0, 0)
    return pl.pallas_call(
        _expert_kernel,
        grid_spec=pltpu.PrefetchScalarGridSpec(
            num_scalar_prefetch=11,
            grid=(N_EXPERTS,),
            in_specs=[
                pl.BlockSpec(memory_space=pl.ANY),
                pl.BlockSpec((1, D_MODEL, D_FF), w_map),
                pl.BlockSpec((1, D_MODEL, D_FF), w_map),
                pl.BlockSpec((1, D_FF, D_MODEL), w_map),
            ],
            out_specs=pl.BlockSpec(memory_space=pl.ANY),
            scratch_shapes=[pltpu.VMEM((N_XBUF, MOE_BLOCK, ROW_CHUNKS, LANES), PACKED_DTYPE),
                            pltpu.VMEM((2, MOE_BLOCK, ROW_CHUNKS, LANES), PACKED_DTYPE),
                            pltpu.VMEM((D_MODEL, D_FF), bf16),
                            pltpu.VMEM((D_MODEL, D_FF), bf16),
                            pltpu.VMEM((D_FF, D_MODEL), bf16),
                            pltpu.SemaphoreType.DMA((N_XBUF,)),
                            pltpu.SemaphoreType.DMA((2,))],
        ),
        out_shape=jax.ShapeDtypeStruct((n_slots, ROW_CHUNKS, LANES), PACKED_DTYPE),
        compiler_params=pltpu.CompilerParams(
            dimension_semantics=("arbitrary",), vmem_limit_bytes=VMEM_LIMIT),
        name="experts",
    )(tabs["nblk"], tabs["blk0"], tabs["block_e"], tabs["nused"], tabs["r0"], tabs["nvalid"], tabs["tlo"],
      tabs["thi"], tabs["cnt"], tabs["src"], tabs["rbase"], xs, w_gate, w_up, w_down)


def _combine_kernel(cnt_ref, loc_ref, dst_ref, h_ref, meta_ref, yg_hbm, o_ref, ybuf, sem, *, n_steps, nsub):
    i = pl.program_id(0)
    t = T_SORT
    f32, bf16 = jnp.float32, jnp.bfloat16

    slot_in = i % 2
    tile0 = jnp.minimum(i, n_steps - 1) * nsub
    for s in range(nsub):
        for e in range(N_EXPERTS):
            k = (tile0 + s) * N_EXPERTS + e
            rows = jnp.where(i < n_steps, cnt_ref[k], 0)

            @pl.when(rows > 0)
            def _():
                pltpu.make_async_copy(yg_hbm.at[pl.ds(dst_ref[k], rows)],
                                      ybuf.at[slot_in, pl.ds(s * 2 * t + loc_ref[k], rows)],
                                      sem.at[slot_in]).start(priority=e % 2)

    @pl.when(i > 0)
    def _():
        slot = (i - 1) % 2
        pltpu.make_async_copy(yg_hbm.at[pl.ds(0, nsub * 2 * t)], ybuf.at[slot], sem.at[slot]).wait()
        lane = lax.broadcasted_iota(jnp.int32, (t, 2 * t), 1).astype(f32)
        for s in range(nsub):
            y = _unpack_rows(ybuf[slot, s * 2 * t:(s + 1) * 2 * t])
            meta = meta_ref[s * t:(s + 1) * t, :]
            pick1 = jnp.where(lane == meta[:, 0:1], 1.0, 0.0).astype(bf16)
            pick2 = jnp.where(lane == meta[:, 1:2], 1.0, 0.0).astype(bf16)
            y1 = _dot(pick1, y)
            y2 = _dot(pick2, y)
            o_ref[s * t:(s + 1) * t, :] = h_ref[s * t:(s + 1) * t, :] + (meta[:, 2:3] * y1 + meta[:, 3:4] * y2)


def _combine(cnt, loc, dst, h, meta, yg):
    n = h.shape[0]
    tm = TM_OUT
    nsub = tm // T_SORT
    n_steps = n // tm
    prev = lambda i, *_: (jnp.maximum(i - 1, 0), 0)
    return pl.pallas_call(
        functools.partial(_combine_kernel, n_steps=n_steps, nsub=nsub),
        grid_spec=pltpu.PrefetchScalarGridSpec(
            num_scalar_prefetch=3,
            grid=(n_steps + 1,),
            in_specs=[
                pl.BlockSpec((tm, D_MODEL), prev),
                pl.BlockSpec((tm, LANES), prev),
                pl.BlockSpec(memory_space=pl.ANY),
            ],
            out_specs=pl.BlockSpec((tm, D_MODEL), prev),
            scratch_shapes=[pltpu.VMEM((2, nsub * 2 * T_SORT, ROW_CHUNKS, LANES), PACKED_DTYPE),
                            pltpu.SemaphoreType.DMA((2,))],
        ),
        out_shape=jax.ShapeDtypeStruct((n, D_MODEL), jnp.float32),
        compiler_params=pltpu.CompilerParams(
            dimension_semantics=("arbitrary",), vmem_limit_bytes=VMEM_LIMIT),
        name="combine",
    )(cnt, loc, dst, h, meta, yg)


def _routing_tables(cnt_out, n_tiles, n_tok):
    i32 = jnp.int32
    cnt = cnt_out.reshape(n_tiles, N_EXPERTS, LANES)[:, :, 0].astype(i32)
    count = jnp.sum(cnt, axis=0)
    padded = ((count + MOE_BLOCK - 1) // MOE_BLOCK) * MOE_BLOCK
    pad_end = jnp.cumsum(padded)
    pad_start = pad_end - padded
    run_end = jnp.cumsum(cnt, axis=0)
    rbase = run_end - cnt
    dst = pad_start[None, :] + rbase
    loc = jnp.cumsum(cnt, axis=1) - cnt
    src = loc + (jnp.arange(n_tiles, dtype=i32) * (2 * T_SORT))[:, None]
    n_blocks = (2 * n_tok) // MOE_BLOCK + N_EXPERTS
    block_start = jnp.arange(n_blocks, dtype=i32) * MOE_BLOCK
    block_e = jnp.minimum(jnp.sum(pad_end[None, :] <= block_start[:, None], axis=1), N_EXPERTS - 1).astype(i32)
    nused = (pad_end[-1] // MOE_BLOCK).astype(i32).reshape(1)
    ex = jnp.arange(N_EXPERTS, dtype=i32)
    sel = (block_e[None, :] == ex[:, None]).astype(i32)
    pick = lambda per_expert: jnp.sum(per_expert[..., :, None] * sel, axis=-2)
    r0 = block_start - pick(pad_start)
    nvalid = jnp.clip(pick(count) - r0, 0, MOE_BLOCK)
    tlo = jnp.sum(pick(run_end) <= r0[None, :], axis=0)
    thi = jnp.sum(pick(rbase) < (r0 + MOE_BLOCK)[None, :], axis=0)
    flat = lambda a: a.reshape(-1).astype(i32)
    return dict(cnt=flat(cnt), src=flat(src), dst=flat(dst), loc=flat(loc), rbase=flat(rbase),
                block_e=block_e, nused=nused, r0=flat(r0), nvalid=flat(nvalid), tlo=flat(tlo), thi=flat(thi),
                nblk=flat(padded // MOE_BLOCK), blk0=flat(pad_start // MOE_BLOCK), n_slots=n_blocks * MOE_BLOCK)


def _stage1(x, positions, attn_norm_gain, w_in, q_norm_gain, k_norm_gain, conv_w, conv_out_gain):
    bsz, seq, _ = x.shape
    n = bsz * seq
    f32, bf16 = jnp.float32, jnp.bfloat16
    w = w_in[0]
    wb = w.astype(bf16)
    wvt = w[:, 2 * ATTN_WIDTH:3 * ATTN_WIDTH].T.astype(bf16)
    scale = DK ** -0.5 * LOG2E
    gqk = jnp.concatenate([jnp.tile(q_norm_gain[0].astype(f32), 2 * N_HEADS) * scale,
                           jnp.tile(k_norm_gain[0].astype(f32), 2 * N_HEADS)]).reshape(1, -1)
    freqs = (ROPE_THETA ** (-jnp.arange(0, ROT_DIM, 2, dtype=f32) / ROT_DIM)).reshape(SUBLANES, 1)
    return _inproj(x.reshape(n, D_MODEL), positions.reshape(1, n),
                   attn_norm_gain[0].reshape(1, -1).astype(f32), wb, wvt, wb, gqk, freqs,
                   conv_w[0].astype(f32), conv_out_gain[0].reshape(1, -1).astype(f32), seq)


def kernel(x, positions, attn_norm_gain, w_in, q_norm_gain, k_norm_gain, lambda_q1, lambda_k1, lambda_q2, lambda_k2, subln_gain, conv_w, conv_out_gain, w_out, ffn_norm_gain, w_group_router, w_expert_router, w_gate, w_up, w_down):
    bsz, seq, _ = x.shape
    n = bsz * seq
    f32, bf16 = jnp.float32, jnp.bfloat16
    assert TM_OUT % T_SORT == 0 and seq % TM_IN == 0 and seq % TQ == 0 and TQ % KV_FULL == 0
    q, k, vt, conv_o = _stage1(x, positions, attn_norm_gain, w_in, q_norm_gain, k_norm_gain,
                                conv_w, conv_out_gain)
    attn_o = _attention(q, k, vt,
                        lambda_q1[0].reshape(1, -1).astype(f32), lambda_k1[0].reshape(1, -1).astype(f32),
                        lambda_q2[0].reshape(1, -1).astype(f32), lambda_k2[0].reshape(1, -1).astype(f32),
                        subln_gain[0].reshape(-1, 1).astype(f32), bsz, seq)

    wr = jnp.concatenate([
        w_group_router[0].astype(f32), jnp.zeros((D_MODEL, EXPERT_ROW0 - N_GROUPS), f32),
        jnp.transpose(w_expert_router[0].astype(f32), (1, 0, 2)).reshape(D_MODEL, N_EXPERTS),
        jnp.zeros((D_MODEL, ROUTER_ROWS - EXPERT_ROW0 - N_EXPERTS), f32)], axis=1)
    wrh = wr.astype(bf16)
    wr2 = jnp.concatenate([wrh, (wr - wrh.astype(f32)).astype(bf16)], axis=1)
    h, xs, meta, cnt_out = _outproj(x.reshape(n, D_MODEL), attn_o, conv_o, w_out[0].astype(bf16),
                                    ffn_norm_gain[0].reshape(1, -1).astype(f32), wr2)

    tabs = _routing_tables(cnt_out, n // T_SORT, n)
    yg = _experts(tabs, xs, w_gate[0], w_up[0], w_down[0])
    out = _combine(tabs["cnt"], tabs["loc"], tabs["dst"], h, meta, yg)
    return out.reshape(x.shape)
```

```python
import functools
import math

import jax
import jax.numpy as jnp
from jax import lax
from jax.experimental import pallas as pl
from jax.experimental.pallas import tpu as pltpu

D_MODEL = 1024
N_HEADS = 4
DK = 64
DV = 128
ROT_DIM = 16
ROPE_THETA = 500000.0
ATTN_WIDTH = N_HEADS * DV
CONV_WIDTH = 512
NORM_EPS = 1e-6
LOG2E = 1.4426950408889634
LAMBDA_INIT = 0.8 - 0.6 * math.exp(-0.3 * 0)
N_GROUPS = 4
EPG = 8
N_EXPERTS = N_GROUPS * EPG
D_FF = 512
MOE_BLOCK = 256

LANES = 128
SUBLANES = 8
ROW_WORDS = D_MODEL // 2
ROW_CHUNKS = ROW_WORDS // LANES
PACKED_DTYPE = jnp.uint32

TM_IN = 1024
TQ = 512
HQ = TQ // 2
KV_FULL = 512
SUM_ROWS = 16
T_SORT = 256
TM_OUT = 512
N_XBUF = 3
GATHER_UNROLL = 24
GATHER_PARTS = 3
VMEM_LIMIT = 48 * 1024 * 1024


def _nt_dot(a, b):
    return lax.dot_general(a, b, (((1,), (1,)), ((), ())), preferred_element_type=jnp.float32)


def _dot(a, b):
    return jnp.dot(a, b, preferred_element_type=jnp.float32)


def _split3(x):
    h = x.astype(jnp.bfloat16)
    r = x - h.astype(jnp.float32)
    m = r.astype(jnp.bfloat16)
    l = (r - m.astype(jnp.float32)).astype(jnp.bfloat16)
    return h, m, l


def _split2(x):
    h = x.astype(jnp.bfloat16)
    l = (x - h.astype(jnp.float32)).astype(jnp.bfloat16)
    return h, l


def _inproj_kernel(x_ref, pos_ref, g1_ref, wqk_ref, wvt_ref, wc_ref, gqk_ref, freq_ref,
                   cw_ref, cg_ref,
                   q_ref, k_ref, vt_ref, conv_ref,
                   carry_ref, *, tiles_per_seq):
    tm = x_ref.shape[0]
    i = pl.program_id(0)

    x = x_ref[...]
    ms = jnp.mean(x * x, axis=-1, keepdims=True)
    hn = (x * lax.rsqrt(ms + NORM_EPS) * g1_ref[...]).astype(jnp.bfloat16)

    pos = pos_ref[...].astype(jnp.float32)
    ang = freq_ref[...] * pos
    lane_r = lax.broadcasted_iota(jnp.int32, (LANES, SUBLANES), 0)
    f_c = lax.broadcasted_iota(jnp.int32, (LANES, SUBLANES), 1)
    in_rot = (lane_r % DK) < ROT_DIM
    expand = jnp.where(in_rot & ((lane_r % (ROT_DIM // 2)) == f_c), 1.0, 0.0).astype(jnp.bfloat16)

    def to_rows(t):
        h, m, l = _split3(t)
        r = _dot(expand, h) + _dot(expand, m) + _dot(expand, l)
        return r.T

    cos_r = to_rows(jnp.cos(ang))
    sin_r = to_rows(jnp.sin(ang))
    lane = lax.broadcasted_iota(jnp.int32, (tm, LANES), 1)
    d = lane % DK
    cos_r = jnp.where(d < ROT_DIM, cos_r, 1.0)
    sin_lo = jnp.where(d < ROT_DIM // 2, -sin_r, 0.0)
    sin_hi = jnp.where((d >= ROT_DIM // 2) & (d < ROT_DIM), sin_r, 0.0)

    qk = _dot(hn, wqk_ref[...])
    seg_r = lax.broadcasted_iota(jnp.int32, (2 * LANES, 2 * LANES), 0) // DK
    seg_c = lax.broadcasted_iota(jnp.int32, (2 * LANES, 2 * LANES), 1) // DK
    seg_mean = jnp.where(seg_r == seg_c, 1.0 / DK, 0.0).astype(jnp.bfloat16)
    half = ROT_DIM // 2
    for c2 in range(4):
        blk = qk[:, c2 * 256:(c2 + 1) * 256]
        msq = _dot((blk * blk).astype(jnp.bfloat16), seg_mean)
        y2 = blk * lax.rsqrt(msq + NORM_EPS) * gqk_ref[:, c2 * 256:(c2 + 1) * 256]
        for c1 in range(2):
            c = c2 * 2 + c1
            y = y2[:, c1 * LANES:(c1 + 1) * LANES]
            rot = (y * cos_r
                   + pltpu.roll(y, LANES - half, 1) * sin_lo
                   + pltpu.roll(y, half, 1) * sin_hi)
            if c < N_HEADS:
                q_ref[:, c * LANES:(c + 1) * LANES] = rot.astype(jnp.bfloat16)
            else:
                h = c - N_HEADS
                k_ref[:, h * LANES:(h + 1) * LANES] = rot.astype(jnp.bfloat16)

    vt_ref[...] = _nt_dot(wvt_ref[...], hn).astype(jnp.bfloat16)

    cp = _dot(hn, wc_ref[...])
    cb = cp[:, :CONV_WIDTH]
    y = cp[:, CONV_WIDTH:2 * CONV_WIDTH] * cp[:, 2 * CONV_WIDTH:]

    @pl.when(i % tiles_per_seq == 0)
    def _():
        carry_ref[...] = jnp.zeros_like(carry_ref)

    prev = carry_ref[...]
    row = lax.broadcasted_iota(jnp.int32, (tm, CONV_WIDTH), 0)
    p1 = prev[SUBLANES - 1:SUBLANES, :]
    p2 = prev[SUBLANES - 2:SUBLANES - 1, :]
    y1 = jnp.where(row == 0, p1, pltpu.roll(y, 1, 0))
    y2 = jnp.where(row == 0, p2, jnp.where(row == 1, p1, pltpu.roll(y, 2, 0)))
    carry_ref[...] = y[tm - SUBLANES:, :]
    z = cw_ref[0:1, :] * y2 + cw_ref[1:2, :] * y1 + cw_ref[2:3, :] * y
    co = cb * z
    cms = jnp.mean(co * co, axis=-1, keepdims=True)
    conv_ref[...] = (co * lax.rsqrt(cms + NORM_EPS) * cg_ref[...]).astype(jnp.bfloat16)


def _inproj(x2, pos_row, g1, wqk, wvt, wc, gqk, freqs, cw, cg, seq):
    n = x2.shape[0]
    tm = TM_IN
    grid = (n // tm,)
    const = lambda i: (0, 0)
    return pl.pallas_call(
        functools.partial(_inproj_kernel, tiles_per_seq=seq // tm),
        grid=grid,
        in_specs=[
            pl.BlockSpec((tm, D_MODEL), lambda i: (i, 0)),
            pl.BlockSpec((1, tm), lambda i: (0, i)),
            pl.BlockSpec((1, D_MODEL), const),
            pl.BlockSpec((D_MODEL, 2 * ATTN_WIDTH), const),
            pl.BlockSpec((ATTN_WIDTH, D_MODEL), const),
            pl.BlockSpec((D_MODEL, 3 * CONV_WIDTH), lambda i: (0, 1)),
            pl.BlockSpec((1, 1024), const),
            pl.BlockSpec((SUBLANES, 1), const),
            pl.BlockSpec((3, CONV_WIDTH), const),
            pl.BlockSpec((1, CONV_WIDTH), const),
        ],
        out_specs=[
            pl.BlockSpec((tm, ATTN_WIDTH), lambda i: (i, 0)),
            pl.BlockSpec((tm, ATTN_WIDTH), lambda i: (i, 0)),
            pl.BlockSpec((ATTN_WIDTH, tm), lambda i: (0, i)),
            pl.BlockSpec((tm, CONV_WIDTH), lambda i: (i, 0)),
        ],
        out_shape=[
            jax.ShapeDtypeStruct((n, ATTN_WIDTH), jnp.bfloat16),
            jax.ShapeDtypeStruct((n, ATTN_WIDTH), jnp.bfloat16),
            jax.ShapeDtypeStruct((ATTN_WIDTH, n), jnp.bfloat16),
            jax.ShapeDtypeStruct((n, CONV_WIDTH), jnp.bfloat16),
        ],
        scratch_shapes=[pltpu.VMEM((SUBLANES, CONV_WIDTH), jnp.float32)],
        compiler_params=pltpu.CompilerParams(
            dimension_semantics=("arbitrary",), vmem_limit_bytes=VMEM_LIMIT),
        name="inproj",
    )(x2, pos_row, g1, wqk, wvt, wc, gqk, freqs, cw, cg)


def _attn_kernel(q_ref, k_ref, vt_ref, lq1_ref, lk1_ref, lq2_ref, lk2_ref, sg_ref, o_ref, acc_ref, *, nq):
    lam = (jnp.exp(jnp.sum(lq1_ref[...] * lk1_ref[...], axis=-1, keepdims=True))
           - jnp.exp(jnp.sum(lq2_ref[...] * lk2_ref[...], axis=-1, keepdims=True))
           + LAMBDA_INIT)
    tri = (lax.broadcasted_iota(jnp.int32, (HQ, HQ), 0)
           <= lax.broadcasted_iota(jnp.int32, (HQ, HQ), 1))
    row_d = lax.broadcasted_iota(jnp.int32, (LANES, HQ), 0)
    ones_rows = jnp.ones((SUM_ROWS, KV_FULL), jnp.bfloat16)

    def masked(s, n_tri):
        blocks = [jnp.where(tri, s[:, c * HQ:(c + 1) * HQ], -jnp.inf) for c in range(n_tri)]
        return jnp.concatenate(blocks + [s[:, n_tri * HQ:]], axis=1) if n_tri * HQ < s.shape[1] else \
            jnp.concatenate(blocks, axis=1)

    todo = {}
    for t in range(nq):
        lo = t * TQ
        todo[t] = ([(lo, lo + HQ, "mixed"), (lo + HQ, lo + TQ, "upper")]
                   + [(k0, k0 + KV_FULL, "full") for k0 in range(0, lo, KV_FULL)])
    tasks = []
    while any(todo.values()):
        for t in reversed(range(nq)):
            if todo[t]:
                tasks.append((t,) + todo[t].pop(0))
    last_task = {t: max(i for i, tk in enumerate(tasks) if tk[0] == t) for t in range(nq)}
    qzt, m_run = {}, {}

    def q_transposed(t):
        parts = []
        for half in range(2):
            r0 = t * TQ + half * HQ
            qt = q_ref[r0:r0 + HQ, :].astype(jnp.float32).T
            parts.append(jnp.where(row_d < DK, qt, 0.0))
            parts.append(jnp.where(row_d >= DK, qt, 0.0))
        return jnp.concatenate(parts, axis=1).astype(jnp.bfloat16)

    def scores(task):
        t, k0, k1, kind = task
        if t not in qzt:
            qzt[t] = q_transposed(t)
        kb = k_ref[k0:k1, :]
        if kind == "upper":
            return masked(_dot(kb, qzt[t][:, 2 * HQ:]), 2)
        s = _dot(kb, qzt[t])
        return masked(s, 2) if kind == "mixed" else s

    def softmax(task, s):
        t, _, _, kind = task
        mx = jnp.max(s, axis=0, keepdims=True)
        if kind == "mixed":
            m_run[t] = mx
            return jnp.exp2(s - mx).astype(jnp.bfloat16), None
        m_old = m_run[t][:, 2 * HQ:] if kind == "upper" else m_run[t]
        m_new = jnp.maximum(m_old, mx)
        alpha = jnp.exp2(m_old - m_new)
        p = jnp.exp2(s - m_new)
        m_run[t] = jnp.concatenate([m_run[t][:, :2 * HQ], m_new], axis=1) if kind == "upper" else m_new
        return p.astype(jnp.bfloat16), alpha

    def accumulate(task, p, alpha):
        t, k0, k1, kind = task
        vta = jnp.concatenate([vt_ref[:, k0:k1], ones_rows[:, :k1 - k0]], axis=0)
        pv = _dot(vta, p)
        if kind == "mixed":
            acc_ref[t] = pv
        elif kind == "upper":
            acc_ref[t, :, 2 * HQ:] = alpha * acc_ref[t, :, 2 * HQ:] + pv
        else:
            acc_ref[t] = alpha * acc_ref[t] + pv

    def finish(t):
        o_all = acc_ref[t, 0:DV, :] * (1.0 / acc_ref[t, DV:DV + 1, :])
        for half in range(2):
            o = (o_all[:, (2 * half) * HQ:(2 * half + 1) * HQ]
                 - lam * o_all[:, (2 * half + 1) * HQ:(2 * half + 2) * HQ])
            ms = jnp.mean(o * o, axis=0, keepdims=True)
            on = o * lax.rsqrt(ms + NORM_EPS) * sg_ref[...] * (1.0 - LAMBDA_INIT)
            r0 = t * TQ + half * HQ
            o_ref[r0:r0 + HQ, :] = on.T.astype(jnp.bfloat16)

    n = len(tasks)
    s_prev, p_prev = None, None
    for step in range(n + 2):
        if step >= 2:
            accumulate(tasks[step - 2], *p_prev)
            if last_task[tasks[step - 2][0]] == step - 2:
                finish(tasks[step - 2][0])
        if 1 <= step <= n:
            p_prev = softmax(tasks[step - 1], s_prev)
        if step < n:
            s_prev = scores(tasks[step])


def _attention(q, k, vt, lq1, lk1, lq2, lk2, sg_col, bsz, seq):
    n = q.shape[0]
    nq = seq // TQ
    vec = lambda b, h: (0, 0)
    return pl.pallas_call(
        functools.partial(_attn_kernel, nq=nq),
        grid=(bsz, N_HEADS),
        in_specs=[
            pl.BlockSpec((seq, DV), lambda b, h: (b, h)),
            pl.BlockSpec((seq, DV), lambda b, h: (b, h)),
            pl.BlockSpec((DV, seq), lambda b, h: (h, b)),
            pl.BlockSpec((1, DK), vec), pl.BlockSpec((1, DK), vec),
            pl.BlockSpec((1, DK), vec), pl.BlockSpec((1, DK), vec),
            pl.BlockSpec((DV, 1), vec),
        ],
        out_specs=pl.BlockSpec((seq, DV), lambda b, h: (b, h)),
        out_shape=jax.ShapeDtypeStruct((n, ATTN_WIDTH), jnp.bfloat16),
        scratch_shapes=[pltpu.VMEM((nq, DV + SUM_ROWS, 4 * HQ), jnp.float32)],
        compiler_params=pltpu.CompilerParams(
            dimension_semantics=("arbitrary", "arbitrary"), vmem_limit_bytes=VMEM_LIMIT),
        name="attn",
    )(q, k, vt, lq1, lk1, lq2, lk2, sg_col)


ROUTER_ROWS = 128
EXPERT_ROW0 = 32


def _outproj_kernel(x_ref, attn_ref, conv_ref, wo_ref, g2_ref, wr2_ref,
                    h_ref, xs_ref, meta_ref, cnt_ref):
    tm = x_ref.shape[0]
    t = T_SORT
    f32, bf16 = jnp.float32, jnp.bfloat16
    a = jnp.concatenate([attn_ref[...], conv_ref[...]], axis=1)
    h = x_ref[...] + _dot(a, wo_ref[...])
    h_ref[...] = h
    ms = jnp.mean(h * h, axis=-1, keepdims=True)
    hn = h * lax.rsqrt(ms + NORM_EPS) * g2_ref[...]
    hn_hi, hn_lo = _split2(hn)

    hh = _dot(hn_hi, wr2_ref[...])
    logits = hh[:, :ROUTER_ROWS] + hh[:, ROUTER_ROWS:] + _dot(hn_lo, wr2_ref[:, :ROUTER_ROWS])
    lt = logits.T
    row8 = lax.broadcasted_iota(jnp.int32, (SUBLANES, tm), 0).astype(f32)
    neg_inf = -jnp.inf

    def first_argmax(v):
        mx = jnp.max(v, axis=0, keepdims=True)
        idx = jnp.min(jnp.where(v == mx, row8, float(SUBLANES)), axis=0, keepdims=True)
        return mx, idx

    g_log = jnp.where(row8 < N_GROUPS, lt[0:SUBLANES, :], neg_inf)
    g_max, g_sel = first_argmax(g_log)
    g_gate = 1.0 / jnp.sum(jnp.exp(g_log - g_max), axis=0, keepdims=True)
    e_log = jnp.zeros((EPG, tm), f32)
    for g in range(N_GROUPS):
        rows = lt[EXPERT_ROW0 + g * EPG:EXPERT_ROW0 + (g + 1) * EPG, :]
        e_log = jnp.where(g_sel == float(g), rows, e_log)
    v1, i1 = first_argmax(e_log)
    v2, i2 = first_argmax(jnp.where(row8 == i1, neg_inf, e_log))
    tt = jnp.exp(v2 - v1)
    w1 = g_gate / (1.0 + tt)
    w2 = g_gate * tt / (1.0 + tt)
    e1 = g_sel * float(EPG) + i1
    e2 = g_sel * float(EPG) + i2

    row32 = lax.broadcasted_iota(jnp.int32, (N_EXPERTS, tm), 0).astype(f32)
    oh1 = row32 == e1
    oh2 = row32 == e2
    c = jnp.where(oh1 | oh2, 1.0, 0.0).astype(bf16)
    tok_r = lax.broadcasted_iota(jnp.int32, (tm, tm), 0)
    tok_c = lax.broadcasted_iota(jnp.int32, (tm, tm), 1)
    same_tile = (tok_r // t) == (tok_c // t)
    rank = _dot(c, jnp.where(same_tile & (tok_r < tok_c), 1.0, 0.0).astype(bf16))
    cnt_b = _dot(c, jnp.where(same_tile, 1.0, 0.0).astype(bf16))
    ex_r = lax.broadcasted_iota(jnp.int32, (N_EXPERTS, N_EXPERTS), 0)
    ex_c = lax.broadcasted_iota(jnp.int32, (N_EXPERTS, N_EXPERTS), 1)
    lower = jnp.where(ex_c < ex_r, 1.0, 0.0).astype(bf16)
    start_b = _dot(lower, cnt_b.astype(bf16))
    pos_e = start_b + rank
    p1 = jnp.sum(jnp.where(oh1, pos_e, 0.0), axis=0, keepdims=True)
    p2 = jnp.sum(jnp.where(oh2, pos_e, 0.0), axis=0, keepdims=True)

    srow = lax.broadcasted_iota(jnp.int32, (2 * t, t), 0).astype(f32)
    for s in range(tm // t):
        cols = slice(s * t, (s + 1) * t)
        perm = jnp.where((srow == p1[:, cols]) | (srow == p2[:, cols]), 1.0, 0.0).astype(bf16)
        xs = _dot(perm, hn_hi[cols, :])
        xs_ref[s * 2 * t:(s + 1) * 2 * t] = _pack_rows(xs)
        cnt_ref[s * N_EXPERTS:(s + 1) * N_EXPERTS, :] = cnt_b[:, s * t:s * t + LANES]

    meta = jnp.concatenate([p1, p2, w1, w2, jnp.zeros((LANES - 4, tm), f32)], axis=0)
    meta_ref[...] = meta.T


def _outproj(x2, attn_o, conv_o, wo, g2, wr2):
    n = x2.shape[0]
    tm = TM_OUT
    nsub = tm // T_SORT
    const = lambda i: (0, 0)

    return pl.pallas_call(
        _outproj_kernel,
        grid=(n // tm,),
        in_specs=[
            pl.BlockSpec((tm, D_MODEL), lambda i: (i, 0)),
            pl.BlockSpec((tm, ATTN_WIDTH), lambda i: (i, 0)),
            pl.BlockSpec((tm, CONV_WIDTH), lambda i: (i, 0)),
            pl.BlockSpec((D_MODEL, D_MODEL), const),
            pl.BlockSpec((1, D_MODEL), const),
            pl.BlockSpec((D_MODEL, 2 * ROUTER_ROWS), const),
        ],
        out_specs=[
            pl.BlockSpec((tm, D_MODEL), lambda i: (i, 0)),
            pl.BlockSpec((2 * tm, ROW_CHUNKS, LANES), lambda i: (i, 0, 0)),
            pl.BlockSpec((tm, LANES), lambda i: (i, 0)),
            pl.BlockSpec((nsub * N_EXPERTS, LANES), lambda i: (i, 0)),
        ],
        out_shape=[
            jax.ShapeDtypeStruct((n, D_MODEL), jnp.float32),
            jax.ShapeDtypeStruct((2 * n, ROW_CHUNKS, LANES), PACKED_DTYPE),
            jax.ShapeDtypeStruct((n, LANES), jnp.float32),
            jax.ShapeDtypeStruct((n // T_SORT * N_EXPERTS, LANES), jnp.float32),
        ],
        compiler_params=pltpu.CompilerParams(
            dimension_semantics=("arbitrary",), vmem_limit_bytes=VMEM_LIMIT),
        name="outproj",
    )(x2, attn_o, conv_o, wo, g2, wr2)


def _pack_rows(x):
    r = x.shape[0]
    w = pltpu.pack_elementwise([x[:, :ROW_WORDS], x[:, ROW_WORDS:]], packed_dtype=jnp.bfloat16)
    return pltpu.bitcast(w, PACKED_DTYPE).reshape(r, ROW_CHUNKS, LANES)


def _packed_zero_rows(r):
    z = jnp.zeros((r, ROW_CHUNKS, LANES), jnp.float32)
    w = pltpu.pack_elementwise([z, z], packed_dtype=jnp.bfloat16)
    return pltpu.bitcast(w, PACKED_DTYPE)


def _unpack_rows(u):
    r = u.shape[0]
    w = u.reshape(r, ROW_WORDS)
    lo = pltpu.unpack_elementwise(w, index=0, packed_dtype=jnp.bfloat16, unpacked_dtype=jnp.float32)
    hi = pltpu.unpack_elementwise(w, index=1, packed_dtype=jnp.bfloat16, unpacked_dtype=jnp.float32)
    return jnp.concatenate([lo, hi], axis=1).astype(jnp.bfloat16)


def _expert_kernel(nblk_ref, blk0_ref, be_ref, nused_ref, r0_ref, nvalid_ref, tlo_ref, thi_ref,
                   cnt_ref, src_ref, rbase_ref,
                   xs_hbm, wg_ref, wu_ref, wd_ref, y_hbm,
                   xbuf, ybuf, wg_s, wu_s, wd_s, sem, ysem):
    e = pl.program_id(0)
    nused = nused_ref[0]
    n_blocks = be_ref.shape[0]
    n_tiles = cnt_ref.shape[0] // N_EXPERTS

    def start_run(blk, slot, t, live, priority=0):
        r0 = r0_ref[blk]
        k = t * N_EXPERTS + be_ref[blk]
        lo = jnp.maximum(rbase_ref[k], r0)
        hi = jnp.minimum(rbase_ref[k] + cnt_ref[k], r0 + MOE_BLOCK)
        rows = jnp.where(live, hi - lo, 0)

        @pl.when(rows > 0)
        def _():
            pltpu.make_async_copy(xs_hbm.at[pl.ds(src_ref[k] + lo - rbase_ref[k], rows)],
                                  xbuf.at[slot, pl.ds(lo - r0, rows)], sem.at[slot]).start(priority=priority)

    def gather_rolled(blk, slot, t_from):
        def run(t, carry):
            start_run(blk, slot, t, True)
            return carry
        lax.fori_loop(t_from, thi_ref[blk], run, 0)

    def gather_unrolled(blk, slot, live, part):
        blk = jnp.minimum(blk, n_blocks - 1)
        per = GATHER_UNROLL // GATHER_PARTS
        for r in range(part * per, (part + 1) * per):
            t = tlo_ref[blk] + r
            start_run(blk, slot, jnp.minimum(t, n_tiles - 1), live & (t < thi_ref[blk]), priority=r % 2)
        if part == GATHER_PARTS - 1:
            @pl.when(live & (tlo_ref[blk] + GATHER_UNROLL < thi_ref[blk]))
            def _():
                gather_rolled(blk, slot, tlo_ref[blk] + GATHER_UNROLL)

    def y_copy(blk, slot):
        return pltpu.make_async_copy(ybuf.at[slot], y_hbm.at[pl.ds(blk * MOE_BLOCK, MOE_BLOCK)], ysem.at[slot])

    @pl.when(e == 0)
    def _():
        for slot in range(N_XBUF):
            xbuf[slot] = _packed_zero_rows(MOE_BLOCK)
        gather_rolled(0, 0, tlo_ref[0])
        for blk in range(1, N_XBUF - 1):
            @pl.when(nused > blk)
            def _():
                gather_rolled(blk, blk, tlo_ref[blk])
        for slot in range(2):
            ybuf[slot] = _packed_zero_rows(MOE_BLOCK)
            y_copy(slot, slot).start()

    @pl.when(nblk_ref[e] > 0)
    def _():
        def block(b, first):
            slot = b % N_XBUF
            yslot = b % 2
            nv = nvalid_ref[b]
            y_copy(b, yslot).wait()
            pltpu.make_async_copy(xs_hbm.at[pl.ds(0, nv)], xbuf.at[slot, pl.ds(0, nv)], sem.at[slot]).wait()

            ahead = b + N_XBUF - 1
            x = _unpack_rows(xbuf[slot])
            if first:
                wg_s[...] = wg_ref[0].astype(jnp.bfloat16)
            g = _dot(x, wg_s[...])
            gather_unrolled(ahead, ahead % N_XBUF, ahead < nused, 0)
            if first:
                wu_s[...] = wu_ref[0].astype(jnp.bfloat16)
            u = _dot(x, wu_s[...])
            act = (g / (1.0 + jnp.exp(-g)) * u).astype(jnp.bfloat16)
            gather_unrolled(ahead, ahead % N_XBUF, ahead < nused, 1)
            if first:
                wd_s[...] = wd_ref[0].astype(jnp.bfloat16)
            y = _dot(act, wd_s[...])
            ybuf[yslot] = _pack_rows(y)
            y_copy(b, yslot).start()
            gather_unrolled(ahead, ahead % N_XBUF, ahead < nused, 2)

        block(blk0_ref[e], True)

        def later_block(c, carry):
            block(blk0_ref[e] + c, False)
            return carry
        lax.fori_loop(1, nblk_ref[e], later_block, 0)

    @pl.when(e == N_EXPERTS - 1)
    def _():
        for slot in range(2):
            y_copy(slot, slot).wait()
        ybuf[0] = _packed_zero_rows(MOE_BLOCK)

        def zero_block(b, carry):
            y_copy(b, 0).start()
            y_copy(b, 0).wait()
            return carry
        lax.fori_loop(nused, n_blocks, zero_block, 0)


def _experts(tabs, xs, w_gate, w_up, w_down):
    n_slots = tabs["n_slots"]
    bf16 = jnp.bfloat16
    w_map = lambda e, *_: (e, 0, 0)
    return pl.pallas_call(
        _expert_kernel,
        grid_spec=pltpu.PrefetchScalarGridSpec(
            num_scalar_prefetch=11,
            grid=(N_EXPERTS,),
            in_specs=[
                pl.BlockSpec(memory_space=pl.ANY),
                pl.BlockSpec((1, D_MODEL, D_FF), w_map),
                pl.BlockSpec((1, D_MODEL, D_FF), w_map),
                pl.BlockSpec((1, D_FF, D_MODEL), w_map),
            ],
            out_specs=pl.BlockSpec(memory_space=pl.ANY),
            scratch_shapes=[pltpu.VMEM((N_XBUF, MOE_BLOCK, ROW_CHUNKS, LANES), PACKED_DTYPE),
                            pltpu.VMEM((2, MOE_BLOCK, ROW_CHUNKS, LANES), PACKED_DTYPE),
                            pltpu.VMEM((D_MODEL, D_FF), bf16),
                            pltpu.VMEM((D_MODEL, D_FF), bf16),
                            pltpu.VMEM((D_FF, D_MODEL), bf16),
                            pltpu.SemaphoreType.DMA((N_XBUF,)),
                            pltpu.SemaphoreType.DMA((2,))],
        ),
        out_shape=jax.ShapeDtypeStruct((n_slots, ROW_CHUNKS, LANES), PACKED_DTYPE),
        compiler_params=pltpu.CompilerParams(
            dimension_semantics=("arbitrary",), vmem_limit_bytes=VMEM_LIMIT),
        name="experts",
    )(tabs["nblk"], tabs["blk0"], tabs["block_e"], tabs["nused"], tabs["r0"], tabs["nvalid"], tabs["tlo"],
      tabs["thi"], tabs["cnt"], tabs["src"], tabs["rbase"], xs, w_gate, w_up, w_down)


def _combine_kernel(cnt_ref, loc_ref, dst_ref, h_ref, meta_ref, yg_hbm, o_ref, ybuf, sem, *, n_steps, nsub):
    i = pl.program_id(0)
    t = T_SORT
    f32, bf16 = jnp.float32, jnp.bfloat16

    slot_in = i % 2
    tile0 = jnp.minimum(i, n_steps - 1) * nsub
    for s in range(nsub):
        for e in range(N_EXPERTS):
            k = (tile0 + s) * N_EXPERTS + e
            rows = jnp.where(i < n_steps, cnt_ref[k], 0)

            @pl.when(rows > 0)
            def _():
                pltpu.make_async_copy(yg_hbm.at[pl.ds(dst_ref[k], rows)],
                                      ybuf.at[slot_in, pl.ds(s * 2 * t + loc_ref[k], rows)],
                                      sem.at[slot_in]).start(priority=e % 2)

    @pl.when(i > 0)
    def _():
        slot = (i - 1) % 2
        pltpu.make_async_copy(yg_hbm.at[pl.ds(0, nsub * 2 * t)], ybuf.at[slot], sem.at[slot]).wait()
        lane = lax.broadcasted_iota(jnp.int32, (t, 2 * t), 1).astype(f32)
        for s in range(nsub):
            y = _unpack_rows(ybuf[slot, s * 2 * t:(s + 1) * 2 * t])
            meta = meta_ref[s * t:(s + 1) * t, :]
            pick1 = jnp.where(lane == meta[:, 0:1], 1.0, 0.0).astype(bf16)
            pick2 = jnp.where(lane == meta[:, 1:2], 1.0, 0.0).astype(bf16)
            y12 = _dot(jnp.concatenate([pick1, pick2], axis=0), y)
            o_ref[s * t:(s + 1) * t, :] = h_ref[s * t:(s + 1) * t, :] + (meta[:, 2:3] * y12[:t]
                                                                         + meta[:, 3:4] * y12[t:])


def _combine(cnt, loc, dst, h, meta, yg):
    n = h.shape[0]
    tm = TM_OUT
    nsub = tm // T_SORT
    n_steps = n // tm
    prev = lambda i, *_: (jnp.maximum(i - 1, 0), 0)
    return pl.pallas_call(
        functools.partial(_combine_kernel, n_steps=n_steps, nsub=nsub),
        grid_spec=pltpu.PrefetchScalarGridSpec(
            num_scalar_prefetch=3,
            grid=(n_steps + 1,),
            in_specs=[
                pl.BlockSpec((tm, D_MODEL), prev),
                pl.BlockSpec((tm, LANES), prev),
                pl.BlockSpec(memory_space=pl.ANY),
            ],
            out_specs=pl.BlockSpec((tm, D_MODEL), prev),
            scratch_shapes=[pltpu.VMEM((2, nsub * 2 * T_SORT, ROW_CHUNKS, LANES), PACKED_DTYPE),
                            pltpu.SemaphoreType.DMA((2,))],
        ),
        out_shape=jax.ShapeDtypeStruct((n, D_MODEL), jnp.float32),
        compiler_params=pltpu.CompilerParams(
            dimension_semantics=("arbitrary",), vmem_limit_bytes=VMEM_LIMIT),
        name="combine",
    )(cnt, loc, dst, h, meta, yg)


def _routing_tables(cnt_out, n_tiles, n_tok):
    i32 = jnp.int32
    cnt = cnt_out.reshape(n_tiles, N_EXPERTS, LANES)[:, :, 0].astype(i32)
    count = jnp.sum(cnt, axis=0)
    padded = ((count + MOE_BLOCK - 1) // MOE_BLOCK) * MOE_BLOCK
    pad_end = jnp.cumsum(padded)
    pad_start = pad_end - padded
    run_end = jnp.cumsum(cnt, axis=0)
    rbase = run_end - cnt
    dst = pad_start[None, :] + rbase
    loc = jnp.cumsum(cnt, axis=1) - cnt
    src = loc + (jnp.arange(n_tiles, dtype=i32) * (2 * T_SORT))[:, None]
    n_blocks = (2 * n_tok) // MOE_BLOCK + N_EXPERTS
    block_start = jnp.arange(n_blocks, dtype=i32) * MOE_BLOCK
    block_e = jnp.minimum(jnp.sum(pad_end[None, :] <= block_start[:, None], axis=1), N_EXPERTS - 1).astype(i32)
    nused = (pad_end[-1] // MOE_BLOCK).astype(i32).reshape(1)
    ex = jnp.arange(N_EXPERTS, dtype=i32)
    sel = (block_e[None, :] == ex[:, None]).astype(i32)
    pick = lambda per_expert: jnp.sum(per_expert[..., :, None] * sel, axis=-2)
    r0 = block_start - pick(pad_start)
    nvalid = jnp.clip(pick(count) - r0, 0, MOE_BLOCK)
    tlo = jnp.sum(pick(run_end) <= r0[None, :], axis=0)
    thi = jnp.sum(pick(rbase) < (r0 + MOE_BLOCK)[None, :], axis=0)
    flat = lambda a: a.reshape(-1).astype(i32)
    return dict(cnt=flat(cnt), src=flat(src), dst=flat(dst), loc=flat(loc), rbase=flat(rbase),
                block_e=block_e, nused=nused, r0=flat(r0), nvalid=flat(nvalid), tlo=flat(tlo), thi=flat(thi),
                nblk=flat(padded // MOE_BLOCK), blk0=flat(pad_start // MOE_BLOCK), n_slots=n_blocks * MOE_BLOCK)


def _stage1(x, positions, attn_norm_gain, w_in, q_norm_gain, k_norm_gain, conv_w, conv_out_gain):
    bsz, seq, _ = x.shape
    n = bsz * seq
    f32, bf16 = jnp.float32, jnp.bfloat16
    w = w_in[0]
    wb = w.astype(bf16)
    wvt = w[:, 2 * ATTN_WIDTH:3 * ATTN_WIDTH].T.astype(bf16)
    scale = DK ** -0.5 * LOG2E
    gqk = jnp.concatenate([jnp.tile(q_norm_gain[0].astype(f32), 2 * N_HEADS) * scale,
                           jnp.tile(k_norm_gain[0].astype(f32), 2 * N_HEADS)]).reshape(1, -1)
    freqs = (ROPE_THETA ** (-jnp.arange(0, ROT_DIM, 2, dtype=f32) / ROT_DIM)).reshape(SUBLANES, 1)
    return _inproj(x.reshape(n, D_MODEL), positions.reshape(1, n),
                   attn_norm_gain[0].reshape(1, -1).astype(f32), wb, wvt, wb, gqk, freqs,
                   conv_w[0].astype(f32), conv_out_gain[0].reshape(1, -1).astype(f32), seq)


def kernel(x, positions, attn_norm_gain, w_in, q_norm_gain, k_norm_gain, lambda_q1, lambda_k1, lambda_q2, lambda_k2, subln_gain, conv_w, conv_out_gain, w_out, ffn_norm_gain, w_group_router, w_expert_router, w_gate, w_up, w_down):
    bsz, seq, _ = x.shape
    n = bsz * seq
    f32, bf16 = jnp.float32, jnp.bfloat16
    assert TM_OUT % T_SORT == 0 and seq % TM_IN == 0 and seq % TQ == 0 and TQ % KV_FULL == 0
    q, k, vt, conv_o = _stage1(x, positions, attn_norm_gain, w_in, q_norm_gain, k_norm_gain,
                                conv_w, conv_out_gain)
    attn_o = _attention(q, k, vt,
                        lambda_q1[0].reshape(1, -1).astype(f32), lambda_k1[0].reshape(1, -1).astype(f32),
                        lambda_q2[0].reshape(1, -1).astype(f32), lambda_k2[0].reshape(1, -1).astype(f32),
                        subln_gain[0].reshape(-1, 1).astype(f32), bsz, seq)

    wr = jnp.concatenate([
        w_group_router[0].astype(f32), jnp.zeros((D_MODEL, EXPERT_ROW0 - N_GROUPS), f32),
        jnp.transpose(w_expert_router[0].astype(f32), (1, 0, 2)).reshape(D_MODEL, N_EXPERTS),
        jnp.zeros((D_MODEL, ROUTER_ROWS - EXPERT_ROW0 - N_EXPERTS), f32)], axis=1)
    wrh = wr.astype(bf16)
    wr2 = jnp.concatenate([wrh, (wr - wrh.astype(f32)).astype(bf16)], axis=1)
    h, xs, meta, cnt_out = _outproj(x.reshape(n, D_MODEL), attn_o, conv_o, w_out[0].astype(bf16),
                                    ffn_norm_gain[0].reshape(1, -1).astype(f32), wr2)

    tabs = _routing_tables(cnt_out, n // T_SORT, n)
    yg = _experts(tabs, xs, w_gate[0], w_up[0], w_down[0])
    out = _combine(tabs["cnt"], tabs["loc"], tabs["dst"], h, meta, yg)
    return out.reshape(x.shape)
```

```python
import functools
import math

import jax
import jax.numpy as jnp
from jax import lax
from jax.experimental import pallas as pl
from jax.experimental.pallas import tpu as pltpu

D_MODEL = 1024
N_HEADS = 4
DK = 64
DV = 128
ROT_DIM = 16
ROPE_THETA = 500000.0
ATTN_WIDTH = N_HEADS * DV
CONV_WIDTH = 512
NORM_EPS = 1e-6
LOG2E = 1.4426950408889634
LAMBDA_INIT = 0.8 - 0.6 * math.exp(-0.3 * 0)
N_GROUPS = 4
EPG = 8
N_EXPERTS = N_GROUPS * EPG
D_FF = 512
MOE_BLOCK = 256

LANES = 128
SUBLANES = 8
ROW_WORDS = D_MODEL // 2
ROW_CHUNKS = ROW_WORDS // LANES
PACKED_DTYPE = jnp.uint32

TM_IN = 1024
TQ = 512
HQ = TQ // 2
KV_FULL = 512
SUM_ROWS = 16
T_SORT = 256
TM_OUT = 512
N_XBUF = 3
GATHER_UNROLL = 24
GATHER_PARTS = 3
GATHER_PRIORITY = 1
VMEM_LIMIT = 48 * 1024 * 1024


def _nt_dot(a, b):
    return lax.dot_general(a, b, (((1,), (1,)), ((), ())), preferred_element_type=jnp.float32)


def _dot(a, b):
    return jnp.dot(a, b, preferred_element_type=jnp.float32)


def _split3(x):
    h = x.astype(jnp.bfloat16)
    r = x - h.astype(jnp.float32)
    m = r.astype(jnp.bfloat16)
    l = (r - m.astype(jnp.float32)).astype(jnp.bfloat16)
    return h, m, l


def _split2(x):
    h = x.astype(jnp.bfloat16)
    l = (x - h.astype(jnp.float32)).astype(jnp.bfloat16)
    return h, l


def _inproj_kernel(x_ref, pos_ref, g1_ref, wqk_ref, wvt_ref, wc_ref, gqk_ref, freq_ref,
                   cw_ref, cg_ref,
                   q_ref, k_ref, vt_ref, conv_ref,
                   carry_ref, *, tiles_per_seq):
    tm = x_ref.shape[0]
    i = pl.program_id(0)

    x = x_ref[...]
    ms = jnp.mean(x * x, axis=-1, keepdims=True)
    hn = (x * lax.rsqrt(ms + NORM_EPS) * g1_ref[...]).astype(jnp.bfloat16)

    pos = pos_ref[...].astype(jnp.float32)
    ang = freq_ref[...] * pos
    lane_r = lax.broadcasted_iota(jnp.int32, (LANES, SUBLANES), 0)
    f_c = lax.broadcasted_iota(jnp.int32, (LANES, SUBLANES), 1)
    in_rot = (lane_r % DK) < ROT_DIM
    expand = jnp.where(in_rot & ((lane_r % (ROT_DIM // 2)) == f_c), 1.0, 0.0).astype(jnp.bfloat16)

    def to_rows(t):
        h, m, l = _split3(t)
        r = _dot(expand, h) + _dot(expand, m) + _dot(expand, l)
        return r.T

    cos_r = to_rows(jnp.cos(ang))
    sin_r = to_rows(jnp.sin(ang))
    lane = lax.broadcasted_iota(jnp.int32, (tm, LANES), 1)
    d = lane % DK
    cos_r = jnp.where(d < ROT_DIM, cos_r, 1.0)
    sin_lo = jnp.where(d < ROT_DIM // 2, -sin_r, 0.0)
    sin_hi = jnp.where((d >= ROT_DIM // 2) & (d < ROT_DIM), sin_r, 0.0)

    qk = _dot(hn, wqk_ref[...])
    seg_r = lax.broadcasted_iota(jnp.int32, (2 * LANES, 2 * LANES), 0) // DK
    seg_c = lax.broadcasted_iota(jnp.int32, (2 * LANES, 2 * LANES), 1) // DK
    seg_mean = jnp.where(seg_r == seg_c, 1.0 / DK, 0.0).astype(jnp.bfloat16)
    half = ROT_DIM // 2
    for c2 in range(4):
        blk = qk[:, c2 * 256:(c2 + 1) * 256]
        msq = _dot((blk * blk).astype(jnp.bfloat16), seg_mean)
        y2 = blk * lax.rsqrt(msq + NORM_EPS) * gqk_ref[:, c2 * 256:(c2 + 1) * 256]
        for c1 in range(2):
            c = c2 * 2 + c1
            y = y2[:, c1 * LANES:(c1 + 1) * LANES]
            rot = (y * cos_r
                   + pltpu.roll(y, LANES - half, 1) * sin_lo
                   + pltpu.roll(y, half, 1) * sin_hi)
            if c < N_HEADS:
                q_ref[:, c * LANES:(c + 1) * LANES] = rot.astype(jnp.bfloat16)
            else:
                h = c - N_HEADS
                k_ref[:, h * LANES:(h + 1) * LANES] = rot.astype(jnp.bfloat16)

    vt_ref[...] = _nt_dot(wvt_ref[...], hn).astype(jnp.bfloat16)

    cp = _dot(hn, wc_ref[...])
    cb = cp[:, :CONV_WIDTH]
    y = cp[:, CONV_WIDTH:2 * CONV_WIDTH] * cp[:, 2 * CONV_WIDTH:]

    @pl.when(i % tiles_per_seq == 0)
    def _():
        carry_ref[...] = jnp.zeros_like(carry_ref)

    prev = carry_ref[...]
    row = lax.broadcasted_iota(jnp.int32, (tm, CONV_WIDTH), 0)
    p1 = prev[SUBLANES - 1:SUBLANES, :]
    p2 = prev[SUBLANES - 2:SUBLANES - 1, :]
    y1 = jnp.where(row == 0, p1, pltpu.roll(y, 1, 0))
    y2 = jnp.where(row == 0, p2, jnp.where(row == 1, p1, pltpu.roll(y, 2, 0)))
    carry_ref[...] = y[tm - SUBLANES:, :]
    z = cw_ref[0:1, :] * y2 + cw_ref[1:2, :] * y1 + cw_ref[2:3, :] * y
    co = cb * z
    cms = jnp.mean(co * co, axis=-1, keepdims=True)
    conv_ref[...] = (co * lax.rsqrt(cms + NORM_EPS) * cg_ref[...]).astype(jnp.bfloat16)


def _inproj(x2, pos_row, g1, wqk, wvt, wc, gqk, freqs, cw, cg, seq):
    n = x2.shape[0]
    tm = TM_IN
    grid = (n // tm,)
    const = lambda i: (0, 0)
    return pl.pallas_call(
        functools.partial(_inproj_kernel, tiles_per_seq=seq // tm),
        grid=grid,
        in_specs=[
            pl.BlockSpec((tm, D_MODEL), lambda i: (i, 0)),
            pl.BlockSpec((1, tm), lambda i: (0, i)),
            pl.BlockSpec((1, D_MODEL), const),
            pl.BlockSpec((D_MODEL, 2 * ATTN_WIDTH), const),
            pl.BlockSpec((ATTN_WIDTH, D_MODEL), const),
            pl.BlockSpec((D_MODEL, 3 * CONV_WIDTH), lambda i: (0, 1)),
            pl.BlockSpec((1, 1024), const),
            pl.BlockSpec((SUBLANES, 1), const),
            pl.BlockSpec((3, CONV_WIDTH), const),
            pl.BlockSpec((1, CONV_WIDTH), const),
        ],
        out_specs=[
            pl.BlockSpec((tm, ATTN_WIDTH), lambda i: (i, 0)),
            pl.BlockSpec((tm, ATTN_WIDTH), lambda i: (i, 0)),
            pl.BlockSpec((ATTN_WIDTH, tm), lambda i: (0, i)),
            pl.BlockSpec((tm, CONV_WIDTH), lambda i: (i, 0)),
        ],
        out_shape=[
            jax.ShapeDtypeStruct((n, ATTN_WIDTH), jnp.bfloat16),
            jax.ShapeDtypeStruct((n, ATTN_WIDTH), jnp.bfloat16),
            jax.ShapeDtypeStruct((ATTN_WIDTH, n), jnp.bfloat16),
            jax.ShapeDtypeStruct((n, CONV_WIDTH), jnp.bfloat16),
        ],
        scratch_shapes=[pltpu.VMEM((SUBLANES, CONV_WIDTH), jnp.float32)],
        compiler_params=pltpu.CompilerParams(
            dimension_semantics=("arbitrary",), vmem_limit_bytes=VMEM_LIMIT),
        name="inproj",
    )(x2, pos_row, g1, wqk, wvt, wc, gqk, freqs, cw, cg)


def _attn_kernel(q_ref, k_ref, vt_ref, lq1_ref, lk1_ref, lq2_ref, lk2_ref, sg_ref, o_ref, acc_ref, *, nq):
    lam = (jnp.exp(jnp.sum(lq1_ref[...] * lk1_ref[...], axis=-1, keepdims=True))
           - jnp.exp(jnp.sum(lq2_ref[...] * lk2_ref[...], axis=-1, keepdims=True))
           + LAMBDA_INIT)
    tri = (lax.broadcasted_iota(jnp.int32, (HQ, HQ), 0)
           <= lax.broadcasted_iota(jnp.int32, (HQ, HQ), 1))
    row_d = lax.broadcasted_iota(jnp.int32, (LANES, HQ), 0)
    ones_rows = jnp.ones((SUM_ROWS, KV_FULL), jnp.bfloat16)

    def masked(s, n_tri):
        blocks = [jnp.where(tri, s[:, c * HQ:(c + 1) * HQ], -jnp.inf) for c in range(n_tri)]
        return jnp.concatenate(blocks + [s[:, n_tri * HQ:]], axis=1) if n_tri * HQ < s.shape[1] else \
            jnp.concatenate(blocks, axis=1)

    todo = {}
    for t in range(nq):
        lo = t * TQ
        todo[t] = ([(lo, lo + HQ, "mixed"), (lo + HQ, lo + TQ, "upper")]
                   + [(k0, k0 + KV_FULL, "full") for k0 in range(0, lo, KV_FULL)])
    tasks = []
    while any(todo.values()):
        for t in reversed(range(nq)):
            if todo[t]:
                tasks.append((t,) + todo[t].pop(0))
    last_task = {t: max(i for i, tk in enumerate(tasks) if tk[0] == t) for t in range(nq)}
    qzt, m_run = {}, {}

    def q_transposed(t):
        parts = []
        for half in range(2):
            r0 = t * TQ + half * HQ
            qt = q_ref[r0:r0 + HQ, :].astype(jnp.float32).T
            parts.append(jnp.where(row_d < DK, qt, 0.0))
            parts.append(jnp.where(row_d >= DK, qt, 0.0))
        return jnp.concatenate(parts, axis=1).astype(jnp.bfloat16)

    def scores(task):
        t, k0, k1, kind = task
        if t not in qzt:
            qzt[t] = q_transposed(t)
        kb = k_ref[k0:k1, :]
        if kind == "upper":
            return masked(_dot(kb, qzt[t][:, 2 * HQ:]), 2)
        s = _dot(kb, qzt[t])
        return masked(s, 2) if kind == "mixed" else s

    def softmax(task, s):
        t, _, _, kind = task
        mx = jnp.max(s, axis=0, keepdims=True)
        if kind == "mixed":
            m_run[t] = mx
            return jnp.exp2(s - mx).astype(jnp.bfloat16), None
        m_old = m_run[t][:, 2 * HQ:] if kind == "upper" else m_run[t]
        m_new = jnp.maximum(m_old, mx)
        alpha = jnp.exp2(m_old - m_new)
        p = jnp.exp2(s - m_new)
        m_run[t] = jnp.concatenate([m_run[t][:, :2 * HQ], m_new], axis=1) if kind == "upper" else m_new
        return p.astype(jnp.bfloat16), alpha

    def accumulate(task, p, alpha):
        t, k0, k1, kind = task
        vta = jnp.concatenate([vt_ref[:, k0:k1], ones_rows[:, :k1 - k0]], axis=0)
        pv = _dot(vta, p)
        if kind == "mixed":
            acc_ref[t] = pv
        elif kind == "upper":
            acc_ref[t, :, 2 * HQ:] = alpha * acc_ref[t, :, 2 * HQ:] + pv
        else:
            acc_ref[t] = alpha * acc_ref[t] + pv

    def finish(t):
        o_all = acc_ref[t, 0:DV, :] * (1.0 / acc_ref[t, DV:DV + 1, :])
        for half in range(2):
            o = (o_all[:, (2 * half) * HQ:(2 * half + 1) * HQ]
                 - lam * o_all[:, (2 * half + 1) * HQ:(2 * half + 2) * HQ])
            ms = jnp.mean(o * o, axis=0, keepdims=True)
            on = o * lax.rsqrt(ms + NORM_EPS) * sg_ref[...] * (1.0 - LAMBDA_INIT)
            r0 = t * TQ + half * HQ
            o_ref[r0:r0 + HQ, :] = on.T.astype(jnp.bfloat16)

    n = len(tasks)
    s_prev, p_prev = None, None
    for step in range(n + 2):
        if step >= 2:
            accumulate(tasks[step - 2], *p_prev)
            if last_task[tasks[step - 2][0]] == step - 2:
                finish(tasks[step - 2][0])
        if 1 <= step <= n:
            p_prev = softmax(tasks[step - 1], s_prev)
        if step < n:
            s_prev = scores(tasks[step])


def _attention(q, k, vt, lq1, lk1, lq2, lk2, sg_col, bsz, seq):
    n = q.shape[0]
    nq = seq // TQ
    vec = lambda b, h: (0, 0)
    return pl.pallas_call(
        functools.partial(_attn_kernel, nq=nq),
        grid=(bsz, N_HEADS),
        in_specs=[
            pl.BlockSpec((seq, DV), lambda b, h: (b, h)),
            pl.BlockSpec((seq, DV), lambda b, h: (b, h)),
            pl.BlockSpec((DV, seq), lambda b, h: (h, b)),
            pl.BlockSpec((1, DK), vec), pl.BlockSpec((1, DK), vec),
            pl.BlockSpec((1, DK), vec), pl.BlockSpec((1, DK), vec),
            pl.BlockSpec((DV, 1), vec),
        ],
        out_specs=pl.BlockSpec((seq, DV), lambda b, h: (b, h)),
        out_shape=jax.ShapeDtypeStruct((n, ATTN_WIDTH), jnp.bfloat16),
        scratch_shapes=[pltpu.VMEM((nq, DV + SUM_ROWS, 4 * HQ), jnp.float32)],
        compiler_params=pltpu.CompilerParams(
            dimension_semantics=("arbitrary", "arbitrary"), vmem_limit_bytes=VMEM_LIMIT),
        name="attn",
    )(q, k, vt, lq1, lk1, lq2, lk2, sg_col)


ROUTER_ROWS = 128
EXPERT_ROW0 = 32


def _outproj_kernel(x_ref, attn_ref, conv_ref, wo_ref, g2_ref, wr2_ref,
                    h_ref, xs_ref, meta_ref, cnt_ref):
    tm = x_ref.shape[0]
    t = T_SORT
    f32, bf16 = jnp.float32, jnp.bfloat16
    a = jnp.concatenate([attn_ref[...], conv_ref[...]], axis=1)
    h = x_ref[...] + _dot(a, wo_ref[...])
    h_ref[...] = h
    ms = jnp.mean(h * h, axis=-1, keepdims=True)
    hn = h * lax.rsqrt(ms + NORM_EPS) * g2_ref[...]
    hn_hi, hn_lo = _split2(hn)

    hh = _dot(hn_hi, wr2_ref[...])
    logits = hh[:, :ROUTER_ROWS] + hh[:, ROUTER_ROWS:] + _dot(hn_lo, wr2_ref[:, :ROUTER_ROWS])
    lt = logits.T
    row8 = lax.broadcasted_iota(jnp.int32, (SUBLANES, tm), 0).astype(f32)
    neg_inf = -jnp.inf

    def first_argmax(v):
        mx = jnp.max(v, axis=0, keepdims=True)
        idx = jnp.min(jnp.where(v == mx, row8, float(SUBLANES)), axis=0, keepdims=True)
        return mx, idx

    g_log = jnp.where(row8 < N_GROUPS, lt[0:SUBLANES, :], neg_inf)
    g_max, g_sel = first_argmax(g_log)
    g_gate = 1.0 / jnp.sum(jnp.exp(g_log - g_max), axis=0, keepdims=True)
    e_log = jnp.zeros((EPG, tm), f32)
    for g in range(N_GROUPS):
        rows = lt[EXPERT_ROW0 + g * EPG:EXPERT_ROW0 + (g + 1) * EPG, :]
        e_log = jnp.where(g_sel == float(g), rows, e_log)
    v1, i1 = first_argmax(e_log)
    v2, i2 = first_argmax(jnp.where(row8 == i1, neg_inf, e_log))
    tt = jnp.exp(v2 - v1)
    w1 = g_gate / (1.0 + tt)
    w2 = g_gate * tt / (1.0 + tt)
    e1 = g_sel * float(EPG) + i1
    e2 = g_sel * float(EPG) + i2

    row32 = lax.broadcasted_iota(jnp.int32, (N_EXPERTS, tm), 0).astype(f32)
    oh1 = row32 == e1
    oh2 = row32 == e2
    c = jnp.where(oh1 | oh2, 1.0, 0.0).astype(bf16)
    tok_r = lax.broadcasted_iota(jnp.int32, (tm, tm), 0)
    tok_c = lax.broadcasted_iota(jnp.int32, (tm, tm), 1)
    same_tile = (tok_r // t) == (tok_c // t)
    rank = _dot(c, jnp.where(same_tile & (tok_r < tok_c), 1.0, 0.0).astype(bf16))
    cnt_b = _dot(c, jnp.where(same_tile, 1.0, 0.0).astype(bf16))
    ex_r = lax.broadcasted_iota(jnp.int32, (N_EXPERTS, N_EXPERTS), 0)
    ex_c = lax.broadcasted_iota(jnp.int32, (N_EXPERTS, N_EXPERTS), 1)
    lower = jnp.where(ex_c < ex_r, 1.0, 0.0).astype(bf16)
    start_b = _dot(lower, cnt_b.astype(bf16))
    pos_e = start_b + rank
    p1 = jnp.sum(jnp.where(oh1, pos_e, 0.0), axis=0, keepdims=True)
    p2 = jnp.sum(jnp.where(oh2, pos_e, 0.0), axis=0, keepdims=True)

    srow = lax.broadcasted_iota(jnp.int32, (2 * t, t), 0).astype(f32)
    for s in range(tm // t):
        cols = slice(s * t, (s + 1) * t)
        perm = jnp.where((srow == p1[:, cols]) | (srow == p2[:, cols]), 1.0, 0.0).astype(bf16)
        xs = _dot(perm, hn_hi[cols, :])
        xs_ref[s * 2 * t:(s + 1) * 2 * t] = _pack_rows(xs)
        cnt_ref[s * N_EXPERTS:(s + 1) * N_EXPERTS, :] = cnt_b[:, s * t:s * t + LANES]

    meta = jnp.concatenate([p1, p2, w1, w2, jnp.zeros((LANES - 4, tm), f32)], axis=0)
    meta_ref[...] = meta.T


def _outproj(x2, attn_o, conv_o, wo, g2, wr2):
    n = x2.shape[0]
    tm = TM_OUT
    nsub = tm // T_SORT
    const = lambda i: (0, 0)

    return pl.pallas_call(
        _outproj_kernel,
        grid=(n // tm,),
        in_specs=[
            pl.BlockSpec((tm, D_MODEL), lambda i: (i, 0)),
            pl.BlockSpec((tm, ATTN_WIDTH), lambda i: (i, 0)),
            pl.BlockSpec((tm, CONV_WIDTH), lambda i: (i, 0)),
            pl.BlockSpec((D_MODEL, D_MODEL), const),
            pl.BlockSpec((1, D_MODEL), const),
            pl.BlockSpec((D_MODEL, 2 * ROUTER_ROWS), const),
        ],
        out_specs=[
            pl.BlockSpec((tm, D_MODEL), lambda i: (i, 0)),
            pl.BlockSpec((2 * tm, ROW_CHUNKS, LANES), lambda i: (i, 0, 0)),
            pl.BlockSpec((tm, LANES), lambda i: (i, 0)),
            pl.BlockSpec((nsub * N_EXPERTS, LANES), lambda i: (i, 0)),
        ],
        out_shape=[
            jax.ShapeDtypeStruct((n, D_MODEL), jnp.float32),
            jax.ShapeDtypeStruct((2 * n, ROW_CHUNKS, LANES), PACKED_DTYPE),
            jax.ShapeDtypeStruct((n, LANES), jnp.float32),
            jax.ShapeDtypeStruct((n // T_SORT * N_EXPERTS, LANES), jnp.float32),
        ],
        compiler_params=pltpu.CompilerParams(
            dimension_semantics=("arbitrary",), vmem_limit_bytes=VMEM_LIMIT),
        name="outproj",
    )(x2, attn_o, conv_o, wo, g2, wr2)


def _pack_rows(x):
    r = x.shape[0]
    w = pltpu.pack_elementwise([x[:, :ROW_WORDS], x[:, ROW_WORDS:]], packed_dtype=jnp.bfloat16)
    return pltpu.bitcast(w, PACKED_DTYPE).reshape(r, ROW_CHUNKS, LANES)


def _packed_zero_rows(r):
    z = jnp.zeros((r, ROW_CHUNKS, LANES), jnp.float32)
    w = pltpu.pack_elementwise([z, z], packed_dtype=jnp.bfloat16)
    return pltpu.bitcast(w, PACKED_DTYPE)


def _unpack_rows(u):
    r = u.shape[0]
    w = u.reshape(r, ROW_WORDS)
    lo = pltpu.unpack_elementwise(w, index=0, packed_dtype=jnp.bfloat16, unpacked_dtype=jnp.float32)
    hi = pltpu.unpack_elementwise(w, index=1, packed_dtype=jnp.bfloat16, unpacked_dtype=jnp.float32)
    return jnp.concatenate([lo, hi], axis=1).astype(jnp.bfloat16)


def _expert_kernel(nblk_ref, blk0_ref, be_ref, nused_ref, r0_ref, nvalid_ref, tlo_ref, thi_ref,
                   cnt_ref, src_ref, rbase_ref,
                   xs_hbm, wg_ref, wu_ref, wd_ref, y_hbm,
                   xbuf, ybuf, wg_s, wu_s, wd_s, sem, ysem):
    e = pl.program_id(0)
    nused = nused_ref[0]
    n_blocks = be_ref.shape[0]
    n_tiles = cnt_ref.shape[0] // N_EXPERTS

    def start_run(blk, slot, t, live):
        r0 = r0_ref[blk]
        k = t * N_EXPERTS + be_ref[blk]
        lo = jnp.maximum(rbase_ref[k], r0)
        hi = jnp.minimum(rbase_ref[k] + cnt_ref[k], r0 + MOE_BLOCK)
        rows = jnp.where(live, hi - lo, 0)

        @pl.when(rows > 0)
        def _():
            pltpu.make_async_copy(xs_hbm.at[pl.ds(src_ref[k] + lo - rbase_ref[k], rows)],
                                  xbuf.at[slot, pl.ds(lo - r0, rows)], sem.at[slot]).start(priority=GATHER_PRIORITY)

    def gather_rolled(blk, slot, t_from):
        def run(t, carry):
            start_run(blk, slot, t, True)
            return carry
        lax.fori_loop(t_from, thi_ref[blk], run, 0)

    def gather_unrolled(blk, slot, live, part):
        blk = jnp.minimum(blk, n_blocks - 1)
        per = GATHER_UNROLL // GATHER_PARTS
        for r in range(part * per, (part + 1) * per):
            t = tlo_ref[blk] + r
            start_run(blk, slot, jnp.minimum(t, n_tiles - 1), live & (t < thi_ref[blk]))
        if part == GATHER_PARTS - 1:
            @pl.when(live & (tlo_ref[blk] + GATHER_UNROLL < thi_ref[blk]))
            def _():
                gather_rolled(blk, slot, tlo_ref[blk] + GATHER_UNROLL)

    def y_copy(blk, slot):
        return pltpu.make_async_copy(ybuf.at[slot], y_hbm.at[pl.ds(blk * MOE_BLOCK, MOE_BLOCK)], ysem.at[slot])

    @pl.when(e == 0)
    def _():
        for slot in range(N_XBUF):
            xbuf[slot] = _packed_zero_rows(MOE_BLOCK)
        gather_rolled(0, 0, tlo_ref[0])
        for blk in range(1, N_XBUF - 1):
            @pl.when(nused > blk)
            def _():
                gather_rolled(blk, blk, tlo_ref[blk])
        for slot in range(2):
            ybuf[slot] = _packed_zero_rows(MOE_BLOCK)
            y_copy(slot, slot).start()

    @pl.when(nblk_ref[e] > 0)
    def _():
        def block(b, first):
            slot = b % N_XBUF
            yslot = b % 2
            nv = nvalid_ref[b]
            y_copy(b, yslot).wait()
            pltpu.make_async_copy(xs_hbm.at[pl.ds(0, nv)], xbuf.at[slot, pl.ds(0, nv)], sem.at[slot]).wait()

            ahead = b + N_XBUF - 1
            x = _unpack_rows(xbuf[slot])
            if first:
                wg_s[...] = wg_ref[0].astype(jnp.bfloat16)
            g = _dot(x, wg_s[...])
            gather_unrolled(ahead, ahead % N_XBUF, ahead < nused, 0)
            if first:
                wu_s[...] = wu_ref[0].astype(jnp.bfloat16)
            u = _dot(x, wu_s[...])
            act = (g / (1.0 + jnp.exp(-g)) * u).astype(jnp.bfloat16)
            gather_unrolled(ahead, ahead % N_XBUF, ahead < nused, 1)
            if first:
                wd_s[...] = wd_ref[0].astype(jnp.bfloat16)
            y = _dot(act, wd_s[...])
            ybuf[yslot] = _pack_rows(y)
            y_copy(b, yslot).start()
            gather_unrolled(ahead, ahead % N_XBUF, ahead < nused, 2)

        block(blk0_ref[e], True)

        def later_block(c, carry):
            block(blk0_ref[e] + c, False)
            return carry
        lax.fori_loop(1, nblk_ref[e], later_block, 0)

    @pl.when(e == N_EXPERTS - 1)
    def _():
        for slot in range(2):
            y_copy(slot, slot).wait()
        ybuf[0] = _packed_zero_rows(MOE_BLOCK)

        def zero_block(b, carry):
            y_copy(b, 0).start()
            y_copy(b, 0).wait()
            return carry
        lax.fori_loop(nused, n_blocks, zero_block, 0)


def _experts(tabs, xs, w_gate, w_up, w_down):
    n_slots = tabs["n_slots"]
    bf16 = jnp.bfloat16
    w_map = lambda e, *_: (e, 0, 0)
    return pl.pallas_call(
        _expert_kernel,
        grid_spec=pltpu.PrefetchScalarGridSpec(
            num_scalar_prefetch=11,
            grid=(N_EXPERTS,),
            in_specs=[
                pl.BlockSpec(memory_space=pl.ANY),
                pl.BlockSpec((1, D_MODEL, D_FF), w_map),
                pl.BlockSpec((1, D_MODEL, D_FF), w_map),
                pl.BlockSpec((1, D_FF, D_MODEL), w_map),
            ],
            out_specs=pl.BlockSpec(memory_space=pl.ANY),
            scratch_shapes=[pltpu.VMEM((N_XBUF, MOE_BLOCK, ROW_CHUNKS, LANES), PACKED_DTYPE),
                            pltpu.VMEM((2, MOE_BLOCK, ROW_CHUNKS, LANES), PACKED_DTYPE),
                            pltpu.VMEM((D_MODEL, D_FF), bf16),
                            pltpu.VMEM((D_MODEL, D_FF), bf16),
                            pltpu.VMEM((D_FF, D_MODEL), bf16),
                            pltpu.SemaphoreType.DMA((N_XBUF,)),
                            pltpu.SemaphoreType.DMA((2,))],
        ),
        out_shape=jax.ShapeDtypeStruct((n_slots, ROW_CHUNKS, LANES), PACKED_DTYPE),
        compiler_params=pltpu.CompilerParams(
            dimension_semantics=("arbitrary",), vmem_limit_bytes=VMEM_LIMIT),
        name="experts",
    )(tabs["nblk"], tabs["blk0"], tabs["block_e"], tabs["nused"], tabs["r0"], tabs["nvalid"], tabs["tlo"],
      tabs["thi"], tabs["cnt"], tabs["src"], tabs["rbase"], xs, w_gate, w_up, w_down)


def _combine_kernel(cnt_ref, loc_ref, dst_ref, h_ref, meta_ref, yg_hbm, o_ref, ybuf, sem, *, n_steps, nsub):
    i = pl.program_id(0)
    t = T_SORT
    f32, bf16 = jnp.float32, jnp.bfloat16

    slot_in = i % 2
    tile0 = jnp.minimum(i, n_steps - 1) * nsub
    for s in range(nsub):
        for e in range(N_EXPERTS):
            k = (tile0 + s) * N_EXPERTS + e
            rows = jnp.where(i < n_steps, cnt_ref[k], 0)

            @pl.when(rows > 0)
            def _():
                pltpu.make_async_copy(yg_hbm.at[pl.ds(dst_ref[k], rows)],
                                      ybuf.at[slot_in, pl.ds(s * 2 * t + loc_ref[k], rows)],
                                      sem.at[slot_in]).start(priority=GATHER_PRIORITY)

    @pl.when(i > 0)
    def _():
        slot = (i - 1) % 2
        pltpu.make_async_copy(yg_hbm.at[pl.ds(0, nsub * 2 * t)], ybuf.at[slot], sem.at[slot]).wait()
        lane = lax.broadcasted_iota(jnp.int32, (t, 2 * t), 1).astype(f32)
        for s in range(nsub):
            y = _unpack_rows(ybuf[slot, s * 2 * t:(s + 1) * 2 * t])
            meta = meta_ref[s * t:(s + 1) * t, :]
            pick1 = jnp.where(lane == meta[:, 0:1], 1.0, 0.0).astype(bf16)
            pick2 = jnp.where(lane == meta[:, 1:2], 1.0, 0.0).astype(bf16)
            y12 = _dot(jnp.concatenate([pick1, pick2], axis=0), y)
            o_ref[s * t:(s + 1) * t, :] = h_ref[s * t:(s + 1) * t, :] + (meta[:, 2:3] * y12[:t]
                                                                         + meta[:, 3:4] * y12[t:])


def _combine(cnt, loc, dst, h, meta, yg):
    n = h.shape[0]
    tm = TM_OUT
    nsub = tm // T_SORT
    n_steps = n // tm
    prev = lambda i, *_: (jnp.maximum(i - 1, 0), 0)
    return pl.pallas_call(
        functools.partial(_combine_kernel, n_steps=n_steps, nsub=nsub),
        grid_spec=pltpu.PrefetchScalarGridSpec(
            num_scalar_prefetch=3,
            grid=(n_steps + 1,),
            in_specs=[
                pl.BlockSpec((tm, D_MODEL), prev),
                pl.BlockSpec((tm, LANES), prev),
                pl.BlockSpec(memory_space=pl.ANY),
            ],
            out_specs=pl.BlockSpec((tm, D_MODEL), prev),
            scratch_shapes=[pltpu.VMEM((2, nsub * 2 * T_SORT, ROW_CHUNKS, LANES), PACKED_DTYPE),
                            pltpu.SemaphoreType.DMA((2,))],
        ),
        out_shape=jax.ShapeDtypeStruct((n, D_MODEL), jnp.float32),
        compiler_params=pltpu.CompilerParams(
            dimension_semantics=("arbitrary",), vmem_limit_bytes=VMEM_LIMIT),
        name="combine",
    )(cnt, loc, dst, h, meta, yg)


def _routing_tables(cnt_out, n_tiles, n_tok):
    i32 = jnp.int32
    cnt = cnt_out.reshape(n_tiles, N_EXPERTS, LANES)[:, :, 0].astype(i32)
    count = jnp.sum(cnt, axis=0)
    padded = ((count + MOE_BLOCK - 1) // MOE_BLOCK) * MOE_BLOCK
    pad_end = jnp.cumsum(padded)
    pad_start = pad_end - padded
    run_end = jnp.cumsum(cnt, axis=0)
    rbase = run_end - cnt
    dst = pad_start[None, :] + rbase
    loc = jnp.cumsum(cnt, axis=1) - cnt
    src = loc + (jnp.arange(n_tiles, dtype=i32) * (2 * T_SORT))[:, None]
    n_blocks = (2 * n_tok) // MOE_BLOCK + N_EXPERTS
    block_start = jnp.arange(n_blocks, dtype=i32) * MOE_BLOCK
    block_e = jnp.minimum(jnp.sum(pad_end[None, :] <= block_start[:, None], axis=1), N_EXPERTS - 1).astype(i32)
    nused = (pad_end[-1] // MOE_BLOCK).astype(i32).reshape(1)
    ex = jnp.arange(N_EXPERTS, dtype=i32)
    sel = (block_e[None, :] == ex[:, None]).astype(i32)
    pick = lambda per_expert: jnp.sum(per_expert[..., :, None] * sel, axis=-2)
    r0 = block_start - pick(pad_start)
    nvalid = jnp.clip(pick(count) - r0, 0, MOE_BLOCK)
    tlo = jnp.sum(pick(run_end) <= r0[None, :], axis=0)
    thi = jnp.sum(pick(rbase) < (r0 + MOE_BLOCK)[None, :], axis=0)
    flat = lambda a: a.reshape(-1).astype(i32)
    return dict(cnt=flat(cnt), src=flat(src), dst=flat(dst), loc=flat(loc), rbase=flat(rbase),
                block_e=block_e, nused=nused, r0=flat(r0), nvalid=flat(nvalid), tlo=flat(tlo), thi=flat(thi),
                nblk=flat(padded // MOE_BLOCK), blk0=flat(pad_start // MOE_BLOCK), n_slots=n_blocks * MOE_BLOCK)


def _stage1(x, positions, attn_norm_gain, w_in, q_norm_gain, k_norm_gain, conv_w, conv_out_gain):
    bsz, seq, _ = x.shape
    n = bsz * seq
    f32, bf16 = jnp.float32, jnp.bfloat16
    w = w_in[0]
    wb = w.astype(bf16)
    wvt = w[:, 2 * ATTN_WIDTH:3 * ATTN_WIDTH].T.astype(bf16)
    scale = DK ** -0.5 * LOG2E
    gqk = jnp.concatenate([jnp.tile(q_norm_gain[0].astype(f32), 2 * N_HEADS) * scale,
                           jnp.tile(k_norm_gain[0].astype(f32), 2 * N_HEADS)]).reshape(1, -1)
    freqs = (ROPE_THETA ** (-jnp.arange(0, ROT_DIM, 2, dtype=f32) / ROT_DIM)).reshape(SUBLANES, 1)
    return _inproj(x.reshape(n, D_MODEL), positions.reshape(1, n),
                   attn_norm_gain[0].reshape(1, -1).astype(f32), wb, wvt, wb, gqk, freqs,
                   conv_w[0].astype(f32), conv_out_gain[0].reshape(1, -1).astype(f32), seq)


def kernel(x, positions, attn_norm_gain, w_in, q_norm_gain, k_norm_gain, lambda_q1, lambda_k1, lambda_q2, lambda_k2, subln_gain, conv_w, conv_out_gain, w_out, ffn_norm_gain, w_group_router, w_expert_router, w_gate, w_up, w_down):
    bsz, seq, _ = x.shape
    n = bsz * seq
    f32, bf16 = jnp.float32, jnp.bfloat16
    assert TM_OUT % T_SORT == 0 and seq % TM_IN == 0 and seq % TQ == 0 and TQ % KV_FULL == 0
    q, k, vt, conv_o = _stage1(x, positions, attn_norm_gain, w_in, q_norm_gain, k_norm_gain,
                                conv_w, conv_out_gain)
    attn_o = _attention(q, k, vt,
                        lambda_q1[0].reshape(1, -1).astype(f32), lambda_k1[0].reshape(1, -1).astype(f32),
                        lambda_q2[0].reshape(1, -1).astype(f32), lambda_k2[0].reshape(1, -1).astype(f32),
                        subln_gain[0].reshape(-1, 1).astype(f32), bsz, seq)

    wr = jnp.concatenate([
        w_group_router[0].astype(f32), jnp.zeros((D_MODEL, EXPERT_ROW0 - N_GROUPS), f32),
        jnp.transpose(w_expert_router[0].astype(f32), (1, 0, 2)).reshape(D_MODEL, N_EXPERTS),
        jnp.zeros((D_MODEL, ROUTER_ROWS - EXPERT_ROW0 - N_EXPERTS), f32)], axis=1)
    wrh = wr.astype(bf16)
    wr2 = jnp.concatenate([wrh, (wr - wrh.astype(f32)).astype(bf16)], axis=1)
    h, xs, meta, cnt_out = _outproj(x.reshape(n, D_MODEL), attn_o, conv_o, w_out[0].astype(bf16),
                                    ffn_norm_gain[0].reshape(1, -1).astype(f32), wr2)

    tabs = _routing_tables(cnt_out, n // T_SORT, n)
    yg = _experts(tabs, xs, w_gate[0], w_up[0], w_down[0])
    out = _combine(tabs["cnt"], tabs["loc"], tabs["dst"], h, meta, yg)
    return out.reshape(x.shape)
```

```python
import functools
import math

import jax
import jax.numpy as jnp
from jax import lax
from jax.experimental import pallas as pl
from jax.experimental.pallas import tpu as pltpu

D_MODEL = 1024
N_HEADS = 4
DK = 64
DV = 128
ROT_DIM = 16
ROPE_THETA = 500000.0
ATTN_WIDTH = N_HEADS * DV
CONV_WIDTH = 512
NORM_EPS = 1e-6
LOG2E = 1.4426950408889634
LAMBDA_INIT = 0.8 - 0.6 * math.exp(-0.3 * 0)
N_GROUPS = 4
EPG = 8
N_EXPERTS = N_GROUPS * EPG
D_FF = 512
MOE_BLOCK = 256

LANES = 128
SUBLANES = 8
ROW_WORDS = D_MODEL // 2
ROW_CHUNKS = ROW_WORDS // LANES
PACKED_DTYPE = jnp.uint32

TM_IN = 1024
TQ = 512
HQ = TQ // 2
KV_FULL = 512
SUM_ROWS = 16
T_SORT = 256
TM_OUT = 512
N_XBUF = 3
GATHER_UNROLL = 12
GATHER_PARTS = 3
GATHER_PRIORITY = 1
VMEM_LIMIT = 48 * 1024 * 1024


def _nt_dot(a, b):
    return lax.dot_general(a, b, (((1,), (1,)), ((), ())), preferred_element_type=jnp.float32)


def _dot(a, b):
    return jnp.dot(a, b, preferred_element_type=jnp.float32)


def _split3(x):
    h = x.astype(jnp.bfloat16)
    r = x - h.astype(jnp.float32)
    m = r.astype(jnp.bfloat16)
    l = (r - m.astype(jnp.float32)).astype(jnp.bfloat16)
    return h, m, l


def _split2(x):
    h = x.astype(jnp.bfloat16)
    l = (x - h.astype(jnp.float32)).astype(jnp.bfloat16)
    return h, l


def _inproj_kernel(x_ref, pos_ref, g1_ref, wqk_ref, wvt_ref, wc_ref, gqk_ref, freq_ref,
                   cw_ref, cg_ref,
                   q_ref, k_ref, vt_ref, conv_ref,
                   carry_ref, *, tiles_per_seq):
    tm = x_ref.shape[0]
    i = pl.program_id(0)

    x = x_ref[...]
    ms = jnp.mean(x * x, axis=-1, keepdims=True)
    hn = (x * lax.rsqrt(ms + NORM_EPS) * g1_ref[...]).astype(jnp.bfloat16)

    pos = pos_ref[...].astype(jnp.float32)
    ang = freq_ref[...] * pos
    lane_r = lax.broadcasted_iota(jnp.int32, (LANES, SUBLANES), 0)
    f_c = lax.broadcasted_iota(jnp.int32, (LANES, SUBLANES), 1)
    in_rot = (lane_r % DK) < ROT_DIM
    expand = jnp.where(in_rot & ((lane_r % (ROT_DIM // 2)) == f_c), 1.0, 0.0).astype(jnp.bfloat16)

    def to_rows(t):
        h, m, l = _split3(t)
        r = _dot(expand, h) + _dot(expand, m) + _dot(expand, l)
        return r.T

    cos_r = to_rows(jnp.cos(ang))
    sin_r = to_rows(jnp.sin(ang))
    lane = lax.broadcasted_iota(jnp.int32, (tm, LANES), 1)
    d = lane % DK
    cos_r = jnp.where(d < ROT_DIM, cos_r, 1.0)
    sin_lo = jnp.where(d < ROT_DIM // 2, -sin_r, 0.0)
    sin_hi = jnp.where((d >= ROT_DIM // 2) & (d < ROT_DIM), sin_r, 0.0)

    qk = _dot(hn, wqk_ref[...])
    seg_r = lax.broadcasted_iota(jnp.int32, (2 * LANES, 2 * LANES), 0) // DK
    seg_c = lax.broadcasted_iota(jnp.int32, (2 * LANES, 2 * LANES), 1) // DK
    seg_mean = jnp.where(seg_r == seg_c, 1.0 / DK, 0.0).astype(jnp.bfloat16)
    half = ROT_DIM // 2
    for c2 in range(4):
        blk = qk[:, c2 * 256:(c2 + 1) * 256]
        msq = _dot((blk * blk).astype(jnp.bfloat16), seg_mean)
        y2 = blk * lax.rsqrt(msq + NORM_EPS) * gqk_ref[:, c2 * 256:(c2 + 1) * 256]
        for c1 in range(2):
            c = c2 * 2 + c1
            y = y2[:, c1 * LANES:(c1 + 1) * LANES]
            rot = (y * cos_r
                   + pltpu.roll(y, LANES - half, 1) * sin_lo
                   + pltpu.roll(y, half, 1) * sin_hi)
            if c < N_HEADS:
                q_ref[:, c * LANES:(c + 1) * LANES] = rot.astype(jnp.bfloat16)
            else:
                h = c - N_HEADS
                k_ref[:, h * LANES:(h + 1) * LANES] = rot.astype(jnp.bfloat16)

    vt_ref[...] = _nt_dot(wvt_ref[...], hn).astype(jnp.bfloat16)

    cp = _dot(hn, wc_ref[...])
    cb = cp[:, :CONV_WIDTH]
    y = cp[:, CONV_WIDTH:2 * CONV_WIDTH] * cp[:, 2 * CONV_WIDTH:]

    @pl.when(i % tiles_per_seq == 0)
    def _():
        carry_ref[...] = jnp.zeros_like(carry_ref)

    prev = carry_ref[...]
    row = lax.broadcasted_iota(jnp.int32, (tm, CONV_WIDTH), 0)
    p1 = prev[SUBLANES - 1:SUBLANES, :]
    p2 = prev[SUBLANES - 2:SUBLANES - 1, :]
    y1 = jnp.where(row == 0, p1, pltpu.roll(y, 1, 0))
    y2 = jnp.where(row == 0, p2, jnp.where(row == 1, p1, pltpu.roll(y, 2, 0)))
    carry_ref[...] = y[tm - SUBLANES:, :]
    z = cw_ref[0:1, :] * y2 + cw_ref[1:2, :] * y1 + cw_ref[2:3, :] * y
    co = cb * z
    cms = jnp.mean(co * co, axis=-1, keepdims=True)
    conv_ref[...] = (co * lax.rsqrt(cms + NORM_EPS) * cg_ref[...]).astype(jnp.bfloat16)


def _inproj(x2, pos_row, g1, wqk, wvt, wc, gqk, freqs, cw, cg, seq):
    n = x2.shape[0]
    tm = TM_IN
    grid = (n // tm,)
    const = lambda i: (0, 0)
    return pl.pallas_call(
        functools.partial(_inproj_kernel, tiles_per_seq=seq // tm),
        grid=grid,
        in_specs=[
            pl.BlockSpec((tm, D_MODEL), lambda i: (i, 0)),
            pl.BlockSpec((1, tm), lambda i: (0, i)),
            pl.BlockSpec((1, D_MODEL), const),
            pl.BlockSpec((D_MODEL, 2 * ATTN_WIDTH), const),
            pl.BlockSpec((ATTN_WIDTH, D_MODEL), const),
            pl.BlockSpec((D_MODEL, 3 * CONV_WIDTH), lambda i: (0, 1)),
            pl.BlockSpec((1, 1024), const),
            pl.BlockSpec((SUBLANES, 1), const),
            pl.BlockSpec((3, CONV_WIDTH), const),
            pl.BlockSpec((1, CONV_WIDTH), const),
        ],
        out_specs=[
            pl.BlockSpec((tm, ATTN_WIDTH), lambda i: (i, 0)),
            pl.BlockSpec((tm, ATTN_WIDTH), lambda i: (i, 0)),
            pl.BlockSpec((ATTN_WIDTH, tm), lambda i: (0, i)),
            pl.BlockSpec((tm, CONV_WIDTH), lambda i: (i, 0)),
        ],
        out_shape=[
            jax.ShapeDtypeStruct((n, ATTN_WIDTH), jnp.bfloat16),
            jax.ShapeDtypeStruct((n, ATTN_WIDTH), jnp.bfloat16),
            jax.ShapeDtypeStruct((ATTN_WIDTH, n), jnp.bfloat16),
            jax.ShapeDtypeStruct((n, CONV_WIDTH), jnp.bfloat16),
        ],
        scratch_shapes=[pltpu.VMEM((SUBLANES, CONV_WIDTH), jnp.float32)],
        compiler_params=pltpu.CompilerParams(
            dimension_semantics=("arbitrary",), vmem_limit_bytes=VMEM_LIMIT),
        name="inproj",
    )(x2, pos_row, g1, wqk, wvt, wc, gqk, freqs, cw, cg)


def _attn_kernel(q_ref, k_ref, vt_ref, lq1_ref, lk1_ref, lq2_ref, lk2_ref, sg_ref, o_ref, acc_ref, *, nq):
    lam = (jnp.exp(jnp.sum(lq1_ref[...] * lk1_ref[...], axis=-1, keepdims=True))
           - jnp.exp(jnp.sum(lq2_ref[...] * lk2_ref[...], axis=-1, keepdims=True))
           + LAMBDA_INIT)
    tri = (lax.broadcasted_iota(jnp.int32, (HQ, HQ), 0)
           <= lax.broadcasted_iota(jnp.int32, (HQ, HQ), 1))
    row_d = lax.broadcasted_iota(jnp.int32, (LANES, HQ), 0)
    ones_rows = jnp.ones((SUM_ROWS, KV_FULL), jnp.bfloat16)

    def masked(s, n_tri):
        blocks = [jnp.where(tri, s[:, c * HQ:(c + 1) * HQ], -jnp.inf) for c in range(n_tri)]
        return jnp.concatenate(blocks + [s[:, n_tri * HQ:]], axis=1) if n_tri * HQ < s.shape[1] else \
            jnp.concatenate(blocks, axis=1)

    todo = {}
    for t in range(nq):
        lo = t * TQ
        todo[t] = ([(lo, lo + HQ, "mixed"), (lo + HQ, lo + TQ, "upper")]
                   + [(k0, k0 + KV_FULL, "full") for k0 in range(0, lo, KV_FULL)])
    tasks = []
    while any(todo.values()):
        for t in reversed(range(nq)):
            if todo[t]:
                tasks.append((t,) + todo[t].pop(0))
    last_task = {t: max(i for i, tk in enumerate(tasks) if tk[0] == t) for t in range(nq)}
    qzt, m_run = {}, {}

    def q_transposed(t):
        parts = []
        for half in range(2):
            r0 = t * TQ + half * HQ
            qt = q_ref[r0:r0 + HQ, :].astype(jnp.float32).T
            parts.append(jnp.where(row_d < DK, qt, 0.0))
            parts.append(jnp.where(row_d >= DK, qt, 0.0))
        return jnp.concatenate(parts, axis=1).astype(jnp.bfloat16)

    def scores(task):
        t, k0, k1, kind = task
        if t not in qzt:
            qzt[t] = q_transposed(t)
        kb = k_ref[k0:k1, :]
        if kind == "upper":
            return masked(_dot(kb, qzt[t][:, 2 * HQ:]), 2)
        s = _dot(kb, qzt[t])
        return masked(s, 2) if kind == "mixed" else s

    def softmax(task, s):
        t, _, _, kind = task
        mx = jnp.max(s, axis=0, keepdims=True)
        if kind == "mixed":
            m_run[t] = mx
            return jnp.exp2(s - mx).astype(jnp.bfloat16), None
        m_old = m_run[t][:, 2 * HQ:] if kind == "upper" else m_run[t]
        m_new = jnp.maximum(m_old, mx)
        alpha = jnp.exp2(m_old - m_new)
        p = jnp.exp2(s - m_new)
        m_run[t] = jnp.concatenate([m_run[t][:, :2 * HQ], m_new], axis=1) if kind == "upper" else m_new
        return p.astype(jnp.bfloat16), alpha

    def accumulate(task, p, alpha):
        t, k0, k1, kind = task
        vta = jnp.concatenate([vt_ref[:, k0:k1], ones_rows[:, :k1 - k0]], axis=0)
        pv = _dot(vta, p)
        if kind == "mixed":
            acc_ref[t] = pv
        elif kind == "upper":
            acc_ref[t, :, 2 * HQ:] = alpha * acc_ref[t, :, 2 * HQ:] + pv
        else:
            acc_ref[t] = alpha * acc_ref[t] + pv

    def finish(t):
        o_all = acc_ref[t, 0:DV, :] * (1.0 / acc_ref[t, DV:DV + 1, :])
        for half in range(2):
            o = (o_all[:, (2 * half) * HQ:(2 * half + 1) * HQ]
                 - lam * o_all[:, (2 * half + 1) * HQ:(2 * half + 2) * HQ])
            ms = jnp.mean(o * o, axis=0, keepdims=True)
            on = o * lax.rsqrt(ms + NORM_EPS) * sg_ref[...] * (1.0 - LAMBDA_INIT)
            r0 = t * TQ + half * HQ
            o_ref[r0:r0 + HQ, :] = on.T.astype(jnp.bfloat16)

    n = len(tasks)
    s_prev, p_prev = None, None
    for step in range(n + 2):
        if step >= 2:
            accumulate(tasks[step - 2], *p_prev)
            if last_task[tasks[step - 2][0]] == step - 2:
                finish(tasks[step - 2][0])
        if 1 <= step <= n:
            p_prev = softmax(tasks[step - 1], s_prev)
        if step < n:
            s_prev = scores(tasks[step])


def _attention(q, k, vt, lq1, lk1, lq2, lk2, sg_col, bsz, seq):
    n = q.shape[0]
    nq = seq // TQ
    vec = lambda b, h: (0, 0)
    return pl.pallas_call(
        functools.partial(_attn_kernel, nq=nq),
        grid=(bsz, N_HEADS),
        in_specs=[
            pl.BlockSpec((seq, DV), lambda b, h: (b, h)),
            pl.BlockSpec((seq, DV), lambda b, h: (b, h)),
            pl.BlockSpec((DV, seq), lambda b, h: (h, b)),
            pl.BlockSpec((1, DK), vec), pl.BlockSpec((1, DK), vec),
            pl.BlockSpec((1, DK), vec), pl.BlockSpec((1, DK), vec),
            pl.BlockSpec((DV, 1), vec),
        ],
        out_specs=pl.BlockSpec((seq, DV), lambda b, h: (b, h)),
        out_shape=jax.ShapeDtypeStruct((n, ATTN_WIDTH), jnp.bfloat16),
        scratch_shapes=[pltpu.VMEM((nq, DV + SUM_ROWS, 4 * HQ), jnp.float32)],
        compiler_params=pltpu.CompilerParams(
            dimension_semantics=("arbitrary", "arbitrary"), vmem_limit_bytes=VMEM_LIMIT),
        name="attn",
    )(q, k, vt, lq1, lk1, lq2, lk2, sg_col)


ROUTER_ROWS = 128
EXPERT_ROW0 = 32


def _outproj_kernel(x_ref, attn_ref, conv_ref, wo_ref, g2_ref, wr2_ref,
                    h_ref, xs_ref, meta_ref, cnt_ref):
    tm = x_ref.shape[0]
    t = T_SORT
    f32, bf16 = jnp.float32, jnp.bfloat16
    a = jnp.concatenate([attn_ref[...], conv_ref[...]], axis=1)
    h = x_ref[...] + _dot(a, wo_ref[...])
    h_ref[...] = h
    ms = jnp.mean(h * h, axis=-1, keepdims=True)
    hn = h * lax.rsqrt(ms + NORM_EPS) * g2_ref[...]
    hn_hi, hn_lo = _split2(hn)

    hh = _dot(hn_hi, wr2_ref[...])
    logits = hh[:, :ROUTER_ROWS] + hh[:, ROUTER_ROWS:] + _dot(hn_lo, wr2_ref[:, :ROUTER_ROWS])
    lt = logits.T
    row8 = lax.broadcasted_iota(jnp.int32, (SUBLANES, tm), 0).astype(f32)
    neg_inf = -jnp.inf

    def first_argmax(v):
        mx = jnp.max(v, axis=0, keepdims=True)
        idx = jnp.min(jnp.where(v == mx, row8, float(SUBLANES)), axis=0, keepdims=True)
        return mx, idx

    g_log = jnp.where(row8 < N_GROUPS, lt[0:SUBLANES, :], neg_inf)
    g_max, g_sel = first_argmax(g_log)
    g_gate = 1.0 / jnp.sum(jnp.exp(g_log - g_max), axis=0, keepdims=True)
    e_log = jnp.zeros((EPG, tm), f32)
    for g in range(N_GROUPS):
        rows = lt[EXPERT_ROW0 + g * EPG:EXPERT_ROW0 + (g + 1) * EPG, :]
        e_log = jnp.where(g_sel == float(g), rows, e_log)
    v1, i1 = first_argmax(e_log)
    v2, i2 = first_argmax(jnp.where(row8 == i1, neg_inf, e_log))
    tt = jnp.exp(v2 - v1)
    w1 = g_gate / (1.0 + tt)
    w2 = g_gate * tt / (1.0 + tt)
    e1 = g_sel * float(EPG) + i1
    e2 = g_sel * float(EPG) + i2

    row32 = lax.broadcasted_iota(jnp.int32, (N_EXPERTS, tm), 0).astype(f32)
    oh1 = row32 == e1
    oh2 = row32 == e2
    c = jnp.where(oh1 | oh2, 1.0, 0.0).astype(bf16)
    tok_r = lax.broadcasted_iota(jnp.int32, (tm, tm), 0)
    tok_c = lax.broadcasted_iota(jnp.int32, (tm, tm), 1)
    same_tile = (tok_r // t) == (tok_c // t)
    earlier = tok_r < tok_c
    ex_r = lax.broadcasted_iota(jnp.int32, (N_EXPERTS, N_EXPERTS), 0)
    ex_c = lax.broadcasted_iota(jnp.int32, (N_EXPERTS, N_EXPERTS), 1)
    lower = jnp.where(ex_c < ex_r, 1.0, 0.0).astype(bf16)

    def where_row(onehot, per_expert):
        return jnp.sum(jnp.where(onehot, per_expert, 0.0), axis=0, keepdims=True)

    rank = _dot(c, jnp.where(same_tile & earlier, 1.0, 0.0).astype(bf16))
    cnt_b = _dot(c, jnp.where(same_tile, 1.0, 0.0).astype(bf16))
    pos_tile = _dot(lower, cnt_b.astype(bf16)) + rank
    p1 = where_row(oh1, pos_tile)
    p2 = where_row(oh2, pos_tile)
    rank_s = _dot(c, jnp.where(earlier, 1.0, 0.0).astype(bf16))
    cnt_s = jnp.broadcast_to(jnp.sum(c.astype(f32), axis=1, keepdims=True), (N_EXPERTS, tm))
    half = jnp.floor(cnt_s * 0.5)
    pos_step = 2.0 * _dot(lower, half.astype(bf16)) + _dot(lower, (cnt_s - 2.0 * half).astype(bf16)) + rank_s
    q1 = where_row(oh1, pos_step)
    q2 = where_row(oh2, pos_step)

    srow = lax.broadcasted_iota(jnp.int32, (2 * tm, tm), 0).astype(f32)
    perm = jnp.where((srow == q1) | (srow == q2), 1.0, 0.0).astype(bf16)
    xs_ref[...] = _pack_rows(_dot(perm, hn_hi))
    for s in range(tm // t):
        cnt_ref[s * N_EXPERTS:(s + 1) * N_EXPERTS, :] = cnt_b[:, s * t:s * t + LANES]

    meta = jnp.concatenate([p1, p2, w1, w2, jnp.zeros((LANES - 4, tm), f32)], axis=0)
    meta_ref[...] = meta.T


def _outproj(x2, attn_o, conv_o, wo, g2, wr2):
    n = x2.shape[0]
    tm = TM_OUT
    nsub = tm // T_SORT
    const = lambda i: (0, 0)

    return pl.pallas_call(
        _outproj_kernel,
        grid=(n // tm,),
        in_specs=[
            pl.BlockSpec((tm, D_MODEL), lambda i: (i, 0)),
            pl.BlockSpec((tm, ATTN_WIDTH), lambda i: (i, 0)),
            pl.BlockSpec((tm, CONV_WIDTH), lambda i: (i, 0)),
            pl.BlockSpec((D_MODEL, D_MODEL), const),
            pl.BlockSpec((1, D_MODEL), const),
            pl.BlockSpec((D_MODEL, 2 * ROUTER_ROWS), const),
        ],
        out_specs=[
            pl.BlockSpec((tm, D_MODEL), lambda i: (i, 0)),
            pl.BlockSpec((2 * tm, ROW_CHUNKS, LANES), lambda i: (i, 0, 0)),
            pl.BlockSpec((tm, LANES), lambda i: (i, 0)),
            pl.BlockSpec((nsub * N_EXPERTS, LANES), lambda i: (i, 0)),
        ],
        out_shape=[
            jax.ShapeDtypeStruct((n, D_MODEL), jnp.float32),
            jax.ShapeDtypeStruct((2 * n, ROW_CHUNKS, LANES), PACKED_DTYPE),
            jax.ShapeDtypeStruct((n, LANES), jnp.float32),
            jax.ShapeDtypeStruct((n // T_SORT * N_EXPERTS, LANES), jnp.float32),
        ],
        compiler_params=pltpu.CompilerParams(
            dimension_semantics=("arbitrary",), vmem_limit_bytes=VMEM_LIMIT),
        name="outproj",
    )(x2, attn_o, conv_o, wo, g2, wr2)


def _pack_rows(x):
    r = x.shape[0]
    w = pltpu.pack_elementwise([x[:, :ROW_WORDS], x[:, ROW_WORDS:]], packed_dtype=jnp.bfloat16)
    return pltpu.bitcast(w, PACKED_DTYPE).reshape(r, ROW_CHUNKS, LANES)


def _packed_zero_rows(r):
    z = jnp.zeros((r, ROW_CHUNKS, LANES), jnp.float32)
    w = pltpu.pack_elementwise([z, z], packed_dtype=jnp.bfloat16)
    return pltpu.bitcast(w, PACKED_DTYPE)


def _unpack_rows(u):
    r = u.shape[0]
    w = u.reshape(r, ROW_WORDS)
    lo = pltpu.unpack_elementwise(w, index=0, packed_dtype=jnp.bfloat16, unpacked_dtype=jnp.float32)
    hi = pltpu.unpack_elementwise(w, index=1, packed_dtype=jnp.bfloat16, unpacked_dtype=jnp.float32)
    return jnp.concatenate([lo, hi], axis=1).astype(jnp.bfloat16)


def _expert_kernel(nblk_ref, blk0_ref, be_ref, nused_ref, r0_ref, nvalid_ref, tlo_ref, thi_ref,
                   cnt_ref, src_ref, rbase_ref,
                   xs_hbm, wg_ref, wu_ref, wd_ref, y_hbm,
                   xbuf, ybuf, wg_s, wu_s, wd_s, sem, ysem):
    e = pl.program_id(0)
    nused = nused_ref[0]
    n_blocks = be_ref.shape[0]
    n_tiles = cnt_ref.shape[0] // N_EXPERTS

    def start_run(blk, slot, t, live):
        r0 = r0_ref[blk]
        k = t * N_EXPERTS + be_ref[blk]
        lo = jnp.maximum(rbase_ref[k], r0)
        hi = jnp.minimum(rbase_ref[k] + cnt_ref[k], r0 + MOE_BLOCK)
        rows = jnp.where(live, hi - lo, 0)

        @pl.when(rows > 0)
        def _():
            pltpu.make_async_copy(xs_hbm.at[pl.ds(src_ref[k] + lo - rbase_ref[k], rows)],
                                  xbuf.at[slot, pl.ds(lo - r0, rows)], sem.at[slot]).start(priority=GATHER_PRIORITY)

    def gather_rolled(blk, slot, t_from):
        def run(t, carry):
            start_run(blk, slot, t, True)
            return carry
        lax.fori_loop(t_from, thi_ref[blk], run, 0)

    def gather_unrolled(blk, slot, live, part):
        blk = jnp.minimum(blk, n_blocks - 1)
        per = GATHER_UNROLL // GATHER_PARTS
        for r in range(part * per, (part + 1) * per):
            t = tlo_ref[blk] + r
            start_run(blk, slot, jnp.minimum(t, n_tiles - 1), live & (t < thi_ref[blk]))
        if part == GATHER_PARTS - 1:
            @pl.when(live & (tlo_ref[blk] + GATHER_UNROLL < thi_ref[blk]))
            def _():
                gather_rolled(blk, slot, tlo_ref[blk] + GATHER_UNROLL)

    def y_copy(blk, slot):
        return pltpu.make_async_copy(ybuf.at[slot], y_hbm.at[pl.ds(blk * MOE_BLOCK, MOE_BLOCK)], ysem.at[slot])

    @pl.when(e == 0)
    def _():
        for slot in range(N_XBUF):
            xbuf[slot] = _packed_zero_rows(MOE_BLOCK)
        gather_rolled(0, 0, tlo_ref[0])
        for blk in range(1, N_XBUF - 1):
            @pl.when(nused > blk)
            def _():
                gather_rolled(blk, blk, tlo_ref[blk])
        for slot in range(2):
            ybuf[slot] = _packed_zero_rows(MOE_BLOCK)
            y_copy(slot, slot).start()

    @pl.when(nblk_ref[e] > 0)
    def _():
        def block(b, first):
            slot = b % N_XBUF
            yslot = b % 2
            nv = nvalid_ref[b]
            y_copy(b, yslot).wait()
            pltpu.make_async_copy(xs_hbm.at[pl.ds(0, nv)], xbuf.at[slot, pl.ds(0, nv)], sem.at[slot]).wait()

            ahead = b + N_XBUF - 1
            x = _unpack_rows(xbuf[slot])
            if first:
                wg_s[...] = wg_ref[0].astype(jnp.bfloat16)
            g = _dot(x, wg_s[...])
            gather_unrolled(ahead, ahead % N_XBUF, ahead < nused, 0)
            if first:
                wu_s[...] = wu_ref[0].astype(jnp.bfloat16)
            u = _dot(x, wu_s[...])
            act = (g / (1.0 + jnp.exp(-g)) * u).astype(jnp.bfloat16)
            gather_unrolled(ahead, ahead % N_XBUF, ahead < nused, 1)
            if first:
                wd_s[...] = wd_ref[0].astype(jnp.bfloat16)
            y = _dot(act, wd_s[...])
            ybuf[yslot] = _pack_rows(y)
            y_copy(b, yslot).start()
            gather_unrolled(ahead, ahead % N_XBUF, ahead < nused, 2)

        block(blk0_ref[e], True)

        def later_block(c, carry):
            block(blk0_ref[e] + c, False)
            return carry
        lax.fori_loop(1, nblk_ref[e], later_block, 0)

    @pl.when(e == N_EXPERTS - 1)
    def _():
        for slot in range(2):
            y_copy(slot, slot).wait()
        ybuf[0] = _packed_zero_rows(MOE_BLOCK)

        def zero_block(b, carry):
            y_copy(b, 0).start()
            y_copy(b, 0).wait()
            return carry
        lax.fori_loop(nused, n_blocks, zero_block, 0)


def _experts(tabs, xs, w_gate, w_up, w_down):
    n_slots = tabs["n_slots"]
    bf16 = jnp.bfloat16
    w_map = lambda e, *_: (e, 0, 0)
    return pl.pallas_call(
        _expert_kernel,
        grid_spec=pltpu.PrefetchScalarGridSpec(
            num_scalar_prefetch=11,
            grid=(N_EXPERTS,),
            in_specs=[
                pl.BlockSpec(memory_space=pl.ANY),
                pl.BlockSpec((1, D_MODEL, D_FF), w_map),
                pl.BlockSpec((1, D_MODEL, D_FF), w_map),
                pl.BlockSpec((1, D_FF, D_MODEL), w_map),
            ],
            out_specs=pl.BlockSpec(memory_space=pl.ANY),
            scratch_shapes=[pltpu.VMEM((N_XBUF, MOE_BLOCK, ROW_CHUNKS, LANES), PACKED_DTYPE),
                            pltpu.VMEM((2, MOE_BLOCK, ROW_CHUNKS, LANES), PACKED_DTYPE),
                            pltpu.VMEM((D_MODEL, D_FF), bf16),
                            pltpu.VMEM((D_MODEL, D_FF), bf16),
                            pltpu.VMEM((D_FF, D_MODEL), bf16),
                            pltpu.SemaphoreType.DMA((N_XBUF,)),
                            pltpu.SemaphoreType.DMA((2,))],
        ),
        out_shape=jax.ShapeDtypeStruct((n_slots, ROW_CHUNKS, LANES), PACKED_DTYPE),
        compiler_params=pltpu.CompilerParams(
            dimension_semantics=("arbitrary",), vmem_limit_bytes=VMEM_LIMIT),
        name="experts",
    )(tabs["nblk"], tabs["blk0"], tabs["block_e"], tabs["nused"], tabs["r0"], tabs["nvalid"], tabs["tlo"],
      tabs["thi"], tabs["cnt_s"], tabs["src"], tabs["rbase"], xs, w_gate, w_up, w_down)


def _combine_kernel(cnt_ref, loc_ref, dst_ref, h_ref, meta_ref, yg_hbm, o_ref, ybuf, sem, *, n_steps, nsub):
    i = pl.program_id(0)
    t = T_SORT
    f32, bf16 = jnp.float32, jnp.bfloat16

    slot_in = i % 2
    tile0 = jnp.minimum(i, n_steps - 1) * nsub
    for s in range(nsub):
        for e in range(N_EXPERTS):
            k = (tile0 + s) * N_EXPERTS + e
            rows = jnp.where(i < n_steps, cnt_ref[k], 0)

            @pl.when(rows > 0)
            def _():
                pltpu.make_async_copy(yg_hbm.at[pl.ds(dst_ref[k], rows)],
                                      ybuf.at[slot_in, pl.ds(s * 2 * t + loc_ref[k], rows)],
                                      sem.at[slot_in]).start(priority=GATHER_PRIORITY)

    @pl.when(i > 0)
    def _():
        slot = (i - 1) % 2
        pltpu.make_async_copy(yg_hbm.at[pl.ds(0, nsub * 2 * t)], ybuf.at[slot], sem.at[slot]).wait()
        lane = lax.broadcasted_iota(jnp.int32, (t, 2 * t), 1).astype(f32)
        for s in range(nsub):
            y = _unpack_rows(ybuf[slot, s * 2 * t:(s + 1) * 2 * t])
            meta = meta_ref[s * t:(s + 1) * t, :]
            pick1 = jnp.where(lane == meta[:, 0:1], 1.0, 0.0).astype(bf16)
            pick2 = jnp.where(lane == meta[:, 1:2], 1.0, 0.0).astype(bf16)
            y12 = _dot(jnp.concatenate([pick1, pick2], axis=0), y)
            o_ref[s * t:(s + 1) * t, :] = h_ref[s * t:(s + 1) * t, :] + (meta[:, 2:3] * y12[:t]
                                                                         + meta[:, 3:4] * y12[t:])


def _combine(cnt, loc, dst, h, meta, yg):
    n = h.shape[0]
    tm = TM_OUT
    nsub = tm // T_SORT
    n_steps = n // tm
    prev = lambda i, *_: (jnp.maximum(i - 1, 0), 0)
    return pl.pallas_call(
        functools.partial(_combine_kernel, n_steps=n_steps, nsub=nsub),
        grid_spec=pltpu.PrefetchScalarGridSpec(
            num_scalar_prefetch=3,
            grid=(n_steps + 1,),
            in_specs=[
                pl.BlockSpec((tm, D_MODEL), prev),
                pl.BlockSpec((tm, LANES), prev),
                pl.BlockSpec(memory_space=pl.ANY),
            ],
            out_specs=pl.BlockSpec((tm, D_MODEL), prev),
            scratch_shapes=[pltpu.VMEM((2, nsub * 2 * T_SORT, ROW_CHUNKS, LANES), PACKED_DTYPE),
                            pltpu.SemaphoreType.DMA((2,))],
        ),
        out_shape=jax.ShapeDtypeStruct((n, D_MODEL), jnp.float32),
        compiler_params=pltpu.CompilerParams(
            dimension_semantics=("arbitrary",), vmem_limit_bytes=VMEM_LIMIT),
        name="combine",
    )(cnt, loc, dst, h, meta, yg)


def _routing_tables(cnt_out, n_tiles, n_tok):
    i32 = jnp.int32
    cnt = cnt_out.reshape(n_tiles, N_EXPERTS, LANES)[:, :, 0].astype(i32)
    count = jnp.sum(cnt, axis=0)
    padded = ((count + MOE_BLOCK - 1) // MOE_BLOCK) * MOE_BLOCK
    pad_end = jnp.cumsum(padded)
    pad_start = pad_end - padded
    run_end = jnp.cumsum(cnt, axis=0)
    dst = pad_start[None, :] + run_end - cnt
    loc = jnp.cumsum(cnt, axis=1) - cnt
    n_steps = n_tiles * T_SORT // TM_OUT
    cnt_s = jnp.sum(cnt.reshape(n_steps, TM_OUT // T_SORT, N_EXPERTS), axis=1)
    run_end = jnp.cumsum(cnt_s, axis=0)
    rbase = run_end - cnt_s
    src = (jnp.cumsum(cnt_s, axis=1) - cnt_s) + (jnp.arange(n_steps, dtype=i32) * (2 * TM_OUT))[:, None]
    n_blocks = (2 * n_tok) // MOE_BLOCK + N_EXPERTS
    block_start = jnp.arange(n_blocks, dtype=i32) * MOE_BLOCK
    block_e = jnp.minimum(jnp.sum(pad_end[None, :] <= block_start[:, None], axis=1), N_EXPERTS - 1).astype(i32)
    nused = (pad_end[-1] // MOE_BLOCK).astype(i32).reshape(1)
    ex = jnp.arange(N_EXPERTS, dtype=i32)
    sel = (block_e[None, :] == ex[:, None]).astype(i32)
    pick = lambda per_expert: jnp.sum(per_expert[..., :, None] * sel, axis=-2)
    r0 = block_start - pick(pad_start)
    nvalid = jnp.clip(pick(count) - r0, 0, MOE_BLOCK)
    tlo = jnp.sum(pick(run_end) <= r0[None, :], axis=0)
    thi = jnp.sum(pick(rbase) < (r0 + MOE_BLOCK)[None, :], axis=0)
    flat = lambda a: a.reshape(-1).astype(i32)
    return dict(cnt=flat(cnt), dst=flat(dst), loc=flat(loc), cnt_s=flat(cnt_s), src=flat(src), rbase=flat(rbase),
                block_e=block_e, nused=nused, r0=flat(r0), nvalid=flat(nvalid), tlo=flat(tlo), thi=flat(thi),
                nblk=flat(padded // MOE_BLOCK), blk0=flat(pad_start // MOE_BLOCK), n_slots=n_blocks * MOE_BLOCK)


def _stage1(x, positions, attn_norm_gain, w_in, q_norm_gain, k_norm_gain, conv_w, conv_out_gain):
    bsz, seq, _ = x.shape
    n = bsz * seq
    f32, bf16 = jnp.float32, jnp.bfloat16
    w = w_in[0]
    wb = w.astype(bf16)
    wvt = w[:, 2 * ATTN_WIDTH:3 * ATTN_WIDTH].T.astype(bf16)
    scale = DK ** -0.5 * LOG2E
    gqk = jnp.concatenate([jnp.tile(q_norm_gain[0].astype(f32), 2 * N_HEADS) * scale,
                           jnp.tile(k_norm_gain[0].astype(f32), 2 * N_HEADS)]).reshape(1, -1)
    freqs = (ROPE_THETA ** (-jnp.arange(0, ROT_DIM, 2, dtype=f32) / ROT_DIM)).reshape(SUBLANES, 1)
    return _inproj(x.reshape(n, D_MODEL), positions.reshape(1, n),
                   attn_norm_gain[0].reshape(1, -1).astype(f32), wb, wvt, wb, gqk, freqs,
                   conv_w[0].astype(f32), conv_out_gain[0].reshape(1, -1).astype(f32), seq)


def kernel(x, positions, attn_norm_gain, w_in, q_norm_gain, k_norm_gain, lambda_q1, lambda_k1, lambda_q2, lambda_k2, subln_gain, conv_w, conv_out_gain, w_out, ffn_norm_gain, w_group_router, w_expert_router, w_gate, w_up, w_down):
    bsz, seq, _ = x.shape
    n = bsz * seq
    f32, bf16 = jnp.float32, jnp.bfloat16
    assert TM_OUT % T_SORT == 0 and seq % TM_IN == 0 and seq % TQ == 0 and TQ % KV_FULL == 0
    q, k, vt, conv_o = _stage1(x, positions, attn_norm_gain, w_in, q_norm_gain, k_norm_gain,
                                conv_w, conv_out_gain)
    attn_o = _attention(q, k, vt,
                        lambda_q1[0].reshape(1, -1).astype(f32), lambda_k1[0].reshape(1, -1).astype(f32),
                        lambda_q2[0].reshape(1, -1).astype(f32), lambda_k2[0].reshape(1, -1).astype(f32),
                        subln_gain[0].reshape(-1, 1).astype(f32), bsz, seq)

    wr = jnp.concatenate([
        w_group_router[0].astype(f32), jnp.zeros((D_MODEL, EXPERT_ROW0 - N_GROUPS), f32),
        jnp.transpose(w_expert_router[0].astype(f32), (1, 0, 2)).reshape(D_MODEL, N_EXPERTS),
        jnp.zeros((D_MODEL, ROUTER_ROWS - EXPERT_ROW0 - N_EXPERTS), f32)], axis=1)
    wrh = wr.astype(bf16)
    wr2 = jnp.concatenate([wrh, (wr - wrh.astype(f32)).astype(bf16)], axis=1)
    h, xs, meta, cnt_out = _outproj(x.reshape(n, D_MODEL), attn_o, conv_o, w_out[0].astype(bf16),
                                    ffn_norm_gain[0].reshape(1, -1).astype(f32), wr2)

    tabs = _routing_tables(cnt_out, n // T_SORT, n)
    yg = _experts(tabs, xs, w_gate[0], w_up[0], w_down[0])
    out = _combine(tabs["cnt"], tabs["loc"], tabs["dst"], h, meta, yg)
    return out.reshape(x.shape)
```

```python
import functools
import math

import jax
import jax.numpy as jnp
from jax import lax
from jax.experimental import pallas as pl
from jax.experimental.pallas import tpu as pltpu

D_MODEL = 1024
N_HEADS = 4
DK = 64
DV = 128
ROT_DIM = 16
ROPE_THETA = 500000.0
ATTN_WIDTH = N_HEADS * DV
CONV_WIDTH = 512
NORM_EPS = 1e-6
LOG2E = 1.4426950408889634
LAMBDA_INIT = 0.8 - 0.6 * math.exp(-0.3 * 0)
N_GROUPS = 4
EPG = 8
N_EXPERTS = N_GROUPS * EPG
D_FF = 512
MOE_BLOCK = 256

LANES = 128
SUBLANES = 8
ROW_WORDS = D_MODEL // 2
ROW_CHUNKS = ROW_WORDS // LANES
PACKED_DTYPE = jnp.uint32

TM_IN = 1024
TQ = 512
HQ = TQ // 2
KV_FULL = 512
SUM_ROWS = 16
T_SORT = 256
TM_OUT = 512
N_XBUF = 3
GATHER_UNROLL = 24
GATHER_PARTS = 3
GATHER_PRIORITY = 1
VMEM_LIMIT = 48 * 1024 * 1024


def _nt_dot(a, b):
    return lax.dot_general(a, b, (((1,), (1,)), ((), ())), preferred_element_type=jnp.float32)


def _dot(a, b):
    return jnp.dot(a, b, preferred_element_type=jnp.float32)


def _split3(x):
    h = x.astype(jnp.bfloat16)
    r = x - h.astype(jnp.float32)
    m = r.astype(jnp.bfloat16)
    l = (r - m.astype(jnp.float32)).astype(jnp.bfloat16)
    return h, m, l


def _split2(x):
    h = x.astype(jnp.bfloat16)
    l = (x - h.astype(jnp.float32)).astype(jnp.bfloat16)
    return h, l


def _inproj_kernel(x_ref, pos_ref, g1_ref, wqk_ref, wvt_ref, wc_ref, gqk_ref, freq_ref,
                   cw_ref, cg_ref,
                   q_ref, k_ref, vt_ref, conv_ref,
                   carry_ref, *, tiles_per_seq):
    tm = x_ref.shape[0]
    i = pl.program_id(0)

    x = x_ref[...]
    ms = jnp.mean(x * x, axis=-1, keepdims=True)
    hn = (x * lax.rsqrt(ms + NORM_EPS) * g1_ref[...]).astype(jnp.bfloat16)

    pos = pos_ref[...].astype(jnp.float32)
    ang = freq_ref[...] * pos
    lane_r = lax.broadcasted_iota(jnp.int32, (LANES, SUBLANES), 0)
    f_c = lax.broadcasted_iota(jnp.int32, (LANES, SUBLANES), 1)
    in_rot = (lane_r % DK) < ROT_DIM
    expand = jnp.where(in_rot & ((lane_r % (ROT_DIM // 2)) == f_c), 1.0, 0.0).astype(jnp.bfloat16)

    def to_rows(t):
        h, m, l = _split3(t)
        r = _dot(expand, h) + _dot(expand, m) + _dot(expand, l)
        return r.T

    cos_r = to_rows(jnp.cos(ang))
    sin_r = to_rows(jnp.sin(ang))
    lane = lax.broadcasted_iota(jnp.int32, (tm, LANES), 1)
    d = lane % DK
    cos_r = jnp.where(d < ROT_DIM, cos_r, 1.0)
    sin_lo = jnp.where(d < ROT_DIM // 2, -sin_r, 0.0)
    sin_hi = jnp.where((d >= ROT_DIM // 2) & (d < ROT_DIM), sin_r, 0.0)

    qk = _dot(hn, wqk_ref[...])
    seg_r = lax.broadcasted_iota(jnp.int32, (2 * LANES, 2 * LANES), 0) // DK
    seg_c = lax.broadcasted_iota(jnp.int32, (2 * LANES, 2 * LANES), 1) // DK
    seg_mean = jnp.where(seg_r == seg_c, 1.0 / DK, 0.0).astype(jnp.bfloat16)
    half = ROT_DIM // 2
    for c2 in range(4):
        blk = qk[:, c2 * 256:(c2 + 1) * 256]
        msq = _dot((blk * blk).astype(jnp.bfloat16), seg_mean)
        y2 = blk * lax.rsqrt(msq + NORM_EPS) * gqk_ref[:, c2 * 256:(c2 + 1) * 256]
        for c1 in range(2):
            c = c2 * 2 + c1
            y = y2[:, c1 * LANES:(c1 + 1) * LANES]
            rot = (y * cos_r
                   + pltpu.roll(y, LANES - half, 1) * sin_lo
                   + pltpu.roll(y, half, 1) * sin_hi)
            if c < N_HEADS:
                q_ref[:, c * LANES:(c + 1) * LANES] = rot.astype(jnp.bfloat16)
            else:
                h = c - N_HEADS
                k_ref[:, h * LANES:(h + 1) * LANES] = rot.astype(jnp.bfloat16)

    vt_ref[...] = _nt_dot(wvt_ref[...], hn).astype(jnp.bfloat16)

    cp = _dot(hn, wc_ref[...])
    cb = cp[:, :CONV_WIDTH]
    y = cp[:, CONV_WIDTH:2 * CONV_WIDTH] * cp[:, 2 * CONV_WIDTH:]

    @pl.when(i % tiles_per_seq == 0)
    def _():
        carry_ref[...] = jnp.zeros_like(carry_ref)

    prev = carry_ref[...]
    row = lax.broadcasted_iota(jnp.int32, (tm, CONV_WIDTH), 0)
    p1 = prev[SUBLANES - 1:SUBLANES, :]
    p2 = prev[SUBLANES - 2:SUBLANES - 1, :]
    y1 = jnp.where(row == 0, p1, pltpu.roll(y, 1, 0))
    y2 = jnp.where(row == 0, p2, jnp.where(row == 1, p1, pltpu.roll(y, 2, 0)))
    carry_ref[...] = y[tm - SUBLANES:, :]
    z = cw_ref[0:1, :] * y2 + cw_ref[1:2, :] * y1 + cw_ref[2:3, :] * y
    co = cb * z
    cms = jnp.mean(co * co, axis=-1, keepdims=True)
    conv_ref[...] = (co * lax.rsqrt(cms + NORM_EPS) * cg_ref[...]).astype(jnp.bfloat16)


def _inproj(x2, pos_row, g1, wqk, wvt, wc, gqk, freqs, cw, cg, seq):
    n = x2.shape[0]
    tm = TM_IN
    grid = (n // tm,)
    const = lambda i: (0, 0)
    return pl.pallas_call(
        functools.partial(_inproj_kernel, tiles_per_seq=seq // tm),
        grid=grid,
        in_specs=[
            pl.BlockSpec((tm, D_MODEL), lambda i: (i, 0)),
            pl.BlockSpec((1, tm), lambda i: (0, i)),
            pl.BlockSpec((1, D_MODEL), const),
            pl.BlockSpec((D_MODEL, 2 * ATTN_WIDTH), const),
            pl.BlockSpec((ATTN_WIDTH, D_MODEL), const),
            pl.BlockSpec((D_MODEL, 3 * CONV_WIDTH), lambda i: (0, 1)),
            pl.BlockSpec((1, 1024), const),
            pl.BlockSpec((SUBLANES, 1), const),
            pl.BlockSpec((3, CONV_WIDTH), const),
            pl.BlockSpec((1, CONV_WIDTH), const),
        ],
        out_specs=[
            pl.BlockSpec((tm, ATTN_WIDTH), lambda i: (i, 0)),
            pl.BlockSpec((tm, ATTN_WIDTH), lambda i: (i, 0)),
            pl.BlockSpec((ATTN_WIDTH, tm), lambda i: (0, i)),
            pl.BlockSpec((tm, CONV_WIDTH), lambda i: (i, 0)),
        ],
        out_shape=[
            jax.ShapeDtypeStruct((n, ATTN_WIDTH), jnp.bfloat16),
            jax.ShapeDtypeStruct((n, ATTN_WIDTH), jnp.bfloat16),
            jax.ShapeDtypeStruct((ATTN_WIDTH, n), jnp.bfloat16),
            jax.ShapeDtypeStruct((n, CONV_WIDTH), jnp.bfloat16),
        ],
        scratch_shapes=[pltpu.VMEM((SUBLANES, CONV_WIDTH), jnp.float32)],
        compiler_params=pltpu.CompilerParams(
            dimension_semantics=("arbitrary",), vmem_limit_bytes=VMEM_LIMIT),
        name="inproj",
    )(x2, pos_row, g1, wqk, wvt, wc, gqk, freqs, cw, cg)


def _attn_kernel(q_ref, k_ref, vt_ref, lq1_ref, lk1_ref, lq2_ref, lk2_ref, sg_ref, wg_ref, wu_ref, wd_ref,
                 o_ref, wgb_ref, wub_ref, wdb_ref, acc_ref, *, nq):
    lam = (jnp.exp(jnp.sum(lq1_ref[...] * lk1_ref[...], axis=-1, keepdims=True))
           - jnp.exp(jnp.sum(lq2_ref[...] * lk2_ref[...], axis=-1, keepdims=True))
           + LAMBDA_INIT)
    wgb_ref[...] = wg_ref[...].astype(jnp.bfloat16)
    wub_ref[...] = wu_ref[...].astype(jnp.bfloat16)
    wdb_ref[...] = wd_ref[...].astype(jnp.bfloat16)
    tri = (lax.broadcasted_iota(jnp.int32, (HQ, HQ), 0)
           <= lax.broadcasted_iota(jnp.int32, (HQ, HQ), 1))
    row_d = lax.broadcasted_iota(jnp.int32, (LANES, HQ), 0)
    ones_rows = jnp.ones((SUM_ROWS, KV_FULL), jnp.bfloat16)

    def masked(s, n_tri):
        blocks = [jnp.where(tri, s[:, c * HQ:(c + 1) * HQ], -jnp.inf) for c in range(n_tri)]
        return jnp.concatenate(blocks + [s[:, n_tri * HQ:]], axis=1) if n_tri * HQ < s.shape[1] else \
            jnp.concatenate(blocks, axis=1)

    todo = {}
    for t in range(nq):
        lo = t * TQ
        todo[t] = ([(lo, lo + HQ, "mixed"), (lo + HQ, lo + TQ, "upper")]
                   + [(k0, k0 + KV_FULL, "full") for k0 in range(0, lo, KV_FULL)])
    tasks = []
    while any(todo.values()):
        for t in reversed(range(nq)):
            if todo[t]:
                tasks.append((t,) + todo[t].pop(0))
    last_task = {t: max(i for i, tk in enumerate(tasks) if tk[0] == t) for t in range(nq)}
    qzt, m_run = {}, {}

    def q_transposed(t):
        parts = []
        for half in range(2):
            r0 = t * TQ + half * HQ
            qt = q_ref[r0:r0 + HQ, :].astype(jnp.float32).T
            parts.append(jnp.where(row_d < DK, qt, 0.0))
            parts.append(jnp.where(row_d >= DK, qt, 0.0))
        return jnp.concatenate(parts, axis=1).astype(jnp.bfloat16)

    def scores(task):
        t, k0, k1, kind = task
        if t not in qzt:
            qzt[t] = q_transposed(t)
        kb = k_ref[k0:k1, :]
        if kind == "upper":
            return masked(_dot(kb, qzt[t][:, 2 * HQ:]), 2)
        s = _dot(kb, qzt[t])
        return masked(s, 2) if kind == "mixed" else s

    def softmax(task, s):
        t, _, _, kind = task
        mx = jnp.max(s, axis=0, keepdims=True)
        if kind == "mixed":
            m_run[t] = mx
            return jnp.exp2(s - mx).astype(jnp.bfloat16), None
        m_old = m_run[t][:, 2 * HQ:] if kind == "upper" else m_run[t]
        m_new = jnp.maximum(m_old, mx)
        alpha = jnp.exp2(m_old - m_new)
        p = jnp.exp2(s - m_new)
        m_run[t] = jnp.concatenate([m_run[t][:, :2 * HQ], m_new], axis=1) if kind == "upper" else m_new
        return p.astype(jnp.bfloat16), alpha

    def accumulate(task, p, alpha):
        t, k0, k1, kind = task
        vta = jnp.concatenate([vt_ref[:, k0:k1], ones_rows[:, :k1 - k0]], axis=0)
        pv = _dot(vta, p)
        if kind == "mixed":
            acc_ref[t] = pv
        elif kind == "upper":
            acc_ref[t, :, 2 * HQ:] = alpha * acc_ref[t, :, 2 * HQ:] + pv
        else:
            acc_ref[t] = alpha * acc_ref[t] + pv

    def finish(t):
        o_all = acc_ref[t, 0:DV, :] * (1.0 / acc_ref[t, DV:DV + 1, :])
        for half in range(2):
            o = (o_all[:, (2 * half) * HQ:(2 * half + 1) * HQ]
                 - lam * o_all[:, (2 * half + 1) * HQ:(2 * half + 2) * HQ])
            ms = jnp.mean(o * o, axis=0, keepdims=True)
            on = o * lax.rsqrt(ms + NORM_EPS) * sg_ref[...] * (1.0 - LAMBDA_INIT)
            r0 = t * TQ + half * HQ
            o_ref[r0:r0 + HQ, :] = on.T.astype(jnp.bfloat16)

    n = len(tasks)
    s_prev, p_prev = None, None
    for step in range(n + 2):
        if step >= 2:
            accumulate(tasks[step - 2], *p_prev)
            if last_task[tasks[step - 2][0]] == step - 2:
                finish(tasks[step - 2][0])
        if 1 <= step <= n:
            p_prev = softmax(tasks[step - 1], s_prev)
        if step < n:
            s_prev = scores(tasks[step])


def _attention(q, k, vt, lq1, lk1, lq2, lk2, sg_col, w_gate, w_up, w_down, bsz, seq):
    n = q.shape[0]
    nq = seq // TQ
    steps = bsz * N_HEADS
    assert N_EXPERTS % steps == 0
    epw = N_EXPERTS // steps
    vec = lambda b, h: (0, 0)
    w_map = lambda b, h: (b * N_HEADS + h, 0, 0)
    bf16 = jnp.bfloat16
    return pl.pallas_call(
        functools.partial(_attn_kernel, nq=nq),
        grid=(bsz, N_HEADS),
        in_specs=[
            pl.BlockSpec((seq, DV), lambda b, h: (b, h)),
            pl.BlockSpec((seq, DV), lambda b, h: (b, h)),
            pl.BlockSpec((DV, seq), lambda b, h: (h, b)),
            pl.BlockSpec((1, DK), vec), pl.BlockSpec((1, DK), vec),
            pl.BlockSpec((1, DK), vec), pl.BlockSpec((1, DK), vec),
            pl.BlockSpec((DV, 1), vec),
            pl.BlockSpec((epw, D_MODEL, D_FF), w_map),
            pl.BlockSpec((epw, D_MODEL, D_FF), w_map),
            pl.BlockSpec((epw, D_FF, D_MODEL), w_map),
        ],
        out_specs=[
            pl.BlockSpec((seq, DV), lambda b, h: (b, h)),
            pl.BlockSpec((epw, D_MODEL, D_FF), w_map),
            pl.BlockSpec((epw, D_MODEL, D_FF), w_map),
            pl.BlockSpec((epw, D_FF, D_MODEL), w_map),
        ],
        out_shape=[
            jax.ShapeDtypeStruct((n, ATTN_WIDTH), bf16),
            jax.ShapeDtypeStruct((N_EXPERTS, D_MODEL, D_FF), bf16),
            jax.ShapeDtypeStruct((N_EXPERTS, D_MODEL, D_FF), bf16),
            jax.ShapeDtypeStruct((N_EXPERTS, D_FF, D_MODEL), bf16),
        ],
        scratch_shapes=[pltpu.VMEM((nq, DV + SUM_ROWS, 4 * HQ), jnp.float32)],
        compiler_params=pltpu.CompilerParams(
            dimension_semantics=("arbitrary", "arbitrary"), vmem_limit_bytes=VMEM_LIMIT),
        name="attn",
    )(q, k, vt, lq1, lk1, lq2, lk2, sg_col, w_gate, w_up, w_down)


ROUTER_ROWS = 128
EXPERT_ROW0 = 32


def _outproj_kernel(x_ref, attn_ref, conv_ref, wo_ref, g2_ref, wr2_ref,
                    h_ref, xs_ref, meta_ref, cnt_ref):
    tm = x_ref.shape[0]
    t = T_SORT
    f32, bf16 = jnp.float32, jnp.bfloat16
    a = jnp.concatenate([attn_ref[...], conv_ref[...]], axis=1)
    h = x_ref[...] + _dot(a, wo_ref[...])
    h_ref[...] = h
    ms = jnp.mean(h * h, axis=-1, keepdims=True)
    hn = h * lax.rsqrt(ms + NORM_EPS) * g2_ref[...]
    hn_hi, hn_lo = _split2(hn)

    hh = _dot(hn_hi, wr2_ref[...])
    logits = hh[:, :ROUTER_ROWS] + hh[:, ROUTER_ROWS:] + _dot(hn_lo, wr2_ref[:, :ROUTER_ROWS])
    lt = logits.T
    row8 = lax.broadcasted_iota(jnp.int32, (SUBLANES, tm), 0).astype(f32)
    neg_inf = -jnp.inf

    def first_argmax(v):
        mx = jnp.max(v, axis=0, keepdims=True)
        idx = jnp.min(jnp.where(v == mx, row8, float(SUBLANES)), axis=0, keepdims=True)
        return mx, idx

    g_log = jnp.where(row8 < N_GROUPS, lt[0:SUBLANES, :], neg_inf)
    g_max, g_sel = first_argmax(g_log)
    g_gate = 1.0 / jnp.sum(jnp.exp(g_log - g_max), axis=0, keepdims=True)
    e_log = jnp.zeros((EPG, tm), f32)
    for g in range(N_GROUPS):
        rows = lt[EXPERT_ROW0 + g * EPG:EXPERT_ROW0 + (g + 1) * EPG, :]
        e_log = jnp.where(g_sel == float(g), rows, e_log)
    v1, i1 = first_argmax(e_log)
    v2, i2 = first_argmax(jnp.where(row8 == i1, neg_inf, e_log))
    tt = jnp.exp(v2 - v1)
    w1 = g_gate / (1.0 + tt)
    w2 = g_gate * tt / (1.0 + tt)
    e1 = g_sel * float(EPG) + i1
    e2 = g_sel * float(EPG) + i2

    row32 = lax.broadcasted_iota(jnp.int32, (N_EXPERTS, tm), 0).astype(f32)
    oh1 = row32 == e1
    oh2 = row32 == e2
    c = jnp.where(oh1 | oh2, 1.0, 0.0).astype(bf16)
    tok_r = lax.broadcasted_iota(jnp.int32, (tm, tm), 0)
    tok_c = lax.broadcasted_iota(jnp.int32, (tm, tm), 1)
    same_tile = (tok_r // t) == (tok_c // t)
    rank = _dot(c, jnp.where(same_tile & (tok_r < tok_c), 1.0, 0.0).astype(bf16))
    cnt_b = _dot(c, jnp.where(same_tile, 1.0, 0.0).astype(bf16))
    ex_r = lax.broadcasted_iota(jnp.int32, (N_EXPERTS, N_EXPERTS), 0)
    ex_c = lax.broadcasted_iota(jnp.int32, (N_EXPERTS, N_EXPERTS), 1)
    lower = jnp.where(ex_c < ex_r, 1.0, 0.0).astype(bf16)
    start_b = _dot(lower, cnt_b.astype(bf16))
    pos_e = start_b + rank
    p1 = jnp.sum(jnp.where(oh1, pos_e, 0.0), axis=0, keepdims=True)
    p2 = jnp.sum(jnp.where(oh2, pos_e, 0.0), axis=0, keepdims=True)

    srow = lax.broadcasted_iota(jnp.int32, (2 * t, t), 0).astype(f32)
    for s in range(tm // t):
        cols = slice(s * t, (s + 1) * t)
        perm = jnp.where((srow == p1[:, cols]) | (srow == p2[:, cols]), 1.0, 0.0).astype(bf16)
        xs = _dot(perm, hn_hi[cols, :])
        xs_ref[s * 2 * t:(s + 1) * 2 * t] = _pack_rows(xs)
        cnt_ref[s * N_EXPERTS:(s + 1) * N_EXPERTS, :] = cnt_b[:, s * t:s * t + LANES]

    meta = jnp.concatenate([p1, p2, w1, w2, jnp.zeros((LANES - 4, tm), f32)], axis=0)
    meta_ref[...] = meta.T


def _outproj(x2, attn_o, conv_o, wo, g2, wr2):
    n = x2.shape[0]
    tm = TM_OUT
    nsub = tm // T_SORT
    const = lambda i: (0, 0)

    return pl.pallas_call(
        _outproj_kernel,
        grid=(n // tm,),
        in_specs=[
            pl.BlockSpec((tm, D_MODEL), lambda i: (i, 0)),
            pl.BlockSpec((tm, ATTN_WIDTH), lambda i: (i, 0)),
            pl.BlockSpec((tm, CONV_WIDTH), lambda i: (i, 0)),
            pl.BlockSpec((D_MODEL, D_MODEL), const),
            pl.BlockSpec((1, D_MODEL), const),
            pl.BlockSpec((D_MODEL, 2 * ROUTER_ROWS), const),
        ],
        out_specs=[
            pl.BlockSpec((tm, D_MODEL), lambda i: (i, 0)),
            pl.BlockSpec((2 * tm, ROW_CHUNKS, LANES), lambda i: (i, 0, 0)),
            pl.BlockSpec((tm, LANES), lambda i: (i, 0)),
            pl.BlockSpec((nsub * N_EXPERTS, LANES), lambda i: (i, 0)),
        ],
        out_shape=[
            jax.ShapeDtypeStruct((n, D_MODEL), jnp.float32),
            jax.ShapeDtypeStruct((2 * n, ROW_CHUNKS, LANES), PACKED_DTYPE),
            jax.ShapeDtypeStruct((n, LANES), jnp.float32),
            jax.ShapeDtypeStruct((n // T_SORT * N_EXPERTS, LANES), jnp.float32),
        ],
        compiler_params=pltpu.CompilerParams(
            dimension_semantics=("arbitrary",), vmem_limit_bytes=VMEM_LIMIT),
        name="outproj",
    )(x2, attn_o, conv_o, wo, g2, wr2)


def _pack_rows(x):
    r = x.shape[0]
    w = pltpu.pack_elementwise([x[:, :ROW_WORDS], x[:, ROW_WORDS:]], packed_dtype=jnp.bfloat16)
    return pltpu.bitcast(w, PACKED_DTYPE).reshape(r, ROW_CHUNKS, LANES)


def _packed_zero_rows(r):
    z = jnp.zeros((r, ROW_CHUNKS, LANES), jnp.float32)
    w = pltpu.pack_elementwise([z, z], packed_dtype=jnp.bfloat16)
    return pltpu.bitcast(w, PACKED_DTYPE)


def _unpack_rows(u):
    r = u.shape[0]
    w = u.reshape(r, ROW_WORDS)
    lo = pltpu.unpack_elementwise(w, index=0, packed_dtype=jnp.bfloat16, unpacked_dtype=jnp.float32)
    hi = pltpu.unpack_elementwise(w, index=1, packed_dtype=jnp.bfloat16, unpacked_dtype=jnp.float32)
    return jnp.concatenate([lo, hi], axis=1).astype(jnp.bfloat16)


def _expert_kernel(nblk_ref, blk0_ref, be_ref, nused_ref, r0_ref, nvalid_ref, tlo_ref, thi_ref,
                   cnt_ref, src_ref, rbase_ref,
                   xs_hbm, wg_ref, wu_ref, wd_ref, y_hbm,
                   xbuf, ybuf, sem, ysem):
    e = pl.program_id(0)
    nused = nused_ref[0]
    n_blocks = be_ref.shape[0]
    n_tiles = cnt_ref.shape[0] // N_EXPERTS

    def start_run(blk, slot, t, live):
        r0 = r0_ref[blk]
        k = t * N_EXPERTS + be_ref[blk]
        lo = jnp.maximum(rbase_ref[k], r0)
        hi = jnp.minimum(rbase_ref[k] + cnt_ref[k], r0 + MOE_BLOCK)
        rows = jnp.where(live, hi - lo, 0)

        @pl.when(rows > 0)
        def _():
            pltpu.make_async_copy(xs_hbm.at[pl.ds(src_ref[k] + lo - rbase_ref[k], rows)],
                                  xbuf.at[slot, pl.ds(lo - r0, rows)], sem.at[slot]).start(priority=GATHER_PRIORITY)

    def gather_rolled(blk, slot, t_from):
        def run(t, carry):
            start_run(blk, slot, t, True)
            return carry
        lax.fori_loop(t_from, thi_ref[blk], run, 0)

    def gather_unrolled(blk, slot, live, part):
        blk = jnp.minimum(blk, n_blocks - 1)
        per = GATHER_UNROLL // GATHER_PARTS
        for r in range(part * per, (part + 1) * per):
            t = tlo_ref[blk] + r
            start_run(blk, slot, jnp.minimum(t, n_tiles - 1), live & (t < thi_ref[blk]))
        if part == GATHER_PARTS - 1:
            @pl.when(live & (tlo_ref[blk] + GATHER_UNROLL < thi_ref[blk]))
            def _():
                gather_rolled(blk, slot, tlo_ref[blk] + GATHER_UNROLL)

    def y_copy(blk, slot):
        return pltpu.make_async_copy(ybuf.at[slot], y_hbm.at[pl.ds(blk * MOE_BLOCK, MOE_BLOCK)], ysem.at[slot])

    @pl.when(e == 0)
    def _():
        for slot in range(N_XBUF):
            xbuf[slot] = _packed_zero_rows(MOE_BLOCK)
        gather_rolled(0, 0, tlo_ref[0])
        for blk in range(1, N_XBUF - 1):
            @pl.when(nused > blk)
            def _():
                gather_rolled(blk, blk, tlo_ref[blk])
        for slot in range(2):
            ybuf[slot] = _packed_zero_rows(MOE_BLOCK)
            y_copy(slot, slot).start()

    @pl.when(nblk_ref[e] > 0)
    def _():
        def block(b):
            slot = b % N_XBUF
            yslot = b % 2
            nv = nvalid_ref[b]
            y_copy(b, yslot).wait()
            pltpu.make_async_copy(xs_hbm.at[pl.ds(0, nv)], xbuf.at[slot, pl.ds(0, nv)], sem.at[slot]).wait()

            ahead = b + N_XBUF - 1
            x = _unpack_rows(xbuf[slot])
            g = _dot(x, wg_ref[0])
            gather_unrolled(ahead, ahead % N_XBUF, ahead < nused, 0)
            u = _dot(x, wu_ref[0])
            act = (g / (1.0 + jnp.exp(-g)) * u).astype(jnp.bfloat16)
            gather_unrolled(ahead, ahead % N_XBUF, ahead < nused, 1)
            y = _dot(act, wd_ref[0])
            ybuf[yslot] = _pack_rows(y)
            y_copy(b, yslot).start()
            gather_unrolled(ahead, ahead % N_XBUF, ahead < nused, 2)

        def loop_body(c, carry):
            block(blk0_ref[e] + c)
            return carry
        lax.fori_loop(0, nblk_ref[e], loop_body, 0)

    @pl.when(e == N_EXPERTS - 1)
    def _():
        for slot in range(2):
            y_copy(slot, slot).wait()
        ybuf[0] = _packed_zero_rows(MOE_BLOCK)

        def zero_block(b, carry):
            y_copy(b, 0).start()
            y_copy(b, 0).wait()
            return carry
        lax.fori_loop(nused, n_blocks, zero_block, 0)


def _experts(tabs, xs, w_gate, w_up, w_down):
    n_slots = tabs["n_slots"]
    w_map = lambda e, *_: (e, 0, 0)
    return pl.pallas_call(
        _expert_kernel,
        grid_spec=pltpu.PrefetchScalarGridSpec(
            num_scalar_prefetch=11,
            grid=(N_EXPERTS,),
            in_specs=[
                pl.BlockSpec(memory_space=pl.ANY),
                pl.BlockSpec((1, D_MODEL, D_FF), w_map),
                pl.BlockSpec((1, D_MODEL, D_FF), w_map),
                pl.BlockSpec((1, D_FF, D_MODEL), w_map),
            ],
            out_specs=pl.BlockSpec(memory_space=pl.ANY),
            scratch_shapes=[pltpu.VMEM((N_XBUF, MOE_BLOCK, ROW_CHUNKS, LANES), PACKED_DTYPE),
                            pltpu.VMEM((2, MOE_BLOCK, ROW_CHUNKS, LANES), PACKED_DTYPE),
                            pltpu.SemaphoreType.DMA((N_XBUF,)),
                            pltpu.SemaphoreType.DMA((2,))],
        ),
        out_shape=jax.ShapeDtypeStruct((n_slots, ROW_CHUNKS, LANES), PACKED_DTYPE),
        compiler_params=pltpu.CompilerParams(
            dimension_semantics=("arbitrary",), vmem_limit_bytes=VMEM_LIMIT),
        name="experts",
    )(tabs["nblk"], tabs["blk0"], tabs["block_e"], tabs["nused"], tabs["r0"], tabs["nvalid"], tabs["tlo"],
      tabs["thi"], tabs["cnt"], tabs["src"], tabs["rbase"], xs, w_gate, w_up, w_down)


def _combine_kernel(cnt_ref, loc_ref, dst_ref, h_ref, meta_ref, yg_hbm, o_ref, ybuf, sem, *, n_steps, nsub):
    i = pl.program_id(0)
    t = T_SORT
    f32, bf16 = jnp.float32, jnp.bfloat16

    slot_in = i % 2
    tile0 = jnp.minimum(i, n_steps - 1) * nsub
    for s in range(nsub):
        for e in range(N_EXPERTS):
            k = (tile0 + s) * N_EXPERTS + e
            rows = jnp.where(i < n_steps, cnt_ref[k], 0)

            @pl.when(rows > 0)
            def _():
                pltpu.make_async_copy(yg_hbm.at[pl.ds(dst_ref[k], rows)],
                                      ybuf.at[slot_in, pl.ds(s * 2 * t + loc_ref[k], rows)],
                                      sem.at[slot_in]).start(priority=GATHER_PRIORITY)

    @pl.when(i > 0)
    def _():
        slot = (i - 1) % 2
        pltpu.make_async_copy(yg_hbm.at[pl.ds(0, nsub * 2 * t)], ybuf.at[slot], sem.at[slot]).wait()
        lane = lax.broadcasted_iota(jnp.int32, (t, 2 * t), 1).astype(f32)
        for s in range(nsub):
            y = _unpack_rows(ybuf[slot, s * 2 * t:(s + 1) * 2 * t])
            meta = meta_ref[s * t:(s + 1) * t, :]
            pick1 = jnp.where(lane == meta[:, 0:1], 1.0, 0.0).astype(bf16)
            pick2 = jnp.where(lane == meta[:, 1:2], 1.0, 0.0).astype(bf16)
            y12 = _dot(jnp.concatenate([pick1, pick2], axis=0), y)
            o_ref[s * t:(s + 1) * t, :] = h_ref[s * t:(s + 1) * t, :] + (meta[:, 2:3] * y12[:t]
                                                                         + meta[:, 3:4] * y12[t:])


def _combine(cnt, loc, dst, h, meta, yg):
    n = h.shape[0]
    tm = TM_OUT
    nsub = tm // T_SORT
    n_steps = n // tm
    prev = lambda i, *_: (jnp.maximum(i - 1, 0), 0)
    return pl.pallas_call(
        functools.partial(_combine_kernel, n_steps=n_steps, nsub=nsub),
        grid_spec=pltpu.PrefetchScalarGridSpec(
            num_scalar_prefetch=3,
            grid=(n_steps + 1,),
            in_specs=[
                pl.BlockSpec((tm, D_MODEL), prev),
                pl.BlockSpec((tm, LANES), prev),
                pl.BlockSpec(memory_space=pl.ANY),
            ],
            out_specs=pl.BlockSpec((tm, D_MODEL), prev),
            scratch_shapes=[pltpu.VMEM((2, nsub * 2 * T_SORT, ROW_CHUNKS, LANES), PACKED_DTYPE),
                            pltpu.SemaphoreType.DMA((2,))],
        ),
        out_shape=jax.ShapeDtypeStruct((n, D_MODEL), jnp.float32),
        compiler_params=pltpu.CompilerParams(
            dimension_semantics=("arbitrary",), vmem_limit_bytes=VMEM_LIMIT),
        name="combine",
    )(cnt, loc, dst, h, meta, yg)


def _routing_tables(cnt_out, n_tiles, n_tok):
    i32 = jnp.int32
    cnt = cnt_out.reshape(n_tiles, N_EXPERTS, LANES)[:, :, 0].astype(i32)
    count = jnp.sum(cnt, axis=0)
    padded = ((count + MOE_BLOCK - 1) // MOE_BLOCK) * MOE_BLOCK
    pad_end = jnp.cumsum(padded)
    pad_start = pad_end - padded
    run_end = jnp.cumsum(cnt, axis=0)
    rbase = run_end - cnt
    dst = pad_start[None, :] + rbase
    loc = jnp.cumsum(cnt, axis=1) - cnt
    src = loc + (jnp.arange(n_tiles, dtype=i32) * (2 * T_SORT))[:, None]
    n_blocks = (2 * n_tok) // MOE_BLOCK + N_EXPERTS
    block_start = jnp.arange(n_blocks, dtype=i32) * MOE_BLOCK
    block_e = jnp.minimum(jnp.sum(pad_end[None, :] <= block_start[:, None], axis=1), N_EXPERTS - 1).astype(i32)
    nused = (pad_end[-1] // MOE_BLOCK).astype(i32).reshape(1)
    ex = jnp.arange(N_EXPERTS, dtype=i32)
    sel = (block_e[None, :] == ex[:, None]).astype(i32)
    pick = lambda per_expert: jnp.sum(per_expert[..., :, None] * sel, axis=-2)
    r0 = block_start - pick(pad_start)
    nvalid = jnp.clip(pick(count) - r0, 0, MOE_BLOCK)
    tlo = jnp.sum(pick(run_end) <= r0[None, :], axis=0)
    thi = jnp.sum(pick(rbase) < (r0 + MOE_BLOCK)[None, :], axis=0)
    flat = lambda a: a.reshape(-1).astype(i32)
    return dict(cnt=flat(cnt), src=flat(src), dst=flat(dst), loc=flat(loc), rbase=flat(rbase),
                block_e=block_e, nused=nused, r0=flat(r0), nvalid=flat(nvalid), tlo=flat(tlo), thi=flat(thi),
                nblk=flat(padded // MOE_BLOCK), blk0=flat(pad_start // MOE_BLOCK), n_slots=n_blocks * MOE_BLOCK)


def _stage1(x, positions, attn_norm_gain, w_in, q_norm_gain, k_norm_gain, conv_w, conv_out_gain):
    bsz, seq, _ = x.shape
    n = bsz * seq
    f32, bf16 = jnp.float32, jnp.bfloat16
    w = w_in[0]
    wb = w.astype(bf16)
    wvt = w[:, 2 * ATTN_WIDTH:3 * ATTN_WIDTH].T.astype(bf16)
    scale = DK ** -0.5 * LOG2E
    gqk = jnp.concatenate([jnp.tile(q_norm_gain[0].astype(f32), 2 * N_HEADS) * scale,
                           jnp.tile(k_norm_gain[0].astype(f32), 2 * N_HEADS)]).reshape(1, -1)
    freqs = (ROPE_THETA ** (-jnp.arange(0, ROT_DIM, 2, dtype=f32) / ROT_DIM)).reshape(SUBLANES, 1)
    return _inproj(x.reshape(n, D_MODEL), positions.reshape(1, n),
                   attn_norm_gain[0].reshape(1, -1).astype(f32), wb, wvt, wb, gqk, freqs,
                   conv_w[0].astype(f32), conv_out_gain[0].reshape(1, -1).astype(f32), seq)


def kernel(x, positions, attn_norm_gain, w_in, q_norm_gain, k_norm_gain, lambda_q1, lambda_k1, lambda_q2, lambda_k2, subln_gain, conv_w, conv_out_gain, w_out, ffn_norm_gain, w_group_router, w_expert_router, w_gate, w_up, w_down):
    bsz, seq, _ = x.shape
    n = bsz * seq
    f32, bf16 = jnp.float32, jnp.bfloat16
    assert TM_OUT % T_SORT == 0 and seq % TM_IN == 0 and seq % TQ == 0 and TQ % KV_FULL == 0
    q, k, vt, conv_o = _stage1(x, positions, attn_norm_gain, w_in, q_norm_gain, k_norm_gain,
                                conv_w, conv_out_gain)
    attn_o, wg_b, wu_b, wd_b = _attention(q, k, vt,
                        lambda_q1[0].reshape(1, -1).astype(f32), lambda_k1[0].reshape(1, -1).astype(f32),
                        lambda_q2[0].reshape(1, -1).astype(f32), lambda_k2[0].reshape(1, -1).astype(f32),
                        subln_gain[0].reshape(-1, 1).astype(f32), w_gate[0], w_up[0], w_down[0], bsz, seq)

    wr = jnp.concatenate([
        w_group_router[0].astype(f32), jnp.zeros((D_MODEL, EXPERT_ROW0 - N_GROUPS), f32),
        jnp.transpose(w_expert_router[0].astype(f32), (1, 0, 2)).reshape(D_MODEL, N_EXPERTS),
        jnp.zeros((D_MODEL, ROUTER_ROWS - EXPERT_ROW0 - N_EXPERTS), f32)], axis=1)
    wrh = wr.astype(bf16)
    wr2 = jnp.concatenate([wrh, (wr - wrh.astype(f32)).astype(bf16)], axis=1)
    h, xs, meta, cnt_out = _outproj(x.reshape(n, D_MODEL), attn_o, conv_o, w_out[0].astype(bf16),
                                    ffn_norm_gain[0].reshape(1, -1).astype(f32), wr2)

    tabs = _routing_tables(cnt_out, n // T_SORT, n)
    yg = _experts(tabs, xs, wg_b, wu_b, wd_b)
    out = _combine(tabs["cnt"], tabs["loc"], tabs["dst"], h, meta, yg)
    return out.reshape(x.shape)
```

```python
import functools
import math

import jax
import jax.numpy as jnp
from jax import lax
from jax.experimental import pallas as pl
from jax.experimental.pallas import tpu as pltpu

D_MODEL = 1024
N_HEADS = 4
DK = 64
DV = 128
ROT_DIM = 16
ROPE_THETA = 500000.0
ATTN_WIDTH = N_HEADS * DV
CONV_WIDTH = 512
NORM_EPS = 1e-6
LOG2E = 1.4426950408889634
LAMBDA_INIT = 0.8 - 0.6 * math.exp(-0.3 * 0)
N_GROUPS = 4
EPG = 8
N_EXPERTS = N_GROUPS * EPG
D_FF = 512
MOE_BLOCK = 256

LANES = 128
SUBLANES = 8
ROW_WORDS = D_MODEL // 2
ROW_CHUNKS = ROW_WORDS // LANES
PACKED_DTYPE = jnp.uint32

TM_IN = 1024
QK_CHUNK = 256
TQ = 512
HQ = TQ // 2
KV_FULL = 512
SUM_ROWS = 16
T_SORT = 256
TM_OUT = 512
N_XBUF = 3
GATHER_UNROLL = 24
GATHER_PARTS = 3
GATHER_PRIORITY = 1
VMEM_LIMIT = 48 * 1024 * 1024


def _nt_dot(a, b):
    return lax.dot_general(a, b, (((1,), (1,)), ((), ())), preferred_element_type=jnp.float32)


def _dot(a, b):
    return jnp.dot(a, b, preferred_element_type=jnp.float32)


def _split3(x):
    h = x.astype(jnp.bfloat16)
    r = x - h.astype(jnp.float32)
    m = r.astype(jnp.bfloat16)
    l = (r - m.astype(jnp.float32)).astype(jnp.bfloat16)
    return h, m, l


def _split2(x):
    h = x.astype(jnp.bfloat16)
    l = (x - h.astype(jnp.float32)).astype(jnp.bfloat16)
    return h, l


def _inproj_kernel(x_ref, pos_ref, g1_ref, wqk_ref, wvt_ref, wc_ref, gqk_ref, freq_ref,
                   cw_ref, cg_ref,
                   q_ref, k_ref, vt_ref, conv_ref,
                   carry_ref, *, tiles_per_seq):
    tm = x_ref.shape[0]
    tp = tm // 2
    i = pl.program_id(0)
    half = ROT_DIM // 2
    lane_r = lax.broadcasted_iota(jnp.int32, (LANES, SUBLANES), 0)
    f_c = lax.broadcasted_iota(jnp.int32, (LANES, SUBLANES), 1)
    in_rot = (lane_r % DK) < ROT_DIM
    expand = jnp.where(in_rot & ((lane_r % half) == f_c), 1.0, 0.0).astype(jnp.bfloat16)
    seg_r = lax.broadcasted_iota(jnp.int32, (QK_CHUNK, QK_CHUNK), 0) // DK
    seg_c = lax.broadcasted_iota(jnp.int32, (QK_CHUNK, QK_CHUNK), 1) // DK
    seg_mean = jnp.where(seg_r == seg_c, 1.0 / DK, 0.0).astype(jnp.bfloat16)
    d = lax.broadcasted_iota(jnp.int32, (tp, LANES), 1) % DK
    row = lax.broadcasted_iota(jnp.int32, (tp, CONV_WIDTH), 0)

    def normed(p):
        x = x_ref[p * tp:(p + 1) * tp, :]
        ms = jnp.mean(x * x, axis=-1, keepdims=True)
        return (x * lax.rsqrt(ms + NORM_EPS) * g1_ref[...]).astype(jnp.bfloat16)

    def rotary_tables(p):
        pos = pos_ref[:, p * tp:(p + 1) * tp].astype(jnp.float32)
        ang = freq_ref[...] * pos

        def to_rows(t):
            h, m, l = _split3(t)
            return (_dot(expand, h) + _dot(expand, m) + _dot(expand, l)).T

        cos_r, sin_r = to_rows(jnp.cos(ang)), to_rows(jnp.sin(ang))
        return (jnp.where(d < ROT_DIM, cos_r, 1.0),
                jnp.where(d < half, -sin_r, 0.0),
                jnp.where((d >= half) & (d < ROT_DIM), sin_r, 0.0))

    def qk_product(hn, c2):
        return _dot(hn, wqk_ref[:, c2 * QK_CHUNK:(c2 + 1) * QK_CHUNK])

    def qk_rows(p, c2, blk, tables):
        rows = slice(p * tp, (p + 1) * tp)
        cos_r, sin_lo, sin_hi = tables
        msq = _dot((blk * blk).astype(jnp.bfloat16), seg_mean)
        y2 = blk * lax.rsqrt(msq + NORM_EPS) * gqk_ref[:, c2 * QK_CHUNK:(c2 + 1) * QK_CHUNK]
        for c1 in range(QK_CHUNK // LANES):
            c = c2 * (QK_CHUNK // LANES) + c1
            y = y2[:, c1 * LANES:(c1 + 1) * LANES]
            rot = (y * cos_r
                   + pltpu.roll(y, LANES - half, 1) * sin_lo
                   + pltpu.roll(y, half, 1) * sin_hi).astype(jnp.bfloat16)
            if c < N_HEADS:
                q_ref[rows, c * LANES:(c + 1) * LANES] = rot
            else:
                k_ref[rows, (c - N_HEADS) * LANES:(c - N_HEADS + 1) * LANES] = rot

    def conv_product(hn, j):
        return _dot(hn, wc_ref[:, j * CONV_WIDTH:(j + 1) * CONV_WIDTH])

    def v_rows(p, hn):
        vt_ref[:, p * tp:(p + 1) * tp] = _nt_dot(wvt_ref[...], hn).astype(jnp.bfloat16)

    def conv_rows(p, cp, prev):
        cb, cc, cu = cp
        y = cc * cu
        p1 = prev[SUBLANES - 1:SUBLANES, :]
        p2 = prev[SUBLANES - 2:SUBLANES - 1, :]
        y1 = jnp.where(row == 0, p1, pltpu.roll(y, 1, 0))
        y2 = jnp.where(row == 0, p2, jnp.where(row == 1, p1, pltpu.roll(y, 2, 0)))
        z = cw_ref[0:1, :] * y2 + cw_ref[1:2, :] * y1 + cw_ref[2:3, :] * y
        co = cb * z
        cms = jnp.mean(co * co, axis=-1, keepdims=True)
        conv_ref[p * tp:(p + 1) * tp, :] = (co * lax.rsqrt(cms + NORM_EPS) * cg_ref[...]).astype(jnp.bfloat16)
        return y[tp - SUBLANES:, :]

    @pl.when(i % tiles_per_seq == 0)
    def _():
        carry_ref[...] = jnp.zeros_like(carry_ref)

    n_qk = 2 * ATTN_WIDTH // QK_CHUNK
    hn0, tables0 = normed(0), rotary_tables(0)
    qk0 = [qk_product(hn0, c2) for c2 in range(n_qk)]
    hn1, tables1 = normed(1), rotary_tables(1)
    qk1 = []
    for c2 in range(n_qk):
        qk1.append(qk_product(hn1, c2))
        qk_rows(0, c2, qk0[c2], tables0)
    cp0 = []
    for c2 in range(n_qk):
        if c2 < 3:
            cp0.append(conv_product(hn0, c2))
        qk_rows(1, c2, qk1[c2], tables1)
    cp1 = [conv_product(hn1, j) for j in range(3)]
    v_rows(0, hn0)
    v_rows(1, hn1)
    prev = conv_rows(0, cp0, carry_ref[...])
    carry_ref[...] = conv_rows(1, cp1, prev)


def _inproj(x2, pos_row, g1, wqk, wvt, wc, gqk, freqs, cw, cg, seq):
    n = x2.shape[0]
    tm = TM_IN
    grid = (n // tm,)
    const = lambda i: (0, 0)
    return pl.pallas_call(
        functools.partial(_inproj_kernel, tiles_per_seq=seq // tm),
        grid=grid,
        in_specs=[
            pl.BlockSpec((tm, D_MODEL), lambda i: (i, 0)),
            pl.BlockSpec((1, tm), lambda i: (0, i)),
            pl.BlockSpec((1, D_MODEL), const),
            pl.BlockSpec((D_MODEL, 2 * ATTN_WIDTH), const),
            pl.BlockSpec((ATTN_WIDTH, D_MODEL), const),
            pl.BlockSpec((D_MODEL, 3 * CONV_WIDTH), lambda i: (0, 1)),
            pl.BlockSpec((1, 1024), const),
            pl.BlockSpec((SUBLANES, 1), const),
            pl.BlockSpec((3, CONV_WIDTH), const),
            pl.BlockSpec((1, CONV_WIDTH), const),
        ],
        out_specs=[
            pl.BlockSpec((tm, ATTN_WIDTH), lambda i: (i, 0)),
            pl.BlockSpec((tm, ATTN_WIDTH), lambda i: (i, 0)),
            pl.BlockSpec((ATTN_WIDTH, tm), lambda i: (0, i)),
            pl.BlockSpec((tm, CONV_WIDTH), lambda i: (i, 0)),
        ],
        out_shape=[
            jax.ShapeDtypeStruct((n, ATTN_WIDTH), jnp.bfloat16),
            jax.ShapeDtypeStruct((n, ATTN_WIDTH), jnp.bfloat16),
            jax.ShapeDtypeStruct((ATTN_WIDTH, n), jnp.bfloat16),
            jax.ShapeDtypeStruct((n, CONV_WIDTH), jnp.bfloat16),
        ],
        scratch_shapes=[pltpu.VMEM((SUBLANES, CONV_WIDTH), jnp.float32)],
        compiler_params=pltpu.CompilerParams(
            dimension_semantics=("arbitrary",), vmem_limit_bytes=VMEM_LIMIT),
        name="inproj",
    )(x2, pos_row, g1, wqk, wvt, wc, gqk, freqs, cw, cg)


def _attn_kernel(q_ref, k_ref, vt_ref, lq1_ref, lk1_ref, lq2_ref, lk2_ref, sg_ref, wg_ref, wu_ref, wd_ref,
                 o_ref, wgb_ref, wub_ref, wdb_ref, acc_ref, *, nq):
    lam = (jnp.exp(jnp.sum(lq1_ref[...] * lk1_ref[...], axis=-1, keepdims=True))
           - jnp.exp(jnp.sum(lq2_ref[...] * lk2_ref[...], axis=-1, keepdims=True))
           + LAMBDA_INIT)
    tri =(lax.broadcasted_iota(jnp.int32, (HQ, HQ), 0)
           <= lax.broadcasted_iota(jnp.int32, (HQ, HQ), 1))
    row_d = lax.broadcasted_iota(jnp.int32, (LANES, HQ), 0)
    ones_rows = jnp.ones((SUM_ROWS, KV_FULL), jnp.bfloat16)

    def masked(s, n_tri):
        blocks = [jnp.where(tri, s[:, c * HQ:(c + 1) * HQ], -jnp.inf) for c in range(n_tri)]
        return jnp.concatenate(blocks + [s[:, n_tri * HQ:]], axis=1) if n_tri * HQ < s.shape[1] else \
            jnp.concatenate(blocks, axis=1)

    todo = {}
    for t in range(nq):
        lo = t * TQ
        todo[t] = ([(lo, lo + HQ, "mixed"), (lo + HQ, lo + TQ, "upper")]
                   + [(k0, k0 + KV_FULL, "full") for k0 in range(0, lo, KV_FULL)])
    tasks = []
    while any(todo.values()):
        for t in reversed(range(nq)):
            if todo[t]:
                tasks.append((t,) + todo[t].pop(0))
    last_task = {t: max(i for i, tk in enumerate(tasks) if tk[0] == t) for t in range(nq)}
    qzt, m_run = {}, {}

    def q_transposed(t):
        parts = []
        for half in range(2):
            r0 = t * TQ + half * HQ
            qt = q_ref[r0:r0 + HQ, :].astype(jnp.float32).T
            parts.append(jnp.where(row_d < DK, qt, 0.0))
            parts.append(jnp.where(row_d >= DK, qt, 0.0))
        return jnp.concatenate(parts, axis=1).astype(jnp.bfloat16)

    def scores(task):
        t, k0, k1, kind = task
        if t not in qzt:
            qzt[t] = q_transposed(t)
        kb = k_ref[k0:k1, :]
        if kind == "upper":
            return masked(_dot(kb, qzt[t][:, 2 * HQ:]), 2)
        s = _dot(kb, qzt[t])
        return masked(s, 2) if kind == "mixed" else s

    def softmax(task, s):
        t, _, _, kind = task
        mx = jnp.max(s, axis=0, keepdims=True)
        if kind == "mixed":
            m_run[t] = mx
            return jnp.exp2(s - mx).astype(jnp.bfloat16), None
        m_old = m_run[t][:, 2 * HQ:] if kind == "upper" else m_run[t]
        m_new = jnp.maximum(m_old, mx)
        alpha = jnp.exp2(m_old - m_new)
        p = jnp.exp2(s - m_new)
        m_run[t] = jnp.concatenate([m_run[t][:, :2 * HQ], m_new], axis=1) if kind == "upper" else m_new
        return p.astype(jnp.bfloat16), alpha

    def accumulate(task, p, alpha):
        t, k0, k1, kind = task
        vta = jnp.concatenate([vt_ref[:, k0:k1], ones_rows[:, :k1 - k0]], axis=0)
        pv = _dot(vta, p)
        if kind == "mixed":
            acc_ref[t] = pv
        elif kind == "upper":
            acc_ref[t, :, 2 * HQ:] = alpha * acc_ref[t, :, 2 * HQ:] + pv
        else:
            acc_ref[t] = alpha * acc_ref[t] + pv

    def finish(t):
        o_all = acc_ref[t, 0:DV, :] * (1.0 / acc_ref[t, DV:DV + 1, :])
        for half in range(2):
            o = (o_all[:, (2 * half) * HQ:(2 * half + 1) * HQ]
                 - lam * o_all[:, (2 * half + 1) * HQ:(2 * half + 2) * HQ])
            ms = jnp.mean(o * o, axis=0, keepdims=True)
            on = o * lax.rsqrt(ms + NORM_EPS) * sg_ref[...] * (1.0 - LAMBDA_INIT)
            r0 = t * TQ + half * HQ
            o_ref[r0:r0 + HQ, :] = on.T.astype(jnp.bfloat16)

    n = len(tasks)

    def cast_share(step):
        for src, dst in ((wg_ref, wgb_ref), (wu_ref, wub_ref), (wd_ref, wdb_ref)):
            rows = src.shape[1]
            per = -(-rows // (n + 2) // SUM_ROWS) * SUM_ROWS
            r0, r1 = min(step * per, rows), min((step + 1) * per, rows)
            if r1 > r0:
                dst[:, r0:r1, :] = src[:, r0:r1, :].astype(jnp.bfloat16)

    s_prev, p_prev = None, None
    for step in range(n + 2):
        cast_share(step)
        if step >= 2:
            accumulate(tasks[step - 2], *p_prev)
            if last_task[tasks[step - 2][0]] == step - 2:
                finish(tasks[step - 2][0])
        if 1 <= step <= n:
            p_prev = softmax(tasks[step - 1], s_prev)
        if step < n:
            s_prev = scores(tasks[step])


def _attention(q, k, vt, lq1, lk1, lq2, lk2, sg_col, w_gate, w_up, w_down, bsz, seq):
    n = q.shape[0]
    nq = seq // TQ
    steps = bsz * N_HEADS
    assert N_EXPERTS % steps == 0
    epw = N_EXPERTS // steps
    vec = lambda b, h: (0, 0)
    w_map = lambda b, h: (b * N_HEADS + h, 0, 0)
    bf16 = jnp.bfloat16
    return pl.pallas_call(
        functools.partial(_attn_kernel, nq=nq),
        grid=(bsz, N_HEADS),
        in_specs=[
            pl.BlockSpec((seq, DV), lambda b, h: (b, h)),
            pl.BlockSpec((seq, DV), lambda b, h: (b, h)),
            pl.BlockSpec((DV, seq), lambda b, h: (h, b)),
            pl.BlockSpec((1, DK), vec), pl.BlockSpec((1, DK), vec),
            pl.BlockSpec((1, DK), vec), pl.BlockSpec((1, DK), vec),
            pl.BlockSpec((DV, 1), vec),
            pl.BlockSpec((epw, D_MODEL, D_FF), w_map),
            pl.BlockSpec((epw, D_MODEL, D_FF), w_map),
            pl.BlockSpec((epw, D_FF, D_MODEL), w_map),
        ],
        out_specs=[
            pl.BlockSpec((seq, DV), lambda b, h: (b, h)),
            pl.BlockSpec((epw, D_MODEL, D_FF), w_map),
            pl.BlockSpec((epw, D_MODEL, D_FF), w_map),
            pl.BlockSpec((epw, D_FF, D_MODEL), w_map),
        ],
        out_shape=[
            jax.ShapeDtypeStruct((n, ATTN_WIDTH), bf16),
            jax.ShapeDtypeStruct((N_EXPERTS, D_MODEL, D_FF), bf16),
            jax.ShapeDtypeStruct((N_EXPERTS, D_MODEL, D_FF), bf16),
            jax.ShapeDtypeStruct((N_EXPERTS, D_FF, D_MODEL), bf16),
        ],
        scratch_shapes=[pltpu.VMEM((nq, DV + SUM_ROWS, 4 * HQ), jnp.float32)],
        compiler_params=pltpu.CompilerParams(
            dimension_semantics=("arbitrary", "arbitrary"), vmem_limit_bytes=VMEM_LIMIT),
        name="attn",
    )(q, k, vt, lq1, lk1, lq2, lk2, sg_col, w_gate, w_up, w_down)


ROUTER_ROWS = 128
EXPERT_ROW0 = 32


def _outproj_kernel(x_ref, attn_ref, conv_ref, wo_ref, g2_ref, wr2_ref,
                    h_ref, xs_ref, meta_ref, cnt_ref):
    tm = x_ref.shape[0]
    t = T_SORT
    f32, bf16 = jnp.float32, jnp.bfloat16
    a = jnp.concatenate([attn_ref[...], conv_ref[...]], axis=1)
    h = x_ref[...] + _dot(a, wo_ref[...])
    h_ref[...] = h
    ms = jnp.mean(h * h, axis=-1, keepdims=True)
    hn = h * lax.rsqrt(ms + NORM_EPS) * g2_ref[...]
    hn_hi, hn_lo = _split2(hn)

    hh = _dot(hn_hi, wr2_ref[...])
    logits = hh[:, :ROUTER_ROWS] + hh[:, ROUTER_ROWS:] + _dot(hn_lo, wr2_ref[:, :ROUTER_ROWS])
    lt = logits.T
    row8 = lax.broadcasted_iota(jnp.int32, (SUBLANES, tm), 0).astype(f32)
    neg_inf = -jnp.inf

    def first_argmax(v):
        mx = jnp.max(v, axis=0, keepdims=True)
        idx = jnp.min(jnp.where(v == mx, row8, float(SUBLANES)), axis=0, keepdims=True)
        return mx, idx

    g_log = jnp.where(row8 < N_GROUPS, lt[0:SUBLANES, :], neg_inf)
    g_max, g_sel = first_argmax(g_log)
    g_gate = 1.0 / jnp.sum(jnp.exp(g_log - g_max), axis=0, keepdims=True)
    e_log = jnp.zeros((EPG, tm), f32)
    for g in range(N_GROUPS):
        rows = lt[EXPERT_ROW0 + g * EPG:EXPERT_ROW0 + (g + 1) * EPG, :]
        e_log = jnp.where(g_sel == float(g), rows, e_log)
    v1, i1 = first_argmax(e_log)
    v2, i2 = first_argmax(jnp.where(row8 == i1, neg_inf, e_log))
    tt = jnp.exp(v2 - v1)
    w1 = g_gate / (1.0 + tt)
    w2 = g_gate * tt / (1.0 + tt)
    e1 = g_sel * float(EPG) + i1
    e2 = g_sel * float(EPG) + i2

    row32 = lax.broadcasted_iota(jnp.int32, (N_EXPERTS, tm), 0).astype(f32)
    oh1 = row32 == e1
    oh2 = row32 == e2
    c = jnp.where(oh1 | oh2, 1.0, 0.0).astype(bf16)
    tok_r = lax.broadcasted_iota(jnp.int32, (tm, tm), 0)
    tok_c = lax.broadcasted_iota(jnp.int32, (tm, tm), 1)
    same_tile = (tok_r // t) == (tok_c // t)
    rank = _dot(c, jnp.where(same_tile & (tok_r < tok_c), 1.0, 0.0).astype(bf16))
    cnt_b = _dot(c, jnp.where(same_tile, 1.0, 0.0).astype(bf16))
    ex_r = lax.broadcasted_iota(jnp.int32, (N_EXPERTS, N_EXPERTS), 0)
    ex_c = lax.broadcasted_iota(jnp.int32, (N_EXPERTS, N_EXPERTS), 1)
    lower = jnp.where(ex_c < ex_r, 1.0, 0.0).astype(bf16)
    start_b = _dot(lower, cnt_b.astype(bf16))
    pos_e = start_b + rank
    p1 = jnp.sum(jnp.where(oh1, pos_e, 0.0), axis=0, keepdims=True)
    p2 = jnp.sum(jnp.where(oh2, pos_e, 0.0), axis=0, keepdims=True)

    srow = lax.broadcasted_iota(jnp.int32, (2 * t, t), 0).astype(f32)
    for s in range(tm // t):
        cols = slice(s * t, (s + 1) * t)
        perm = jnp.where((srow == p1[:, cols]) | (srow == p2[:, cols]), 1.0, 0.0).astype(bf16)
        xs = _dot(perm, hn_hi[cols, :])
        xs_ref[s * 2 * t:(s + 1) * 2 * t] = _pack_rows(xs)
        cnt_ref[s * N_EXPERTS:(s + 1) * N_EXPERTS, :] = cnt_b[:, s * t:s * t + LANES]

    meta = jnp.concatenate([p1, p2, w1, w2, jnp.zeros((LANES - 4, tm), f32)], axis=0)
    meta_ref[...] = meta.T


def _outproj(x2, attn_o, conv_o, wo, g2, wr2):
    n = x2.shape[0]
    tm = TM_OUT
    nsub = tm // T_SORT
    const = lambda i: (0, 0)

    return pl.pallas_call(
        _outproj_kernel,
        grid=(n // tm,),
        in_specs=[
            pl.BlockSpec((tm, D_MODEL), lambda i: (i, 0)),
            pl.BlockSpec((tm, ATTN_WIDTH), lambda i: (i, 0)),
            pl.BlockSpec((tm, CONV_WIDTH), lambda i: (i, 0)),
            pl.BlockSpec((D_MODEL, D_MODEL), const),
            pl.BlockSpec((1, D_MODEL), const),
            pl.BlockSpec((D_MODEL, 2 * ROUTER_ROWS), const),
        ],
        out_specs=[
            pl.BlockSpec((tm, D_MODEL), lambda i: (i, 0)),
            pl.BlockSpec((2 * tm, ROW_CHUNKS, LANES), lambda i: (i, 0, 0)),
            pl.BlockSpec((tm, LANES), lambda i: (i, 0)),
            pl.BlockSpec((nsub * N_EXPERTS, LANES), lambda i: (i, 0)),
        ],
        out_shape=[
            jax.ShapeDtypeStruct((n, D_MODEL), jnp.float32),
            jax.ShapeDtypeStruct((2 * n, ROW_CHUNKS, LANES), PACKED_DTYPE),
            jax.ShapeDtypeStruct((n, LANES), jnp.float32),
            jax.ShapeDtypeStruct((n // T_SORT * N_EXPERTS, LANES), jnp.float32),
        ],
        compiler_params=pltpu.CompilerParams(
            dimension_semantics=("arbitrary",), vmem_limit_bytes=VMEM_LIMIT),
        name="outproj",
    )(x2, attn_o, conv_o, wo, g2, wr2)


def _pack_rows(x):
    r = x.shape[0]
    w = pltpu.pack_elementwise([x[:, :ROW_WORDS], x[:, ROW_WORDS:]], packed_dtype=jnp.bfloat16)
    return pltpu.bitcast(w, PACKED_DTYPE).reshape(r, ROW_CHUNKS, LANES)


def _packed_zero_rows(r):
    z = jnp.zeros((r, ROW_CHUNKS, LANES), jnp.float32)
    w = pltpu.pack_elementwise([z, z], packed_dtype=jnp.bfloat16)
    return pltpu.bitcast(w, PACKED_DTYPE)


def _unpack_rows(u):
    r = u.shape[0]
    w = u.reshape(r, ROW_WORDS)
    lo = pltpu.unpack_elementwise(w, index=0, packed_dtype=jnp.bfloat16, unpacked_dtype=jnp.float32)
    hi = pltpu.unpack_elementwise(w, index=1, packed_dtype=jnp.bfloat16, unpacked_dtype=jnp.float32)
    return jnp.concatenate([lo, hi], axis=1).astype(jnp.bfloat16)


def _expert_kernel(nblk_ref, blk0_ref, be_ref, nused_ref, r0_ref, nvalid_ref, tlo_ref, thi_ref,
                   cnt_ref, src_ref, rbase_ref,
                   xs_hbm, wg_ref, wu_ref, wd_ref, y_hbm,
                   xbuf, ybuf, sem, ysem):
    e = pl.program_id(0)
    nused = nused_ref[0]
    n_blocks = be_ref.shape[0]
    n_tiles = cnt_ref.shape[0] // N_EXPERTS

    def start_run(blk, slot, t, live):
        r0 = r0_ref[blk]
        k = t * N_EXPERTS + be_ref[blk]
        lo = jnp.maximum(rbase_ref[k], r0)
        hi = jnp.minimum(rbase_ref[k] + cnt_ref[k], r0 + MOE_BLOCK)
        rows = jnp.where(live, hi - lo, 0)

        @pl.when(rows > 0)
        def _():
            pltpu.make_async_copy(xs_hbm.at[pl.ds(src_ref[k] + lo - rbase_ref[k], rows)],
                                  xbuf.at[slot, pl.ds(lo - r0, rows)], sem.at[slot]).start(priority=GATHER_PRIORITY)

    def gather_rolled(blk, slot, t_from):
        def run(t, carry):
            start_run(blk, slot, t, True)
            return carry
        lax.fori_loop(t_from, thi_ref[blk], run, 0)

    def gather_unrolled(blk, slot, live, part):
        blk = jnp.minimum(blk, n_blocks - 1)
        per = GATHER_UNROLL // GATHER_PARTS
        for r in range(part * per, (part + 1) * per):
            t = tlo_ref[blk] + r
            start_run(blk, slot, jnp.minimum(t, n_tiles - 1), live & (t < thi_ref[blk]))
        if part == GATHER_PARTS - 1:
            @pl.when(live & (tlo_ref[blk] + GATHER_UNROLL < thi_ref[blk]))
            def _():
                gather_rolled(blk, slot, tlo_ref[blk] + GATHER_UNROLL)

    def y_copy(blk, slot):
        return pltpu.make_async_copy(ybuf.at[slot], y_hbm.at[pl.ds(blk * MOE_BLOCK, MOE_BLOCK)], ysem.at[slot])

    @pl.when(e == 0)
    def _():
        for slot in range(N_XBUF):
            xbuf[slot] = _packed_zero_rows(MOE_BLOCK)
        gather_rolled(0, 0, tlo_ref[0])
        for blk in range(1, N_XBUF - 1):
            @pl.when(nused > blk)
            def _():
                gather_rolled(blk, blk, tlo_ref[blk])
        for slot in range(2):
            ybuf[slot] = _packed_zero_rows(MOE_BLOCK)
            y_copy(slot, slot).start()

    @pl.when(nblk_ref[e] > 0)
    def _():
        def block(b):
            slot = b % N_XBUF
            yslot = b % 2
            nv = nvalid_ref[b]
            y_copy(b, yslot).wait()
            pltpu.make_async_copy(xs_hbm.at[pl.ds(0, nv)], xbuf.at[slot, pl.ds(0, nv)], sem.at[slot]).wait()

            ahead = b + N_XBUF - 1
            x = _unpack_rows(xbuf[slot])
            g = _dot(x, wg_ref[0])
            gather_unrolled(ahead, ahead % N_XBUF, ahead < nused, 0)
            u = _dot(x, wu_ref[0])
            act = (g / (1.0 + jnp.exp(-g)) * u).astype(jnp.bfloat16)
            gather_unrolled(ahead, ahead % N_XBUF, ahead < nused, 1)
            y = _dot(act, wd_ref[0])
            ybuf[yslot] = _pack_rows(y)
            y_copy(b, yslot).start()
            gather_unrolled(ahead, ahead % N_XBUF, ahead < nused, 2)

        def loop_body(c, carry):
            block(blk0_ref[e] + c)
            return carry
        lax.fori_loop(0, nblk_ref[e], loop_body, 0)

    @pl.when(e == N_EXPERTS - 1)
    def _():
        for slot in range(2):
            y_copy(slot, slot).wait()
        ybuf[0] = _packed_zero_rows(MOE_BLOCK)

        def zero_block(b, carry):
            y_copy(b, 0).start()
            y_copy(b, 0).wait()
            return carry
        lax.fori_loop(nused, n_blocks, zero_block, 0)


def _experts(tabs, xs, w_gate, w_up, w_down):
    n_slots = tabs["n_slots"]
    w_map = lambda e, *_: (e, 0, 0)
    return pl.pallas_call(
        _expert_kernel,
        grid_spec=pltpu.PrefetchScalarGridSpec(
            num_scalar_prefetch=11,
            grid=(N_EXPERTS,),
            in_specs=[
                pl.BlockSpec(memory_space=pl.ANY),
                pl.BlockSpec((1, D_MODEL, D_FF), w_map),
                pl.BlockSpec((1, D_MODEL, D_FF), w_map),
                pl.BlockSpec((1, D_FF, D_MODEL), w_map),
            ],
            out_specs=pl.BlockSpec(memory_space=pl.ANY),
            scratch_shapes=[pltpu.VMEM((N_XBUF, MOE_BLOCK, ROW_CHUNKS, LANES), PACKED_DTYPE),
                            pltpu.VMEM((2, MOE_BLOCK, ROW_CHUNKS, LANES), PACKED_DTYPE),
                            pltpu.SemaphoreType.DMA((N_XBUF,)),
                            pltpu.SemaphoreType.DMA((2,))],
        ),
        out_shape=jax.ShapeDtypeStruct((n_slots, ROW_CHUNKS, LANES), PACKED_DTYPE),
        compiler_params=pltpu.CompilerParams(
            dimension_semantics=("arbitrary",), vmem_limit_bytes=VMEM_LIMIT),
        name="experts",
    )(tabs["nblk"], tabs["blk0"], tabs["block_e"], tabs["nused"], tabs["r0"], tabs["nvalid"], tabs["tlo"],
      tabs["thi"], tabs["cnt"], tabs["src"], tabs["rbase"], xs, w_gate, w_up, w_down)


def _combine_kernel(cnt_ref, loc_ref, dst_ref, h_ref, meta_ref, yg_hbm, o_ref, ybuf, sem, *, n_steps, nsub):
    i = pl.program_id(0)
    t = T_SORT
    f32, bf16 = jnp.float32, jnp.bfloat16

    slot_in = i % 2
    tile0 = jnp.minimum(i, n_steps - 1) * nsub
    for s in range(nsub):
        for e in range(N_EXPERTS):
            k = (tile0 + s) * N_EXPERTS + e
            rows = jnp.where(i < n_steps, cnt_ref[k], 0)

            @pl.when(rows > 0)
            def _():
                pltpu.make_async_copy(yg_hbm.at[pl.ds(dst_ref[k], rows)],
                                      ybuf.at[slot_in, pl.ds(s * 2 * t + loc_ref[k], rows)],
                                      sem.at[slot_in]).start(priority=GATHER_PRIORITY)

    @pl.when(i > 0)
    def _():
        slot = (i - 1) % 2
        pltpu.make_async_copy(yg_hbm.at[pl.ds(0, nsub * 2 * t)], ybuf.at[slot], sem.at[slot]).wait()
        lane = lax.broadcasted_iota(jnp.int32, (t, 2 * t), 1).astype(f32)
        for s in range(nsub):
            y = _unpack_rows(ybuf[slot, s * 2 * t:(s + 1) * 2 * t])
            meta = meta_ref[s * t:(s + 1) * t, :]
            pick1 = jnp.where(lane == meta[:, 0:1], 1.0, 0.0).astype(bf16)
            pick2 = jnp.where(lane == meta[:, 1:2], 1.0, 0.0).astype(bf16)
            y12 = _dot(jnp.concatenate([pick1, pick2], axis=0), y)
            o_ref[s * t:(s + 1) * t, :] = h_ref[s * t:(s + 1) * t, :] + (meta[:, 2:3] * y12[:t]
                                                                         + meta[:, 3:4] * y12[t:])


def _combine(cnt, loc, dst, h, meta, yg):
    n = h.shape[0]
    tm = TM_OUT
    nsub = tm // T_SORT
    n_steps = n // tm
    prev = lambda i, *_: (jnp.maximum(i - 1, 0), 0)
    return pl.pallas_call(
        functools.partial(_combine_kernel, n_steps=n_steps, nsub=nsub),
        grid_spec=pltpu.PrefetchScalarGridSpec(
            num_scalar_prefetch=3,
            grid=(n_steps + 1,),
            in_specs=[
                pl.BlockSpec((tm, D_MODEL), prev),
                pl.BlockSpec((tm, LANES), prev),
                pl.BlockSpec(memory_space=pl.ANY),
            ],
            out_specs=pl.BlockSpec((tm, D_MODEL), prev),
            scratch_shapes=[pltpu.VMEM((2, nsub * 2 * T_SORT, ROW_CHUNKS, LANES), PACKED_DTYPE),
                            pltpu.SemaphoreType.DMA((2,))],
        ),
        out_shape=jax.ShapeDtypeStruct((n, D_MODEL), jnp.float32),
        compiler_params=pltpu.CompilerParams(
            dimension_semantics=("arbitrary",), vmem_limit_bytes=VMEM_LIMIT),
        name="combine",
    )(cnt, loc, dst, h, meta, yg)


def _routing_tables(cnt_out, n_tiles, n_tok):
    i32 = jnp.int32
    cnt = cnt_out.reshape(n_tiles, N_EXPERTS, LANES)[:, :, 0].astype(i32)
    count = jnp.sum(cnt, axis=0)
    padded = ((count + MOE_BLOCK - 1) // MOE_BLOCK) * MOE_BLOCK
    pad_end = jnp.cumsum(padded)
    pad_start = pad_end - padded
    run_end = jnp.cumsum(cnt, axis=0)
    rbase = run_end - cnt
    dst = pad_start[None, :] + rbase
    loc = jnp.cumsum(cnt, axis=1) - cnt
    src = loc + (jnp.arange(n_tiles, dtype=i32) * (2 * T_SORT))[:, None]
    n_blocks = (2 * n_tok) // MOE_BLOCK + N_EXPERTS
    block_start = jnp.arange(n_blocks, dtype=i32) * MOE_BLOCK
    block_e = jnp.minimum(jnp.sum(pad_end[None, :] <= block_start[:, None], axis=1), N_EXPERTS - 1).astype(i32)
    nused = (pad_end[-1] // MOE_BLOCK).astype(i32).reshape(1)
    ex = jnp.arange(N_EXPERTS, dtype=i32)
    sel = (block_e[None, :] == ex[:, None]).astype(i32)
    pick = lambda per_expert: jnp.sum(per_expert[..., :, None] * sel, axis=-2)
    r0 = block_start - pick(pad_start)
    nvalid = jnp.clip(pick(count) - r0, 0, MOE_BLOCK)
    tlo = jnp.sum(pick(run_end) <= r0[None, :], axis=0)
    thi = jnp.sum(pick(rbase) < (r0 + MOE_BLOCK)[None, :], axis=0)
    flat = lambda a: a.reshape(-1).astype(i32)
    return dict(cnt=flat(cnt), src=flat(src), dst=flat(dst), loc=flat(loc), rbase=flat(rbase),
                block_e=block_e, nused=nused, r0=flat(r0), nvalid=flat(nvalid), tlo=flat(tlo), thi=flat(thi),
                nblk=flat(padded // MOE_BLOCK), blk0=flat(pad_start // MOE_BLOCK), n_slots=n_blocks * MOE_BLOCK)


def _stage1(x, positions, attn_norm_gain, w_in, q_norm_gain, k_norm_gain, conv_w, conv_out_gain):
    bsz, seq, _ = x.shape
    n = bsz * seq
    f32, bf16 = jnp.float32, jnp.bfloat16
    w = w_in[0]
    wb = w.astype(bf16)
    wvt = w[:, 2 * ATTN_WIDTH:3 * ATTN_WIDTH].T.astype(bf16)
    scale = DK ** -0.5 * LOG2E
    gqk = jnp.concatenate([jnp.tile(q_norm_gain[0].astype(f32), 2 * N_HEADS) * scale,
                           jnp.tile(k_norm_gain[0].astype(f32), 2 * N_HEADS)]).reshape(1, -1)
    freqs = (ROPE_THETA ** (-jnp.arange(0, ROT_DIM, 2, dtype=f32) / ROT_DIM)).reshape(SUBLANES, 1)
    return _inproj(x.reshape(n, D_MODEL), positions.reshape(1, n),
                   attn_norm_gain[0].reshape(1, -1).astype(f32), wb, wvt, wb, gqk, freqs,
                   conv_w[0].astype(f32), conv_out_gain[0].reshape(1, -1).astype(f32), seq)


def kernel(x, positions, attn_norm_gain, w_in, q_norm_gain, k_norm_gain, lambda_q1, lambda_k1, lambda_q2, lambda_k2, subln_gain, conv_w, conv_out_gain, w_out, ffn_norm_gain, w_group_router, w_expert_router, w_gate, w_up, w_down):
    bsz, seq, _ = x.shape
    n = bsz * seq
    f32, bf16 = jnp.float32, jnp.bfloat16
    assert TM_OUT % T_SORT == 0 and seq % TM_IN == 0 and seq % TQ == 0 and TQ % KV_FULL == 0
    q, k, vt, conv_o = _stage1(x, positions, attn_norm_gain, w_in, q_norm_gain, k_norm_gain,
                                conv_w, conv_out_gain)
    attn_o, wg_b, wu_b, wd_b = _attention(q, k, vt,
                        lambda_q1[0].reshape(1, -1).astype(f32), lambda_k1[0].reshape(1, -1).astype(f32),
                        lambda_q2[0].reshape(1, -1).astype(f32), lambda_k2[0].reshape(1, -1).astype(f32),
                        subln_gain[0].reshape(-1, 1).astype(f32), w_gate[0], w_up[0], w_down[0], bsz, seq)

    wr = jnp.concatenate([
        w_group_router[0].astype(f32), jnp.zeros((D_MODEL, EXPERT_ROW0 - N_GROUPS), f32),
        jnp.transpose(w_expert_router[0].astype(f32), (1, 0, 2)).reshape(D_MODEL, N_EXPERTS),
        jnp.zeros((D_MODEL, ROUTER_ROWS - EXPERT_ROW0 - N_EXPERTS), f32)], axis=1)
    wrh = wr.astype(bf16)
    wr2 = jnp.concatenate([wrh, (wr - wrh.astype(f32)).astype(bf16)], axis=1)
    h, xs, meta, cnt_out = _outproj(x.reshape(n, D_MODEL), attn_o, conv_o, w_out[0].astype(bf16),
                                    ffn_norm_gain[0].reshape(1, -1).astype(f32), wr2)

    tabs = _routing_tables(cnt_out, n // T_SORT, n)
    yg = _experts(tabs, xs, wg_b, wu_b, wd_b)
    out = _combine(tabs["cnt"], tabs["loc"], tabs["dst"], h, meta, yg)
    return out.reshape(x.shape)
```

```python
import functools
import math

import jax
import jax.numpy as jnp
from jax import lax
from jax.experimental import pallas as pl
from jax.experimental.pallas import tpu as pltpu

D_MODEL = 1024
N_HEADS = 4
DK = 64
DV = 128
ROT_DIM = 16
ROPE_THETA = 500000.0
ATTN_WIDTH = N_HEADS * DV
CONV_WIDTH = 512
NORM_EPS = 1e-6
LOG2E = 1.4426950408889634
LAMBDA_INIT = 0.8 - 0.6 * math.exp(-0.3 * 0)
N_GROUPS = 4
EPG = 8
N_EXPERTS = N_GROUPS * EPG
D_FF = 512
MOE_BLOCK = 256

LANES = 128
SUBLANES = 8
ROW_WORDS = D_MODEL // 2
ROW_CHUNKS = ROW_WORDS // LANES
PACKED_DTYPE = jnp.uint32

TM_IN = 1024
QK_CHUNK = 256
TQ = 512
HQ = TQ // 2
KV_FULL = 512
SUM_ROWS = 16
T_SORT = 256
TM_OUT = 512
N_XBUF = 3
N_YBUF = 3
GATHER_UNROLL = 24
GATHER_PARTS = 3
GATHER_PRIORITY = 1
VMEM_LIMIT = 48 * 1024 * 1024


def _nt_dot(a, b):
    return lax.dot_general(a, b, (((1,), (1,)), ((), ())), preferred_element_type=jnp.float32)


def _dot(a, b):
    return jnp.dot(a, b, preferred_element_type=jnp.float32)


def _split3(x):
    h = x.astype(jnp.bfloat16)
    r = x - h.astype(jnp.float32)
    m = r.astype(jnp.bfloat16)
    l = (r - m.astype(jnp.float32)).astype(jnp.bfloat16)
    return h, m, l


def _split2(x):
    h = x.astype(jnp.bfloat16)
    l = (x - h.astype(jnp.float32)).astype(jnp.bfloat16)
    return h, l


def _inproj_kernel(x_ref, pos_ref, g1_ref, wqk_ref, wvt_ref, wc_ref, gqk_ref, freq_ref,
                   cw_ref, cg_ref,
                   q_ref, k_ref, vt_ref, conv_ref,
                   carry_ref, *, tiles_per_seq):
    tm = x_ref.shape[0]
    tp = tm // 2
    i = pl.program_id(0)
    half = ROT_DIM // 2
    lane_r = lax.broadcasted_iota(jnp.int32, (LANES, SUBLANES), 0)
    f_c = lax.broadcasted_iota(jnp.int32, (LANES, SUBLANES), 1)
    in_rot = (lane_r % DK) < ROT_DIM
    expand = jnp.where(in_rot & ((lane_r % half) == f_c), 1.0, 0.0).astype(jnp.bfloat16)
    seg_r = lax.broadcasted_iota(jnp.int32, (QK_CHUNK, QK_CHUNK), 0) // DK
    seg_c = lax.broadcasted_iota(jnp.int32, (QK_CHUNK, QK_CHUNK), 1) // DK
    seg_mean = jnp.where(seg_r == seg_c, 1.0 / DK, 0.0).astype(jnp.bfloat16)
    d = lax.broadcasted_iota(jnp.int32, (tp, LANES), 1) % DK
    row = lax.broadcasted_iota(jnp.int32, (tp, CONV_WIDTH), 0)

    def normed(p):
        x = x_ref[p * tp:(p + 1) * tp, :]
        ms = jnp.mean(x * x, axis=-1, keepdims=True)
        return (x * lax.rsqrt(ms + NORM_EPS) * g1_ref[...]).astype(jnp.bfloat16)

    def rotary_tables(p):
        pos = pos_ref[:, p * tp:(p + 1) * tp].astype(jnp.float32)
        ang = freq_ref[...] * pos

        def to_rows(t):
            h, m, l = _split3(t)
            return (_dot(expand, h) + _dot(expand, m) + _dot(expand, l)).T

        cos_r, sin_r = to_rows(jnp.cos(ang)), to_rows(jnp.sin(ang))
        return (jnp.where(d < ROT_DIM, cos_r, 1.0),
                jnp.where(d < half, -sin_r, 0.0),
                jnp.where((d >= half) & (d < ROT_DIM), sin_r, 0.0))

    def qk_product(hn, c2):
        return _dot(hn, wqk_ref[:, c2 * QK_CHUNK:(c2 + 1) * QK_CHUNK])

    def qk_rows(p, c2, blk, tables):
        rows = slice(p * tp, (p + 1) * tp)
        cos_r, sin_lo, sin_hi = tables
        msq = _dot((blk * blk).astype(jnp.bfloat16), seg_mean)
        y2 = blk * lax.rsqrt(msq + NORM_EPS) * gqk_ref[:, c2 * QK_CHUNK:(c2 + 1) * QK_CHUNK]
        for c1 in range(QK_CHUNK // LANES):
            c = c2 * (QK_CHUNK // LANES) + c1
            y = y2[:, c1 * LANES:(c1 + 1) * LANES]
            rot = (y * cos_r
                   + pltpu.roll(y, LANES - half, 1) * sin_lo
                   + pltpu.roll(y, half, 1) * sin_hi).astype(jnp.bfloat16)
            if c < N_HEADS:
                q_ref[rows, c * LANES:(c + 1) * LANES] = rot
            else:
                k_ref[rows, (c - N_HEADS) * LANES:(c - N_HEADS + 1) * LANES] = rot

    def conv_product(hn, j):
        return _dot(hn, wc_ref[:, j * CONV_WIDTH:(j + 1) * CONV_WIDTH])

    def v_rows(p, hn):
        vt_ref[:, p * tp:(p + 1) * tp] = _nt_dot(wvt_ref[...], hn).astype(jnp.bfloat16)

    def conv_rows(p, cp, prev):
        cb, cc, cu = cp
        y = cc * cu
        p1 = prev[SUBLANES - 1:SUBLANES, :]
        p2 = prev[SUBLANES - 2:SUBLANES - 1, :]
        y1 = jnp.where(row == 0, p1, pltpu.roll(y, 1, 0))
        y2 = jnp.where(row == 0, p2, jnp.where(row == 1, p1, pltpu.roll(y, 2, 0)))
        z = cw_ref[0:1, :] * y2 + cw_ref[1:2, :] * y1 + cw_ref[2:3, :] * y
        co = cb * z
        cms = jnp.mean(co * co, axis=-1, keepdims=True)
        conv_ref[p * tp:(p + 1) * tp, :] = (co * lax.rsqrt(cms + NORM_EPS) * cg_ref[...]).astype(jnp.bfloat16)
        return y[tp - SUBLANES:, :]

    @pl.when(i % tiles_per_seq == 0)
    def _():
        carry_ref[...] = jnp.zeros_like(carry_ref)

    n_qk = 2 * ATTN_WIDTH // QK_CHUNK
    hn0, tables0 = normed(0), rotary_tables(0)
    qk0 = [qk_product(hn0, c2) for c2 in range(n_qk)]
    hn1, tables1 = normed(1), rotary_tables(1)
    qk1 = []
    for c2 in range(n_qk):
        qk1.append(qk_product(hn1, c2))
        qk_rows(0, c2, qk0[c2], tables0)
    cp0 = []
    for c2 in range(n_qk):
        if c2 < 3:
            cp0.append(conv_product(hn0, c2))
        qk_rows(1, c2, qk1[c2], tables1)
    cp1 = [conv_product(hn1, j) for j in range(3)]
    v_rows(0, hn0)
    v_rows(1, hn1)
    prev = conv_rows(0, cp0, carry_ref[...])
    carry_ref[...] = conv_rows(1, cp1, prev)


def _inproj(x2, pos_row, g1, wqk, wvt, wc, gqk, freqs, cw, cg, seq):
    n = x2.shape[0]
    tm = TM_IN
    grid = (n // tm,)
    const = lambda i: (0, 0)
    return pl.pallas_call(
        functools.partial(_inproj_kernel, tiles_per_seq=seq // tm),
        grid=grid,
        in_specs=[
            pl.BlockSpec((tm, D_MODEL), lambda i: (i, 0)),
            pl.BlockSpec((1, tm), lambda i: (0, i)),
            pl.BlockSpec((1, D_MODEL), const),
            pl.BlockSpec((D_MODEL, 2 * ATTN_WIDTH), const),
            pl.BlockSpec((ATTN_WIDTH, D_MODEL), const),
            pl.BlockSpec((D_MODEL, 3 * CONV_WIDTH), lambda i: (0, 1)),
            pl.BlockSpec((1, 1024), const),
            pl.BlockSpec((SUBLANES, 1), const),
            pl.BlockSpec((3, CONV_WIDTH), const),
            pl.BlockSpec((1, CONV_WIDTH), const),
        ],
        out_specs=[
            pl.BlockSpec((tm, ATTN_WIDTH), lambda i: (i, 0)),
            pl.BlockSpec((tm, ATTN_WIDTH), lambda i: (i, 0)),
            pl.BlockSpec((ATTN_WIDTH, tm), lambda i: (0, i)),
            pl.BlockSpec((tm, CONV_WIDTH), lambda i: (i, 0)),
        ],
        out_shape=[
            jax.ShapeDtypeStruct((n, ATTN_WIDTH), jnp.bfloat16),
            jax.ShapeDtypeStruct((n, ATTN_WIDTH), jnp.bfloat16),
            jax.ShapeDtypeStruct((ATTN_WIDTH, n), jnp.bfloat16),
            jax.ShapeDtypeStruct((n, CONV_WIDTH), jnp.bfloat16),
        ],
        scratch_shapes=[pltpu.VMEM((SUBLANES, CONV_WIDTH), jnp.float32)],
        compiler_params=pltpu.CompilerParams(
            dimension_semantics=("arbitrary",), vmem_limit_bytes=VMEM_LIMIT),
        name="inproj",
    )(x2, pos_row, g1, wqk, wvt, wc, gqk, freqs, cw, cg)


def _attn_kernel(q_ref, k_ref, vt_ref, lq1_ref, lk1_ref, lq2_ref, lk2_ref, sg_ref, wg_ref, wu_ref, wd_ref,
                 o_ref, wgb_ref, wub_ref, wdb_ref, acc_ref, *, nq):
    lam = (jnp.exp(jnp.sum(lq1_ref[...] * lk1_ref[...], axis=-1, keepdims=True))
           - jnp.exp(jnp.sum(lq2_ref[...] * lk2_ref[...], axis=-1, keepdims=True))
           + LAMBDA_INIT)
    tri =(lax.broadcasted_iota(jnp.int32, (HQ, HQ), 0)
           <= lax.broadcasted_iota(jnp.int32, (HQ, HQ), 1))
    row_d = lax.broadcasted_iota(jnp.int32, (LANES, HQ), 0)
    ones_rows = jnp.ones((SUM_ROWS, KV_FULL), jnp.bfloat16)

    def masked(s, n_tri):
        blocks = [jnp.where(tri, s[:, c * HQ:(c + 1) * HQ], -jnp.inf) for c in range(n_tri)]
        return jnp.concatenate(blocks + [s[:, n_tri * HQ:]], axis=1) if n_tri * HQ < s.shape[1] else \
            jnp.concatenate(blocks, axis=1)

    todo = {}
    for t in range(nq):
        lo = t * TQ
        todo[t] = ([(lo, lo + HQ, "mixed"), (lo + HQ, lo + TQ, "upper")]
                   + [(k0, k0 + KV_FULL, "full") for k0 in range(0, lo, KV_FULL)])
    tasks = []
    while any(todo.values()):
        for t in reversed(range(nq)):
            if todo[t]:
                tasks.append((t,) + todo[t].pop(0))
    last_task = {t: max(i for i, tk in enumerate(tasks) if tk[0] == t) for t in range(nq)}
    qzt, m_run = {}, {}

    def q_transposed(t):
        parts = []
        for half in range(2):
            r0 = t * TQ + half * HQ
            qt = q_ref[r0:r0 + HQ, :].astype(jnp.float32).T
            parts.append(jnp.where(row_d < DK, qt, 0.0))
            parts.append(jnp.where(row_d >= DK, qt, 0.0))
        return jnp.concatenate(parts, axis=1).astype(jnp.bfloat16)

    def scores(task):
        t, k0, k1, kind = task
        if t not in qzt:
            qzt[t] = q_transposed(t)
        kb = k_ref[k0:k1, :]
        if kind == "upper":
            return masked(_dot(kb, qzt[t][:, 2 * HQ:]), 2)
        s = _dot(kb, qzt[t])
        return masked(s, 2) if kind == "mixed" else s

    def softmax(task, s):
        t, _, _, kind = task
        mx = jnp.max(s, axis=0, keepdims=True)
        if kind == "mixed":
            m_run[t] = mx
            return jnp.exp2(s - mx).astype(jnp.bfloat16), None
        m_old = m_run[t][:, 2 * HQ:] if kind == "upper" else m_run[t]
        m_new = jnp.maximum(m_old, mx)
        alpha = jnp.exp2(m_old - m_new)
        p = jnp.exp2(s - m_new)
        m_run[t] = jnp.concatenate([m_run[t][:, :2 * HQ], m_new], axis=1) if kind == "upper" else m_new
        return p.astype(jnp.bfloat16), alpha

    def accumulate(task, p, alpha):
        t, k0, k1, kind = task
        vta = jnp.concatenate([vt_ref[:, k0:k1], ones_rows[:, :k1 - k0]], axis=0)
        pv = _dot(vta, p)
        if kind == "mixed":
            acc_ref[t] = pv
        elif kind == "upper":
            acc_ref[t, :, 2 * HQ:] = alpha * acc_ref[t, :, 2 * HQ:] + pv
        else:
            acc_ref[t] = alpha * acc_ref[t] + pv

    def finish(t):
        o_all = acc_ref[t, 0:DV, :] * (1.0 / acc_ref[t, DV:DV + 1, :])
        for half in range(2):
            o = (o_all[:, (2 * half) * HQ:(2 * half + 1) * HQ]
                 - lam * o_all[:, (2 * half + 1) * HQ:(2 * half + 2) * HQ])
            ms = jnp.mean(o * o, axis=0, keepdims=True)
            on = o * lax.rsqrt(ms + NORM_EPS) * sg_ref[...] * (1.0 - LAMBDA_INIT)
            r0 = t * TQ + half * HQ
            o_ref[r0:r0 + HQ, :] = on.T.astype(jnp.bfloat16)

    n = len(tasks)

    def cast_share(step):
        for src, dst in ((wg_ref, wgb_ref), (wu_ref, wub_ref), (wd_ref, wdb_ref)):
            rows = src.shape[1]
            per = -(-rows // (n + 2) // SUM_ROWS) * SUM_ROWS
            r0, r1 = min(step * per, rows), min((step + 1) * per, rows)
            if r1 > r0:
                dst[:, r0:r1, :] = src[:, r0:r1, :].astype(jnp.bfloat16)

    s_prev, p_prev = None, None
    for step in range(n + 2):
        cast_share(step)
        if step >= 2:
            accumulate(tasks[step - 2], *p_prev)
            if last_task[tasks[step - 2][0]] == step - 2:
                finish(tasks[step - 2][0])
        if 1 <= step <= n:
            p_prev = softmax(tasks[step - 1], s_prev)
        if step < n:
            s_prev = scores(tasks[step])


def _attention(q, k, vt, lq1, lk1, lq2, lk2, sg_col, w_gate, w_up, w_down, bsz, seq):
    n = q.shape[0]
    nq = seq // TQ
    steps = bsz * N_HEADS
    assert N_EXPERTS % steps == 0
    epw = N_EXPERTS // steps
    vec = lambda b, h: (0, 0)
    w_map = lambda b, h: (b * N_HEADS + h, 0, 0)
    bf16 = jnp.bfloat16
    return pl.pallas_call(
        functools.partial(_attn_kernel, nq=nq),
        grid=(bsz, N_HEADS),
        in_specs=[
            pl.BlockSpec((seq, DV), lambda b, h: (b, h)),
            pl.BlockSpec((seq, DV), lambda b, h: (b, h)),
            pl.BlockSpec((DV, seq), lambda b, h: (h, b)),
            pl.BlockSpec((1, DK), vec), pl.BlockSpec((1, DK), vec),
            pl.BlockSpec((1, DK), vec), pl.BlockSpec((1, DK), vec),
            pl.BlockSpec((DV, 1), vec),
            pl.BlockSpec((epw, D_MODEL, D_FF), w_map),
            pl.BlockSpec((epw, D_MODEL, D_FF), w_map),
            pl.BlockSpec((epw, D_FF, D_MODEL), w_map),
        ],
        out_specs=[
            pl.BlockSpec((seq, DV), lambda b, h: (b, h)),
            pl.BlockSpec((epw, D_MODEL, D_FF), w_map),
            pl.BlockSpec((epw, D_MODEL, D_FF), w_map),
            pl.BlockSpec((epw, D_FF, D_MODEL), w_map),
        ],
        out_shape=[
            jax.ShapeDtypeStruct((n, ATTN_WIDTH), bf16),
            jax.ShapeDtypeStruct((N_EXPERTS, D_MODEL, D_FF), bf16),
            jax.ShapeDtypeStruct((N_EXPERTS, D_MODEL, D_FF), bf16),
            jax.ShapeDtypeStruct((N_EXPERTS, D_FF, D_MODEL), bf16),
        ],
        scratch_shapes=[pltpu.VMEM((nq, DV + SUM_ROWS, 4 * HQ), jnp.float32)],
        compiler_params=pltpu.CompilerParams(
            dimension_semantics=("arbitrary", "arbitrary"), vmem_limit_bytes=VMEM_LIMIT),
        name="attn",
    )(q, k, vt, lq1, lk1, lq2, lk2, sg_col, w_gate, w_up, w_down)


ROUTER_ROWS = 128
EXPERT_ROW0 = 32


def _outproj_kernel(x_ref, attn_ref, conv_ref, wo_ref, g2_ref, wr2_ref,
                    h_ref, xs_ref, meta_ref, cnt_ref):
    tm = x_ref.shape[0]
    t = T_SORT
    f32, bf16 = jnp.float32, jnp.bfloat16
    a = jnp.concatenate([attn_ref[...], conv_ref[...]], axis=1)
    h = x_ref[...] + _dot(a, wo_ref[...])
    h_ref[...] = h
    ms = jnp.mean(h * h, axis=-1, keepdims=True)
    hn = h * lax.rsqrt(ms + NORM_EPS) * g2_ref[...]
    hn_hi, hn_lo = _split2(hn)

    hh = _dot(hn_hi, wr2_ref[...])
    logits = hh[:, :ROUTER_ROWS] + hh[:, ROUTER_ROWS:] + _dot(hn_lo, wr2_ref[:, :ROUTER_ROWS])
    lt = logits.T
    row8 = lax.broadcasted_iota(jnp.int32, (SUBLANES, tm), 0).astype(f32)
    neg_inf = -jnp.inf

    def first_argmax(v):
        mx = jnp.max(v, axis=0, keepdims=True)
        idx = jnp.min(jnp.where(v == mx, row8, float(SUBLANES)), axis=0, keepdims=True)
        return mx, idx

    g_log = jnp.where(row8 < N_GROUPS, lt[0:SUBLANES, :], neg_inf)
    g_max, g_sel = first_argmax(g_log)
    g_gate = 1.0 / jnp.sum(jnp.exp(g_log - g_max), axis=0, keepdims=True)
    e_log = jnp.zeros((EPG, tm), f32)
    for g in range(N_GROUPS):
        rows = lt[EXPERT_ROW0 + g * EPG:EXPERT_ROW0 + (g + 1) * EPG, :]
        e_log = jnp.where(g_sel == float(g), rows, e_log)
    v1, i1 = first_argmax(e_log)
    v2, i2 = first_argmax(jnp.where(row8 == i1, neg_inf, e_log))
    tt = jnp.exp(v2 - v1)
    w1 = g_gate / (1.0 + tt)
    w2 = g_gate * tt / (1.0 + tt)
    e1 = g_sel * float(EPG) + i1
    e2 = g_sel * float(EPG) + i2

    row32 = lax.broadcasted_iota(jnp.int32, (N_EXPERTS, tm), 0).astype(f32)
    oh1 = row32 == e1
    oh2 = row32 == e2
    c = jnp.where(oh1 | oh2, 1.0, 0.0).astype(bf16)
    tok_r = lax.broadcasted_iota(jnp.int32, (tm, tm), 0)
    tok_c = lax.broadcasted_iota(jnp.int32, (tm, tm), 1)
    same_tile = (tok_r // t) == (tok_c // t)
    rank = _dot(c, jnp.where(same_tile & (tok_r < tok_c), 1.0, 0.0).astype(bf16))
    cnt_b = _dot(c, jnp.where(same_tile, 1.0, 0.0).astype(bf16))
    ex_r = lax.broadcasted_iota(jnp.int32, (N_EXPERTS, N_EXPERTS), 0)
    ex_c = lax.broadcasted_iota(jnp.int32, (N_EXPERTS, N_EXPERTS), 1)
    lower = jnp.where(ex_c < ex_r, 1.0, 0.0).astype(bf16)
    start_b = _dot(lower, cnt_b.astype(bf16))
    pos_e = start_b + rank
    p1 = jnp.sum(jnp.where(oh1, pos_e, 0.0), axis=0, keepdims=True)
    p2 = jnp.sum(jnp.where(oh2, pos_e, 0.0), axis=0, keepdims=True)

    srow = lax.broadcasted_iota(jnp.int32, (2 * t, t), 0).astype(f32)
    for s in range(tm // t):
        cols = slice(s * t, (s + 1) * t)
        perm = jnp.where((srow == p1[:, cols]) | (srow == p2[:, cols]), 1.0, 0.0).astype(bf16)
        xs = _dot(perm, hn_hi[cols, :])
        xs_ref[s * 2 * t:(s + 1) * 2 * t] = _pack_rows(xs)
        cnt_ref[s * N_EXPERTS:(s + 1) * N_EXPERTS, :] = cnt_b[:, s * t:s * t + LANES]

    meta = jnp.concatenate([p1, p2, w1, w2, jnp.zeros((LANES - 4, tm), f32)], axis=0)
    meta_ref[...] = meta.T


def _outproj(x2, attn_o, conv_o, wo, g2, wr2):
    n = x2.shape[0]
    tm = TM_OUT
    nsub = tm // T_SORT
    const = lambda i: (0, 0)

    return pl.pallas_call(
        _outproj_kernel,
        grid=(n // tm,),
        in_specs=[
            pl.BlockSpec((tm, D_MODEL), lambda i: (i, 0)),
            pl.BlockSpec((tm, ATTN_WIDTH), lambda i: (i, 0)),
            pl.BlockSpec((tm, CONV_WIDTH), lambda i: (i, 0)),
            pl.BlockSpec((D_MODEL, D_MODEL), const),
            pl.BlockSpec((1, D_MODEL), const),
            pl.BlockSpec((D_MODEL, 2 * ROUTER_ROWS), const),
        ],
        out_specs=[
            pl.BlockSpec((tm, D_MODEL), lambda i: (i, 0)),
            pl.BlockSpec((2 * tm, ROW_CHUNKS, LANES), lambda i: (i, 0, 0)),
            pl.BlockSpec((tm, LANES), lambda i: (i, 0)),
            pl.BlockSpec((nsub * N_EXPERTS, LANES), lambda i: (i, 0)),
        ],
        out_shape=[
            jax.ShapeDtypeStruct((n, D_MODEL), jnp.float32),
            jax.ShapeDtypeStruct((2 * n, ROW_CHUNKS, LANES), PACKED_DTYPE),
            jax.ShapeDtypeStruct((n, LANES), jnp.float32),
            jax.ShapeDtypeStruct((n // T_SORT * N_EXPERTS, LANES), jnp.float32),
        ],
        compiler_params=pltpu.CompilerParams(
            dimension_semantics=("arbitrary",), vmem_limit_bytes=VMEM_LIMIT),
        name="outproj",
    )(x2, attn_o, conv_o, wo, g2, wr2)


def _pack_rows(x):
    r = x.shape[0]
    w = pltpu.pack_elementwise([x[:, :ROW_WORDS], x[:, ROW_WORDS:]], packed_dtype=jnp.bfloat16)
    return pltpu.bitcast(w, PACKED_DTYPE).reshape(r, ROW_CHUNKS, LANES)


def _packed_zero_rows(r):
    z = jnp.zeros((r, ROW_CHUNKS, LANES), jnp.float32)
    w = pltpu.pack_elementwise([z, z], packed_dtype=jnp.bfloat16)
    return pltpu.bitcast(w, PACKED_DTYPE)


def _unpack_rows(u):
    r = u.shape[0]
    w = u.reshape(r, ROW_WORDS)
    lo = pltpu.unpack_elementwise(w, index=0, packed_dtype=jnp.bfloat16, unpacked_dtype=jnp.float32)
    hi = pltpu.unpack_elementwise(w, index=1, packed_dtype=jnp.bfloat16, unpacked_dtype=jnp.float32)
    return jnp.concatenate([lo, hi], axis=1).astype(jnp.bfloat16)


def _expert_kernel(nblk_ref, blk0_ref, be_ref, nused_ref, r0_ref, nvalid_ref, tlo_ref, thi_ref,
                   cnt_ref, src_ref, rbase_ref,
                   xs_hbm, wg_ref, wu_ref, wd_ref, y_hbm,
                   xbuf, ybuf, sem, ysem):
    e = pl.program_id(0)
    nused = nused_ref[0]
    n_blocks = be_ref.shape[0]
    n_tiles = cnt_ref.shape[0] // N_EXPERTS

    def start_run(blk, slot, t, live):
        r0 = r0_ref[blk]
        k = t * N_EXPERTS + be_ref[blk]
        lo = jnp.maximum(rbase_ref[k], r0)
        hi = jnp.minimum(rbase_ref[k] + cnt_ref[k], r0 + MOE_BLOCK)
        rows = jnp.where(live, hi - lo, 0)

        @pl.when(rows > 0)
        def _():
            pltpu.make_async_copy(xs_hbm.at[pl.ds(src_ref[k] + lo - rbase_ref[k], rows)],
                                  xbuf.at[slot, pl.ds(lo - r0, rows)], sem.at[slot]).start(priority=GATHER_PRIORITY)

    def gather_rolled(blk, slot, t_from):
        def run(t, carry):
            start_run(blk, slot, t, True)
            return carry
        lax.fori_loop(t_from, thi_ref[blk], run, 0)

    def gather_unrolled(blk, slot, live, part):
        blk = jnp.minimum(blk, n_blocks - 1)
        per = GATHER_UNROLL // GATHER_PARTS
        for r in range(part * per, (part + 1) * per):
            t = tlo_ref[blk] + r
            start_run(blk, slot, jnp.minimum(t, n_tiles - 1), live & (t < thi_ref[blk]))
        if part == GATHER_PARTS - 1:
            @pl.when(live & (tlo_ref[blk] + GATHER_UNROLL < thi_ref[blk]))
            def _():
                gather_rolled(blk, slot, tlo_ref[blk] + GATHER_UNROLL)

    def y_copy(blk, slot):
        return pltpu.make_async_copy(ybuf.at[slot], y_hbm.at[pl.ds(blk * MOE_BLOCK, MOE_BLOCK)], ysem.at[slot])

    @pl.when(e == 0)
    def _():
        for slot in range(N_XBUF):
            xbuf[slot] = _packed_zero_rows(MOE_BLOCK)
        gather_rolled(0, 0, tlo_ref[0])
        for blk in range(1, N_XBUF - 1):
            @pl.when(nused > blk)
            def _():
                gather_rolled(blk, blk, tlo_ref[blk])
        for slot in range(N_YBUF):
            ybuf[slot] = _packed_zero_rows(MOE_BLOCK)
            y_copy(slot, slot).start()

    @pl.when(nblk_ref[e] > 0)
    def _():
        def blocks(b0, count):
            ids = [b0 + j for j in range(count)]
            for b in ids:
                nv = nvalid_ref[b]
                y_copy(b, b % N_YBUF).wait()
                pltpu.make_async_copy(xs_hbm.at[pl.ds(0, nv)], xbuf.at[b % N_XBUF, pl.ds(0, nv)],
                                      sem.at[b % N_XBUF]).wait()

            def refill(b, part):
                ahead = b + N_XBUF - 1
                gather_unrolled(ahead, ahead % N_XBUF, ahead < nused, part)

            def gate_up(b, part):
                x = _unpack_rows(xbuf[b % N_XBUF])
                g = _dot(x, wg_ref[0])
                refill(b, part)
                return g, _dot(x, wu_ref[0])

            def activation(gu):
                g, u = gu
                return (g / (1.0 + jnp.exp(-g)) * u).astype(jnp.bfloat16)

            def send(b, y):
                ybuf[b % N_YBUF] = _pack_rows(y)
                y_copy(b, b % N_YBUF).start()

            gus = [gate_up(b, 0) for b in ids]
            act = activation(gus[0])
            for j, b in enumerate(ids):
                refill(b, 1)
                y = _dot(act, wd_ref[0])
                if j + 1 < count:
                    act = activation(gus[j + 1])
                refill(b, 2)
                send(b, y)

        first = blk0_ref[e]

        def pair_body(c, carry):
            blocks(first + 2 * c, 2)
            return carry
        lax.fori_loop(0, nblk_ref[e] // 2, pair_body, 0)

        @pl.when(nblk_ref[e] % 2 == 1)
        def _():
            blocks(first + nblk_ref[e] - 1, 1)

    @pl.when(e == N_EXPERTS - 1)
    def _():
        for slot in range(N_YBUF):
            y_copy(slot, slot).wait()
        ybuf[0] = _packed_zero_rows(MOE_BLOCK)

        def zero_block(b, carry):
            y_copy(b, 0).start()
            y_copy(b, 0).wait()
            return carry
        lax.fori_loop(nused, n_blocks, zero_block, 0)


def _experts(tabs, xs, w_gate, w_up, w_down):
    n_slots = tabs["n_slots"]
    w_map = lambda e, *_: (e, 0, 0)
    return pl.pallas_call(
        _expert_kernel,
        grid_spec=pltpu.PrefetchScalarGridSpec(
            num_scalar_prefetch=11,
            grid=(N_EXPERTS,),
            in_specs=[
                pl.BlockSpec(memory_space=pl.ANY),
                pl.BlockSpec((1, D_MODEL, D_FF), w_map),
                pl.BlockSpec((1, D_MODEL, D_FF), w_map),
                pl.BlockSpec((1, D_FF, D_MODEL), w_map),
            ],
            out_specs=pl.BlockSpec(memory_space=pl.ANY),
            scratch_shapes=[pltpu.VMEM((N_XBUF, MOE_BLOCK, ROW_CHUNKS, LANES), PACKED_DTYPE),
                            pltpu.VMEM((N_YBUF, MOE_BLOCK, ROW_CHUNKS, LANES), PACKED_DTYPE),
                            pltpu.SemaphoreType.DMA((N_XBUF,)),
                            pltpu.SemaphoreType.DMA((N_YBUF,))],
        ),
        out_shape=jax.ShapeDtypeStruct((n_slots, ROW_CHUNKS, LANES), PACKED_DTYPE),
        compiler_params=pltpu.CompilerParams(
            dimension_semantics=("arbitrary",), vmem_limit_bytes=VMEM_LIMIT),
        name="experts",
    )(tabs["nblk"], tabs["blk0"], tabs["block_e"], tabs["nused"], tabs["r0"], tabs["nvalid"], tabs["tlo"],
      tabs["thi"], tabs["cnt"], tabs["src"], tabs["rbase"], xs, w_gate, w_up, w_down)


def _combine_kernel(cnt_ref, loc_ref, dst_ref, h_ref, meta_ref, yg_hbm, o_ref, ybuf, sem, *, n_steps, nsub):
    i = pl.program_id(0)
    t = T_SORT
    f32, bf16 = jnp.float32, jnp.bfloat16

    slot_in = i % 2
    tile0 = jnp.minimum(i, n_steps - 1) * nsub
    for s in range(nsub):
        for e in range(N_EXPERTS):
            k = (tile0 + s) * N_EXPERTS + e
            rows = jnp.where(i < n_steps, cnt_ref[k], 0)

            @pl.when(rows > 0)
            def _():
                pltpu.make_async_copy(yg_hbm.at[pl.ds(dst_ref[k], rows)],
                                      ybuf.at[slot_in, pl.ds(s * 2 * t + loc_ref[k], rows)],
                                      sem.at[slot_in]).start(priority=GATHER_PRIORITY)

    @pl.when(i > 0)
    def _():
        slot = (i - 1) % 2
        pltpu.make_async_copy(yg_hbm.at[pl.ds(0, nsub * 2 * t)], ybuf.at[slot], sem.at[slot]).wait()
        lane = lax.broadcasted_iota(jnp.int32, (t, 2 * t), 1).astype(f32)
        for s in range(nsub):
            y = _unpack_rows(ybuf[slot, s * 2 * t:(s + 1) * 2 * t])
            meta = meta_ref[s * t:(s + 1) * t, :]
            pick1 = jnp.where(lane == meta[:, 0:1], 1.0, 0.0).astype(bf16)
            pick2 = jnp.where(lane == meta[:, 1:2], 1.0, 0.0).astype(bf16)
            y12 = _dot(jnp.concatenate([pick1, pick2], axis=0), y)
            o_ref[s * t:(s + 1) * t, :] = h_ref[s * t:(s + 1) * t, :] + (meta[:, 2:3] * y12[:t]
                                                                         + meta[:, 3:4] * y12[t:])


def _combine(cnt, loc, dst, h, meta, yg):
    n = h.shape[0]
    tm = TM_OUT
    nsub = tm // T_SORT
    n_steps = n // tm
    prev = lambda i, *_: (jnp.maximum(i - 1, 0), 0)
    return pl.pallas_call(
        functools.partial(_combine_kernel, n_steps=n_steps, nsub=nsub),
        grid_spec=pltpu.PrefetchScalarGridSpec(
            num_scalar_prefetch=3,
            grid=(n_steps + 1,),
            in_specs=[
                pl.BlockSpec((tm, D_MODEL), prev),
                pl.BlockSpec((tm, LANES), prev),
                pl.BlockSpec(memory_space=pl.ANY),
            ],
            out_specs=pl.BlockSpec((tm, D_MODEL), prev),
            scratch_shapes=[pltpu.VMEM((2, nsub * 2 * T_SORT, ROW_CHUNKS, LANES), PACKED_DTYPE),
                            pltpu.SemaphoreType.DMA((2,))],
        ),
        out_shape=jax.ShapeDtypeStruct((n, D_MODEL), jnp.float32),
        compiler_params=pltpu.CompilerParams(
            dimension_semantics=("arbitrary",), vmem_limit_bytes=VMEM_LIMIT),
        name="combine",
    )(cnt, loc, dst, h, meta, yg)


def _routing_tables(cnt_out, n_tiles, n_tok):
    i32 = jnp.int32
    cnt = cnt_out.reshape(n_tiles, N_EXPERTS, LANES)[:, :, 0].astype(i32)
    count = jnp.sum(cnt, axis=0)
    padded = ((count + MOE_BLOCK - 1) // MOE_BLOCK) * MOE_BLOCK
    pad_end = jnp.cumsum(padded)
    pad_start = pad_end - padded
    run_end = jnp.cumsum(cnt, axis=0)
    rbase = run_end - cnt
    dst = pad_start[None, :] + rbase
    loc = jnp.cumsum(cnt, axis=1) - cnt
    src = loc + (jnp.arange(n_tiles, dtype=i32) * (2 * T_SORT))[:, None]
    n_blocks = (2 * n_tok) // MOE_BLOCK + N_EXPERTS
    block_start = jnp.arange(n_blocks, dtype=i32) * MOE_BLOCK
    block_e = jnp.minimum(jnp.sum(pad_end[None, :] <= block_start[:, None], axis=1), N_EXPERTS - 1).astype(i32)
    nused = (pad_end[-1] // MOE_BLOCK).astype(i32).reshape(1)
    ex = jnp.arange(N_EXPERTS, dtype=i32)
    sel = (block_e[None, :] == ex[:, None]).astype(i32)
    pick = lambda per_expert: jnp.sum(per_expert[..., :, None] * sel, axis=-2)
    r0 = block_start - pick(pad_start)
    nvalid = jnp.clip(pick(count) - r0, 0, MOE_BLOCK)
    tlo = jnp.sum(pick(run_end) <= r0[None, :], axis=0)
    thi = jnp.sum(pick(rbase) < (r0 + MOE_BLOCK)[None, :], axis=0)
    flat = lambda a: a.reshape(-1).astype(i32)
    return dict(cnt=flat(cnt), src=flat(src), dst=flat(dst), loc=flat(loc), rbase=flat(rbase),
                block_e=block_e, nused=nused, r0=flat(r0), nvalid=flat(nvalid), tlo=flat(tlo), thi=flat(thi),
                nblk=flat(padded // MOE_BLOCK), blk0=flat(pad_start // MOE_BLOCK), n_slots=n_blocks * MOE_BLOCK)


def _stage1(x, positions, attn_norm_gain, w_in, q_norm_gain, k_norm_gain, conv_w, conv_out_gain):
    bsz, seq, _ = x.shape
    n = bsz * seq
    f32, bf16 = jnp.float32, jnp.bfloat16
    w = w_in[0]
    wb = w.astype(bf16)
    wvt = w[:, 2 * ATTN_WIDTH:3 * ATTN_WIDTH].T.astype(bf16)
    scale = DK ** -0.5 * LOG2E
    gqk = jnp.concatenate([jnp.tile(q_norm_gain[0].astype(f32), 2 * N_HEADS) * scale,
                           jnp.tile(k_norm_gain[0].astype(f32), 2 * N_HEADS)]).reshape(1, -1)
    freqs = (ROPE_THETA ** (-jnp.arange(0, ROT_DIM, 2, dtype=f32) / ROT_DIM)).reshape(SUBLANES, 1)
    return _inproj(x.reshape(n, D_MODEL), positions.reshape(1, n),
                   attn_norm_gain[0].reshape(1, -1).astype(f32), wb, wvt, wb, gqk, freqs,
                   conv_w[0].astype(f32), conv_out_gain[0].reshape(1, -1).astype(f32), seq)


def kernel(x, positions, attn_norm_gain, w_in, q_norm_gain, k_norm_gain, lambda_q1, lambda_k1, lambda_q2, lambda_k2, subln_gain, conv_w, conv_out_gain, w_out, ffn_norm_gain, w_group_router, w_expert_router, w_gate, w_up, w_down):
    bsz, seq, _ = x.shape
    n = bsz * seq
    f32, bf16 = jnp.float32, jnp.bfloat16
    assert TM_OUT % T_SORT == 0 and seq % TM_IN == 0 and seq % TQ == 0 and TQ % KV_FULL == 0
    q, k, vt, conv_o = _stage1(x, positions, attn_norm_gain, w_in, q_norm_gain, k_norm_gain,
                                conv_w, conv_out_gain)
    attn_o, wg_b, wu_b, wd_b = _attention(q, k, vt,
                        lambda_q1[0].reshape(1, -1).astype(f32), lambda_k1[0].reshape(1, -1).astype(f32),
                        lambda_q2[0].reshape(1, -1).astype(f32), lambda_k2[0].reshape(1, -1).astype(f32),
                        subln_gain[0].reshape(-1, 1).astype(f32), w_gate[0], w_up[0], w_down[0], bsz, seq)

    wr = jnp.concatenate([
        w_group_router[0].astype(f32), jnp.zeros((D_MODEL, EXPERT_ROW0 - N_GROUPS), f32),
        jnp.transpose(w_expert_router[0].astype(f32), (1, 0, 2)).reshape(D_MODEL, N_EXPERTS),
        jnp.zeros((D_MODEL, ROUTER_ROWS - EXPERT_ROW0 - N_EXPERTS), f32)], axis=1)
    wrh = wr.astype(bf16)
    wr2 = jnp.concatenate([wrh, (wr - wrh.astype(f32)).astype(bf16)], axis=1)
    h, xs, meta, cnt_out = _outproj(x.reshape(n, D_MODEL), attn_o, conv_o, w_out[0].astype(bf16),
                                    ffn_norm_gain[0].reshape(1, -1).astype(f32), wr2)

    tabs = _routing_tables(cnt_out, n // T_SORT, n)
    yg = _experts(tabs, xs, wg_b, wu_b, wd_b)
    out = _combine(tabs["cnt"], tabs["loc"], tabs["dst"], h, meta, yg)
    return out.reshape(x.shape)
```

```python
import functools
import math

import jax
import jax.numpy as jnp
from jax import lax
from jax.experimental import pallas as pl
from jax.experimental.pallas import tpu as pltpu

D_MODEL = 1024
N_HEADS = 4
DK = 64
DV = 128
ROT_DIM = 16
ROPE_THETA = 500000.0
ATTN_WIDTH = N_HEADS * DV
CONV_WIDTH = 512
NORM_EPS = 1e-6
LOG2E = 1.4426950408889634
LAMBDA_INIT = 0.8 - 0.6 * math.exp(-0.3 * 0)
N_GROUPS = 4
EPG = 8
N_EXPERTS = N_GROUPS * EPG
D_FF = 512
MOE_BLOCK = 256

LANES = 128
SUBLANES = 8
ROW_WORDS = D_MODEL // 2
ROW_CHUNKS = ROW_WORDS // LANES
PACKED_DTYPE = jnp.uint32

TM_IN = 1024
QK_CHUNK = 256
TQ = 512
HQ = TQ // 2
KV_FULL = 512
SUM_ROWS = 16
T_SORT = 256
TM_OUT = 512
GATHER_AHEAD = 4
N_XBUF = GATHER_AHEAD + 2
N_YBUF = 3
GATHER_UNROLL = 24
GATHER_PARTS = 3
GATHER_PRIORITY = 1
VMEM_LIMIT = 48 * 1024 * 1024


def _nt_dot(a, b):
    return lax.dot_general(a, b, (((1,), (1,)), ((), ())), preferred_element_type=jnp.float32)


def _dot(a, b):
    return jnp.dot(a, b, preferred_element_type=jnp.float32)


def _split3(x):
    h = x.astype(jnp.bfloat16)
    r = x - h.astype(jnp.float32)
    m = r.astype(jnp.bfloat16)
    l = (r - m.astype(jnp.float32)).astype(jnp.bfloat16)
    return h, m, l


def _split2(x):
    h = x.astype(jnp.bfloat16)
    l = (x - h.astype(jnp.float32)).astype(jnp.bfloat16)
    return h, l


def _inproj_kernel(x_ref, pos_ref, g1_ref, wqk_ref, wvt_ref, wc_ref, gqk_ref, freq_ref,
                   cw_ref, cg_ref,
                   q_ref, k_ref, vt_ref, conv_ref,
                   carry_ref, *, tiles_per_seq):
    tm = x_ref.shape[0]
    tp = tm // 2
    i = pl.program_id(0)
    half = ROT_DIM // 2
    lane_r = lax.broadcasted_iota(jnp.int32, (LANES, SUBLANES), 0)
    f_c = lax.broadcasted_iota(jnp.int32, (LANES, SUBLANES), 1)
    in_rot = (lane_r % DK) < ROT_DIM
    expand = jnp.where(in_rot & ((lane_r % half) == f_c), 1.0, 0.0).astype(jnp.bfloat16)
    seg_r = lax.broadcasted_iota(jnp.int32, (QK_CHUNK, QK_CHUNK), 0) // DK
    seg_c = lax.broadcasted_iota(jnp.int32, (QK_CHUNK, QK_CHUNK), 1) // DK
    seg_mean = jnp.where(seg_r == seg_c, 1.0 / DK, 0.0).astype(jnp.bfloat16)
    d = lax.broadcasted_iota(jnp.int32, (tp, LANES), 1) % DK
    row = lax.broadcasted_iota(jnp.int32, (tp, CONV_WIDTH), 0)

    def normed(p):
        x = x_ref[p * tp:(p + 1) * tp, :]
        ms = jnp.mean(x * x, axis=-1, keepdims=True)
        return (x * lax.rsqrt(ms + NORM_EPS) * g1_ref[...]).astype(jnp.bfloat16)

    def rotary_tables(p):
        pos = pos_ref[:, p * tp:(p + 1) * tp].astype(jnp.float32)
        ang = freq_ref[...] * pos

        def to_rows(t):
            h, m, l = _split3(t)
            return (_dot(expand, h) + _dot(expand, m) + _dot(expand, l)).T

        cos_r, sin_r = to_rows(jnp.cos(ang)), to_rows(jnp.sin(ang))
        return (jnp.where(d < ROT_DIM, cos_r, 1.0),
                jnp.where(d < half, -sin_r, 0.0),
                jnp.where((d >= half) & (d < ROT_DIM), sin_r, 0.0))

    def qk_product(hn, c2):
        return _dot(hn, wqk_ref[:, c2 * QK_CHUNK:(c2 + 1) * QK_CHUNK])

    def qk_rows(p, c2, blk, tables):
        rows = slice(p * tp, (p + 1) * tp)
        cos_r, sin_lo, sin_hi = tables
        msq = _dot((blk * blk).astype(jnp.bfloat16), seg_mean)
        y2 = blk * lax.rsqrt(msq + NORM_EPS) * gqk_ref[:, c2 * QK_CHUNK:(c2 + 1) * QK_CHUNK]
        for c1 in range(QK_CHUNK // LANES):
            c = c2 * (QK_CHUNK // LANES) + c1
            y = y2[:, c1 * LANES:(c1 + 1) * LANES]
            rot = (y * cos_r
                   + pltpu.roll(y, LANES - half, 1) * sin_lo
                   + pltpu.roll(y, half, 1) * sin_hi).astype(jnp.bfloat16)
            if c < N_HEADS:
                q_ref[rows, c * LANES:(c + 1) * LANES] = rot
            else:
                k_ref[rows, (c - N_HEADS) * LANES:(c - N_HEADS + 1) * LANES] = rot

    def conv_product(hn, j):
        return _dot(hn, wc_ref[:, j * CONV_WIDTH:(j + 1) * CONV_WIDTH])

    def v_rows(p, hn):
        vt_ref[:, p * tp:(p + 1) * tp] = _nt_dot(wvt_ref[...], hn).astype(jnp.bfloat16)

    def conv_rows(p, cp, prev):
        cb, cc, cu = cp
        y = cc * cu
        p1 = prev[SUBLANES - 1:SUBLANES, :]
        p2 = prev[SUBLANES - 2:SUBLANES - 1, :]
        y1 = jnp.where(row == 0, p1, pltpu.roll(y, 1, 0))
        y2 = jnp.where(row == 0, p2, jnp.where(row == 1, p1, pltpu.roll(y, 2, 0)))
        z = cw_ref[0:1, :] * y2 + cw_ref[1:2, :] * y1 + cw_ref[2:3, :] * y
        co = cb * z
        cms = jnp.mean(co * co, axis=-1, keepdims=True)
        conv_ref[p * tp:(p + 1) * tp, :] = (co * lax.rsqrt(cms + NORM_EPS) * cg_ref[...]).astype(jnp.bfloat16)
        return y[tp - SUBLANES:, :]

    @pl.when(i % tiles_per_seq == 0)
    def _():
        carry_ref[...] = jnp.zeros_like(carry_ref)

    n_qk = 2 * ATTN_WIDTH // QK_CHUNK
    hn0, tables0 = normed(0), rotary_tables(0)
    qk0 = [qk_product(hn0, c2) for c2 in range(n_qk)]
    hn1, tables1 = normed(1), rotary_tables(1)
    qk1 = []
    for c2 in range(n_qk):
        qk1.append(qk_product(hn1, c2))
        qk_rows(0, c2, qk0[c2], tables0)
    cp0 = []
    for c2 in range(n_qk):
        if c2 < 3:
            cp0.append(conv_product(hn0, c2))
        qk_rows(1, c2, qk1[c2], tables1)
    cp1 = [conv_product(hn1, j) for j in range(3)]
    v_rows(0, hn0)
    v_rows(1, hn1)
    prev = conv_rows(0, cp0, carry_ref[...])
    carry_ref[...] = conv_rows(1, cp1, prev)


def _inproj(x2, pos_row, g1, wqk, wvt, wc, gqk, freqs, cw, cg, seq):
    n = x2.shape[0]
    tm = TM_IN
    grid = (n // tm,)
    const = lambda i: (0, 0)
    return pl.pallas_call(
        functools.partial(_inproj_kernel, tiles_per_seq=seq // tm),
        grid=grid,
        in_specs=[
            pl.BlockSpec((tm, D_MODEL), lambda i: (i, 0)),
            pl.BlockSpec((1, tm), lambda i: (0, i)),
            pl.BlockSpec((1, D_MODEL), const),
            pl.BlockSpec((D_MODEL, 2 * ATTN_WIDTH), const),
            pl.BlockSpec((ATTN_WIDTH, D_MODEL), const),
            pl.BlockSpec((D_MODEL, 3 * CONV_WIDTH), lambda i: (0, 1)),
            pl.BlockSpec((1, 1024), const),
            pl.BlockSpec((SUBLANES, 1), const),
            pl.BlockSpec((3, CONV_WIDTH), const),
            pl.BlockSpec((1, CONV_WIDTH), const),
        ],
        out_specs=[
            pl.BlockSpec((tm, ATTN_WIDTH), lambda i: (i, 0)),
            pl.BlockSpec((tm, ATTN_WIDTH), lambda i: (i, 0)),
            pl.BlockSpec((ATTN_WIDTH, tm), lambda i: (0, i)),
            pl.BlockSpec((tm, CONV_WIDTH), lambda i: (i, 0)),
        ],
        out_shape=[
            jax.ShapeDtypeStruct((n, ATTN_WIDTH), jnp.bfloat16),
            jax.ShapeDtypeStruct((n, ATTN_WIDTH), jnp.bfloat16),
            jax.ShapeDtypeStruct((ATTN_WIDTH, n), jnp.bfloat16),
            jax.ShapeDtypeStruct((n, CONV_WIDTH), jnp.bfloat16),
        ],
        scratch_shapes=[pltpu.VMEM((SUBLANES, CONV_WIDTH), jnp.float32)],
        compiler_params=pltpu.CompilerParams(
            dimension_semantics=("arbitrary",), vmem_limit_bytes=VMEM_LIMIT),
        name="inproj",
    )(x2, pos_row, g1, wqk, wvt, wc, gqk, freqs, cw, cg)


def _attn_kernel(q_ref, k_ref, vt_ref, lq1_ref, lk1_ref, lq2_ref, lk2_ref, sg_ref, wg_ref, wu_ref, wd_ref,
                 o_ref, wgb_ref, wub_ref, wdb_ref, acc_ref, *, nq):
    lam = (jnp.exp(jnp.sum(lq1_ref[...] * lk1_ref[...], axis=-1, keepdims=True))
           - jnp.exp(jnp.sum(lq2_ref[...] * lk2_ref[...], axis=-1, keepdims=True))
           + LAMBDA_INIT)
    tri =(lax.broadcasted_iota(jnp.int32, (HQ, HQ), 0)
           <= lax.broadcasted_iota(jnp.int32, (HQ, HQ), 1))
    row_d = lax.broadcasted_iota(jnp.int32, (LANES, HQ), 0)
    ones_rows = jnp.ones((SUM_ROWS, KV_FULL), jnp.bfloat16)

    def masked(s, n_tri):
        blocks = [jnp.where(tri, s[:, c * HQ:(c + 1) * HQ], -jnp.inf) for c in range(n_tri)]
        return jnp.concatenate(blocks + [s[:, n_tri * HQ:]], axis=1) if n_tri * HQ < s.shape[1] else \
            jnp.concatenate(blocks, axis=1)

    todo = {}
    for t in range(nq):
        lo = t * TQ
        todo[t] = ([(lo, lo + HQ, "mixed"), (lo + HQ, lo + TQ, "upper")]
                   + [(k0, k0 + KV_FULL, "full") for k0 in range(0, lo, KV_FULL)])
    tasks = []
    while any(todo.values()):
        for t in reversed(range(nq)):
            if todo[t]:
                tasks.append((t,) + todo[t].pop(0))
    last_task = {t: max(i for i, tk in enumerate(tasks) if tk[0] == t) for t in range(nq)}
    qzt, m_run = {}, {}

    def q_transposed(t):
        parts = []
        for half in range(2):
            r0 = t * TQ + half * HQ
            qt = q_ref[r0:r0 + HQ, :].astype(jnp.float32).T
            parts.append(jnp.where(row_d < DK, qt, 0.0))
            parts.append(jnp.where(row_d >= DK, qt, 0.0))
        return jnp.concatenate(parts, axis=1).astype(jnp.bfloat16)

    def scores(task):
        t, k0, k1, kind = task
        if t not in qzt:
            qzt[t] = q_transposed(t)
        kb = k_ref[k0:k1, :]
        if kind == "upper":
            return masked(_dot(kb, qzt[t][:, 2 * HQ:]), 2)
        s = _dot(kb, qzt[t])
        return masked(s, 2) if kind == "mixed" else s

    def softmax(task, s):
        t, _, _, kind = task
        mx = jnp.max(s, axis=0, keepdims=True)
        if kind == "mixed":
            m_run[t] = mx
            return jnp.exp2(s - mx).astype(jnp.bfloat16), None
        m_old = m_run[t][:, 2 * HQ:] if kind == "upper" else m_run[t]
        m_new = jnp.maximum(m_old, mx)
        alpha = jnp.exp2(m_old - m_new)
        p = jnp.exp2(s - m_new)
        m_run[t] = jnp.concatenate([m_run[t][:, :2 * HQ], m_new], axis=1) if kind == "upper" else m_new
        return p.astype(jnp.bfloat16), alpha

    def accumulate(task, p, alpha):
        t, k0, k1, kind = task
        vta = jnp.concatenate([vt_ref[:, k0:k1], ones_rows[:, :k1 - k0]], axis=0)
        pv = _dot(vta, p)
        if kind == "mixed":
            acc_ref[t] = pv
        elif kind == "upper":
            acc_ref[t, :, 2 * HQ:] = alpha * acc_ref[t, :, 2 * HQ:] + pv
        else:
            acc_ref[t] = alpha * acc_ref[t] + pv

    def finish(t):
        o_all = acc_ref[t, 0:DV, :] * (1.0 / acc_ref[t, DV:DV + 1, :])
        for half in range(2):
            o = (o_all[:, (2 * half) * HQ:(2 * half + 1) * HQ]
                 - lam * o_all[:, (2 * half + 1) * HQ:(2 * half + 2) * HQ])
            ms = jnp.mean(o * o, axis=0, keepdims=True)
            on = o * lax.rsqrt(ms + NORM_EPS) * sg_ref[...] * (1.0 - LAMBDA_INIT)
            r0 = t * TQ + half * HQ
            o_ref[r0:r0 + HQ, :] = on.T.astype(jnp.bfloat16)

    n = len(tasks)

    def cast_share(step):
        for src, dst in ((wg_ref, wgb_ref), (wu_ref, wub_ref), (wd_ref, wdb_ref)):
            rows = src.shape[1]
            per = -(-rows // (n + 2) // SUM_ROWS) * SUM_ROWS
            r0, r1 = min(step * per, rows), min((step + 1) * per, rows)
            if r1 > r0:
                dst[:, r0:r1, :] = src[:, r0:r1, :].astype(jnp.bfloat16)

    s_prev, p_prev = None, None
    for step in range(n + 2):
        cast_share(step)
        if step >= 2:
            accumulate(tasks[step - 2], *p_prev)
            if last_task[tasks[step - 2][0]] == step - 2:
                finish(tasks[step - 2][0])
        if 1 <= step <= n:
            p_prev = softmax(tasks[step - 1], s_prev)
        if step < n:
            s_prev = scores(tasks[step])


def _attention(q, k, vt, lq1, lk1, lq2, lk2, sg_col, w_gate, w_up, w_down, bsz, seq):
    n = q.shape[0]
    nq = seq // TQ
    steps = bsz * N_HEADS
    assert N_EXPERTS % steps == 0
    epw = N_EXPERTS // steps
    vec = lambda b, h: (0, 0)
    w_map = lambda b, h: (b * N_HEADS + h, 0, 0)
    bf16 = jnp.bfloat16
    return pl.pallas_call(
        functools.partial(_attn_kernel, nq=nq),
        grid=(bsz, N_HEADS),
        in_specs=[
            pl.BlockSpec((seq, DV), lambda b, h: (b, h)),
            pl.BlockSpec((seq, DV), lambda b, h: (b, h)),
            pl.BlockSpec((DV, seq), lambda b, h: (h, b)),
            pl.BlockSpec((1, DK), vec), pl.BlockSpec((1, DK), vec),
            pl.BlockSpec((1, DK), vec), pl.BlockSpec((1, DK), vec),
            pl.BlockSpec((DV, 1), vec),
            pl.BlockSpec((epw, D_MODEL, D_FF), w_map),
            pl.BlockSpec((epw, D_MODEL, D_FF), w_map),
            pl.BlockSpec((epw, D_FF, D_MODEL), w_map),
        ],
        out_specs=[
            pl.BlockSpec((seq, DV), lambda b, h: (b, h)),
            pl.BlockSpec((epw, D_MODEL, D_FF), w_map),
            pl.BlockSpec((epw, D_MODEL, D_FF), w_map),
            pl.BlockSpec((epw, D_FF, D_MODEL), w_map),
        ],
        out_shape=[
            jax.ShapeDtypeStruct((n, ATTN_WIDTH), bf16),
            jax.ShapeDtypeStruct((N_EXPERTS, D_MODEL, D_FF), bf16),
            jax.ShapeDtypeStruct((N_EXPERTS, D_MODEL, D_FF), bf16),
            jax.ShapeDtypeStruct((N_EXPERTS, D_FF, D_MODEL), bf16),
        ],
        scratch_shapes=[pltpu.VMEM((nq, DV + SUM_ROWS, 4 * HQ), jnp.float32)],
        compiler_params=pltpu.CompilerParams(
            dimension_semantics=("arbitrary", "arbitrary"), vmem_limit_bytes=VMEM_LIMIT),
        name="attn",
    )(q, k, vt, lq1, lk1, lq2, lk2, sg_col, w_gate, w_up, w_down)


ROUTER_ROWS = 128
EXPERT_ROW0 = 32


def _outproj_kernel(x_ref, attn_ref, conv_ref, wo_ref, g2_ref, wr2_ref,
                    h_ref, xs_ref, meta_ref, cnt_ref):
    tm = x_ref.shape[0]
    t = T_SORT
    f32, bf16 = jnp.float32, jnp.bfloat16
    a = jnp.concatenate([attn_ref[...], conv_ref[...]], axis=1)
    h = x_ref[...] + _dot(a, wo_ref[...])
    h_ref[...] = h
    ms = jnp.mean(h * h, axis=-1, keepdims=True)
    hn = h * lax.rsqrt(ms + NORM_EPS) * g2_ref[...]
    hn_hi, hn_lo = _split2(hn)

    hh = _dot(hn_hi, wr2_ref[...])
    logits = hh[:, :ROUTER_ROWS] + hh[:, ROUTER_ROWS:] + _dot(hn_lo, wr2_ref[:, :ROUTER_ROWS])
    lt = logits.T
    row8 = lax.broadcasted_iota(jnp.int32, (SUBLANES, tm), 0).astype(f32)
    neg_inf = -jnp.inf

    def first_argmax(v):
        mx = jnp.max(v, axis=0, keepdims=True)
        idx = jnp.min(jnp.where(v == mx, row8, float(SUBLANES)), axis=0, keepdims=True)
        return mx, idx

    g_log = jnp.where(row8 < N_GROUPS, lt[0:SUBLANES, :], neg_inf)
    g_max, g_sel = first_argmax(g_log)
    g_gate = 1.0 / jnp.sum(jnp.exp(g_log - g_max), axis=0, keepdims=True)
    e_log = jnp.zeros((EPG, tm), f32)
    for g in range(N_GROUPS):
        rows = lt[EXPERT_ROW0 + g * EPG:EXPERT_ROW0 + (g + 1) * EPG, :]
        e_log = jnp.where(g_sel == float(g), rows, e_log)
    v1, i1 = first_argmax(e_log)
    v2, i2 = first_argmax(jnp.where(row8 == i1, neg_inf, e_log))
    tt = jnp.exp(v2 - v1)
    w1 = g_gate / (1.0 + tt)
    w2 = g_gate * tt / (1.0 + tt)
    e1 = g_sel * float(EPG) + i1
    e2 = g_sel * float(EPG) + i2

    row32 = lax.broadcasted_iota(jnp.int32, (N_EXPERTS, tm), 0).astype(f32)
    oh1 = row32 == e1
    oh2 = row32 == e2
    c = jnp.where(oh1 | oh2, 1.0, 0.0).astype(bf16)
    tok_r = lax.broadcasted_iota(jnp.int32, (tm, tm), 0)
    tok_c = lax.broadcasted_iota(jnp.int32, (tm, tm), 1)
    same_tile = (tok_r // t) == (tok_c // t)
    rank = _dot(c, jnp.where(same_tile & (tok_r < tok_c), 1.0, 0.0).astype(bf16))
    cnt_b = _dot(c, jnp.where(same_tile, 1.0, 0.0).astype(bf16))
    ex_r = lax.broadcasted_iota(jnp.int32, (N_EXPERTS, N_EXPERTS), 0)
    ex_c = lax.broadcasted_iota(jnp.int32, (N_EXPERTS, N_EXPERTS), 1)
    lower = jnp.where(ex_c < ex_r, 1.0, 0.0).astype(bf16)
    start_b = _dot(lower, cnt_b.astype(bf16))
    pos_e = start_b + rank
    p1 = jnp.sum(jnp.where(oh1, pos_e, 0.0), axis=0, keepdims=True)
    p2 = jnp.sum(jnp.where(oh2, pos_e, 0.0), axis=0, keepdims=True)

    srow = lax.broadcasted_iota(jnp.int32, (2 * t, t), 0).astype(f32)
    for s in range(tm // t):
        cols = slice(s * t, (s + 1) * t)
        perm = jnp.where((srow == p1[:, cols]) | (srow == p2[:, cols]), 1.0, 0.0).astype(bf16)
        xs = _dot(perm, hn_hi[cols, :])
        xs_ref[s * 2 * t:(s + 1) * 2 * t] = _pack_rows(xs)
        cnt_ref[s * N_EXPERTS:(s + 1) * N_EXPERTS, :] = cnt_b[:, s * t:s * t + LANES]

    meta = jnp.concatenate([p1, p2, w1, w2, jnp.zeros((LANES - 4, tm), f32)], axis=0)
    meta_ref[...] = meta.T


def _outproj(x2, attn_o, conv_o, wo, g2, wr2):
    n = x2.shape[0]
    tm = TM_OUT
    nsub = tm // T_SORT
    const = lambda i: (0, 0)

    return pl.pallas_call(
        _outproj_kernel,
        grid=(n // tm,),
        in_specs=[
            pl.BlockSpec((tm, D_MODEL), lambda i: (i, 0)),
            pl.BlockSpec((tm, ATTN_WIDTH), lambda i: (i, 0)),
            pl.BlockSpec((tm, CONV_WIDTH), lambda i: (i, 0)),
            pl.BlockSpec((D_MODEL, D_MODEL), const),
            pl.BlockSpec((1, D_MODEL), const),
            pl.BlockSpec((D_MODEL, 2 * ROUTER_ROWS), const),
        ],
        out_specs=[
            pl.BlockSpec((tm, D_MODEL), lambda i: (i, 0)),
            pl.BlockSpec((2 * tm, ROW_CHUNKS, LANES), lambda i: (i, 0, 0)),
            pl.BlockSpec((tm, LANES), lambda i: (i, 0)),
            pl.BlockSpec((nsub * N_EXPERTS, LANES), lambda i: (i, 0)),
        ],
        out_shape=[
            jax.ShapeDtypeStruct((n, D_MODEL), jnp.float32),
            jax.ShapeDtypeStruct((2 * n, ROW_CHUNKS, LANES), PACKED_DTYPE),
            jax.ShapeDtypeStruct((n, LANES), jnp.float32),
            jax.ShapeDtypeStruct((n // T_SORT * N_EXPERTS, LANES), jnp.float32),
        ],
        compiler_params=pltpu.CompilerParams(
            dimension_semantics=("arbitrary",), vmem_limit_bytes=VMEM_LIMIT),
        name="outproj",
    )(x2, attn_o, conv_o, wo, g2, wr2)


def _pack_rows(x):
    r = x.shape[0]
    w = pltpu.pack_elementwise([x[:, :ROW_WORDS], x[:, ROW_WORDS:]], packed_dtype=jnp.bfloat16)
    return pltpu.bitcast(w, PACKED_DTYPE).reshape(r, ROW_CHUNKS, LANES)


def _packed_zero_rows(r):
    z = jnp.zeros((r, ROW_CHUNKS, LANES), jnp.float32)
    w = pltpu.pack_elementwise([z, z], packed_dtype=jnp.bfloat16)
    return pltpu.bitcast(w, PACKED_DTYPE)


def _unpack_rows(u):
    r = u.shape[0]
    w = u.reshape(r, ROW_WORDS)
    lo = pltpu.unpack_elementwise(w, index=0, packed_dtype=jnp.bfloat16, unpacked_dtype=jnp.float32)
    hi = pltpu.unpack_elementwise(w, index=1, packed_dtype=jnp.bfloat16, unpacked_dtype=jnp.float32)
    return jnp.concatenate([lo, hi], axis=1).astype(jnp.bfloat16)


def _expert_kernel(nblk_ref, blk0_ref, be_ref, nused_ref, r0_ref, nvalid_ref, tlo_ref, thi_ref,
                   cnt_ref, src_ref, rbase_ref,
                   xs_hbm, wg_ref, wu_ref, wd_ref, y_hbm,
                   xbuf, ybuf, sem, ysem):
    e = pl.program_id(0)
    nused = nused_ref[0]
    n_blocks = be_ref.shape[0]
    n_tiles = cnt_ref.shape[0] // N_EXPERTS

    def start_run(blk, slot, t, live):
        r0 = r0_ref[blk]
        k = t * N_EXPERTS + be_ref[blk]
        lo = jnp.maximum(rbase_ref[k], r0)
        hi = jnp.minimum(rbase_ref[k] + cnt_ref[k], r0 + MOE_BLOCK)
        rows = jnp.where(live, hi - lo, 0)

        @pl.when(rows > 0)
        def _():
            pltpu.make_async_copy(xs_hbm.at[pl.ds(src_ref[k] + lo - rbase_ref[k], rows)],
                                  xbuf.at[slot, pl.ds(lo - r0, rows)], sem.at[slot]).start(priority=GATHER_PRIORITY)

    def gather_rolled(blk, slot, t_from):
        def run(t, carry):
            start_run(blk, slot, t, True)
            return carry
        lax.fori_loop(t_from, thi_ref[blk], run, 0)

    def gather_unrolled(blk, slot, live, part):
        blk = jnp.minimum(blk, n_blocks - 1)
        per = GATHER_UNROLL // GATHER_PARTS
        for r in range(part * per, (part + 1) * per):
            t = tlo_ref[blk] + r
            start_run(blk, slot, jnp.minimum(t, n_tiles - 1), live & (t < thi_ref[blk]))
        if part == GATHER_PARTS - 1:
            @pl.when(live & (tlo_ref[blk] + GATHER_UNROLL < thi_ref[blk]))
            def _():
                gather_rolled(blk, slot, tlo_ref[blk] + GATHER_UNROLL)

    def y_copy(blk, slot):
        return pltpu.make_async_copy(ybuf.at[slot], y_hbm.at[pl.ds(blk * MOE_BLOCK, MOE_BLOCK)], ysem.at[slot])

    @pl.when(e == 0)
    def _():
        for slot in range(N_XBUF):
            xbuf[slot] = _packed_zero_rows(MOE_BLOCK)
        gather_rolled(0, 0, tlo_ref[0])
        for blk in range(1, GATHER_AHEAD):
            @pl.when(nused > blk)
            def _():
                gather_rolled(blk, blk, tlo_ref[blk])
        for slot in range(N_YBUF):
            ybuf[slot] = _packed_zero_rows(MOE_BLOCK)
            y_copy(slot, slot).start()

    @pl.when(nblk_ref[e] > 0)
    def _():
        def blocks(b0, count):
            ids = [b0 + j for j in range(count)]
            for b in ids:
                nv = nvalid_ref[b]
                y_copy(b, b % N_YBUF).wait()
                pltpu.make_async_copy(xs_hbm.at[pl.ds(0, nv)], xbuf.at[b % N_XBUF, pl.ds(0, nv)],
                                      sem.at[b % N_XBUF]).wait()

            def refill(b, part):
                ahead = b + GATHER_AHEAD
                gather_unrolled(ahead, ahead % N_XBUF, ahead < nused, part)

            def gate_up(x):
                return _dot(x, wg_ref[0]), _dot(x, wu_ref[0])

            def activation(gu):
                g, u = gu
                return (g / (1.0 + jnp.exp(-g)) * u).astype(jnp.bfloat16)

            def send(b, y):
                ybuf[b % N_YBUF] = _pack_rows(y)
                y_copy(b, b % N_YBUF).start()

            gus = [gate_up(_unpack_rows(xbuf[b % N_XBUF])) for b in ids]
            act = activation(gus[0])
            for j, b in enumerate(ids):
                refill(b, 0)
                y = _dot(act, wd_ref[0])
                refill(b, 1)
                if j + 1 < count:
                    act = activation(gus[j + 1])
                refill(b, 2)
                send(b, y)

        first = blk0_ref[e]

        def pair_body(c, carry):
            blocks(first + 2 * c, 2)
            return carry
        lax.fori_loop(0, nblk_ref[e] // 2, pair_body, 0)

        @pl.when(nblk_ref[e] % 2 == 1)
        def _():
            blocks(first + nblk_ref[e] - 1, 1)

    @pl.when(e == N_EXPERTS - 1)
    def _():
        for slot in range(N_YBUF):
            y_copy(slot, slot).wait()
        ybuf[0] = _packed_zero_rows(MOE_BLOCK)

        def zero_block(b, carry):
            y_copy(b, 0).start()
            y_copy(b, 0).wait()
            return carry
        lax.fori_loop(nused, n_blocks, zero_block, 0)


def _experts(tabs, xs, w_gate, w_up, w_down):
    n_slots = tabs["n_slots"]
    w_map = lambda e, *_: (e, 0, 0)
    return pl.pallas_call(
        _expert_kernel,
        grid_spec=pltpu.PrefetchScalarGridSpec(
            num_scalar_prefetch=11,
            grid=(N_EXPERTS,),
            in_specs=[
                pl.BlockSpec(memory_space=pl.ANY),
                pl.BlockSpec((1, D_MODEL, D_FF), w_map),
                pl.BlockSpec((1, D_MODEL, D_FF), w_map),
                pl.BlockSpec((1, D_FF, D_MODEL), w_map),
            ],
            out_specs=pl.BlockSpec(memory_space=pl.ANY),
            scratch_shapes=[pltpu.VMEM((N_XBUF, MOE_BLOCK, ROW_CHUNKS, LANES), PACKED_DTYPE),
                            pltpu.VMEM((N_YBUF, MOE_BLOCK, ROW_CHUNKS, LANES), PACKED_DTYPE),
                            pltpu.SemaphoreType.DMA((N_XBUF,)),
                            pltpu.SemaphoreType.DMA((N_YBUF,))],
        ),
        out_shape=jax.ShapeDtypeStruct((n_slots, ROW_CHUNKS, LANES), PACKED_DTYPE),
        compiler_params=pltpu.CompilerParams(
            dimension_semantics=("arbitrary",), vmem_limit_bytes=VMEM_LIMIT),
        name="experts",
    )(tabs["nblk"], tabs["blk0"], tabs["block_e"], tabs["nused"], tabs["r0"], tabs["nvalid"], tabs["tlo"],
      tabs["thi"], tabs["cnt"], tabs["src"], tabs["rbase"], xs, w_gate, w_up, w_down)


def _combine_kernel(cnt_ref, loc_ref, dst_ref, h_ref, meta_ref, yg_hbm, o_ref, ybuf, sem, *, n_steps, nsub):
    i = pl.program_id(0)
    t = T_SORT
    f32, bf16 = jnp.float32, jnp.bfloat16

    slot_in = i % 2
    tile0 = jnp.minimum(i, n_steps - 1) * nsub
    for s in range(nsub):
        for e in range(N_EXPERTS):
            k = (tile0 + s) * N_EXPERTS + e
            rows = jnp.where(i < n_steps, cnt_ref[k], 0)

            @pl.when(rows > 0)
            def _():
                pltpu.make_async_copy(yg_hbm.at[pl.ds(dst_ref[k], rows)],
                                      ybuf.at[slot_in, pl.ds(s * 2 * t + loc_ref[k], rows)],
                                      sem.at[slot_in]).start(priority=GATHER_PRIORITY)

    @pl.when(i > 0)
    def _():
        slot = (i - 1) % 2
        pltpu.make_async_copy(yg_hbm.at[pl.ds(0, nsub * 2 * t)], ybuf.at[slot], sem.at[slot]).wait()
        lane = lax.broadcasted_iota(jnp.int32, (t, 2 * t), 1).astype(f32)
        for s in range(nsub):
            y = _unpack_rows(ybuf[slot, s * 2 * t:(s + 1) * 2 * t])
            meta = meta_ref[s * t:(s + 1) * t, :]
            pick1 = jnp.where(lane == meta[:, 0:1], 1.0, 0.0).astype(bf16)
            pick2 = jnp.where(lane == meta[:, 1:2], 1.0, 0.0).astype(bf16)
            y12 = _dot(jnp.concatenate([pick1, pick2], axis=0), y)
            o_ref[s * t:(s + 1) * t, :] = h_ref[s * t:(s + 1) * t, :] + (meta[:, 2:3] * y12[:t]
                                                                         + meta[:, 3:4] * y12[t:])


def _combine(cnt, loc, dst, h, meta, yg):
    n = h.shape[0]
    tm = TM_OUT
    nsub = tm // T_SORT
    n_steps = n // tm
    prev = lambda i, *_: (jnp.maximum(i - 1, 0), 0)
    return pl.pallas_call(
        functools.partial(_combine_kernel, n_steps=n_steps, nsub=nsub),
        grid_spec=pltpu.PrefetchScalarGridSpec(
            num_scalar_prefetch=3,
            grid=(n_steps + 1,),
            in_specs=[
                pl.BlockSpec((tm, D_MODEL), prev),
                pl.BlockSpec((tm, LANES), prev),
                pl.BlockSpec(memory_space=pl.ANY),
            ],
            out_specs=pl.BlockSpec((tm, D_MODEL), prev),
            scratch_shapes=[pltpu.VMEM((2, nsub * 2 * T_SORT, ROW_CHUNKS, LANES), PACKED_DTYPE),
                            pltpu.SemaphoreType.DMA((2,))],
        ),
        out_shape=jax.ShapeDtypeStruct((n, D_MODEL), jnp.float32),
        compiler_params=pltpu.CompilerParams(
            dimension_semantics=("arbitrary",), vmem_limit_bytes=VMEM_LIMIT),
        name="combine",
    )(cnt, loc, dst, h, meta, yg)


def _routing_tables(cnt_out, n_tiles, n_tok):
    i32 = jnp.int32
    cnt = cnt_out.reshape(n_tiles, N_EXPERTS, LANES)[:, :, 0].astype(i32)
    count = jnp.sum(cnt, axis=0)
    padded = ((count + MOE_BLOCK - 1) // MOE_BLOCK) * MOE_BLOCK
    pad_end = jnp.cumsum(padded)
    pad_start = pad_end - padded
    run_end = jnp.cumsum(cnt, axis=0)
    rbase = run_end - cnt
    dst = pad_start[None, :] + rbase
    loc = jnp.cumsum(cnt, axis=1) - cnt
    src = loc + (jnp.arange(n_tiles, dtype=i32) * (2 * T_SORT))[:, None]
    n_blocks = (2 * n_tok) // MOE_BLOCK + N_EXPERTS
    block_start = jnp.arange(n_blocks, dtype=i32) * MOE_BLOCK
    block_e = jnp.minimum(jnp.sum(pad_end[None, :] <= block_start[:, None], axis=1), N_EXPERTS - 1).astype(i32)
    nused = (pad_end[-1] // MOE_BLOCK).astype(i32).reshape(1)
    ex = jnp.arange(N_EXPERTS, dtype=i32)
    sel = (block_e[None, :] == ex[:, None]).astype(i32)
    pick = lambda per_expert: jnp.sum(per_expert[..., :, None] * sel, axis=-2)
    r0 = block_start - pick(pad_start)
    nvalid = jnp.clip(pick(count) - r0, 0, MOE_BLOCK)
    tlo = jnp.sum(pick(run_end) <= r0[None, :], axis=0)
    thi = jnp.sum(pick(rbase) < (r0 + MOE_BLOCK)[None, :], axis=0)
    flat = lambda a: a.reshape(-1).astype(i32)
    return dict(cnt=flat(cnt), src=flat(src), dst=flat(dst), loc=flat(loc), rbase=flat(rbase),
                block_e=block_e, nused=nused, r0=flat(r0), nvalid=flat(nvalid), tlo=flat(tlo), thi=flat(thi),
                nblk=flat(padded // MOE_BLOCK), blk0=flat(pad_start // MOE_BLOCK), n_slots=n_blocks * MOE_BLOCK)


def _stage1(x, positions, attn_norm_gain, w_in, q_norm_gain, k_norm_gain, conv_w, conv_out_gain):
    bsz, seq, _ = x.shape
    n = bsz * seq
    f32, bf16 = jnp.float32, jnp.bfloat16
    w = w_in[0]
    wb = w.astype(bf16)
    wvt = w[:, 2 * ATTN_WIDTH:3 * ATTN_WIDTH].T.astype(bf16)
    scale = DK ** -0.5 * LOG2E
    gqk = jnp.concatenate([jnp.tile(q_norm_gain[0].astype(f32), 2 * N_HEADS) * scale,
                           jnp.tile(k_norm_gain[0].astype(f32), 2 * N_HEADS)]).reshape(1, -1)
    freqs = (ROPE_THETA ** (-jnp.arange(0, ROT_DIM, 2, dtype=f32) / ROT_DIM)).reshape(SUBLANES, 1)
    return _inproj(x.reshape(n, D_MODEL), positions.reshape(1, n),
                   attn_norm_gain[0].reshape(1, -1).astype(f32), wb, wvt, wb, gqk, freqs,
                   conv_w[0].astype(f32), conv_out_gain[0].reshape(1, -1).astype(f32), seq)


def kernel(x, positions, attn_norm_gain, w_in, q_norm_gain, k_norm_gain, lambda_q1, lambda_k1, lambda_q2, lambda_k2, subln_gain, conv_w, conv_out_gain, w_out, ffn_norm_gain, w_group_router, w_expert_router, w_gate, w_up, w_down):
    bsz, seq, _ = x.shape
    n = bsz * seq
    f32, bf16 = jnp.float32, jnp.bfloat16
    assert TM_OUT % T_SORT == 0 and seq % TM_IN == 0 and seq % TQ == 0 and TQ % KV_FULL == 0
    q, k, vt, conv_o = _stage1(x, positions, attn_norm_gain, w_in, q_norm_gain, k_norm_gain,
                                conv_w, conv_out_gain)
    attn_o, wg_b, wu_b, wd_b = _attention(q, k, vt,
                        lambda_q1[0].reshape(1, -1).astype(f32), lambda_k1[0].reshape(1, -1).astype(f32),
                        lambda_q2[0].reshape(1, -1).astype(f32), lambda_k2[0].reshape(1, -1).astype(f32),
                        subln_gain[0].reshape(-1, 1).astype(f32), w_gate[0], w_up[0], w_down[0], bsz, seq)

    wr = jnp.concatenate([
        w_group_router[0].astype(f32), jnp.zeros((D_MODEL, EXPERT_ROW0 - N_GROUPS), f32),
        jnp.transpose(w_expert_router[0].astype(f32), (1, 0, 2)).reshape(D_MODEL, N_EXPERTS),
        jnp.zeros((D_MODEL, ROUTER_ROWS - EXPERT_ROW0 - N_EXPERTS), f32)], axis=1)
    wrh = wr.astype(bf16)
    wr2 = jnp.concatenate([wrh, (wr - wrh.astype(f32)).astype(bf16)], axis=1)
    h, xs, meta, cnt_out = _outproj(x.reshape(n, D_MODEL), attn_o, conv_o, w_out[0].astype(bf16),
                                    ffn_norm_gain[0].reshape(1, -1).astype(f32), wr2)

    tabs = _routing_tables(cnt_out, n // T_SORT, n)
    yg = _experts(tabs, xs, wg_b, wu_b, wd_b)
    out = _combine(tabs["cnt"], tabs["loc"], tabs["dst"], h, meta, yg)
    return out.reshape(x.shape)
```

```python
import functools
import math

import jax
import jax.numpy as jnp
from jax import lax
from jax.experimental import pallas as pl
from jax.experimental.pallas import tpu as pltpu

D_MODEL = 1024
N_HEADS = 4
DK = 64
DV = 128
ROT_DIM = 16
ROPE_THETA = 500000.0
ATTN_WIDTH = N_HEADS * DV
CONV_WIDTH = 512
NORM_EPS = 1e-6
LOG2E = 1.4426950408889634
LAMBDA_INIT = 0.8 - 0.6 * math.exp(-0.3 * 0)
N_GROUPS = 4
EPG = 8
N_EXPERTS = N_GROUPS * EPG
D_FF = 512
MOE_BLOCK = 256

LANES = 128
SUBLANES = 8
ROW_WORDS = D_MODEL // 2
ROW_CHUNKS = ROW_WORDS // LANES
PACKED_DTYPE = jnp.uint32

TM_IN = 1024
QK_CHUNK = 256
TQ = 512
HQ = TQ // 2
KV_FULL = 512
SUM_ROWS = 16
T_SORT = 256
TM_OUT = 512
N_XBUF = 3
GATHER_UNROLL = 24
GATHER_PARTS = 3
COMBINE_AHEAD = 3
GATHER_PRIORITY = 1
VMEM_LIMIT = 48 * 1024 * 1024


def _nt_dot(a, b):
    return lax.dot_general(a, b, (((1,), (1,)), ((), ())), preferred_element_type=jnp.float32)


def _dot(a, b):
    return jnp.dot(a, b, preferred_element_type=jnp.float32)


def _split3(x):
    h = x.astype(jnp.bfloat16)
    r = x - h.astype(jnp.float32)
    m = r.astype(jnp.bfloat16)
    l = (r - m.astype(jnp.float32)).astype(jnp.bfloat16)
    return h, m, l


def _split2(x):
    h = x.astype(jnp.bfloat16)
    l = (x - h.astype(jnp.float32)).astype(jnp.bfloat16)
    return h, l


def _inproj_kernel(x_ref, pos_ref, g1_ref, wqk_ref, wvt_ref, wc_ref, gqk_ref, freq_ref,
                   cw_ref, cg_ref,
                   q_ref, k_ref, vt_ref, conv_ref,
                   carry_ref, *, tiles_per_seq):
    tm = x_ref.shape[0]
    tp = tm // 2
    i = pl.program_id(0)
    half = ROT_DIM // 2
    lane_r = lax.broadcasted_iota(jnp.int32, (LANES, SUBLANES), 0)
    f_c = lax.broadcasted_iota(jnp.int32, (LANES, SUBLANES), 1)
    in_rot = (lane_r % DK) < ROT_DIM
    expand = jnp.where(in_rot & ((lane_r % half) == f_c), 1.0, 0.0).astype(jnp.bfloat16)
    seg_r = lax.broadcasted_iota(jnp.int32, (QK_CHUNK, QK_CHUNK), 0) // DK
    seg_c = lax.broadcasted_iota(jnp.int32, (QK_CHUNK, QK_CHUNK), 1) // DK
    seg_mean = jnp.where(seg_r == seg_c, 1.0 / DK, 0.0).astype(jnp.bfloat16)
    d = lax.broadcasted_iota(jnp.int32, (tp, LANES), 1) % DK
    row = lax.broadcasted_iota(jnp.int32, (tp, CONV_WIDTH), 0)

    def normed(p):
        x = x_ref[p * tp:(p + 1) * tp, :]
        ms = jnp.mean(x * x, axis=-1, keepdims=True)
        return (x * lax.rsqrt(ms + NORM_EPS) * g1_ref[...]).astype(jnp.bfloat16)

    def rotary_tables(p):
        pos = pos_ref[:, p * tp:(p + 1) * tp].astype(jnp.float32)
        ang = freq_ref[...] * pos

        def to_rows(t):
            h, m, l = _split3(t)
            return (_dot(expand, h) + _dot(expand, m) + _dot(expand, l)).T

        cos_r, sin_r = to_rows(jnp.cos(ang)), to_rows(jnp.sin(ang))
        return (jnp.where(d < ROT_DIM, cos_r, 1.0),
                jnp.where(d < half, -sin_r, 0.0),
                jnp.where((d >= half) & (d < ROT_DIM), sin_r, 0.0))

    def qk_product(hn, c2):
        return _dot(hn, wqk_ref[:, c2 * QK_CHUNK:(c2 + 1) * QK_CHUNK])

    def qk_rows(p, c2, blk, tables):
        rows = slice(p * tp, (p + 1) * tp)
        cos_r, sin_lo, sin_hi = tables
        msq = _dot((blk * blk).astype(jnp.bfloat16), seg_mean)
        y2 = blk * lax.rsqrt(msq + NORM_EPS) * gqk_ref[:, c2 * QK_CHUNK:(c2 + 1) * QK_CHUNK]
        for c1 in range(QK_CHUNK // LANES):
            c = c2 * (QK_CHUNK // LANES) + c1
            y = y2[:, c1 * LANES:(c1 + 1) * LANES]
            rot = (y * cos_r
                   + pltpu.roll(y, LANES - half, 1) * sin_lo
                   + pltpu.roll(y, half, 1) * sin_hi).astype(jnp.bfloat16)
            if c < N_HEADS:
                q_ref[rows, c * LANES:(c + 1) * LANES] = rot
            else:
                k_ref[rows, (c - N_HEADS) * LANES:(c - N_HEADS + 1) * LANES] = rot

    def conv_product(hn, j):
        return _dot(hn, wc_ref[:, j * CONV_WIDTH:(j + 1) * CONV_WIDTH])

    def v_rows(p, hn):
        vt_ref[:, p * tp:(p + 1) * tp] = _nt_dot(wvt_ref[...], hn).astype(jnp.bfloat16)

    def conv_rows(p, cp, prev):
        cb, cc, cu = cp
        y = cc * cu
        p1 = prev[SUBLANES - 1:SUBLANES, :]
        p2 = prev[SUBLANES - 2:SUBLANES - 1, :]
        y1 = jnp.where(row == 0, p1, pltpu.roll(y, 1, 0))
        y2 = jnp.where(row == 0, p2, jnp.where(row == 1, p1, pltpu.roll(y, 2, 0)))
        z = cw_ref[0:1, :] * y2 + cw_ref[1:2, :] * y1 + cw_ref[2:3, :] * y
        co = cb * z
        cms = jnp.mean(co * co, axis=-1, keepdims=True)
        conv_ref[p * tp:(p + 1) * tp, :] = (co * lax.rsqrt(cms + NORM_EPS) * cg_ref[...]).astype(jnp.bfloat16)
        return y[tp - SUBLANES:, :]

    @pl.when(i % tiles_per_seq == 0)
    def _():
        carry_ref[...] = jnp.zeros_like(carry_ref)

    n_qk = 2 * ATTN_WIDTH // QK_CHUNK
    hn0, tables0 = normed(0), rotary_tables(0)
    qk0 = [qk_product(hn0, c2) for c2 in range(n_qk)]
    hn1, tables1 = normed(1), rotary_tables(1)
    qk1 = []
    for c2 in range(n_qk):
        qk1.append(qk_product(hn1, c2))
        qk_rows(0, c2, qk0[c2], tables0)
    cp0 = []
    for c2 in range(n_qk):
        if c2 < 3:
            cp0.append(conv_product(hn0, c2))
        qk_rows(1, c2, qk1[c2], tables1)
    cp1 = [conv_product(hn1, j) for j in range(3)]
    v_rows(0, hn0)
    v_rows(1, hn1)
    prev = conv_rows(0, cp0, carry_ref[...])
    carry_ref[...] = conv_rows(1, cp1, prev)


def _inproj(x2, pos_row, g1, wqk, wvt, wc, gqk, freqs, cw, cg, seq):
    n = x2.shape[0]
    tm = TM_IN
    grid = (n // tm,)
    const = lambda i: (0, 0)
    return pl.pallas_call(
        functools.partial(_inproj_kernel, tiles_per_seq=seq // tm),
        grid=grid,
        in_specs=[
            pl.BlockSpec((tm, D_MODEL), lambda i: (i, 0)),
            pl.BlockSpec((1, tm), lambda i: (0, i)),
            pl.BlockSpec((1, D_MODEL), const),
            pl.BlockSpec((D_MODEL, 2 * ATTN_WIDTH), const),
            pl.BlockSpec((ATTN_WIDTH, D_MODEL), const),
            pl.BlockSpec((D_MODEL, 3 * CONV_WIDTH), lambda i: (0, 1)),
            pl.BlockSpec((1, 1024), const),
            pl.BlockSpec((SUBLANES, 1), const),
            pl.BlockSpec((3, CONV_WIDTH), const),
            pl.BlockSpec((1, CONV_WIDTH), const),
        ],
        out_specs=[
            pl.BlockSpec((tm, ATTN_WIDTH), lambda i: (i, 0)),
            pl.BlockSpec((tm, ATTN_WIDTH), lambda i: (i, 0)),
            pl.BlockSpec((ATTN_WIDTH, tm), lambda i: (0, i)),
            pl.BlockSpec((tm, CONV_WIDTH), lambda i: (i, 0)),
        ],
        out_shape=[
            jax.ShapeDtypeStruct((n, ATTN_WIDTH), jnp.bfloat16),
            jax.ShapeDtypeStruct((n, ATTN_WIDTH), jnp.bfloat16),
            jax.ShapeDtypeStruct((ATTN_WIDTH, n), jnp.bfloat16),
            jax.ShapeDtypeStruct((n, CONV_WIDTH), jnp.bfloat16),
        ],
        scratch_shapes=[pltpu.VMEM((SUBLANES, CONV_WIDTH), jnp.float32)],
        compiler_params=pltpu.CompilerParams(
            dimension_semantics=("arbitrary",), vmem_limit_bytes=VMEM_LIMIT),
        name="inproj",
    )(x2, pos_row, g1, wqk, wvt, wc, gqk, freqs, cw, cg)


def _attn_kernel(q_ref, k_ref, vt_ref, lq1_ref, lk1_ref, lq2_ref, lk2_ref, sg_ref, wg_ref, wu_ref, wd_ref,
                 o_ref, wgb_ref, wub_ref, wdb_ref, acc_ref, *, nq):
    lam = (jnp.exp(jnp.sum(lq1_ref[...] * lk1_ref[...], axis=-1, keepdims=True))
           - jnp.exp(jnp.sum(lq2_ref[...] * lk2_ref[...], axis=-1, keepdims=True))
           + LAMBDA_INIT)
    tri =(lax.broadcasted_iota(jnp.int32, (HQ, HQ), 0)
           <= lax.broadcasted_iota(jnp.int32, (HQ, HQ), 1))
    row_d = lax.broadcasted_iota(jnp.int32, (LANES, HQ), 0)
    ones_rows = jnp.ones((SUM_ROWS, KV_FULL), jnp.bfloat16)

    def masked(s, n_tri):
        blocks = [jnp.where(tri, s[:, c * HQ:(c + 1) * HQ], -jnp.inf) for c in range(n_tri)]
        return jnp.concatenate(blocks + [s[:, n_tri * HQ:]], axis=1) if n_tri * HQ < s.shape[1] else \
            jnp.concatenate(blocks, axis=1)

    todo = {}
    for t in range(nq):
        lo = t * TQ
        todo[t] = ([(lo, lo + HQ, "mixed"), (lo + HQ, lo + TQ, "upper")]
                   + [(k0, k0 + KV_FULL, "full") for k0 in range(0, lo, KV_FULL)])
    tasks = []
    while any(todo.values()):
        for t in reversed(range(nq)):
            if todo[t]:
                tasks.append((t,) + todo[t].pop(0))
    last_task = {t: max(i for i, tk in enumerate(tasks) if tk[0] == t) for t in range(nq)}
    qzt, m_run = {}, {}

    def q_transposed(t):
        parts = []
        for half in range(2):
            r0 = t * TQ + half * HQ
            qt = q_ref[r0:r0 + HQ, :].astype(jnp.float32).T
            parts.append(jnp.where(row_d < DK, qt, 0.0))
            parts.append(jnp.where(row_d >= DK, qt, 0.0))
        return jnp.concatenate(parts, axis=1).astype(jnp.bfloat16)

    def scores(task):
        t, k0, k1, kind = task
        if t not in qzt:
            qzt[t] = q_transposed(t)
        kb = k_ref[k0:k1, :]
        if kind == "upper":
            return masked(_dot(kb, qzt[t][:, 2 * HQ:]), 2)
        s = _dot(kb, qzt[t])
        return masked(s, 2) if kind == "mixed" else s

    def softmax(task, s):
        t, _, _, kind = task
        mx = jnp.max(s, axis=0, keepdims=True)
        if kind == "mixed":
            m_run[t] = mx
            return jnp.exp2(s - mx).astype(jnp.bfloat16), None
        m_old = m_run[t][:, 2 * HQ:] if kind == "upper" else m_run[t]
        m_new = jnp.maximum(m_old, mx)
        alpha = jnp.exp2(m_old - m_new)
        p = jnp.exp2(s - m_new)
        m_run[t] = jnp.concatenate([m_run[t][:, :2 * HQ], m_new], axis=1) if kind == "upper" else m_new
        return p.astype(jnp.bfloat16), alpha

    def accumulate(task, p, alpha):
        t, k0, k1, kind = task
        vta = jnp.concatenate([vt_ref[:, k0:k1], ones_rows[:, :k1 - k0]], axis=0)
        pv = _dot(vta, p)
        if kind == "mixed":
            acc_ref[t] = pv
        elif kind == "upper":
            acc_ref[t, :, 2 * HQ:] = alpha * acc_ref[t, :, 2 * HQ:] + pv
        else:
            acc_ref[t] = alpha * acc_ref[t] + pv

    def finish(t):
        o_all = acc_ref[t, 0:DV, :] * (1.0 / acc_ref[t, DV:DV + 1, :])
        for half in range(2):
            o = (o_all[:, (2 * half) * HQ:(2 * half + 1) * HQ]
                 - lam * o_all[:, (2 * half + 1) * HQ:(2 * half + 2) * HQ])
            ms = jnp.mean(o * o, axis=0, keepdims=True)
            on = o * lax.rsqrt(ms + NORM_EPS) * sg_ref[...] * (1.0 - LAMBDA_INIT)
            r0 = t * TQ + half * HQ
            o_ref[r0:r0 + HQ, :] = on.T.astype(jnp.bfloat16)

    n = len(tasks)

    def cast_share(step):
        for src, dst in ((wg_ref, wgb_ref), (wu_ref, wub_ref), (wd_ref, wdb_ref)):
            rows = src.shape[1]
            per = -(-rows // (n + 2) // SUM_ROWS) * SUM_ROWS
            r0, r1 = min(step * per, rows), min((step + 1) * per, rows)
            if r1 > r0:
                dst[:, r0:r1, :] = src[:, r0:r1, :].astype(jnp.bfloat16)

    s_prev, p_prev = None, None
    for step in range(n + 2):
        cast_share(step)
        if step >= 2:
            accumulate(tasks[step - 2], *p_prev)
            if last_task[tasks[step - 2][0]] == step - 2:
                finish(tasks[step - 2][0])
        if 1 <= step <= n:
            p_prev = softmax(tasks[step - 1], s_prev)
        if step < n:
            s_prev = scores(tasks[step])


def _attention(q, k, vt, lq1, lk1, lq2, lk2, sg_col, w_gate, w_up, w_down, bsz, seq):
    n = q.shape[0]
    nq = seq // TQ
    steps = bsz * N_HEADS
    assert N_EXPERTS % steps == 0
    epw = N_EXPERTS // steps
    vec = lambda b, h: (0, 0)
    w_map = lambda b, h: (b * N_HEADS + h, 0, 0)
    bf16 = jnp.bfloat16
    return pl.pallas_call(
        functools.partial(_attn_kernel, nq=nq),
        grid=(bsz, N_HEADS),
        in_specs=[
            pl.BlockSpec((seq, DV), lambda b, h: (b, h)),
            pl.BlockSpec((seq, DV), lambda b, h: (b, h)),
            pl.BlockSpec((DV, seq), lambda b, h: (h, b)),
            pl.BlockSpec((1, DK), vec), pl.BlockSpec((1, DK), vec),
            pl.BlockSpec((1, DK), vec), pl.BlockSpec((1, DK), vec),
            pl.BlockSpec((DV, 1), vec),
            pl.BlockSpec((epw, D_MODEL, D_FF), w_map),
            pl.BlockSpec((epw, D_MODEL, D_FF), w_map),
            pl.BlockSpec((epw, D_FF, D_MODEL), w_map),
        ],
        out_specs=[
            pl.BlockSpec((seq, DV), lambda b, h: (b, h)),
            pl.BlockSpec((epw, D_MODEL, D_FF), w_map),
            pl.BlockSpec((epw, D_MODEL, D_FF), w_map),
            pl.BlockSpec((epw, D_FF, D_MODEL), w_map),
        ],
        out_shape=[
            jax.ShapeDtypeStruct((n, ATTN_WIDTH), bf16),
            jax.ShapeDtypeStruct((N_EXPERTS, D_MODEL, D_FF), bf16),
            jax.ShapeDtypeStruct((N_EXPERTS, D_MODEL, D_FF), bf16),
            jax.ShapeDtypeStruct((N_EXPERTS, D_FF, D_MODEL), bf16),
        ],
        scratch_shapes=[pltpu.VMEM((nq, DV + SUM_ROWS, 4 * HQ), jnp.float32)],
        compiler_params=pltpu.CompilerParams(
            dimension_semantics=("arbitrary", "arbitrary"), vmem_limit_bytes=VMEM_LIMIT),
        name="attn",
    )(q, k, vt, lq1, lk1, lq2, lk2, sg_col, w_gate, w_up, w_down)


ROUTER_ROWS = 128
EXPERT_ROW0 = 32


def _outproj_kernel(x_ref, attn_ref, conv_ref, wo_ref, g2_ref, wr2_ref,
                    h_ref, xs_ref, meta_ref, cnt_ref):
    tm = x_ref.shape[0]
    t = T_SORT
    f32, bf16 = jnp.float32, jnp.bfloat16
    a = jnp.concatenate([attn_ref[...], conv_ref[...]], axis=1)
    h = x_ref[...] + _dot(a, wo_ref[...])
    h_ref[...] = h
    ms = jnp.mean(h * h, axis=-1, keepdims=True)
    hn = h * lax.rsqrt(ms + NORM_EPS) * g2_ref[...]
    hn_hi, hn_lo = _split2(hn)

    hh = _dot(hn_hi, wr2_ref[...])
    logits = hh[:, :ROUTER_ROWS] + hh[:, ROUTER_ROWS:] + _dot(hn_lo, wr2_ref[:, :ROUTER_ROWS])
    lt = logits.T
    row8 = lax.broadcasted_iota(jnp.int32, (SUBLANES, tm), 0).astype(f32)
    neg_inf = -jnp.inf

    def first_argmax(v):
        mx = jnp.max(v, axis=0, keepdims=True)
        idx = jnp.min(jnp.where(v == mx, row8, float(SUBLANES)), axis=0, keepdims=True)
        return mx, idx

    g_log = jnp.where(row8 < N_GROUPS, lt[0:SUBLANES, :], neg_inf)
    g_max, g_sel = first_argmax(g_log)
    g_gate = 1.0 / jnp.sum(jnp.exp(g_log - g_max), axis=0, keepdims=True)
    e_log = jnp.zeros((EPG, tm), f32)
    for g in range(N_GROUPS):
        rows = lt[EXPERT_ROW0 + g * EPG:EXPERT_ROW0 + (g + 1) * EPG, :]
        e_log = jnp.where(g_sel == float(g), rows, e_log)
    v1, i1 = first_argmax(e_log)
    v2, i2 = first_argmax(jnp.where(row8 == i1, neg_inf, e_log))
    tt = jnp.exp(v2 - v1)
    w1 = g_gate / (1.0 + tt)
    w2 = g_gate * tt / (1.0 + tt)
    e1 = g_sel * float(EPG) + i1
    e2 = g_sel * float(EPG) + i2

    row32 = lax.broadcasted_iota(jnp.int32, (N_EXPERTS, tm), 0).astype(f32)
    oh1 = row32 == e1
    oh2 = row32 == e2
    c = jnp.where(oh1 | oh2, 1.0, 0.0).astype(bf16)
    tok_r = lax.broadcasted_iota(jnp.int32, (tm, tm), 0)
    tok_c = lax.broadcasted_iota(jnp.int32, (tm, tm), 1)
    same_tile = (tok_r // t) == (tok_c // t)
    rank = _dot(c, jnp.where(same_tile & (tok_r < tok_c), 1.0, 0.0).astype(bf16))
    cnt_b = _dot(c, jnp.where(same_tile, 1.0, 0.0).astype(bf16))
    ex_r = lax.broadcasted_iota(jnp.int32, (N_EXPERTS, N_EXPERTS), 0)
    ex_c = lax.broadcasted_iota(jnp.int32, (N_EXPERTS, N_EXPERTS), 1)
    lower = jnp.where(ex_c < ex_r, 1.0, 0.0).astype(bf16)
    start_b = _dot(lower, cnt_b.astype(bf16))
    pos_e = start_b + rank
    p1 = jnp.sum(jnp.where(oh1, pos_e, 0.0), axis=0, keepdims=True)
    p2 = jnp.sum(jnp.where(oh2, pos_e, 0.0), axis=0, keepdims=True)

    srow = lax.broadcasted_iota(jnp.int32, (2 * t, t), 0).astype(f32)
    for s in range(tm // t):
        cols = slice(s * t, (s + 1) * t)
        perm = jnp.where((srow == p1[:, cols]) | (srow == p2[:, cols]), 1.0, 0.0).astype(bf16)
        xs = _dot(perm, hn_hi[cols, :])
        xs_ref[s * 2 * t:(s + 1) * 2 * t] = _pack_rows(xs)
        cnt_ref[s * N_EXPERTS:(s + 1) * N_EXPERTS, :] = cnt_b[:, s * t:s * t + LANES]

    meta = jnp.concatenate([p1, p2, w1, w2, jnp.zeros((LANES - 4, tm), f32)], axis=0)
    meta_ref[...] = meta.T


def _outproj(x2, attn_o, conv_o, wo, g2, wr2):
    n = x2.shape[0]
    tm = TM_OUT
    nsub = tm // T_SORT
    const = lambda i: (0, 0)

    return pl.pallas_call(
        _outproj_kernel,
        grid=(n // tm,),
        in_specs=[
            pl.BlockSpec((tm, D_MODEL), lambda i: (i, 0)),
            pl.BlockSpec((tm, ATTN_WIDTH), lambda i: (i, 0)),
            pl.BlockSpec((tm, CONV_WIDTH), lambda i: (i, 0)),
            pl.BlockSpec((D_MODEL, D_MODEL), const),
            pl.BlockSpec((1, D_MODEL), const),
            pl.BlockSpec((D_MODEL, 2 * ROUTER_ROWS), const),
        ],
        out_specs=[
            pl.BlockSpec((tm, D_MODEL), lambda i: (i, 0)),
            pl.BlockSpec((2 * tm, ROW_CHUNKS, LANES), lambda i: (i, 0, 0)),
            pl.BlockSpec((tm, LANES), lambda i: (i, 0)),
            pl.BlockSpec((nsub * N_EXPERTS, LANES), lambda i: (i, 0)),
        ],
        out_shape=[
            jax.ShapeDtypeStruct((n, D_MODEL), jnp.float32),
            jax.ShapeDtypeStruct((2 * n, ROW_CHUNKS, LANES), PACKED_DTYPE),
            jax.ShapeDtypeStruct((n, LANES), jnp.float32),
            jax.ShapeDtypeStruct((n // T_SORT * N_EXPERTS, LANES), jnp.float32),
        ],
        compiler_params=pltpu.CompilerParams(
            dimension_semantics=("arbitrary",), vmem_limit_bytes=VMEM_LIMIT),
        name="outproj",
    )(x2, attn_o, conv_o, wo, g2, wr2)


def _pack_rows(x):
    r = x.shape[0]
    w = pltpu.pack_elementwise([x[:, :ROW_WORDS], x[:, ROW_WORDS:]], packed_dtype=jnp.bfloat16)
    return pltpu.bitcast(w, PACKED_DTYPE).reshape(r, ROW_CHUNKS, LANES)


def _packed_zero_rows(r):
    z = jnp.zeros((r, ROW_CHUNKS, LANES), jnp.float32)
    w = pltpu.pack_elementwise([z, z], packed_dtype=jnp.bfloat16)
    return pltpu.bitcast(w, PACKED_DTYPE)


def _unpack_rows(u):
    r = u.shape[0]
    w = u.reshape(r, ROW_WORDS)
    lo = pltpu.unpack_elementwise(w, index=0, packed_dtype=jnp.bfloat16, unpacked_dtype=jnp.float32)
    hi = pltpu.unpack_elementwise(w, index=1, packed_dtype=jnp.bfloat16, unpacked_dtype=jnp.float32)
    return jnp.concatenate([lo, hi], axis=1).astype(jnp.bfloat16)


def _expert_kernel(nblk_ref, blk0_ref, be_ref, nused_ref, r0_ref, nvalid_ref, tlo_ref, thi_ref,
                   cnt_ref, src_ref, rbase_ref,
                   xs_hbm, wg_ref, wu_ref, wd_ref, y_hbm,
                   xbuf, ybuf, sem, ysem):
    e = pl.program_id(0)
    nused = nused_ref[0]
    n_blocks = be_ref.shape[0]
    n_tiles = cnt_ref.shape[0] // N_EXPERTS

    def start_run(blk, slot, t, live):
        r0 = r0_ref[blk]
        k = t * N_EXPERTS + be_ref[blk]
        lo = jnp.maximum(rbase_ref[k], r0)
        hi = jnp.minimum(rbase_ref[k] + cnt_ref[k], r0 + MOE_BLOCK)
        rows = jnp.where(live, hi - lo, 0)

        @pl.when(rows > 0)
        def _():
            pltpu.make_async_copy(xs_hbm.at[pl.ds(src_ref[k] + lo - rbase_ref[k], rows)],
                                  xbuf.at[slot, pl.ds(lo - r0, rows)], sem.at[slot]).start(priority=GATHER_PRIORITY)

    def gather_rolled(blk, slot, t_from):
        def run(t, carry):
            start_run(blk, slot, t, True)
            return carry
        lax.fori_loop(t_from, thi_ref[blk], run, 0)

    def gather_unrolled(blk, slot, live, part):
        blk = jnp.minimum(blk, n_blocks - 1)
        per = GATHER_UNROLL // GATHER_PARTS
        for r in range(part * per, (part + 1) * per):
            t = tlo_ref[blk] + r
            start_run(blk, slot, jnp.minimum(t, n_tiles - 1), live & (t < thi_ref[blk]))
        if part == GATHER_PARTS - 1:
            @pl.when(live & (tlo_ref[blk] + GATHER_UNROLL < thi_ref[blk]))
            def _():
                gather_rolled(blk, slot, tlo_ref[blk] + GATHER_UNROLL)

    def y_copy(blk, slot):
        return pltpu.make_async_copy(ybuf.at[slot], y_hbm.at[pl.ds(blk * MOE_BLOCK, MOE_BLOCK)], ysem.at[slot])

    @pl.when(e == 0)
    def _():
        for slot in range(N_XBUF):
            xbuf[slot] = _packed_zero_rows(MOE_BLOCK)
        gather_rolled(0, 0, tlo_ref[0])
        for blk in range(1, N_XBUF - 1):
            @pl.when(nused > blk)
            def _():
                gather_rolled(blk, blk, tlo_ref[blk])
        for slot in range(2):
            ybuf[slot] = _packed_zero_rows(MOE_BLOCK)
            y_copy(slot, slot).start()

    @pl.when(nblk_ref[e] > 0)
    def _():
        def block(b):
            slot = b % N_XBUF
            yslot = b % 2
            nv = nvalid_ref[b]
            y_copy(b, yslot).wait()
            pltpu.make_async_copy(xs_hbm.at[pl.ds(0, nv)], xbuf.at[slot, pl.ds(0, nv)], sem.at[slot]).wait()

            ahead = b + N_XBUF - 1
            x = _unpack_rows(xbuf[slot])
            g = _dot(x, wg_ref[0])
            gather_unrolled(ahead, ahead % N_XBUF, ahead < nused, 0)
            u = _dot(x, wu_ref[0])
            act = (g / (1.0 + jnp.exp(-g)) * u).astype(jnp.bfloat16)
            gather_unrolled(ahead, ahead % N_XBUF, ahead < nused, 1)
            y = _dot(act, wd_ref[0])
            ybuf[yslot] = _pack_rows(y)
            y_copy(b, yslot).start()
            gather_unrolled(ahead, ahead % N_XBUF, ahead < nused, 2)

        def loop_body(c, carry):
            block(blk0_ref[e] + c)
            return carry
        lax.fori_loop(0, nblk_ref[e], loop_body, 0)

    @pl.when(e == N_EXPERTS - 1)
    def _():
        for slot in range(2):
            y_copy(slot, slot).wait()
        ybuf[0] = _packed_zero_rows(MOE_BLOCK)

        def zero_block(b, carry):
            y_copy(b, 0).start()
            y_copy(b, 0).wait()
            return carry
        lax.fori_loop(nused, n_blocks, zero_block, 0)


def _experts(tabs, xs, w_gate, w_up, w_down):
    n_slots = tabs["n_slots"]
    w_map = lambda e, *_: (e, 0, 0)
    return pl.pallas_call(
        _expert_kernel,
        grid_spec=pltpu.PrefetchScalarGridSpec(
            num_scalar_prefetch=11,
            grid=(N_EXPERTS,),
            in_specs=[
                pl.BlockSpec(memory_space=pl.ANY),
                pl.BlockSpec((1, D_MODEL, D_FF), w_map),
                pl.BlockSpec((1, D_MODEL, D_FF), w_map),
                pl.BlockSpec((1, D_FF, D_MODEL), w_map),
            ],
            out_specs=pl.BlockSpec(memory_space=pl.ANY),
            scratch_shapes=[pltpu.VMEM((N_XBUF, MOE_BLOCK, ROW_CHUNKS, LANES), PACKED_DTYPE),
                            pltpu.VMEM((2, MOE_BLOCK, ROW_CHUNKS, LANES), PACKED_DTYPE),
                            pltpu.SemaphoreType.DMA((N_XBUF,)),
                            pltpu.SemaphoreType.DMA((2,))],
        ),
        out_shape=jax.ShapeDtypeStruct((n_slots, ROW_CHUNKS, LANES), PACKED_DTYPE),
        compiler_params=pltpu.CompilerParams(
            dimension_semantics=("arbitrary",), vmem_limit_bytes=VMEM_LIMIT),
        name="experts",
    )(tabs["nblk"], tabs["blk0"], tabs["block_e"], tabs["nused"], tabs["r0"], tabs["nvalid"], tabs["tlo"],
      tabs["thi"], tabs["cnt"], tabs["src"], tabs["rbase"], xs, w_gate, w_up, w_down)


def _combine_kernel(cnt_ref, loc_ref, dst_ref, h_ref, meta_ref, yg_hbm, o_ref, ybuf, sem, *, n_steps, nsub):
    i = pl.program_id(0)
    t = T_SORT
    f32, bf16 = jnp.float32, jnp.bfloat16

    slot_in = i % (COMBINE_AHEAD + 1)
    tile0 = jnp.minimum(i, n_steps - 1) * nsub
    for s in range(nsub):
        for e in range(N_EXPERTS):
            k = (tile0 + s) * N_EXPERTS + e
            rows = jnp.where(i < n_steps, cnt_ref[k], 0)

            @pl.when(rows > 0)
            def _():
                pltpu.make_async_copy(yg_hbm.at[pl.ds(dst_ref[k], rows)],
                                      ybuf.at[slot_in, pl.ds(s * 2 * t + loc_ref[k], rows)],
                                      sem.at[slot_in]).start(priority=GATHER_PRIORITY)

    @pl.when(i >= COMBINE_AHEAD)
    def _():
        slot = (i - COMBINE_AHEAD) % (COMBINE_AHEAD + 1)
        pltpu.make_async_copy(yg_hbm.at[pl.ds(0, nsub * 2 * t)], ybuf.at[slot], sem.at[slot]).wait()
        lane = lax.broadcasted_iota(jnp.int32, (t, 2 * t), 1).astype(f32)
        for s in range(nsub):
            y = _unpack_rows(ybuf[slot, s * 2 * t:(s + 1) * 2 * t])
            meta = meta_ref[s * t:(s + 1) * t, :]
            pick1 = jnp.where(lane == meta[:, 0:1], 1.0, 0.0).astype(bf16)
            pick2 = jnp.where(lane == meta[:, 1:2], 1.0, 0.0).astype(bf16)
            y12 = _dot(jnp.concatenate([pick1, pick2], axis=0), y)
            o_ref[s * t:(s + 1) * t, :] = h_ref[s * t:(s + 1) * t, :] + (meta[:, 2:3] * y12[:t]
                                                                         + meta[:, 3:4] * y12[t:])


def _combine(cnt, loc, dst, h, meta, yg):
    n = h.shape[0]
    tm = TM_OUT
    nsub = tm // T_SORT
    n_steps = n // tm
    prev = lambda i, *_: (jnp.maximum(i - COMBINE_AHEAD, 0), 0)
    return pl.pallas_call(
        functools.partial(_combine_kernel, n_steps=n_steps, nsub=nsub),
        grid_spec=pltpu.PrefetchScalarGridSpec(
            num_scalar_prefetch=3,
            grid=(n_steps + COMBINE_AHEAD,),
            in_specs=[
                pl.BlockSpec((tm, D_MODEL), prev),
                pl.BlockSpec((tm, LANES), prev),
                pl.BlockSpec(memory_space=pl.ANY),
            ],
            out_specs=pl.BlockSpec((tm, D_MODEL), prev),
            scratch_shapes=[pltpu.VMEM((COMBINE_AHEAD + 1, nsub * 2 * T_SORT, ROW_CHUNKS, LANES), PACKED_DTYPE),
                            pltpu.SemaphoreType.DMA((COMBINE_AHEAD + 1,))],
        ),
        out_shape=jax.ShapeDtypeStruct((n, D_MODEL), jnp.float32),
        compiler_params=pltpu.CompilerParams(
            dimension_semantics=("arbitrary",), vmem_limit_bytes=VMEM_LIMIT),
        name="combine",
    )(cnt, loc, dst, h, meta, yg)


def _routing_tables(cnt_out, n_tiles, n_tok):
    i32 = jnp.int32
    cnt = cnt_out.reshape(n_tiles, N_EXPERTS, LANES)[:, :, 0].astype(i32)
    count = jnp.sum(cnt, axis=0)
    padded = ((count + MOE_BLOCK - 1) // MOE_BLOCK) * MOE_BLOCK
    pad_end = jnp.cumsum(padded)
    pad_start = pad_end - padded
    run_end = jnp.cumsum(cnt, axis=0)
    rbase = run_end - cnt
    dst = pad_start[None, :] + rbase
    loc = jnp.cumsum(cnt, axis=1) - cnt
    src = loc + (jnp.arange(n_tiles, dtype=i32) * (2 * T_SORT))[:, None]
    n_blocks = (2 * n_tok) // MOE_BLOCK + N_EXPERTS
    block_start = jnp.arange(n_blocks, dtype=i32) * MOE_BLOCK
    block_e = jnp.minimum(jnp.sum(pad_end[None, :] <= block_start[:, None], axis=1), N_EXPERTS - 1).astype(i32)
    nused = (pad_end[-1] // MOE_BLOCK).astype(i32).reshape(1)
    ex = jnp.arange(N_EXPERTS, dtype=i32)
    sel = (block_e[None, :] == ex[:, None]).astype(i32)
    pick = lambda per_expert: jnp.sum(per_expert[..., :, None] * sel, axis=-2)
    r0 = block_start - pick(pad_start)
    nvalid = jnp.clip(pick(count) - r0, 0, MOE_BLOCK)
    tlo = jnp.sum(pick(run_end) <= r0[None, :], axis=0)
    thi = jnp.sum(pick(rbase) < (r0 + MOE_BLOCK)[None, :], axis=0)
    flat = lambda a: a.reshape(-1).astype(i32)
    return dict(cnt=flat(cnt), src=flat(src), dst=flat(dst), loc=flat(loc), rbase=flat(rbase),
                block_e=block_e, nused=nused, r0=flat(r0), nvalid=flat(nvalid), tlo=flat(tlo), thi=flat(thi),
                nblk=flat(padded // MOE_BLOCK), blk0=flat(pad_start // MOE_BLOCK), n_slots=n_blocks * MOE_BLOCK)


def _stage1(x, positions, attn_norm_gain, w_in, q_norm_gain, k_norm_gain, conv_w, conv_out_gain):
    bsz, seq, _ = x.shape
    n = bsz * seq
    f32, bf16 = jnp.float32, jnp.bfloat16
    w = w_in[0]
    wb = w.astype(bf16)
    wvt = w[:, 2 * ATTN_WIDTH:3 * ATTN_WIDTH].T.astype(bf16)
    scale = DK ** -0.5 * LOG2E
    gqk = jnp.concatenate([jnp.tile(q_norm_gain[0].astype(f32), 2 * N_HEADS) * scale,
                           jnp.tile(k_norm_gain[0].astype(f32), 2 * N_HEADS)]).reshape(1, -1)
    freqs = (ROPE_THETA ** (-jnp.arange(0, ROT_DIM, 2, dtype=f32) / ROT_DIM)).reshape(SUBLANES, 1)
    return _inproj(x.reshape(n, D_MODEL), positions.reshape(1, n),
                   attn_norm_gain[0].reshape(1, -1).astype(f32), wb, wvt, wb, gqk, freqs,
                   conv_w[0].astype(f32), conv_out_gain[0].reshape(1, -1).astype(f32), seq)


def kernel(x, positions, attn_norm_gain, w_in, q_norm_gain, k_norm_gain, lambda_q1, lambda_k1, lambda_q2, lambda_k2, subln_gain, conv_w, conv_out_gain, w_out, ffn_norm_gain, w_group_router, w_expert_router, w_gate, w_up, w_down):
    bsz, seq, _ = x.shape
    n = bsz * seq
    f32, bf16 = jnp.float32, jnp.bfloat16
    assert TM_OUT % T_SORT == 0 and seq % TM_IN == 0 and seq % TQ == 0 and TQ % KV_FULL == 0
    q, k, vt, conv_o = _stage1(x, positions, attn_norm_gain, w_in, q_norm_gain, k_norm_gain,
                                conv_w, conv_out_gain)
    attn_o, wg_b, wu_b, wd_b = _attention(q, k, vt,
                        lambda_q1[0].reshape(1, -1).astype(f32), lambda_k1[0].reshape(1, -1).astype(f32),
                        lambda_q2[0].reshape(1, -1).astype(f32), lambda_k2[0].reshape(1, -1).astype(f32),
                        subln_gain[0].reshape(-1, 1).astype(f32), w_gate[0], w_up[0], w_down[0], bsz, seq)

    wr = jnp.concatenate([
        w_group_router[0].astype(f32), jnp.zeros((D_MODEL, EXPERT_ROW0 - N_GROUPS), f32),
        jnp.transpose(w_expert_router[0].astype(f32), (1, 0, 2)).reshape(D_MODEL, N_EXPERTS),
        jnp.zeros((D_MODEL, ROUTER_ROWS - EXPERT_ROW0 - N_EXPERTS), f32)], axis=1)
    wrh = wr.astype(bf16)
    wr2 = jnp.concatenate([wrh, (wr - wrh.astype(f32)).astype(bf16)], axis=1)
    h, xs, meta, cnt_out = _outproj(x.reshape(n, D_MODEL), attn_o, conv_o, w_out[0].astype(bf16),
                                    ffn_norm_gain[0].reshape(1, -1).astype(f32), wr2)

    tabs = _routing_tables(cnt_out, n // T_SORT, n)
    yg = _experts(tabs, xs, wg_b, wu_b, wd_b)
    out = _combine(tabs["cnt"], tabs["loc"], tabs["dst"], h, meta, yg)
    return out.reshape(x.shape)
```

```python
import functools
import math

import jax
import jax.numpy as jnp
from jax import lax
from jax.experimental import pallas as pl
from jax.experimental.pallas import tpu as pltpu

D_MODEL = 1024
N_HEADS = 4
DK = 64
DV = 128
ROT_DIM = 16
ROPE_THETA = 500000.0
ATTN_WIDTH = N_HEADS * DV
CONV_WIDTH = 512
NORM_EPS = 1e-6
LOG2E = 1.4426950408889634
LAMBDA_INIT = 0.8 - 0.6 * math.exp(-0.3 * 0)
N_GROUPS = 4
EPG = 8
N_EXPERTS = N_GROUPS * EPG
D_FF = 512
MOE_BLOCK = 256

LANES = 128
SUBLANES = 8
ROW_WORDS = D_MODEL // 2
ROW_CHUNKS = ROW_WORDS // LANES
PACKED_DTYPE = jnp.uint32

TM_IN = 1024
QK_CHUNK = 256
TQ = 512
HQ = TQ // 2
KV_FULL = 512
SUM_ROWS = 16
T_SORT = 256
TM_OUT = 512
TM_COMBINE = 1024
N_XBUF = 3
GATHER_UNROLL = 24
GATHER_PARTS = 3
COMBINE_AHEAD = 2
GATHER_PRIORITY = 1
VMEM_LIMIT = 48 * 1024 * 1024


def _nt_dot(a, b):
    return lax.dot_general(a, b, (((1,), (1,)), ((), ())), preferred_element_type=jnp.float32)


def _dot(a, b):
    return jnp.dot(a, b, preferred_element_type=jnp.float32)


def _split3(x):
    h = x.astype(jnp.bfloat16)
    r = x - h.astype(jnp.float32)
    m = r.astype(jnp.bfloat16)
    l = (r - m.astype(jnp.float32)).astype(jnp.bfloat16)
    return h, m, l


def _split2(x):
    h = x.astype(jnp.bfloat16)
    l = (x - h.astype(jnp.float32)).astype(jnp.bfloat16)
    return h, l


def _inproj_kernel(x_ref, pos_ref, g1_ref, wqk_ref, wvt_ref, wc_ref, gqk_ref, freq_ref,
                   cw_ref, cg_ref,
                   q_ref, k_ref, vt_ref, conv_ref,
                   carry_ref, *, tiles_per_seq):
    tm = x_ref.shape[0]
    tp = tm // 2
    i = pl.program_id(0)
    half = ROT_DIM // 2
    lane_r = lax.broadcasted_iota(jnp.int32, (LANES, SUBLANES), 0)
    f_c = lax.broadcasted_iota(jnp.int32, (LANES, SUBLANES), 1)
    in_rot = (lane_r % DK) < ROT_DIM
    expand = jnp.where(in_rot & ((lane_r % half) == f_c), 1.0, 0.0).astype(jnp.bfloat16)
    seg_r = lax.broadcasted_iota(jnp.int32, (QK_CHUNK, QK_CHUNK), 0) // DK
    seg_c = lax.broadcasted_iota(jnp.int32, (QK_CHUNK, QK_CHUNK), 1) // DK
    seg_mean = jnp.where(seg_r == seg_c, 1.0 / DK, 0.0).astype(jnp.bfloat16)
    d = lax.broadcasted_iota(jnp.int32, (tp, LANES), 1) % DK
    row = lax.broadcasted_iota(jnp.int32, (tp, CONV_WIDTH), 0)

    def normed(p):
        x = x_ref[p * tp:(p + 1) * tp, :]
        ms = jnp.mean(x * x, axis=-1, keepdims=True)
        return (x * lax.rsqrt(ms + NORM_EPS) * g1_ref[...]).astype(jnp.bfloat16)

    def rotary_tables(p):
        pos = pos_ref[:, p * tp:(p + 1) * tp].astype(jnp.float32)
        ang = freq_ref[...] * pos

        def to_rows(t):
            h, m, l = _split3(t)
            return (_dot(expand, h) + _dot(expand, m) + _dot(expand, l)).T

        cos_r, sin_r = to_rows(jnp.cos(ang)), to_rows(jnp.sin(ang))
        return (jnp.where(d < ROT_DIM, cos_r, 1.0),
                jnp.where(d < half, -sin_r, 0.0),
                jnp.where((d >= half) & (d < ROT_DIM), sin_r, 0.0))

    def qk_product(hn, c2):
        return _dot(hn, wqk_ref[:, c2 * QK_CHUNK:(c2 + 1) * QK_CHUNK])

    def qk_rows(p, c2, blk, tables):
        rows = slice(p * tp, (p + 1) * tp)
        cos_r, sin_lo, sin_hi = tables
        msq = _dot((blk * blk).astype(jnp.bfloat16), seg_mean)
        y2 = blk * lax.rsqrt(msq + NORM_EPS) * gqk_ref[:, c2 * QK_CHUNK:(c2 + 1) * QK_CHUNK]
        for c1 in range(QK_CHUNK // LANES):
            c = c2 * (QK_CHUNK // LANES) + c1
            y = y2[:, c1 * LANES:(c1 + 1) * LANES]
            rot = (y * cos_r
                   + pltpu.roll(y, LANES - half, 1) * sin_lo
                   + pltpu.roll(y, half, 1) * sin_hi).astype(jnp.bfloat16)
            if c < N_HEADS:
                q_ref[rows, c * LANES:(c + 1) * LANES] = rot
            else:
                k_ref[rows, (c - N_HEADS) * LANES:(c - N_HEADS + 1) * LANES] = rot

    def conv_product(hn, j):
        return _dot(hn, wc_ref[:, j * CONV_WIDTH:(j + 1) * CONV_WIDTH])

    def v_rows(p, hn):
        vt_ref[:, p * tp:(p + 1) * tp] = _nt_dot(wvt_ref[...], hn).astype(jnp.bfloat16)

    def conv_rows(p, cp, prev):
        cb, cc, cu = cp
        y = cc * cu
        p1 = prev[SUBLANES - 1:SUBLANES, :]
        p2 = prev[SUBLANES - 2:SUBLANES - 1, :]
        y1 = jnp.where(row == 0, p1, pltpu.roll(y, 1, 0))
        y2 = jnp.where(row == 0, p2, jnp.where(row == 1, p1, pltpu.roll(y, 2, 0)))
        z = cw_ref[0:1, :] * y2 + cw_ref[1:2, :] * y1 + cw_ref[2:3, :] * y
        co = cb * z
        cms = jnp.mean(co * co, axis=-1, keepdims=True)
        conv_ref[p * tp:(p + 1) * tp, :] = (co * lax.rsqrt(cms + NORM_EPS) * cg_ref[...]).astype(jnp.bfloat16)
        return y[tp - SUBLANES:, :]

    @pl.when(i % tiles_per_seq == 0)
    def _():
        carry_ref[...] = jnp.zeros_like(carry_ref)

    n_qk = 2 * ATTN_WIDTH // QK_CHUNK
    hn0, tables0 = normed(0), rotary_tables(0)
    qk0 = [qk_product(hn0, c2) for c2 in range(n_qk)]
    hn1, tables1 = normed(1), rotary_tables(1)
    qk1 = []
    for c2 in range(n_qk):
        qk1.append(qk_product(hn1, c2))
        qk_rows(0, c2, qk0[c2], tables0)
    cp0 = []
    for c2 in range(n_qk):
        if c2 < 3:
            cp0.append(conv_product(hn0, c2))
        qk_rows(1, c2, qk1[c2], tables1)
    cp1 = [conv_product(hn1, j) for j in range(3)]
    v_rows(0, hn0)
    v_rows(1, hn1)
    prev = conv_rows(0, cp0, carry_ref[...])
    carry_ref[...] = conv_rows(1, cp1, prev)


def _inproj(x2, pos_row, g1, wqk, wvt, wc, gqk, freqs, cw, cg, seq):
    n = x2.shape[0]
    tm = TM_IN
    grid = (n // tm,)
    const = lambda i: (0, 0)
    return pl.pallas_call(
        functools.partial(_inproj_kernel, tiles_per_seq=seq // tm),
        grid=grid,
        in_specs=[
            pl.BlockSpec((tm, D_MODEL), lambda i: (i, 0)),
            pl.BlockSpec((1, tm), lambda i: (0, i)),
            pl.BlockSpec((1, D_MODEL), const),
            pl.BlockSpec((D_MODEL, 2 * ATTN_WIDTH), const),
            pl.BlockSpec((ATTN_WIDTH, D_MODEL), const),
            pl.BlockSpec((D_MODEL, 3 * CONV_WIDTH), lambda i: (0, 1)),
            pl.BlockSpec((1, 1024), const),
            pl.BlockSpec((SUBLANES, 1), const),
            pl.BlockSpec((3, CONV_WIDTH), const),
            pl.BlockSpec((1, CONV_WIDTH), const),
        ],
        out_specs=[
            pl.BlockSpec((tm, ATTN_WIDTH), lambda i: (i, 0)),
            pl.BlockSpec((tm, ATTN_WIDTH), lambda i: (i, 0)),
            pl.BlockSpec((ATTN_WIDTH, tm), lambda i: (0, i)),
            pl.BlockSpec((tm, CONV_WIDTH), lambda i: (i, 0)),
        ],
        out_shape=[
            jax.ShapeDtypeStruct((n, ATTN_WIDTH), jnp.bfloat16),
            jax.ShapeDtypeStruct((n, ATTN_WIDTH), jnp.bfloat16),
            jax.ShapeDtypeStruct((ATTN_WIDTH, n), jnp.bfloat16),
            jax.ShapeDtypeStruct((n, CONV_WIDTH), jnp.bfloat16),
        ],
        scratch_shapes=[pltpu.VMEM((SUBLANES, CONV_WIDTH), jnp.float32)],
        compiler_params=pltpu.CompilerParams(
            dimension_semantics=("arbitrary",), vmem_limit_bytes=VMEM_LIMIT),
        name="inproj",
    )(x2, pos_row, g1, wqk, wvt, wc, gqk, freqs, cw, cg)


def _attn_kernel(q_ref, k_ref, vt_ref, lq1_ref, lk1_ref, lq2_ref, lk2_ref, sg_ref, wg_ref, wu_ref, wd_ref,
                 o_ref, wgb_ref, wub_ref, wdb_ref, acc_ref, *, nq):
    lam = (jnp.exp(jnp.sum(lq1_ref[...] * lk1_ref[...], axis=-1, keepdims=True))
           - jnp.exp(jnp.sum(lq2_ref[...] * lk2_ref[...], axis=-1, keepdims=True))
           + LAMBDA_INIT)
    tri =(lax.broadcasted_iota(jnp.int32, (HQ, HQ), 0)
           <= lax.broadcasted_iota(jnp.int32, (HQ, HQ), 1))
    row_d = lax.broadcasted_iota(jnp.int32, (LANES, HQ), 0)
    ones_rows = jnp.ones((SUM_ROWS, KV_FULL), jnp.bfloat16)

    def masked(s, n_tri):
        blocks = [jnp.where(tri, s[:, c * HQ:(c + 1) * HQ], -jnp.inf) for c in range(n_tri)]
        return jnp.concatenate(blocks + [s[:, n_tri * HQ:]], axis=1) if n_tri * HQ < s.shape[1] else \
            jnp.concatenate(blocks, axis=1)

    todo = {}
    for t in range(nq):
        lo = t * TQ
        todo[t] = ([(lo, lo + HQ, "mixed"), (lo + HQ, lo + TQ, "upper")]
                   + [(k0, k0 + KV_FULL, "full") for k0 in range(0, lo, KV_FULL)])
    tasks = []
    while any(todo.values()):
        for t in reversed(range(nq)):
            if todo[t]:
                tasks.append((t,) + todo[t].pop(0))
    last_task = {t: max(i for i, tk in enumerate(tasks) if tk[0] == t) for t in range(nq)}
    qzt, m_run = {}, {}

    def q_transposed(t):
        parts = []
        for half in range(2):
            r0 = t * TQ + half * HQ
            qt = q_ref[r0:r0 + HQ, :].astype(jnp.float32).T
            parts.append(jnp.where(row_d < DK, qt, 0.0))
            parts.append(jnp.where(row_d >= DK, qt, 0.0))
        return jnp.concatenate(parts, axis=1).astype(jnp.bfloat16)

    def scores(task):
        t, k0, k1, kind = task
        if t not in qzt:
            qzt[t] = q_transposed(t)
        kb = k_ref[k0:k1, :]
        if kind == "upper":
            return masked(_dot(kb, qzt[t][:, 2 * HQ:]), 2)
        s = _dot(kb, qzt[t])
        return masked(s, 2) if kind == "mixed" else s

    def softmax(task, s):
        t, _, _, kind = task
        mx = jnp.max(s, axis=0, keepdims=True)
        if kind == "mixed":
            m_run[t] = mx
            return jnp.exp2(s - mx).astype(jnp.bfloat16), None
        m_old = m_run[t][:, 2 * HQ:] if kind == "upper" else m_run[t]
        m_new = jnp.maximum(m_old, mx)
        alpha = jnp.exp2(m_old - m_new)
        p = jnp.exp2(s - m_new)
        m_run[t] = jnp.concatenate([m_run[t][:, :2 * HQ], m_new], axis=1) if kind == "upper" else m_new
        return p.astype(jnp.bfloat16), alpha

    def accumulate(task, p, alpha):
        t, k0, k1, kind = task
        vta = jnp.concatenate([vt_ref[:, k0:k1], ones_rows[:, :k1 - k0]], axis=0)
        pv = _dot(vta, p)
        if kind == "mixed":
            acc_ref[t] = pv
        elif kind == "upper":
            acc_ref[t, :, 2 * HQ:] = alpha * acc_ref[t, :, 2 * HQ:] + pv
        else:
            acc_ref[t] = alpha * acc_ref[t] + pv

    def finish(t):
        o_all = acc_ref[t, 0:DV, :] * (1.0 / acc_ref[t, DV:DV + 1, :])
        for half in range(2):
            o = (o_all[:, (2 * half) * HQ:(2 * half + 1) * HQ]
                 - lam * o_all[:, (2 * half + 1) * HQ:(2 * half + 2) * HQ])
            ms = jnp.mean(o * o, axis=0, keepdims=True)
            on = o * lax.rsqrt(ms + NORM_EPS) * sg_ref[...] * (1.0 - LAMBDA_INIT)
            r0 = t * TQ + half * HQ
            o_ref[r0:r0 + HQ, :] = on.T.astype(jnp.bfloat16)

    n = len(tasks)

    def cast_share(step):
        for src, dst in ((wg_ref, wgb_ref), (wu_ref, wub_ref), (wd_ref, wdb_ref)):
            rows = src.shape[1]
            per = -(-rows // (n + 2) // SUM_ROWS) * SUM_ROWS
            r0, r1 = min(step * per, rows), min((step + 1) * per, rows)
            if r1 > r0:
                dst[:, r0:r1, :] = src[:, r0:r1, :].astype(jnp.bfloat16)

    s_prev, p_prev = None, None
    for step in range(n + 2):
        cast_share(step)
        if step >= 2:
            accumulate(tasks[step - 2], *p_prev)
            if last_task[tasks[step - 2][0]] == step - 2:
                finish(tasks[step - 2][0])
        if 1 <= step <= n:
            p_prev = softmax(tasks[step - 1], s_prev)
        if step < n:
            s_prev = scores(tasks[step])


def _attention(q, k, vt, lq1, lk1, lq2, lk2, sg_col, w_gate, w_up, w_down, bsz, seq):
    n = q.shape[0]
    nq = seq // TQ
    steps = bsz * N_HEADS
    assert N_EXPERTS % steps == 0
    epw = N_EXPERTS // steps
    vec = lambda b, h: (0, 0)
    w_map = lambda b, h: (b * N_HEADS + h, 0, 0)
    bf16 = jnp.bfloat16
    return pl.pallas_call(
        functools.partial(_attn_kernel, nq=nq),
        grid=(bsz, N_HEADS),
        in_specs=[
            pl.BlockSpec((seq, DV), lambda b, h: (b, h)),
            pl.BlockSpec((seq, DV), lambda b, h: (b, h)),
            pl.BlockSpec((DV, seq), lambda b, h: (h, b)),
            pl.BlockSpec((1, DK), vec), pl.BlockSpec((1, DK), vec),
            pl.BlockSpec((1, DK), vec), pl.BlockSpec((1, DK), vec),
            pl.BlockSpec((DV, 1), vec),
            pl.BlockSpec((epw, D_MODEL, D_FF), w_map),
            pl.BlockSpec((epw, D_MODEL, D_FF), w_map),
            pl.BlockSpec((epw, D_FF, D_MODEL), w_map),
        ],
        out_specs=[
            pl.BlockSpec((seq, DV), lambda b, h: (b, h)),
            pl.BlockSpec((epw, D_MODEL, D_FF), w_map),
            pl.BlockSpec((epw, D_MODEL, D_FF), w_map),
            pl.BlockSpec((epw, D_FF, D_MODEL), w_map),
        ],
        out_shape=[
            jax.ShapeDtypeStruct((n, ATTN_WIDTH), bf16),
            jax.ShapeDtypeStruct((N_EXPERTS, D_MODEL, D_FF), bf16),
            jax.ShapeDtypeStruct((N_EXPERTS, D_MODEL, D_FF), bf16),
            jax.ShapeDtypeStruct((N_EXPERTS, D_FF, D_MODEL), bf16),
        ],
        scratch_shapes=[pltpu.VMEM((nq, DV + SUM_ROWS, 4 * HQ), jnp.float32)],
        compiler_params=pltpu.CompilerParams(
            dimension_semantics=("arbitrary", "arbitrary"), vmem_limit_bytes=VMEM_LIMIT),
        name="attn",
    )(q, k, vt, lq1, lk1, lq2, lk2, sg_col, w_gate, w_up, w_down)


ROUTER_ROWS = 128
EXPERT_ROW0 = 32


def _outproj_kernel(x_ref, attn_ref, conv_ref, wo_ref, g2_ref, wr2_ref,
                    h_ref, xs_ref, meta_ref, cnt_ref):
    tm = x_ref.shape[0]
    t = T_SORT
    f32, bf16 = jnp.float32, jnp.bfloat16
    a = jnp.concatenate([attn_ref[...], conv_ref[...]], axis=1)
    h = x_ref[...] + _dot(a, wo_ref[...])
    h_ref[...] = h
    ms = jnp.mean(h * h, axis=-1, keepdims=True)
    hn = h * lax.rsqrt(ms + NORM_EPS) * g2_ref[...]
    hn_hi, hn_lo = _split2(hn)

    hh = _dot(hn_hi, wr2_ref[...])
    logits = hh[:, :ROUTER_ROWS] + hh[:, ROUTER_ROWS:] + _dot(hn_lo, wr2_ref[:, :ROUTER_ROWS])
    lt = logits.T
    row8 = lax.broadcasted_iota(jnp.int32, (SUBLANES, tm), 0).astype(f32)
    neg_inf = -jnp.inf

    def first_argmax(v):
        mx = jnp.max(v, axis=0, keepdims=True)
        idx = jnp.min(jnp.where(v == mx, row8, float(SUBLANES)), axis=0, keepdims=True)
        return mx, idx

    g_log = jnp.where(row8 < N_GROUPS, lt[0:SUBLANES, :], neg_inf)
    g_max, g_sel = first_argmax(g_log)
    g_gate = 1.0 / jnp.sum(jnp.exp(g_log - g_max), axis=0, keepdims=True)
    e_log = jnp.zeros((EPG, tm), f32)
    for g in range(N_GROUPS):
        rows = lt[EXPERT_ROW0 + g * EPG:EXPERT_ROW0 + (g + 1) * EPG, :]
        e_log = jnp.where(g_sel == float(g), rows, e_log)
    v1, i1 = first_argmax(e_log)
    v2, i2 = first_argmax(jnp.where(row8 == i1, neg_inf, e_log))
    tt = jnp.exp(v2 - v1)
    w1 = g_gate / (1.0 + tt)
    w2 = g_gate * tt / (1.0 + tt)
    e1 = g_sel * float(EPG) + i1
    e2 = g_sel * float(EPG) + i2

    row32 = lax.broadcasted_iota(jnp.int32, (N_EXPERTS, tm), 0).astype(f32)
    oh1 = row32 == e1
    oh2 = row32 == e2
    c = jnp.where(oh1 | oh2, 1.0, 0.0).astype(bf16)
    tok_r = lax.broadcasted_iota(jnp.int32, (tm, tm), 0)
    tok_c = lax.broadcasted_iota(jnp.int32, (tm, tm), 1)
    same_tile = (tok_r // t) == (tok_c // t)
    rank = _dot(c, jnp.where(same_tile & (tok_r < tok_c), 1.0, 0.0).astype(bf16))
    cnt_b = _dot(c, jnp.where(same_tile, 1.0, 0.0).astype(bf16))
    ex_r = lax.broadcasted_iota(jnp.int32, (N_EXPERTS, N_EXPERTS), 0)
    ex_c = lax.broadcasted_iota(jnp.int32, (N_EXPERTS, N_EXPERTS), 1)
    lower = jnp.where(ex_c < ex_r, 1.0, 0.0).astype(bf16)
    start_b = _dot(lower, cnt_b.astype(bf16))
    pos_e = start_b + rank
    p1 = jnp.sum(jnp.where(oh1, pos_e, 0.0), axis=0, keepdims=True)
    p2 = jnp.sum(jnp.where(oh2, pos_e, 0.0), axis=0, keepdims=True)

    srow = lax.broadcasted_iota(jnp.int32, (2 * t, t), 0).astype(f32)
    for s in range(tm // t):
        cols = slice(s * t, (s + 1) * t)
        perm = jnp.where((srow == p1[:, cols]) | (srow == p2[:, cols]), 1.0, 0.0).astype(bf16)
        xs = _dot(perm, hn_hi[cols, :])
        xs_ref[s * 2 * t:(s + 1) * 2 * t] = _pack_rows(xs)
        cnt_ref[s * N_EXPERTS:(s + 1) * N_EXPERTS, :] = cnt_b[:, s * t:s * t + LANES]

    meta = jnp.concatenate([p1, p2, w1, w2, jnp.zeros((LANES - 4, tm), f32)], axis=0)
    meta_ref[...] = meta.T


def _outproj(x2, attn_o, conv_o, wo, g2, wr2):
    n = x2.shape[0]
    tm = TM_OUT
    nsub = tm // T_SORT
    const = lambda i: (0, 0)

    return pl.pallas_call(
        _outproj_kernel,
        grid=(n // tm,),
        in_specs=[
            pl.BlockSpec((tm, D_MODEL), lambda i: (i, 0)),
            pl.BlockSpec((tm, ATTN_WIDTH), lambda i: (i, 0)),
            pl.BlockSpec((tm, CONV_WIDTH), lambda i: (i, 0)),
            pl.BlockSpec((D_MODEL, D_MODEL), const),
            pl.BlockSpec((1, D_MODEL), const),
            pl.BlockSpec((D_MODEL, 2 * ROUTER_ROWS), const),
        ],
        out_specs=[
            pl.BlockSpec((tm, D_MODEL), lambda i: (i, 0)),
            pl.BlockSpec((2 * tm, ROW_CHUNKS, LANES), lambda i: (i, 0, 0)),
            pl.BlockSpec((tm, LANES), lambda i: (i, 0)),
            pl.BlockSpec((nsub * N_EXPERTS, LANES), lambda i: (i, 0)),
        ],
        out_shape=[
            jax.ShapeDtypeStruct((n, D_MODEL), jnp.float32),
            jax.ShapeDtypeStruct((2 * n, ROW_CHUNKS, LANES), PACKED_DTYPE),
            jax.ShapeDtypeStruct((n, LANES), jnp.float32),
            jax.ShapeDtypeStruct((n // T_SORT * N_EXPERTS, LANES), jnp.float32),
        ],
        compiler_params=pltpu.CompilerParams(
            dimension_semantics=("arbitrary",), vmem_limit_bytes=VMEM_LIMIT),
        name="outproj",
    )(x2, attn_o, conv_o, wo, g2, wr2)


def _pack_rows(x):
    r = x.shape[0]
    w = pltpu.pack_elementwise([x[:, :ROW_WORDS], x[:, ROW_WORDS:]], packed_dtype=jnp.bfloat16)
    return pltpu.bitcast(w, PACKED_DTYPE).reshape(r, ROW_CHUNKS, LANES)


def _packed_zero_rows(r):
    z = jnp.zeros((r, ROW_CHUNKS, LANES), jnp.float32)
    w = pltpu.pack_elementwise([z, z], packed_dtype=jnp.bfloat16)
    return pltpu.bitcast(w, PACKED_DTYPE)


def _unpack_rows(u):
    r = u.shape[0]
    w = u.reshape(r, ROW_WORDS)
    lo = pltpu.unpack_elementwise(w, index=0, packed_dtype=jnp.bfloat16, unpacked_dtype=jnp.float32)
    hi = pltpu.unpack_elementwise(w, index=1, packed_dtype=jnp.bfloat16, unpacked_dtype=jnp.float32)
    return jnp.concatenate([lo, hi], axis=1).astype(jnp.bfloat16)


def _expert_kernel(nblk_ref, blk0_ref, be_ref, nused_ref, r0_ref, nvalid_ref, tlo_ref, thi_ref,
                   cnt_ref, src_ref, rbase_ref,
                   xs_hbm, wg_ref, wu_ref, wd_ref, y_hbm,
                   xbuf, ybuf, sem, ysem):
    e = pl.program_id(0)
    nused = nused_ref[0]
    n_blocks = be_ref.shape[0]
    n_tiles = cnt_ref.shape[0] // N_EXPERTS

    def start_run(blk, slot, t, live):
        r0 = r0_ref[blk]
        k = t * N_EXPERTS + be_ref[blk]
        lo = jnp.maximum(rbase_ref[k], r0)
        hi = jnp.minimum(rbase_ref[k] + cnt_ref[k], r0 + MOE_BLOCK)
        rows = jnp.where(live, hi - lo, 0)

        @pl.when(rows > 0)
        def _():
            pltpu.make_async_copy(xs_hbm.at[pl.ds(src_ref[k] + lo - rbase_ref[k], rows)],
                                  xbuf.at[slot, pl.ds(lo - r0, rows)], sem.at[slot]).start(priority=GATHER_PRIORITY)

    def gather_rolled(blk, slot, t_from):
        def run(t, carry):
            start_run(blk, slot, t, True)
            return carry
        lax.fori_loop(t_from, thi_ref[blk], run, 0)

    def gather_unrolled(blk, slot, live, part):
        blk = jnp.minimum(blk, n_blocks - 1)
        per = GATHER_UNROLL // GATHER_PARTS
        for r in range(part * per, (part + 1) * per):
            t = tlo_ref[blk] + r
            start_run(blk, slot, jnp.minimum(t, n_tiles - 1), live & (t < thi_ref[blk]))
        if part == GATHER_PARTS - 1:
            @pl.when(live & (tlo_ref[blk] + GATHER_UNROLL < thi_ref[blk]))
            def _():
                gather_rolled(blk, slot, tlo_ref[blk] + GATHER_UNROLL)

    def y_copy(blk, slot):
        return pltpu.make_async_copy(ybuf.at[slot], y_hbm.at[pl.ds(blk * MOE_BLOCK, MOE_BLOCK)], ysem.at[slot])

    @pl.when(e == 0)
    def _():
        for slot in range(N_XBUF):
            xbuf[slot] = _packed_zero_rows(MOE_BLOCK)
        gather_rolled(0, 0, tlo_ref[0])
        for blk in range(1, N_XBUF - 1):
            @pl.when(nused > blk)
            def _():
                gather_rolled(blk, blk, tlo_ref[blk])
        for slot in range(2):
            ybuf[slot] = _packed_zero_rows(MOE_BLOCK)
            y_copy(slot, slot).start()

    @pl.when(nblk_ref[e] > 0)
    def _():
        def block(b):
            slot = b % N_XBUF
            yslot = b % 2
            nv = nvalid_ref[b]
            y_copy(b, yslot).wait()
            pltpu.make_async_copy(xs_hbm.at[pl.ds(0, nv)], xbuf.at[slot, pl.ds(0, nv)], sem.at[slot]).wait()

            ahead = b + N_XBUF - 1
            x = _unpack_rows(xbuf[slot])
            g = _dot(x, wg_ref[0])
            gather_unrolled(ahead, ahead % N_XBUF, ahead < nused, 0)
            u = _dot(x, wu_ref[0])
            act = (g / (1.0 + jnp.exp(-g)) * u).astype(jnp.bfloat16)
            gather_unrolled(ahead, ahead % N_XBUF, ahead < nused, 1)
            y = _dot(act, wd_ref[0])
            ybuf[yslot] = _pack_rows(y)
            y_copy(b, yslot).start()
            gather_unrolled(ahead, ahead % N_XBUF, ahead < nused, 2)

        def loop_body(c, carry):
            block(blk0_ref[e] + c)
            return carry
        lax.fori_loop(0, nblk_ref[e], loop_body, 0)

    @pl.when(e == N_EXPERTS - 1)
    def _():
        for slot in range(2):
            y_copy(slot, slot).wait()
        ybuf[0] = _packed_zero_rows(MOE_BLOCK)

        def zero_block(b, carry):
            y_copy(b, 0).start()
            y_copy(b, 0).wait()
            return carry
        lax.fori_loop(nused, n_blocks, zero_block, 0)


def _experts(tabs, xs, w_gate, w_up, w_down):
    n_slots = tabs["n_slots"]
    w_map = lambda e, *_: (e, 0, 0)
    return pl.pallas_call(
        _expert_kernel,
        grid_spec=pltpu.PrefetchScalarGridSpec(
            num_scalar_prefetch=11,
            grid=(N_EXPERTS,),
            in_specs=[
                pl.BlockSpec(memory_space=pl.ANY),
                pl.BlockSpec((1, D_MODEL, D_FF), w_map),
                pl.BlockSpec((1, D_MODEL, D_FF), w_map),
                pl.BlockSpec((1, D_FF, D_MODEL), w_map),
            ],
            out_specs=pl.BlockSpec(memory_space=pl.ANY),
            scratch_shapes=[pltpu.VMEM((N_XBUF, MOE_BLOCK, ROW_CHUNKS, LANES), PACKED_DTYPE),
                            pltpu.VMEM((2, MOE_BLOCK, ROW_CHUNKS, LANES), PACKED_DTYPE),
                            pltpu.SemaphoreType.DMA((N_XBUF,)),
                            pltpu.SemaphoreType.DMA((2,))],
        ),
        out_shape=jax.ShapeDtypeStruct((n_slots, ROW_CHUNKS, LANES), PACKED_DTYPE),
        compiler_params=pltpu.CompilerParams(
            dimension_semantics=("arbitrary",), vmem_limit_bytes=VMEM_LIMIT),
        name="experts",
    )(tabs["nblk"], tabs["blk0"], tabs["block_e"], tabs["nused"], tabs["r0"], tabs["nvalid"], tabs["tlo"],
      tabs["thi"], tabs["cnt"], tabs["src"], tabs["rbase"], xs, w_gate, w_up, w_down)


def _combine_kernel(cnt_ref, loc_ref, dst_ref, h_ref, meta_ref, yg_hbm, o_ref, ybuf, sem, *, n_steps, nsub):
    i = pl.program_id(0)
    t = T_SORT
    f32, bf16 = jnp.float32, jnp.bfloat16

    slot_in = i % (COMBINE_AHEAD + 1)
    tile0 = jnp.minimum(i, n_steps - 1) * nsub
    for s in range(nsub):
        for e in range(N_EXPERTS):
            k = (tile0 + s) * N_EXPERTS + e
            rows = jnp.where(i < n_steps, cnt_ref[k], 0)

            @pl.when(rows > 0)
            def _():
                pltpu.make_async_copy(yg_hbm.at[pl.ds(dst_ref[k], rows)],
                                      ybuf.at[slot_in, pl.ds(s * 2 * t + loc_ref[k], rows)],
                                      sem.at[slot_in]).start(priority=GATHER_PRIORITY)

    @pl.when(i >= COMBINE_AHEAD)
    def _():
        slot = (i - COMBINE_AHEAD) % (COMBINE_AHEAD + 1)
        pltpu.make_async_copy(yg_hbm.at[pl.ds(0, nsub * 2 * t)], ybuf.at[slot], sem.at[slot]).wait()
        lane = lax.broadcasted_iota(jnp.int32, (t, 2 * t), 1).astype(f32)
        for s in range(nsub):
            y = _unpack_rows(ybuf[slot, s * 2 * t:(s + 1) * 2 * t])
            meta = meta_ref[s * t:(s + 1) * t, :]
            pick1 = jnp.where(lane == meta[:, 0:1], 1.0, 0.0).astype(bf16)
            pick2 = jnp.where(lane == meta[:, 1:2], 1.0, 0.0).astype(bf16)
            y12 = _dot(jnp.concatenate([pick1, pick2], axis=0), y)
            o_ref[s * t:(s + 1) * t, :] = h_ref[s * t:(s + 1) * t, :] + (meta[:, 2:3] * y12[:t]
                                                                         + meta[:, 3:4] * y12[t:])


def _combine(cnt, loc, dst, h, meta, yg):
    n = h.shape[0]
    tm = TM_COMBINE
    nsub = tm // T_SORT
    n_steps = n // tm
    prev = lambda i, *_: (jnp.maximum(i - COMBINE_AHEAD, 0), 0)
    return pl.pallas_call(
        functools.partial(_combine_kernel, n_steps=n_steps, nsub=nsub),
        grid_spec=pltpu.PrefetchScalarGridSpec(
            num_scalar_prefetch=3,
            grid=(n_steps + COMBINE_AHEAD,),
            in_specs=[
                pl.BlockSpec((tm, D_MODEL), prev),
                pl.BlockSpec((tm, LANES), prev),
                pl.BlockSpec(memory_space=pl.ANY),
            ],
            out_specs=pl.BlockSpec((tm, D_MODEL), prev),
            scratch_shapes=[pltpu.VMEM((COMBINE_AHEAD + 1, nsub * 2 * T_SORT, ROW_CHUNKS, LANES), PACKED_DTYPE),
                            pltpu.SemaphoreType.DMA((COMBINE_AHEAD + 1,))],
        ),
        out_shape=jax.ShapeDtypeStruct((n, D_MODEL), jnp.float32),
        compiler_params=pltpu.CompilerParams(
            dimension_semantics=("arbitrary",), vmem_limit_bytes=VMEM_LIMIT),
        name="combine",
    )(cnt, loc, dst, h, meta, yg)


def _routing_tables(cnt_out, n_tiles, n_tok):
    i32 = jnp.int32
    cnt = cnt_out.reshape(n_tiles, N_EXPERTS, LANES)[:, :, 0].astype(i32)
    count = jnp.sum(cnt, axis=0)
    padded = ((count + MOE_BLOCK - 1) // MOE_BLOCK) * MOE_BLOCK
    pad_end = jnp.cumsum(padded)
    pad_start = pad_end - padded
    run_end = jnp.cumsum(cnt, axis=0)
    rbase = run_end - cnt
    dst = pad_start[None, :] + rbase
    loc = jnp.cumsum(cnt, axis=1) - cnt
    src = loc + (jnp.arange(n_tiles, dtype=i32) * (2 * T_SORT))[:, None]
    n_blocks = (2 * n_tok) // MOE_BLOCK + N_EXPERTS
    block_start = jnp.arange(n_blocks, dtype=i32) * MOE_BLOCK
    block_e = jnp.minimum(jnp.sum(pad_end[None, :] <= block_start[:, None], axis=1), N_EXPERTS - 1).astype(i32)
    nused = (pad_end[-1] // MOE_BLOCK).astype(i32).reshape(1)
    ex = jnp.arange(N_EXPERTS, dtype=i32)
    sel = (block_e[None, :] == ex[:, None]).astype(i32)
    pick = lambda per_expert: jnp.sum(per_expert[..., :, None] * sel, axis=-2)
    r0 = block_start - pick(pad_start)
    nvalid = jnp.clip(pick(count) - r0, 0, MOE_BLOCK)
    tlo = jnp.sum(pick(run_end) <= r0[None, :], axis=0)
    thi = jnp.sum(pick(rbase) < (r0 + MOE_BLOCK)[None, :], axis=0)
    flat = lambda a: a.reshape(-1).astype(i32)
    return dict(cnt=flat(cnt), src=flat(src), dst=flat(dst), loc=flat(loc), rbase=flat(rbase),
                block_e=block_e, nused=nused, r0=flat(r0), nvalid=flat(nvalid), tlo=flat(tlo), thi=flat(thi),
                nblk=flat(padded // MOE_BLOCK), blk0=flat(pad_start // MOE_BLOCK), n_slots=n_blocks * MOE_BLOCK)


def _stage1(x, positions, attn_norm_gain, w_in, q_norm_gain, k_norm_gain, conv_w, conv_out_gain):
    bsz, seq, _ = x.shape
    n = bsz * seq
    f32, bf16 = jnp.float32, jnp.bfloat16
    w = w_in[0]
    wb = w.astype(bf16)
    wvt = w[:, 2 * ATTN_WIDTH:3 * ATTN_WIDTH].T.astype(bf16)
    scale = DK ** -0.5 * LOG2E
    gqk = jnp.concatenate([jnp.tile(q_norm_gain[0].astype(f32), 2 * N_HEADS) * scale,
                           jnp.tile(k_norm_gain[0].astype(f32), 2 * N_HEADS)]).reshape(1, -1)
    freqs = (ROPE_THETA ** (-jnp.arange(0, ROT_DIM, 2, dtype=f32) / ROT_DIM)).reshape(SUBLANES, 1)
    return _inproj(x.reshape(n, D_MODEL), positions.reshape(1, n),
                   attn_norm_gain[0].reshape(1, -1).astype(f32), wb, wvt, wb, gqk, freqs,
                   conv_w[0].astype(f32), conv_out_gain[0].reshape(1, -1).astype(f32), seq)


def kernel(x, positions, attn_norm_gain, w_in, q_norm_gain, k_norm_gain, lambda_q1, lambda_k1, lambda_q2, lambda_k2, subln_gain, conv_w, conv_out_gain, w_out, ffn_norm_gain, w_group_router, w_expert_router, w_gate, w_up, w_down):
    bsz, seq, _ = x.shape
    n = bsz * seq
    f32, bf16 = jnp.float32, jnp.bfloat16
    assert TM_OUT % T_SORT == 0 and seq % TM_IN == 0 and seq % TQ == 0 and TQ % KV_FULL == 0
    assert TM_COMBINE % T_SORT == 0 and n % TM_COMBINE == 0
    q, k, vt, conv_o = _stage1(x, positions, attn_norm_gain, w_in, q_norm_gain, k_norm_gain,
                                conv_w, conv_out_gain)
    attn_o, wg_b, wu_b, wd_b = _attention(q, k, vt,
                        lambda_q1[0].reshape(1, -1).astype(f32), lambda_k1[0].reshape(1, -1).astype(f32),
                        lambda_q2[0].reshape(1, -1).astype(f32), lambda_k2[0].reshape(1, -1).astype(f32),
                        subln_gain[0].reshape(-1, 1).astype(f32), w_gate[0], w_up[0], w_down[0], bsz, seq)

    wr = jnp.concatenate([
        w_group_router[0].astype(f32), jnp.zeros((D_MODEL, EXPERT_ROW0 - N_GROUPS), f32),
        jnp.transpose(w_expert_router[0].astype(f32), (1, 0, 2)).reshape(D_MODEL, N_EXPERTS),
        jnp.zeros((D_MODEL, ROUTER_ROWS - EXPERT_ROW0 - N_EXPERTS), f32)], axis=1)
    wrh = wr.astype(bf16)
    wr2 = jnp.concatenate([wrh, (wr - wrh.astype(f32)).astype(bf16)], axis=1)
    h, xs, meta, cnt_out = _outproj(x.reshape(n, D_MODEL), attn_o, conv_o, w_out[0].astype(bf16),
                                    ffn_norm_gain[0].reshape(1, -1).astype(f32), wr2)

    tabs = _routing_tables(cnt_out, n // T_SORT, n)
    yg = _experts(tabs, xs, wg_b, wu_b, wd_b)
    out = _combine(tabs["cnt"], tabs["loc"], tabs["dst"], h, meta, yg)
    return out.reshape(x.shape)
```
